```python
import math
import jax, jax.numpy as jnp
from jax import lax
import numpy as np

D_MODEL = 1024
BATCH = 8
SEQ = 2048
DEPTH = 2

CHUNK = 64
HEAD_DIM = 64
EPS = 1e-6
NEG_INF = -1e30
A_Q_HEADS = 8
A_KV_HEADS = 2
A_GROUP = A_Q_HEADS // A_KV_HEADS
A_WINDOW = 128
A_PREV_CHUNKS = A_WINDOW // CHUNK
T5_BUCKETS = 32
T5_MAX_DIST = 128
B_HEADS = 4
B_Q_RANK = 256
B_KV_RANK = 128
B_NOPE_DIM = 64
B_ROPE_DIM = 32
B_QK_DIM = B_NOPE_DIM + B_ROPE_DIM
B_V_DIM = 64
ROPE_THETA = 10000.0
Q_BLOCK = 128
C_HEADS = 4
C_PREV_CHUNKS = 8
C_REL_CLIP = 256
C_REL_SIZE = (CHUNK - 1) + C_REL_CLIP + 1
A_WIDTH = A_Q_HEADS * HEAD_DIM
B_WIDTH = B_HEADS * B_V_DIM
C_WIDTH = C_HEADS * HEAD_DIM
MIX_WIDTH = A_WIDTH + B_WIDTH + C_WIDTH
IN_SPLITS = (A_WIDTH, A_KV_HEADS * HEAD_DIM, A_KV_HEADS * HEAD_DIM,
             B_Q_RANK, B_KV_RANK, B_ROPE_DIM,
             C_WIDTH, C_WIDTH, C_WIDTH)
IN_WIDTH = sum(IN_SPLITS)
N_EXPERTS = 32
TOP_K = 4
D_EXPERT = D_MODEL
SWIGLU_LIMIT = 7.0
SWIGLU_ALPHA = 1.702
EXPERT_BLOCK = 128

kernel_name = "hybrid_chunk_causal_trunk"


def rms_norm(x, g):
    xf = x.astype(jnp.float32)
    y = xf * lax.rsqrt(jnp.mean(xf * xf, axis=-1, keepdims=True) + EPS)
    return (y * g).astype(x.dtype)


def rope(x, pos):
    half = x.shape[-1] // 2
    inv_freq = ROPE_THETA ** (-(jnp.arange(half, dtype=jnp.float32) / half))
    ang = pos.astype(jnp.float32)[:, None] * inv_freq[None, :]
    shape = (ang.shape[0],) + (1,) * (x.ndim - 3) + (half,)
    cos = jnp.cos(ang).reshape(shape)
    sin = jnp.sin(ang).reshape(shape)
    xf = x.astype(jnp.float32)
    x1, x2 = xf[..., :half], xf[..., half:]
    return jnp.concatenate([x1 * cos - x2 * sin, x2 * cos + x1 * sin], axis=-1).astype(x.dtype)


def chunk_band(t, n_prev):
    b, s, h, d = t.shape
    nc = s // CHUNK
    tc = t.reshape(b, nc, CHUNK, h, d)
    tp = jnp.pad(tc, ((0, 0), (n_prev, 0), (0, 0), (0, 0), (0, 0)))
    idx = jnp.arange(nc)[:, None] + jnp.arange(n_prev + 1)[None, :]
    return tp[:, idx].reshape(b, nc, (n_prev + 1) * CHUNK, h, d)


def band_valid(nc, n_prev):
    key_chunk = jnp.arange(nc)[:, None] - n_prev + jnp.arange((n_prev + 1) * CHUNK)[None, :] // CHUNK
    return key_chunk >= 0


def t5_bucket(rel):
    half = T5_BUCKETS // 2
    max_exact = half // 2
    ret = jnp.where(rel > 0, half, 0)
    n = jnp.abs(rel)
    nf = jnp.maximum(n, 1).astype(jnp.float32)
    large = max_exact + (jnp.log(nf / max_exact) / math.log(T5_MAX_DIST / max_exact)
                         * (half - max_exact)).astype(jnp.int32)
    large = jnp.minimum(large, half - 1)
    return ret + jnp.where(n < max_exact, n, large)


def sliding_window_gqa(q, k, v, q_gain, k_gain, sinks, t5_bias):
    b, s, _ = q.shape
    nc = s // CHUNK
    q = rms_norm(q.reshape(b, s, A_KV_HEADS, A_GROUP, HEAD_DIM), q_gain)
    k = rms_norm(k.reshape(b, s, A_KV_HEADS, HEAD_DIM), k_gain)
    v = v.reshape(b, s, A_KV_HEADS, HEAD_DIM)
    qc = q.reshape(b, nc, CHUNK, A_KV_HEADS, A_GROUP, HEAD_DIM)
    kb = chunk_band(k, A_PREV_CHUNKS)
    vb = chunk_band(v, A_PREV_CHUNKS)
    valid = band_valid(nc, A_PREV_CHUNKS)
    logits = jnp.einsum("bnqhgd,bnkhd->bnhgqk", qc, kb).astype(jnp.float32) * (HEAD_DIM ** -0.5) + t5_bias
    logits = jnp.where(valid[None, :, None, None, None, :], logits, NEG_INF)
    sink = sinks.reshape(A_KV_HEADS, A_GROUP)[:, :, None, None].astype(jnp.float32)
    m = jnp.maximum(jnp.max(logits, axis=-1, keepdims=True), sink)
    e = jnp.exp(logits - m)
    p = e / (jnp.sum(e, axis=-1, keepdims=True) + jnp.exp(sink - m))
    o = jnp.einsum("bnhgqk,bnkhd->bnqhgd", p.astype(v.dtype), vb)
    return o.reshape(b, s, A_WIDTH)


def block_causal_attention(q, k, v):
    b, s, h, dq = q.shape
    dv = v.shape[-1]
    nqb = s // Q_BLOCK
    qb = q.reshape(b, nqb, Q_BLOCK, h, dq).transpose(1, 0, 2, 3, 4)
    key_chunk = jnp.arange(s) // CHUNK
    scale = dq ** -0.5

    def one_block(args):
        q_blk, blk = args
        q_chunk = (blk * Q_BLOCK + jnp.arange(Q_BLOCK)) // CHUNK
        mask = key_chunk[None, :] <= q_chunk[:, None]
        logits = jnp.einsum("bqhd,bkhd->bhqk", q_blk, k).astype(jnp.float32) * scale
        p = jax.nn.softmax(jnp.where(mask, logits, NEG_INF), axis=-1)
        return jnp.einsum("bhqk,bkhd->bqhd", p.astype(v.dtype), v)

    out = lax.map(one_block, (qb, jnp.arange(nqb)))
    return out.transpose(1, 0, 2, 3, 4).reshape(b, s, h * dv)


def latent_attention(c_q, c_kv, k_pe, q_a_gain, w_q_b, kv_a_gain, w_kv_b, q_gain, k_gain, pos):
    b, s, _ = c_q.shape
    q = (rms_norm(c_q, q_a_gain) @ w_q_b).reshape(b, s, B_HEADS, B_QK_DIM)
    kv = (rms_norm(c_kv, kv_a_gain) @ w_kv_b).reshape(b, s, B_HEADS, B_NOPE_DIM + B_V_DIM)
    k_nope, v = kv[..., :B_NOPE_DIM], kv[..., B_NOPE_DIM:]
    q_nope = rms_norm(q[..., :B_NOPE_DIM], q_gain[:B_NOPE_DIM])
    q_pe = rope(rms_norm(q[..., B_NOPE_DIM:], q_gain[B_NOPE_DIM:]), pos)
    k_nope = rms_norm(k_nope, k_gain[:B_NOPE_DIM])
    k_pe = rope(rms_norm(k_pe, k_gain[B_NOPE_DIM:]), pos)
    qf = jnp.concatenate([q_nope, q_pe], axis=-1)
    kf = jnp.concatenate([k_nope, jnp.broadcast_to(k_pe[:, :, None, :], (b, s, B_HEADS, B_ROPE_DIM))], axis=-1)
    return block_causal_attention(qf, kf, v)


def chunk_relbias_attention(q, k, v, q_gain, k_gain, rel_table, rel_idx):
    b, s, _ = q.shape
    nc = s // CHUNK
    q = rms_norm(q.reshape(b, s, C_HEADS, HEAD_DIM), q_gain)
    k = rms_norm(k.reshape(b, s, C_HEADS, HEAD_DIM), k_gain)
    v = v.reshape(b, s, C_HEADS, HEAD_DIM)
    qc = q.reshape(b, nc, CHUNK, C_HEADS, HEAD_DIM)
    kb = chunk_band(k, C_PREV_CHUNKS)
    vb = chunk_band(v, C_PREV_CHUNKS)
    valid = band_valid(nc, C_PREV_CHUNKS)
    bias = rel_table[rel_idx].transpose(2, 0, 1).astype(jnp.float32)
    logits = jnp.einsum("bnqhd,bnkhd->bnhqk", qc, kb).astype(jnp.float32) * (HEAD_DIM ** -0.5) + bias
    p = jax.nn.softmax(jnp.where(valid[None, :, None, None, :], logits, NEG_INF), axis=-1)
    o = jnp.einsum("bnhqk,bnkhd->bnqhd", p.astype(v.dtype), vb)
    return o.reshape(b, s, C_WIDTH)


def clamped_swiglu(hgu):
    glu, lin = hgu[..., :D_EXPERT], hgu[..., D_EXPERT:]
    glu = jnp.minimum(glu, SWIGLU_LIMIT)
    lin = jnp.clip(lin, -SWIGLU_LIMIT, SWIGLU_LIMIT)
    return glu * jax.nn.sigmoid(SWIGLU_ALPHA * glu) * (lin + 1.0)


def moe_ffn(h, w_router, b_router, w_gu, b_gu, w_down, b_down):
    b, s, d = h.shape
    t = b * s
    xf = h.reshape(t, d)
    logits = (xf @ w_router + b_router).astype(jnp.float32)
    top_vals, top_idx = lax.top_k(logits, TOP_K)
    gates = jax.nn.softmax(top_vals, axis=-1)
    tk = t * TOP_K
    flat_e = top_idx.reshape(-1)
    flat_tok = jnp.arange(tk, dtype=jnp.int32) // TOP_K
    flat_g = gates.reshape(-1)
    order = jnp.argsort(flat_e)
    se, stok, sg = flat_e[order], flat_tok[order], flat_g[order]
    counts = jnp.zeros((N_EXPERTS,), jnp.int32).at[flat_e].add(1)
    padded = (counts + EXPERT_BLOCK - 1) // EXPERT_BLOCK * EXPERT_BLOCK
    pad_end = jnp.cumsum(padded)
    pad_start = pad_end - padded
    start = jnp.cumsum(counts) - counts
    dest = pad_start[se] + (jnp.arange(tk, dtype=jnp.int32) - start[se])
    n_blocks = -(-(tk + N_EXPERTS * EXPERT_BLOCK) // EXPERT_BLOCK)
    n_rows = n_blocks * EXPERT_BLOCK
    row_tok = jnp.full((n_rows,), t, jnp.int32).at[dest].set(stok)
    row_gate = jnp.zeros((n_rows,), jnp.float32).at[dest].set(sg)
    x_pad = jnp.concatenate([xf, jnp.zeros((1, d), xf.dtype)], axis=0)
    x_rows = x_pad[row_tok].reshape(n_blocks, EXPERT_BLOCK, d)
    blk_e = jnp.minimum(jnp.searchsorted(pad_end, jnp.arange(n_blocks, dtype=jnp.int32) * EXPERT_BLOCK,
                                         side="right"), N_EXPERTS - 1)

    def expert_block(args):
        xb, e = args
        hgu = xb @ w_gu[e] + b_gu[e]
        return clamped_swiglu(hgu) @ w_down[e] + b_down[e]

    y_rows = lax.map(expert_block, (x_rows, blk_e)).reshape(n_rows, d)
    y = jax.ops.segment_sum(y_rows * row_gate[:, None].astype(y_rows.dtype), row_tok, num_segments=t + 1)[:t]
    return y.astype(h.dtype).reshape(b, s, d)


def setup_inputs(seed: int = 0) -> dict:
    key = jax.random.key(seed)
    ks = jax.random.split(key, 32)
    L = DEPTH
    f32 = jnp.float32

    def nrm(k, shape, scale):
        return jax.random.normal(k, shape, f32) * scale

    def gain(k, shape):
        return 1.0 + 0.05 * jax.random.normal(k, shape, f32)

    return {
        "x": nrm(ks[0], (BATCH, SEQ, D_MODEL), 1.0),
        "attn_norm": gain(ks[1], (L, D_MODEL)),
        "w_in": nrm(ks[2], (L, D_MODEL, IN_WIDTH), D_MODEL ** -0.5),
        "a_q_norm": gain(ks[3], (L, HEAD_DIM)),
        "a_k_norm": gain(ks[4], (L, HEAD_DIM)),
        "a_sinks": nrm(ks[5], (L, A_Q_HEADS), 0.5),
        "t5_bias": nrm(ks[6], (T5_BUCKETS, A_Q_HEADS), 0.2),
        "b_q_a_norm": gain(ks[7], (L, B_Q_RANK)),
        "b_w_q_b": nrm(ks[8], (L, B_Q_RANK, B_HEADS * B_QK_DIM), B_Q_RANK ** -0.5),
        "b_kv_a_norm": gain(ks[9], (L, B_KV_RANK)),
        "b_w_kv_b": nrm(ks[10], (L, B_KV_RANK, B_HEADS * (B_NOPE_DIM + B_V_DIM)), B_KV_RANK ** -0.5),
        "b_q_norm": gain(ks[11], (L, B_QK_DIM)),
        "b_k_norm": gain(ks[12], (L, B_QK_DIM)),
        "c_q_norm": gain(ks[13], (L, HEAD_DIM)),
        "c_k_norm": gain(ks[14], (L, HEAD_DIM)),
        "c_rel_bias": nrm(ks[15], (L, C_REL_SIZE, C_HEADS), 0.2),
        "a_out_norm": gain(ks[16], (L, A_WIDTH)),
        "b_out_norm": gain(ks[17], (L, B_WIDTH)),
        "c_out_norm": gain(ks[18], (L, C_WIDTH)),
        "w_out": nrm(ks[19], (L, MIX_WIDTH, D_MODEL), MIX_WIDTH ** -0.5),
        "ffn_norm": gain(ks[20], (L, D_MODEL)),
        "w_router": nrm(ks[21], (L, D_MODEL, N_EXPERTS), D_MODEL ** -0.5),
        "b_router": nrm(ks[22], (L, N_EXPERTS), 0.01),
        "w_gate_up": nrm(ks[23], (L, N_EXPERTS, D_MODEL, 2 * D_EXPERT), D_MODEL ** -0.5),
        "b_gate_up": nrm(ks[24], (L, N_EXPERTS, 2 * D_EXPERT), 0.01),
        "w_down": nrm(ks[25], (L, N_EXPERTS, D_EXPERT, D_MODEL), D_EXPERT ** -0.5),
        "b_down": nrm(ks[26], (L, N_EXPERTS, D_MODEL), 0.01),
    }


def reference(x, attn_norm, w_in, a_q_norm, a_k_norm, a_sinks, t5_bias, b_q_a_norm, b_w_q_b,
              b_kv_a_norm, b_w_kv_b, b_q_norm, b_k_norm, c_q_norm, c_k_norm, c_rel_bias,
              a_out_norm, b_out_norm, c_out_norm, w_out, ffn_norm, w_router, b_router,
              w_gate_up, b_gate_up, w_down, b_down):
    s = x.shape[1]
    pos = jnp.arange(s)
    split_at = np.cumsum(IN_SPLITS)[:-1].tolist()
    wa = (A_PREV_CHUNKS + 1) * CHUNK
    rel_a = jnp.arange(wa)[None, :] - A_PREV_CHUNKS * CHUNK - jnp.arange(CHUNK)[:, None]
    a_bias = t5_bias[t5_bucket(rel_a)].transpose(2, 0, 1).reshape(A_KV_HEADS, A_GROUP, CHUNK, wa)
    a_bias = a_bias.astype(jnp.float32)
    wc = (C_PREV_CHUNKS + 1) * CHUNK
    dist_c = jnp.arange(CHUNK)[:, None] + C_PREV_CHUNKS * CHUNK - jnp.arange(wc)[None, :]
    rel_idx_c = jnp.clip(dist_c, -(CHUNK - 1), C_REL_CLIP) + (CHUNK - 1)

    for l in range(DEPTH):
        h = rms_norm(x, attn_norm[l])
        proj = h @ w_in[l]
        aq, ak, av, bcq, bckv, bkpe, cq, ck, cv = jnp.split(proj, split_at, axis=-1)
        o_a = sliding_window_gqa(aq, ak, av, a_q_norm[l], a_k_norm[l], a_sinks[l], a_bias)
        o_b = latent_attention(bcq, bckv, bkpe, b_q_a_norm[l], b_w_q_b[l], b_kv_a_norm[l], b_w_kv_b[l],
                               b_q_norm[l], b_k_norm[l], pos)
        o_c = chunk_relbias_attention(cq, ck, cv, c_q_norm[l], c_k_norm[l], c_rel_bias[l], rel_idx_c)
        o = jnp.concatenate([rms_norm(o_a, a_out_norm[l]), rms_norm(o_b, b_out_norm[l]),
                             rms_norm(o_c, c_out_norm[l])], axis=-1)
        x = x + o @ w_out[l]
        x = x + moe_ffn(rms_norm(x, ffn_norm[l]), w_router[l], b_router[l], w_gate_up[l],
                        b_gate_up[l], w_down[l], b_down[l])
    return x
```

```python
import functools
import math

import jax
import jax.numpy as jnp
import numpy as np
from jax import lax
from jax.experimental import pallas as pl
from jax.experimental.pallas import tpu as pltpu

D_MODEL = 1024
CHUNK = 64
HEAD_DIM = 64
EPS = 1e-6
NEG_INF = -1e30
A_Q_HEADS = 8
A_KV_HEADS = 2
A_GROUP = 4
A_PREV_CHUNKS = 2
T5_BUCKETS = 32
T5_MAX_DIST = 128
B_HEADS = 4
B_Q_RANK = 256
B_KV_RANK = 128
B_NOPE_DIM = 64
B_ROPE_DIM = 32
B_QK_DIM = 96
B_V_DIM = 64
ROPE_THETA = 10000.0
C_HEADS = 4
C_PREV_CHUNKS = 8
C_REL_CLIP = 256
N_EXPERTS = 32
TOP_K = 4
D_EXPERT = 1024
SWIGLU_LIMIT = 7.0
SWIGLU_ALPHA = 1.702

SEC_AQ = 0
SEC_AKV = 512
SEC_BCQ = 768
SEC_BCKV = 1024
SEC_BKPE = 1152
SEC_CQ = 1280
SEC_CK = 1536
SEC_CV = 1792
PROJ_WIDTH = 2048
B_HEAD_PAD = 128

Q_TILE = 128
A_BAND = Q_TILE + A_PREV_CHUNKS * CHUNK
C_BAND = Q_TILE + C_PREV_CHUNKS * CHUNK
A_PAD = A_PREV_CHUNKS * CHUNK
C_PAD = C_PREV_CHUNKS * CHUNK
B_TILE = 256
ROW_TILE = 512
EXPERT_ROWS = 256
ROUTER_PAD = 128
VMEM_LIMIT = 48 * 1024 * 1024


def _rms(x, gain):
    return x * lax.rsqrt(jnp.mean(x * x, axis=-1, keepdims=True) + EPS) * gain


def _nt_dot(a, b):
    return lax.dot_general(a, b, (((1,), (1,)), ((), ())), preferred_element_type=jnp.float32)


def _in_proj_kernel(x_ref, g_ref, w_ref, o_ref):
    h = _rms(x_ref[...], g_ref[...]).astype(jnp.bfloat16)
    for n in range(PROJ_WIDTH // 512):
        cols = slice(n * 512, (n + 1) * 512)
        o_ref[:, cols] = jnp.dot(h, w_ref[:, cols], preferred_element_type=jnp.float32).astype(o_ref.dtype)


def _in_proj(x2, gain, w):
    t = x2.shape[0]
    return pl.pallas_call(
        _in_proj_kernel,
        grid=(t // ROW_TILE,),
        in_specs=[
            pl.BlockSpec((ROW_TILE, D_MODEL), lambda i: (i, 0)),
            pl.BlockSpec((1, D_MODEL), lambda i: (0, 0)),
            pl.BlockSpec((D_MODEL, PROJ_WIDTH), lambda i: (0, 0)),
        ],
        out_specs=pl.BlockSpec((ROW_TILE, PROJ_WIDTH), lambda i: (i, 0)),
        out_shape=jax.ShapeDtypeStruct((t, PROJ_WIDTH), jnp.bfloat16),
        compiler_params=pltpu.CompilerParams(dimension_semantics=("arbitrary",), vmem_limit_bytes=VMEM_LIMIT),
        name="in_proj",
    )(x2, gain, w)


def _attn_a_kernel(q_ref, kv_ref, bias_ref, qg_ref, kg_ref, sink_ref, o_ref, k_pad, v_pad):
    i = pl.program_id(1)
    seq = kv_ref.shape[1]

    @pl.when(i == 0)
    def _():
        k_pad[0:A_PAD, :] = jnp.zeros((A_PAD, 128), k_pad.dtype)
        v_pad[0:A_PAD, :] = jnp.zeros((A_PAD, 128), v_pad.dtype)

        def fill(r, carry):
            rows = pl.ds(pl.multiple_of(r * 256, 256), 256)
            dst = pl.ds(pl.multiple_of(r * 256 + A_PAD, 128), 256)
            kv = kv_ref[0, rows, :]
            k = kv[:, 0:128].astype(jnp.float32)
            kn = jnp.concatenate([_rms(k[:, 0:64], kg_ref[...]), _rms(k[:, 64:128], kg_ref[...])], axis=1)
            k_pad[dst, :] = kn.astype(k_pad.dtype)
            v_pad[dst, :] = kv[:, 128:256]
            return carry

        lax.fori_loop(0, seq // 256, fill, 0)

    start = pl.multiple_of(i * Q_TILE, Q_TILE)
    kb = k_pad[pl.ds(start, A_BAND), :]
    vb = v_pad[pl.ds(start, A_BAND), :]
    q = q_ref[0].astype(jnp.float32)
    col = lax.broadcasted_iota(jnp.int32, (A_GROUP * Q_TILE, A_BAND), 1)
    key_ok = col + (start - A_PAD) >= 0
    scale = HEAD_DIM ** -0.5
    for hk in range(A_KV_HEADS):
        qs = []
        sinks = []
        for g in range(A_GROUP):
            h = hk * A_GROUP + g
            qh = _rms(q[:, h * 64:(h + 1) * 64], qg_ref[...]) * scale
            qs.append(qh.astype(jnp.bfloat16))
            sinks.append(jnp.full((Q_TILE, 1), sink_ref[h], jnp.float32))
        qs = jnp.concatenate(qs, axis=0)
        sink = jnp.concatenate(sinks, axis=0)
        s = _nt_dot(qs, kb[:, hk * 64:(hk + 1) * 64]) + bias_ref[hk]
        s = jnp.where(key_ok, s, NEG_INF)
        m = jnp.maximum(jnp.max(s, axis=-1, keepdims=True), sink)
        e = jnp.exp(s - m)
        denom = jnp.sum(e, axis=-1, keepdims=True) + jnp.exp(sink - m)
        o = jnp.dot(e.astype(jnp.bfloat16), vb[:, hk * 64:(hk + 1) * 64], preferred_element_type=jnp.float32)
        o = o / denom
        for g in range(A_GROUP):
            h = hk * A_GROUP + g
            o_ref[0, :, h * 64:(h + 1) * 64] = o[g * Q_TILE:(g + 1) * Q_TILE].astype(o_ref.dtype)


def _attn_a(proj3, bias, q_gain, k_gain, sinks):
    b, s, _ = proj3.shape
    return pl.pallas_call(
        _attn_a_kernel,
        grid=(b, s // Q_TILE),
        in_specs=[
            pl.BlockSpec((1, Q_TILE, 512), lambda bi, i: (bi, i, SEC_AQ // 512)),
            pl.BlockSpec((1, s, 256), lambda bi, i: (bi, 0, SEC_AKV // 256)),
            pl.BlockSpec((A_KV_HEADS, A_GROUP * Q_TILE, A_BAND), lambda bi, i: (0, 0, 0)),
            pl.BlockSpec((1, HEAD_DIM), lambda bi, i: (0, 0)),
            pl.BlockSpec((1, HEAD_DIM), lambda bi, i: (0, 0)),
            pl.BlockSpec(memory_space=pltpu.SMEM),
        ],
        out_specs=pl.BlockSpec((1, Q_TILE, 512), lambda bi, i: (bi, i, 0)),
        out_shape=jax.ShapeDtypeStruct((b, s, 512), jnp.bfloat16),
        scratch_shapes=[pltpu.VMEM((s + A_PAD, 128), jnp.bfloat16), pltpu.VMEM((s + A_PAD, 128), jnp.bfloat16)],
        compiler_params=pltpu.CompilerParams(dimension_semantics=("arbitrary", "arbitrary"),
                                             vmem_limit_bytes=VMEM_LIMIT),
        name="attn_a",
    )(proj3, proj3, bias, q_gain, k_gain, sinks)


def _latent_prep_kernel(cq_ref, ckv_ref, kpe_ref, qag_ref, kvag_ref, wq_ref, wk_ref, wv_ref,
                        qg_ref, kg_ref, cos_ref, sa_ref, sb_ref, qo_ref, ko_ref, vo_ref):
    lane = lax.broadcasted_iota(jnp.int32, (ROW_TILE, B_HEAD_PAD), 1)
    is_nope = lane < B_NOPE_DIM
    is_rope = jnp.logical_and(lane >= B_NOPE_DIM, lane < B_QK_DIM)
    cos = cos_ref[...]
    sa = sa_ref[...]
    sb = sb_ref[...]

    def split_norm(x, gain):
        sq = x * x
        nope = jnp.sum(jnp.where(is_nope, sq, 0.0), axis=-1, keepdims=True) * (1.0 / B_NOPE_DIM)
        rope = jnp.sum(jnp.where(is_rope, sq, 0.0), axis=-1, keepdims=True) * (1.0 / B_ROPE_DIM)
        r = jnp.where(is_nope, lax.rsqrt(nope + EPS), lax.rsqrt(rope + EPS))
        return x * r * gain

    def rotate(x):
        return x * cos + pltpu.roll(x, 112, 1) * sa + pltpu.roll(x, 16, 1) * sb

    cq = _rms(cq_ref[...].astype(jnp.float32), qag_ref[...]).astype(jnp.bfloat16)
    ckv = _rms(ckv_ref[...].astype(jnp.float32), kvag_ref[...]).astype(jnp.bfloat16)
    kpe = kpe_ref[...].astype(jnp.float32)
    kpe_n = rotate(split_norm(kpe, kg_ref[...]))
    kpe_n = jnp.where(is_rope, kpe_n, 0.0)
    vo_ref[...] = jnp.dot(ckv, wv_ref[...], preferred_element_type=jnp.float32).astype(vo_ref.dtype)
    qscale = B_QK_DIM ** -0.5
    for h in range(B_HEADS):
        cols = slice(h * B_HEAD_PAD, (h + 1) * B_HEAD_PAD)
        qh = jnp.dot(cq, wq_ref[:, cols], preferred_element_type=jnp.float32)
        qh = rotate(split_norm(qh, qg_ref[...])) * qscale
        qo_ref[:, cols] = qh.astype(qo_ref.dtype)
        kh = jnp.dot(ckv, wk_ref[:, cols], preferred_element_type=jnp.float32)
        kh = jnp.where(is_nope, split_norm(kh, kg_ref[...]), 0.0) + kpe_n
        ko_ref[:, cols] = kh.astype(ko_ref.dtype)


def _latent_prep(proj, q_a_gain, kv_a_gain, wq, wk, wv, q_gain, k_gain, cos, sa, sb, seq):
    t = proj.shape[0]
    pos_blocks = seq // ROW_TILE
    full = lambda shape: pl.BlockSpec(shape, lambda i: (0, 0))
    tab = pl.BlockSpec((ROW_TILE, B_HEAD_PAD), lambda i: (i % pos_blocks, 0))
    return pl.pallas_call(
        _latent_prep_kernel,
        grid=(t // ROW_TILE,),
        in_specs=[
            pl.BlockSpec((ROW_TILE, 256), lambda i: (i, SEC_BCQ // 256)),
            pl.BlockSpec((ROW_TILE, 128), lambda i: (i, SEC_BCKV // 128)),
            pl.BlockSpec((ROW_TILE, 128), lambda i: (i, SEC_BKPE // 128)),
            full((1, B_Q_RANK)), full((1, B_KV_RANK)),
            full((B_Q_RANK, B_HEADS * B_HEAD_PAD)), full((B_KV_RANK, B_HEADS * B_HEAD_PAD)),
            full((B_KV_RANK, B_HEADS * B_V_DIM)),
            full((1, B_HEAD_PAD)), full((1, B_HEAD_PAD)),
            tab, tab, tab,
        ],
        out_specs=[
            pl.BlockSpec((ROW_TILE, B_HEADS * B_HEAD_PAD), lambda i: (i, 0)),
            pl.BlockSpec((ROW_TILE, B_HEADS * B_HEAD_PAD), lambda i: (i, 0)),
            pl.BlockSpec((ROW_TILE, B_HEADS * B_V_DIM), lambda i: (i, 0)),
        ],
        out_shape=[
            jax.ShapeDtypeStruct((t, B_HEADS * B_HEAD_PAD), jnp.bfloat16),
            jax.ShapeDtypeStruct((t, B_HEADS * B_HEAD_PAD), jnp.bfloat16),
            jax.ShapeDtypeStruct((t, B_HEADS * B_V_DIM), jnp.bfloat16),
        ],
        compiler_params=pltpu.CompilerParams(dimension_semantics=("arbitrary",), vmem_limit_bytes=VMEM_LIMIT),
        name="latent_prep",
    )(proj, proj, proj, q_a_gain, kv_a_gain, wq, wk, wv, q_gain, k_gain, cos, sa, sb)


def _attn_b_kernel(q_ref, k_ref, v_ref, o_ref):
    i = pl.program_id(1)
    row_chunk = lax.broadcasted_iota(jnp.int32, (B_TILE, B_TILE), 0) // CHUNK
    col_chunk = lax.broadcasted_iota(jnp.int32, (B_TILE, B_TILE), 1) // CHUNK
    diag_ok = col_chunk <= row_chunk
    for h in range(B_HEADS):
        qh = q_ref[0, :, h * B_HEAD_PAD:(h + 1) * B_HEAD_PAD]

        def block(j, masked):
            rows = pl.ds(pl.multiple_of(j * B_TILE, B_TILE), B_TILE)
            s = _nt_dot(qh, k_ref[0, rows, h * B_HEAD_PAD:(h + 1) * B_HEAD_PAD])
            if masked:
                s = jnp.where(diag_ok, s, NEG_INF)
            return s, v_ref[0, rows, h * B_V_DIM:(h + 1) * B_V_DIM]

        def update(carry, s, v):
            m, l, acc = carry
            m_new = jnp.maximum(m, jnp.max(s, axis=-1, keepdims=True))
            alpha = jnp.exp(m - m_new)
            e = jnp.exp(s - m_new)
            l = alpha * l + jnp.sum(e, axis=-1, keepdims=True)
            acc = alpha * acc + jnp.dot(e.astype(jnp.bfloat16), v, preferred_element_type=jnp.float32)
            return m_new, l, acc

        def body(j, carry):
            s, v = block(j, False)
            return update(carry, s, v)

        init = (jnp.full((B_TILE, 1), NEG_INF, jnp.float32), jnp.zeros((B_TILE, 1), jnp.float32),
                jnp.zeros((B_TILE, B_V_DIM), jnp.float32))
        carry = lax.fori_loop(0, i, body, init)
        s, v = block(i, True)
        _, l, acc = update(carry, s, v)
        o_ref[0, :, h * B_V_DIM:(h + 1) * B_V_DIM] = (acc / l).astype(o_ref.dtype)


def _attn_b(qb, kb, vb):
    b, s, _ = qb.shape
    return pl.pallas_call(
        _attn_b_kernel,
        grid=(b, s // B_TILE),
        in_specs=[
            pl.BlockSpec((1, B_TILE, B_HEADS * B_HEAD_PAD), lambda bi, i: (bi, i, 0)),
            pl.BlockSpec((1, s, B_HEADS * B_HEAD_PAD), lambda bi, i: (bi, 0, 0)),
            pl.BlockSpec((1, s, B_HEADS * B_V_DIM), lambda bi, i: (bi, 0, 0)),
        ],
        out_specs=pl.BlockSpec((1, B_TILE, B_HEADS * B_V_DIM), lambda bi, i: (bi, i, 0)),
        out_shape=jax.ShapeDtypeStruct((b, s, B_HEADS * B_V_DIM), jnp.bfloat16),
        compiler_params=pltpu.CompilerParams(dimension_semantics=("arbitrary", "arbitrary"),
                                             vmem_limit_bytes=VMEM_LIMIT),
        name="attn_b",
    )(qb, kb, vb)


def _attn_c_kernel(q_ref, k_ref, v_ref, bias_ref, qg_ref, kg_ref, o_ref, k_pad, v_pad):
    i = pl.program_id(1)
    seq = k_ref.shape[1]
    width = C_HEADS * HEAD_DIM

    @pl.when(i == 0)
    def _():
        k_pad[0:C_PAD, :] = jnp.zeros((C_PAD, width), k_pad.dtype)
        v_pad[0:C_PAD, :] = jnp.zeros((C_PAD, width), v_pad.dtype)

        def fill(r, carry):
            rows = pl.ds(pl.multiple_of(r * 256, 256), 256)
            dst = pl.ds(pl.multiple_of(r * 256 + C_PAD, 256), 256)
            k = k_ref[0, rows, :].astype(jnp.float32)
            kn = jnp.concatenate([_rms(k[:, h * 64:(h + 1) * 64], kg_ref[...]) for h in range(C_HEADS)], axis=1)
            k_pad[dst, :] = kn.astype(k_pad.dtype)
            v_pad[dst, :] = v_ref[0, rows, :]
            return carry

        lax.fori_loop(0, seq // 256, fill, 0)

    start = pl.multiple_of(i * Q_TILE, Q_TILE)
    kb = k_pad[pl.ds(start, C_BAND), :]
    vb = v_pad[pl.ds(start, C_BAND), :]
    q = q_ref[0].astype(jnp.float32)
    col = lax.broadcasted_iota(jnp.int32, (Q_TILE, C_BAND), 1)
    key_ok = col + (start - C_PAD) >= 0
    scale = HEAD_DIM ** -0.5
    for h in range(C_HEADS):
        hs = slice(h * 64, (h + 1) * 64)
        qh = (_rms(q[:, hs], qg_ref[...]) * scale).astype(jnp.bfloat16)
        s = _nt_dot(qh, kb[:, hs]) + bias_ref[h]
        s = jnp.where(key_ok, s, NEG_INF)
        m = jnp.max(s, axis=-1, keepdims=True)
        e = jnp.exp(s - m)
        denom = jnp.sum(e, axis=-1, keepdims=True)
        o = jnp.dot(e.astype(jnp.bfloat16), vb[:, hs], preferred_element_type=jnp.float32)
        o_ref[0, :, hs] = (o / denom).astype(o_ref.dtype)


def _attn_c(proj3, bias, q_gain, k_gain):
    b, s, _ = proj3.shape
    width = C_HEADS * HEAD_DIM
    return pl.pallas_call(
        _attn_c_kernel,
        grid=(b, s // Q_TILE),
        in_specs=[
            pl.BlockSpec((1, Q_TILE, width), lambda bi, i: (bi, i, SEC_CQ // width)),
            pl.BlockSpec((1, s, width), lambda bi, i: (bi, 0, SEC_CK // width)),
            pl.BlockSpec((1, s, width), lambda bi, i: (bi, 0, SEC_CV // width)),
            pl.BlockSpec((C_HEADS, Q_TILE, C_BAND), lambda bi, i: (0, 0, 0)),
            pl.BlockSpec((1, HEAD_DIM), lambda bi, i: (0, 0)),
            pl.BlockSpec((1, HEAD_DIM), lambda bi, i: (0, 0)),
        ],
        out_specs=pl.BlockSpec((1, Q_TILE, width), lambda bi, i: (bi, i, 0)),
        out_shape=jax.ShapeDtypeStruct((b, s, width), jnp.bfloat16),
        scratch_shapes=[pltpu.VMEM((s + C_PAD, width), jnp.bfloat16), pltpu.VMEM((s + C_PAD, width), jnp.bfloat16)],
        compiler_params=pltpu.CompilerParams(dimension_semantics=("arbitrary", "arbitrary"),
                                             vmem_limit_bytes=VMEM_LIMIT),
        name="attn_c",
    )(proj3, proj3, proj3, bias, q_gain, k_gain)


def _out_proj_kernel(oa_ref, ob_ref, oc_ref, x_ref, ga_ref, gb_ref, gc_ref, w_ref, gf_ref, wr_ref, br_ref,
                     xo_ref, h_ref, lg_ref):
    acc = x_ref[...]
    parts = ((oa_ref, ga_ref, 0, 512), (ob_ref, gb_ref, 512, 256), (oc_ref, gc_ref, 768, 256))
    for ref, g_ref, off, width in parts:
        o = _rms(ref[...].astype(jnp.float32), g_ref[...]).astype(jnp.bfloat16)
        acc = acc + jnp.dot(o, w_ref[off:off + width, :], preferred_element_type=jnp.float32)
    xo_ref[...] = acc
    h = _rms(acc, gf_ref[...])
    h_ref[...] = h.astype(h_ref.dtype)
    lg_ref[...] = jnp.dot(h, wr_ref[...], preferred_element_type=jnp.float32,
                          precision=lax.Precision.HIGHEST) + br_ref[...]


def _out_proj(oa, ob, oc, x2, ga, gb, gc, w, gf, wr, br):
    t = x2.shape[0]
    tile = 256
    row = lambda width: pl.BlockSpec((tile, width), lambda i: (i, 0))
    full = lambda shape: pl.BlockSpec(shape, lambda i: (0, 0))
    return pl.pallas_call(
        _out_proj_kernel,
        grid=(t // tile,),
        in_specs=[row(512), row(256), row(256), row(D_MODEL), full((1, 512)), full((1, 256)), full((1, 256)),
                  full((D_MODEL, D_MODEL)), full((1, D_MODEL)), full((D_MODEL, ROUTER_PAD)), full((1, ROUTER_PAD))],
        out_specs=[row(D_MODEL), row(D_MODEL), row(ROUTER_PAD)],
        out_shape=[jax.ShapeDtypeStruct((t, D_MODEL), jnp.float32),
                   jax.ShapeDtypeStruct((t, D_MODEL), jnp.bfloat16),
                   jax.ShapeDtypeStruct((t, ROUTER_PAD), jnp.float32)],
        compiler_params=pltpu.CompilerParams(dimension_semantics=("arbitrary",), vmem_limit_bytes=VMEM_LIMIT),
        name="out_proj",
    )(oa, ob, oc, x2, ga, gb, gc, w, gf, wr, br)


def _moe_ffn_kernel(blk_e_ref, n_active_ref, x_ref, wgu_ref, bgu_ref, wd_ref, bd_ref, y_ref, wgu_bf, wd_bf):
    i = pl.program_id(0)
    prev = blk_e_ref[jnp.maximum(i - 1, 0)]
    changed = jnp.logical_or(i == 0, blk_e_ref[i] != prev)

    @pl.when(changed)
    def _():
        for c in range(4):
            cols = slice(c * 512, (c + 1) * 512)
            wgu_bf[:, cols] = wgu_ref[0, :, cols].astype(jnp.bfloat16)
        for c in range(2):
            cols = slice(c * 512, (c + 1) * 512)
            wd_bf[:, cols] = wd_ref[0, :, cols].astype(jnp.bfloat16)

    @pl.when(i < n_active_ref[0])
    def _():
        x = x_ref[...]
        glu = jnp.dot(x, wgu_bf[:, 0:D_EXPERT], preferred_element_type=jnp.float32) + bgu_ref[0, :, 0:D_EXPERT]
        lin = jnp.dot(x, wgu_bf[:, D_EXPERT:], preferred_element_type=jnp.float32) + bgu_ref[0, :, D_EXPERT:]
        glu = jnp.minimum(glu, SWIGLU_LIMIT)
        lin = jnp.clip(lin, -SWIGLU_LIMIT, SWIGLU_LIMIT)
        act = glu * jax.nn.sigmoid(SWIGLU_ALPHA * glu) * (lin + 1.0)
        y = jnp.dot(act.astype(jnp.bfloat16), wd_bf[...], preferred_element_type=jnp.float32) + bd_ref[0]
        y_ref[...] = y

    @pl.when(i >= n_active_ref[0])
    def _():
        y_ref[...] = jnp.zeros_like(y_ref)


def _moe_ffn(blk_e, n_active, x_rows, w_gu, b_gu, w_down, b_down):
    n_rows = x_rows.shape[0]
    n_blocks = n_rows // EXPERT_ROWS
    grid_spec = pltpu.PrefetchScalarGridSpec(
        num_scalar_prefetch=2,
        grid=(n_blocks,),
        in_specs=[
            pl.BlockSpec((EXPERT_ROWS, D_MODEL), lambda i, be, na: (i, 0)),
            pl.BlockSpec((1, D_MODEL, 2 * D_EXPERT), lambda i, be, na: (be[i], 0, 0)),
            pl.BlockSpec((1, 1, 2 * D_EXPERT), lambda i, be, na: (be[i], 0, 0)),
            pl.BlockSpec((1, D_EXPERT, D_MODEL), lambda i, be, na: (be[i], 0, 0)),
            pl.BlockSpec((1, 1, D_MODEL), lambda i, be, na: (be[i], 0, 0)),
        ],
        out_specs=pl.BlockSpec((EXPERT_ROWS, D_MODEL), lambda i, be, na: (i, 0)),
        scratch_shapes=[pltpu.VMEM((D_MODEL, 2 * D_EXPERT), jnp.bfloat16), pltpu.VMEM((D_EXPERT, D_MODEL), jnp.bfloat16)],
    )
    return pl.pallas_call(
        _moe_ffn_kernel,
        grid_spec=grid_spec,
        out_shape=jax.ShapeDtypeStruct((n_rows, D_MODEL), jnp.float32),
        compiler_params=pltpu.CompilerParams(dimension_semantics=("arbitrary",), vmem_limit_bytes=VMEM_LIMIT),
        name="moe_ffn",
    )(blk_e, n_active, x_rows, w_gu, b_gu.reshape(N_EXPERTS, 1, -1), w_down, b_down.reshape(N_EXPERTS, 1, -1))


def _t5_bucket(rel):
    half = T5_BUCKETS // 2
    max_exact = half // 2
    ret = np.where(rel > 0, half, 0)
    n = np.abs(rel)
    nf = np.maximum(n, 1).astype(np.float32)
    large = max_exact + (np.log(nf / max_exact) / math.log(T5_MAX_DIST / max_exact)
                         * (half - max_exact)).astype(np.int32)
    large = np.minimum(large, half - 1)
    return ret + np.where(n < max_exact, n, large)


def _band_ok(n_prev, band):
    q_chunk = np.arange(Q_TILE)[:, None] // CHUNK + n_prev
    k_chunk = np.arange(band)[None, :] // CHUNK
    return (k_chunk <= q_chunk) & (k_chunk >= q_chunk - n_prev)


def _a_bias(t5_bias):
    rel = np.arange(A_BAND)[None, :] - A_PAD - np.arange(Q_TILE)[:, None]
    bias = t5_bias[_t5_bucket(rel)].transpose(2, 0, 1).astype(jnp.float32)
    bias = jnp.where(_band_ok(A_PREV_CHUNKS, A_BAND)[None], bias, NEG_INF)
    return bias.reshape(A_KV_HEADS, A_GROUP * Q_TILE, A_BAND)


def _c_bias(rel_table):
    dist = np.arange(Q_TILE)[:, None] + C_PAD - np.arange(C_BAND)[None, :]
    idx = np.clip(dist, -(CHUNK - 1), C_REL_CLIP) + (CHUNK - 1)
    bias = rel_table[idx].transpose(2, 0, 1).astype(jnp.float32)
    return jnp.where(_band_ok(C_PREV_CHUNKS, C_BAND)[None], bias, NEG_INF)


def _rope_tables(seq):
    half = B_ROPE_DIM // 2
    inv_freq = ROPE_THETA ** (-(jnp.arange(half, dtype=jnp.float32) / half))
    ang = jnp.arange(seq, dtype=jnp.float32)[:, None] * inv_freq[None, :]
    cos, sin = jnp.cos(ang), jnp.sin(ang)
    zeros = jnp.zeros((seq, B_NOPE_DIM), jnp.float32)
    tail = jnp.zeros((seq, B_HEAD_PAD - B_QK_DIM), jnp.float32)
    zh = jnp.zeros((seq, half), jnp.float32)
    cos_t = jnp.concatenate([zeros + 1.0, cos, cos, tail + 1.0], axis=1)
    sa = jnp.concatenate([zeros, -sin, zh, tail], axis=1)
    sb = jnp.concatenate([zeros, zh, sin, tail], axis=1)
    return cos_t, sa, sb


def _regroup_w_in(w):
    splits = np.cumsum((512, 128, 128, 256, 128, 32, 256, 256))
    aq, ak, av, bcq, bckv, bkpe, cq, ck, cv = jnp.split(w, splits.tolist(), axis=1)
    z = lambda n: jnp.zeros((w.shape[0], n), w.dtype)
    return jnp.concatenate([aq, ak, av, bcq, bckv, z(64), bkpe, z(32), cq, ck, cv], axis=1).astype(jnp.bfloat16)


def _pad_heads(w, head_dim, keep):
    rank = w.shape[0]
    w = w.reshape(rank, B_HEADS, head_dim)[:, :, :keep]
    w = jnp.pad(w, ((0, 0), (0, 0), (0, B_HEAD_PAD - keep)))
    return w.reshape(rank, B_HEADS * B_HEAD_PAD).astype(jnp.bfloat16)


def _pad_lanes(g, offset, width):
    return jnp.pad(g, (offset, width - offset - g.shape[0])).reshape(1, width)


def _route(logits, t):
    top_vals, top_idx = lax.top_k(logits, TOP_K)
    gates = jax.nn.softmax(top_vals, axis=-1)
    flat_e = top_idx.reshape(-1)
    tk = t * TOP_K
    onehot = (flat_e[:, None] == jnp.arange(N_EXPERTS, dtype=jnp.int32)[None, :]).astype(jnp.int32)
    csum = jnp.cumsum(onehot, axis=0)
    counts = csum[-1]
    rank = jnp.sum((csum - onehot) * onehot, axis=1)
    padded = (counts + EXPERT_ROWS - 1) // EXPERT_ROWS * EXPERT_ROWS
    pad_end = jnp.cumsum(padded)
    pad_start = pad_end - padded
    dest = pad_start[flat_e] + rank
    n_rows = tk + N_EXPERTS * EXPERT_ROWS
    n_blocks = n_rows // EXPERT_ROWS
    row_tok = jnp.full((n_rows,), t, jnp.int32).at[dest].set(jnp.arange(tk, dtype=jnp.int32) // TOP_K)
    blk_e = jnp.minimum(jnp.searchsorted(pad_end, jnp.arange(n_blocks, dtype=jnp.int32) * EXPERT_ROWS, side="right"),
                        N_EXPERTS - 1).astype(jnp.int32)
    n_active = (pad_end[-1] // EXPERT_ROWS).astype(jnp.int32).reshape(1)
    return gates, dest.reshape(t, TOP_K), row_tok, blk_e, n_active


def kernel(x, attn_norm, w_in, a_q_norm, a_k_norm, a_sinks, t5_bias, b_q_a_norm, b_w_q_b, b_kv_a_norm, b_w_kv_b,
           b_q_norm, b_k_norm, c_q_norm, c_k_norm, c_rel_bias, a_out_norm, b_out_norm, c_out_norm, w_out, ffn_norm,
           w_router, b_router, w_gate_up, b_gate_up, w_down, b_down):
    b, s, d = x.shape
    t = b * s
    depth = w_in.shape[0]
    a_bias = _a_bias(t5_bias)
    cos_t, sin_a, sin_b = _rope_tables(s)
    x2 = x.reshape(t, d)
    for l in range(depth):
        proj = _in_proj(x2, attn_norm[l].reshape(1, d), _regroup_w_in(w_in[l]))
        proj3 = proj.reshape(b, s, PROJ_WIDTH)
        o_a = _attn_a(proj3, a_bias, a_q_norm[l].reshape(1, -1), a_k_norm[l].reshape(1, -1), a_sinks[l])
        kv_w = b_w_kv_b[l].reshape(B_KV_RANK, B_HEADS, B_NOPE_DIM + B_V_DIM)
        wv = kv_w[:, :, B_NOPE_DIM:].reshape(B_KV_RANK, B_HEADS * B_V_DIM).astype(jnp.bfloat16)
        qb, kb, vb = _latent_prep(
            proj, b_q_a_norm[l].reshape(1, -1), b_kv_a_norm[l].reshape(1, -1),
            _pad_heads(b_w_q_b[l], B_QK_DIM, B_QK_DIM), _pad_heads(b_w_kv_b[l], B_NOPE_DIM + B_V_DIM, B_NOPE_DIM), wv,
            _pad_lanes(b_q_norm[l], 0, B_HEAD_PAD), _pad_lanes(b_k_norm[l], 0, B_HEAD_PAD), cos_t, sin_a, sin_b, s)
        o_b = _attn_b(qb.reshape(b, s, -1), kb.reshape(b, s, -1), vb.reshape(b, s, -1))
        o_c = _attn_c(proj3, _c_bias(c_rel_bias[l]), c_q_norm[l].reshape(1, -1), c_k_norm[l].reshape(1, -1))
        wr = jnp.pad(w_router[l], ((0, 0), (0, ROUTER_PAD - N_EXPERTS)))
        br = jnp.pad(b_router[l], (0, ROUTER_PAD - N_EXPERTS)).reshape(1, ROUTER_PAD)
        x2, h, logits = _out_proj(
            o_a.reshape(t, -1), o_b.reshape(t, -1), o_c.reshape(t, -1), x2,
            a_out_norm[l].reshape(1, -1), b_out_norm[l].reshape(1, -1), c_out_norm[l].reshape(1, -1),
            w_out[l].astype(jnp.bfloat16), ffn_norm[l].reshape(1, -1), wr, br)
        gates, dest, row_tok, blk_e, n_active = _route(logits[:, :N_EXPERTS], t)
        h_pad = jnp.concatenate([h, jnp.zeros((1, d), h.dtype)], axis=0)
        y_rows = _moe_ffn(blk_e, n_active, h_pad[row_tok], w_gate_up[l], b_gate_up[l], w_down[l], b_down[l])
        x2 = x2 + jnp.sum(y_rows[dest] * gates[:, :, None], axis=1)
    return x2.reshape(b, s, d)
```

```python
import functools
import math

import jax
import jax.numpy as jnp
import numpy as np
from jax import lax
from jax.experimental import pallas as pl
from jax.experimental.pallas import tpu as pltpu

D_MODEL = 1024
CHUNK = 64
HEAD_DIM = 64
EPS = 1e-6
NEG_INF = -1e30
A_Q_HEADS = 8
A_KV_HEADS = 2
A_GROUP = 4
A_PREV_CHUNKS = 2
T5_BUCKETS = 32
T5_MAX_DIST = 128
B_HEADS = 4
B_Q_RANK = 256
B_KV_RANK = 128
B_NOPE_DIM = 64
B_ROPE_DIM = 32
B_QK_DIM = 96
B_V_DIM = 64
ROPE_THETA = 10000.0
C_HEADS = 4
C_PREV_CHUNKS = 8
C_REL_CLIP = 256
N_EXPERTS = 32
TOP_K = 4
D_EXPERT = 1024
SWIGLU_LIMIT = 7.0
SWIGLU_ALPHA = 1.702

SEC_AQ = 0
SEC_AKV = 512
SEC_BCQ = 768
SEC_BCKV = 1024
SEC_BKPE = 1152
SEC_CQ = 1280
SEC_CK = 1536
SEC_CV = 1792
PROJ_WIDTH = 2048
B_HEAD_PAD = 128

Q_TILE = 128
A_BAND = Q_TILE + A_PREV_CHUNKS * CHUNK
C_BAND = Q_TILE + C_PREV_CHUNKS * CHUNK
A_PAD = A_PREV_CHUNKS * CHUNK
C_PAD = C_PREV_CHUNKS * CHUNK
B_TILE = 256
ROW_TILE = 512
EXPERT_ROWS = 256
ROUTER_PAD = 128
SLOT_PAD_TOKENS = EXPERT_ROWS // TOP_K
VMEM_LIMIT = 48 * 1024 * 1024


def _rms(x, gain):
    return x * lax.rsqrt(jnp.mean(x * x, axis=-1, keepdims=True) + EPS) * gain


def _nt_dot(a, b):
    return lax.dot_general(a, b, (((1,), (1,)), ((), ())), preferred_element_type=jnp.float32)


def _in_proj_kernel(x_ref, g_ref, w_ref, o_ref):
    h = _rms(x_ref[...], g_ref[...]).astype(jnp.bfloat16)
    for n in range(PROJ_WIDTH // 512):
        cols = slice(n * 512, (n + 1) * 512)
        o_ref[:, cols] = jnp.dot(h, w_ref[:, cols], preferred_element_type=jnp.float32).astype(o_ref.dtype)


def _in_proj(x2, gain, w):
    t = x2.shape[0]
    return pl.pallas_call(
        _in_proj_kernel,
        grid=(t // ROW_TILE,),
        in_specs=[
            pl.BlockSpec((ROW_TILE, D_MODEL), lambda i: (i, 0)),
            pl.BlockSpec((1, D_MODEL), lambda i: (0, 0)),
            pl.BlockSpec((D_MODEL, PROJ_WIDTH), lambda i: (0, 0)),
        ],
        out_specs=pl.BlockSpec((ROW_TILE, PROJ_WIDTH), lambda i: (i, 0)),
        out_shape=jax.ShapeDtypeStruct((t, PROJ_WIDTH), jnp.bfloat16),
        compiler_params=pltpu.CompilerParams(dimension_semantics=("arbitrary",), vmem_limit_bytes=VMEM_LIMIT),
        name="in_proj",
    )(x2, gain, w)


def _attn_a_kernel(q_ref, kv_ref, bias_ref, qg_ref, kg_ref, sink_ref, o_ref, k_pad, v_pad):
    i = pl.program_id(1)
    seq = kv_ref.shape[1]

    @pl.when(i == 0)
    def _():
        k_pad[0:A_PAD, :] = jnp.zeros((A_PAD, 128), k_pad.dtype)
        v_pad[0:A_PAD, :] = jnp.zeros((A_PAD, 128), v_pad.dtype)

        def fill(r, carry):
            rows = pl.ds(pl.multiple_of(r * 256, 256), 256)
            dst = pl.ds(pl.multiple_of(r * 256 + A_PAD, 128), 256)
            kv = kv_ref[0, rows, :]
            k = kv[:, 0:128].astype(jnp.float32)
            kn = jnp.concatenate([_rms(k[:, 0:64], kg_ref[...]), _rms(k[:, 64:128], kg_ref[...])], axis=1)
            k_pad[dst, :] = kn.astype(k_pad.dtype)
            v_pad[dst, :] = kv[:, 128:256]
            return carry

        lax.fori_loop(0, seq // 256, fill, 0)

    start = pl.multiple_of(i * Q_TILE, Q_TILE)
    kb = k_pad[pl.ds(start, A_BAND), :]
    vb = v_pad[pl.ds(start, A_BAND), :]
    q = q_ref[0].astype(jnp.float32)
    col = lax.broadcasted_iota(jnp.int32, (A_GROUP * Q_TILE, A_BAND), 1)
    key_ok = col + (start - A_PAD) >= 0
    scale = HEAD_DIM ** -0.5
    for hk in range(A_KV_HEADS):
        qs = []
        sinks = []
        for g in range(A_GROUP):
            h = hk * A_GROUP + g
            qh = _rms(q[:, h * 64:(h + 1) * 64], qg_ref[...]) * scale
            qs.append(qh.astype(jnp.bfloat16))
            sinks.append(jnp.full((Q_TILE, 1), sink_ref[h], jnp.float32))
        qs = jnp.concatenate(qs, axis=0)
        sink = jnp.concatenate(sinks, axis=0)
        s = _nt_dot(qs, kb[:, hk * 64:(hk + 1) * 64]) + bias_ref[hk]
        s = jnp.where(key_ok, s, NEG_INF)
        m = jnp.maximum(jnp.max(s, axis=-1, keepdims=True), sink)
        e = jnp.exp(s - m)
        denom = jnp.sum(e, axis=-1, keepdims=True) + jnp.exp(sink - m)
        o = jnp.dot(e.astype(jnp.bfloat16), vb[:, hk * 64:(hk + 1) * 64], preferred_element_type=jnp.float32)
        o = o / denom
        for g in range(A_GROUP):
            h = hk * A_GROUP + g
            o_ref[0, :, h * 64:(h + 1) * 64] = o[g * Q_TILE:(g + 1) * Q_TILE].astype(o_ref.dtype)


def _attn_a(proj3, bias, q_gain, k_gain, sinks):
    b, s, _ = proj3.shape
    return pl.pallas_call(
        _attn_a_kernel,
        grid=(b, s // Q_TILE),
        in_specs=[
            pl.BlockSpec((1, Q_TILE, 512), lambda bi, i: (bi, i, SEC_AQ // 512)),
            pl.BlockSpec((1, s, 256), lambda bi, i: (bi, 0, SEC_AKV // 256)),
            pl.BlockSpec((A_KV_HEADS, A_GROUP * Q_TILE, A_BAND), lambda bi, i: (0, 0, 0)),
            pl.BlockSpec((1, HEAD_DIM), lambda bi, i: (0, 0)),
            pl.BlockSpec((1, HEAD_DIM), lambda bi, i: (0, 0)),
            pl.BlockSpec(memory_space=pltpu.SMEM),
        ],
        out_specs=pl.BlockSpec((1, Q_TILE, 512), lambda bi, i: (bi, i, 0)),
        out_shape=jax.ShapeDtypeStruct((b, s, 512), jnp.bfloat16),
        scratch_shapes=[pltpu.VMEM((s + A_PAD, 128), jnp.bfloat16), pltpu.VMEM((s + A_PAD, 128), jnp.bfloat16)],
        compiler_params=pltpu.CompilerParams(dimension_semantics=("arbitrary", "arbitrary"),
                                             vmem_limit_bytes=VMEM_LIMIT),
        name="attn_a",
    )(proj3, proj3, bias, q_gain, k_gain, sinks)


def _latent_prep_kernel(cq_ref, ckv_ref, kpe_ref, qag_ref, kvag_ref, wq_ref, wk_ref, wv_ref,
                        qg_ref, kg_ref, cos_ref, sa_ref, sb_ref, qo_ref, ko_ref, vo_ref):
    lane = lax.broadcasted_iota(jnp.int32, (ROW_TILE, B_HEAD_PAD), 1)
    is_nope = lane < B_NOPE_DIM
    is_rope = jnp.logical_and(lane >= B_NOPE_DIM, lane < B_QK_DIM)
    cos = cos_ref[...]
    sa = sa_ref[...]
    sb = sb_ref[...]

    def split_norm(x, gain):
        sq = x * x
        nope = jnp.sum(jnp.where(is_nope, sq, 0.0), axis=-1, keepdims=True) * (1.0 / B_NOPE_DIM)
        rope = jnp.sum(jnp.where(is_rope, sq, 0.0), axis=-1, keepdims=True) * (1.0 / B_ROPE_DIM)
        r = jnp.where(is_nope, lax.rsqrt(nope + EPS), lax.rsqrt(rope + EPS))
        return x * r * gain

    def rotate(x):
        return x * cos + pltpu.roll(x, 112, 1) * sa + pltpu.roll(x, 16, 1) * sb

    cq = _rms(cq_ref[...].astype(jnp.float32), qag_ref[...]).astype(jnp.bfloat16)
    ckv = _rms(ckv_ref[...].astype(jnp.float32), kvag_ref[...]).astype(jnp.bfloat16)
    kpe = kpe_ref[...].astype(jnp.float32)
    kpe_n = rotate(split_norm(kpe, kg_ref[...]))
    kpe_n = jnp.where(is_rope, kpe_n, 0.0)
    vo_ref[...] = jnp.dot(ckv, wv_ref[...], preferred_element_type=jnp.float32).astype(vo_ref.dtype)
    qscale = B_QK_DIM ** -0.5
    for h in range(B_HEADS):
        cols = slice(h * B_HEAD_PAD, (h + 1) * B_HEAD_PAD)
        qh = jnp.dot(cq, wq_ref[:, cols], preferred_element_type=jnp.float32)
        qh = rotate(split_norm(qh, qg_ref[...])) * qscale
        qo_ref[:, cols] = qh.astype(qo_ref.dtype)
        kh = jnp.dot(ckv, wk_ref[:, cols], preferred_element_type=jnp.float32)
        kh = jnp.where(is_nope, split_norm(kh, kg_ref[...]), 0.0) + kpe_n
        ko_ref[:, cols] = kh.astype(ko_ref.dtype)


def _latent_prep(proj, q_a_gain, kv_a_gain, wq, wk, wv, q_gain, k_gain, cos, sa, sb, seq):
    t = proj.shape[0]
    pos_blocks = seq // ROW_TILE
    full = lambda shape: pl.BlockSpec(shape, lambda i: (0, 0))
    tab = pl.BlockSpec((ROW_TILE, B_HEAD_PAD), lambda i: (i % pos_blocks, 0))
    return pl.pallas_call(
        _latent_prep_kernel,
        grid=(t // ROW_TILE,),
        in_specs=[
            pl.BlockSpec((ROW_TILE, 256), lambda i: (i, SEC_BCQ // 256)),
            pl.BlockSpec((ROW_TILE, 128), lambda i: (i, SEC_BCKV // 128)),
            pl.BlockSpec((ROW_TILE, 128), lambda i: (i, SEC_BKPE // 128)),
            full((1, B_Q_RANK)), full((1, B_KV_RANK)),
            full((B_Q_RANK, B_HEADS * B_HEAD_PAD)), full((B_KV_RANK, B_HEADS * B_HEAD_PAD)),
            full((B_KV_RANK, B_HEADS * B_V_DIM)),
            full((1, B_HEAD_PAD)), full((1, B_HEAD_PAD)),
            tab, tab, tab,
        ],
        out_specs=[
            pl.BlockSpec((ROW_TILE, B_HEADS * B_HEAD_PAD), lambda i: (i, 0)),
            pl.BlockSpec((ROW_TILE, B_HEADS * B_HEAD_PAD), lambda i: (i, 0)),
            pl.BlockSpec((ROW_TILE, B_HEADS * B_V_DIM), lambda i: (i, 0)),
        ],
        out_shape=[
            jax.ShapeDtypeStruct((t, B_HEADS * B_HEAD_PAD), jnp.bfloat16),
            jax.ShapeDtypeStruct((t, B_HEADS * B_HEAD_PAD), jnp.bfloat16),
            jax.ShapeDtypeStruct((t, B_HEADS * B_V_DIM), jnp.bfloat16),
        ],
        compiler_params=pltpu.CompilerParams(dimension_semantics=("arbitrary",), vmem_limit_bytes=VMEM_LIMIT),
        name="latent_prep",
    )(proj, proj, proj, q_a_gain, kv_a_gain, wq, wk, wv, q_gain, k_gain, cos, sa, sb)


def _attn_b_kernel(q_ref, k_ref, v_ref, o_ref):
    i = pl.program_id(1)
    row_chunk = lax.broadcasted_iota(jnp.int32, (B_TILE, B_TILE), 0) // CHUNK
    col_chunk = lax.broadcasted_iota(jnp.int32, (B_TILE, B_TILE), 1) // CHUNK
    diag_ok = col_chunk <= row_chunk
    for h in range(B_HEADS):
        qh = q_ref[0, :, h * B_HEAD_PAD:(h + 1) * B_HEAD_PAD]

        def block(j, masked):
            rows = pl.ds(pl.multiple_of(j * B_TILE, B_TILE), B_TILE)
            s = _nt_dot(qh, k_ref[0, rows, h * B_HEAD_PAD:(h + 1) * B_HEAD_PAD])
            if masked:
                s = jnp.where(diag_ok, s, NEG_INF)
            return s, v_ref[0, rows, h * B_V_DIM:(h + 1) * B_V_DIM]

        def update(carry, s, v):
            m, l, acc = carry
            m_new = jnp.maximum(m, jnp.max(s, axis=-1, keepdims=True))
            alpha = jnp.exp(m - m_new)
            e = jnp.exp(s - m_new)
            l = alpha * l + jnp.sum(e, axis=-1, keepdims=True)
            acc = alpha * acc + jnp.dot(e.astype(jnp.bfloat16), v, preferred_element_type=jnp.float32)
            return m_new, l, acc

        def body(j, carry):
            s, v = block(j, False)
            return update(carry, s, v)

        init = (jnp.full((B_TILE, 1), NEG_INF, jnp.float32), jnp.zeros((B_TILE, 1), jnp.float32),
                jnp.zeros((B_TILE, B_V_DIM), jnp.float32))
        carry = lax.fori_loop(0, i, body, init)
        s, v = block(i, True)
        _, l, acc = update(carry, s, v)
        o_ref[0, :, h * B_V_DIM:(h + 1) * B_V_DIM] = (acc / l).astype(o_ref.dtype)


def _attn_b(qb, kb, vb):
    b, s, _ = qb.shape
    return pl.pallas_call(
        _attn_b_kernel,
        grid=(b, s // B_TILE),
        in_specs=[
            pl.BlockSpec((1, B_TILE, B_HEADS * B_HEAD_PAD), lambda bi, i: (bi, i, 0)),
            pl.BlockSpec((1, s, B_HEADS * B_HEAD_PAD), lambda bi, i: (bi, 0, 0)),
            pl.BlockSpec((1, s, B_HEADS * B_V_DIM), lambda bi, i: (bi, 0, 0)),
        ],
        out_specs=pl.BlockSpec((1, B_TILE, B_HEADS * B_V_DIM), lambda bi, i: (bi, i, 0)),
        out_shape=jax.ShapeDtypeStruct((b, s, B_HEADS * B_V_DIM), jnp.bfloat16),
        compiler_params=pltpu.CompilerParams(dimension_semantics=("arbitrary", "arbitrary"),
                                             vmem_limit_bytes=VMEM_LIMIT),
        name="attn_b",
    )(qb, kb, vb)


def _attn_c_kernel(q_ref, k_ref, v_ref, bias_ref, qg_ref, kg_ref, o_ref, k_pad, v_pad):
    i = pl.program_id(1)
    seq = k_ref.shape[1]
    width = C_HEADS * HEAD_DIM

    @pl.when(i == 0)
    def _():
        k_pad[0:C_PAD, :] = jnp.zeros((C_PAD, width), k_pad.dtype)
        v_pad[0:C_PAD, :] = jnp.zeros((C_PAD, width), v_pad.dtype)

        def fill(r, carry):
            rows = pl.ds(pl.multiple_of(r * 256, 256), 256)
            dst = pl.ds(pl.multiple_of(r * 256 + C_PAD, 256), 256)
            k = k_ref[0, rows, :].astype(jnp.float32)
            kn = jnp.concatenate([_rms(k[:, h * 64:(h + 1) * 64], kg_ref[...]) for h in range(C_HEADS)], axis=1)
            k_pad[dst, :] = kn.astype(k_pad.dtype)
            v_pad[dst, :] = v_ref[0, rows, :]
            return carry

        lax.fori_loop(0, seq // 256, fill, 0)

    start = pl.multiple_of(i * Q_TILE, Q_TILE)
    kb = k_pad[pl.ds(start, C_BAND), :]
    vb = v_pad[pl.ds(start, C_BAND), :]
    q = q_ref[0].astype(jnp.float32)
    col = lax.broadcasted_iota(jnp.int32, (Q_TILE, C_BAND), 1)
    key_ok = col + (start - C_PAD) >= 0
    scale = HEAD_DIM ** -0.5
    for h in range(C_HEADS):
        hs = slice(h * 64, (h + 1) * 64)
        qh = (_rms(q[:, hs], qg_ref[...]) * scale).astype(jnp.bfloat16)
        s = _nt_dot(qh, kb[:, hs]) + bias_ref[h]
        s = jnp.where(key_ok, s, NEG_INF)
        m = jnp.max(s, axis=-1, keepdims=True)
        e = jnp.exp(s - m)
        denom = jnp.sum(e, axis=-1, keepdims=True)
        o = jnp.dot(e.astype(jnp.bfloat16), vb[:, hs], preferred_element_type=jnp.float32)
        o_ref[0, :, hs] = (o / denom).astype(o_ref.dtype)


def _attn_c(proj3, bias, q_gain, k_gain):
    b, s, _ = proj3.shape
    width = C_HEADS * HEAD_DIM
    return pl.pallas_call(
        _attn_c_kernel,
        grid=(b, s // Q_TILE),
        in_specs=[
            pl.BlockSpec((1, Q_TILE, width), lambda bi, i: (bi, i, SEC_CQ // width)),
            pl.BlockSpec((1, s, width), lambda bi, i: (bi, 0, SEC_CK // width)),
            pl.BlockSpec((1, s, width), lambda bi, i: (bi, 0, SEC_CV // width)),
            pl.BlockSpec((C_HEADS, Q_TILE, C_BAND), lambda bi, i: (0, 0, 0)),
            pl.BlockSpec((1, HEAD_DIM), lambda bi, i: (0, 0)),
            pl.BlockSpec((1, HEAD_DIM), lambda bi, i: (0, 0)),
        ],
        out_specs=pl.BlockSpec((1, Q_TILE, width), lambda bi, i: (bi, i, 0)),
        out_shape=jax.ShapeDtypeStruct((b, s, width), jnp.bfloat16),
        scratch_shapes=[pltpu.VMEM((s + C_PAD, width), jnp.bfloat16), pltpu.VMEM((s + C_PAD, width), jnp.bfloat16)],
        compiler_params=pltpu.CompilerParams(dimension_semantics=("arbitrary", "arbitrary"),
                                             vmem_limit_bytes=VMEM_LIMIT),
        name="attn_c",
    )(proj3, proj3, proj3, bias, q_gain, k_gain)


def _out_proj_kernel(oa_ref, ob_ref, oc_ref, x_ref, ga_ref, gb_ref, gc_ref, w_ref, gf_ref, wr_ref, br_ref,
                     xo_ref, h_ref, ti_ref, tg_ref):
    acc = x_ref[...]
    parts = ((oa_ref, ga_ref, 0, 512), (ob_ref, gb_ref, 512, 256), (oc_ref, gc_ref, 768, 256))
    for ref, g_ref, off, width in parts:
        o = _rms(ref[...].astype(jnp.float32), g_ref[...]).astype(jnp.bfloat16)
        acc = acc + jnp.dot(o, w_ref[off:off + width, :], preferred_element_type=jnp.float32)
    xo_ref[...] = acc
    h = _rms(acc, gf_ref[...])
    h_ref[...] = h
    logits = jnp.dot(h, wr_ref[...], preferred_element_type=jnp.float32,
                     precision=lax.Precision.HIGHEST) + br_ref[...]
    lane = lax.broadcasted_iota(jnp.int32, logits.shape, 1)
    work = jnp.where(lane < N_EXPERTS, logits, -jnp.inf)
    idx_out = jnp.zeros(logits.shape, jnp.int32)
    gate_out = jnp.zeros(logits.shape, jnp.float32)
    denom = jnp.zeros((logits.shape[0], 1), jnp.float32)
    top = None
    for k in range(TOP_K):
        m = jnp.max(work, axis=-1, keepdims=True)
        idx = jnp.min(jnp.where(work == m, lane, ROUTER_PAD), axis=-1, keepdims=True)
        work = jnp.where(lane == idx, -jnp.inf, work)
        top = m if top is None else top
        e = jnp.exp(m - top)
        denom = denom + e
        idx_out = jnp.where(lane == k, idx, idx_out)
        gate_out = jnp.where(lane == k, e, gate_out)
    ti_ref[...] = idx_out
    tg_ref[...] = gate_out / denom


def _out_proj(oa, ob, oc, x2, ga, gb, gc, w, gf, wr, br):
    t = x2.shape[0]
    tile = 256
    row = lambda width: pl.BlockSpec((tile, width), lambda i: (i, 0))
    full = lambda shape: pl.BlockSpec(shape, lambda i: (0, 0))
    return pl.pallas_call(
        _out_proj_kernel,
        grid=(t // tile,),
        in_specs=[row(512), row(256), row(256), row(D_MODEL), full((1, 512)), full((1, 256)), full((1, 256)),
                  full((D_MODEL, D_MODEL)), full((1, D_MODEL)), full((D_MODEL, ROUTER_PAD)), full((1, ROUTER_PAD))],
        out_specs=[row(D_MODEL), row(D_MODEL), row(ROUTER_PAD), row(ROUTER_PAD)],
        out_shape=[jax.ShapeDtypeStruct((t, D_MODEL), jnp.float32),
                   jax.ShapeDtypeStruct((t, D_MODEL), jnp.float32),
                   jax.ShapeDtypeStruct((t, ROUTER_PAD), jnp.int32),
                   jax.ShapeDtypeStruct((t, ROUTER_PAD), jnp.float32)],
        compiler_params=pltpu.CompilerParams(dimension_semantics=("arbitrary",), vmem_limit_bytes=VMEM_LIMIT),
        name="out_proj",
    )(oa, ob, oc, x2, ga, gb, gc, w, gf, wr, br)


def _moe_ffn_kernel(blk_e_ref, n_active_ref, tbl_hbm, h_hbm, wgu_ref, bgu_ref, wd_ref, bd_ref, y_hbm,
                    xbuf, ybuf, idx_smem, wgu_bf, wd_bf, gsem, ssem, isem):
    i = pl.program_id(0)
    nb = pl.num_programs(0)
    n_active = n_active_ref[0]
    slot = i % 2
    nslot = 1 - slot

    def idx_copy(b, s):
        return pltpu.make_async_copy(tbl_hbm.at[b], idx_smem.at[s], isem.at[s])

    def issue_gather(s):
        def body(r, carry):
            tok = idx_smem[s, 0, r]
            pltpu.make_async_copy(h_hbm.at[pl.ds(tok, 1)], xbuf.at[s, pl.ds(r, 1)], gsem.at[s]).start()
            return carry
        lax.fori_loop(0, EXPERT_ROWS, body, 0, unroll=8)

    def issue_scatter(s):
        def body(r, carry):
            dst = idx_smem[s, 1, r]
            pltpu.make_async_copy(ybuf.at[s, pl.ds(r, 1)], y_hbm.at[pl.ds(dst, 1)], ssem.at[s]).start()
            return carry
        lax.fori_loop(0, EXPERT_ROWS, body, 0, unroll=8)

    def wait_gather(s):
        pltpu.make_async_copy(h_hbm.at[pl.ds(0, EXPERT_ROWS)], xbuf.at[s], gsem.at[s]).wait()

    def wait_scatter(s):
        pltpu.make_async_copy(ybuf.at[s], y_hbm.at[pl.ds(0, EXPERT_ROWS)], ssem.at[s]).wait()

    @pl.when(i == 0)
    def _():
        ybuf[0] = jnp.zeros((EXPERT_ROWS, D_MODEL), jnp.float32)
        spare = pltpu.make_async_copy(ybuf.at[0], y_hbm.at[pl.ds(y_hbm.shape[0] - EXPERT_ROWS, EXPERT_ROWS)],
                                      ssem.at[0])
        spare.start()
        spare.wait()
        idx_copy(0, 0).start()
        idx_copy(0, 0).wait()

        @pl.when(n_active > 0)
        def _():
            issue_gather(0)

        @pl.when(nb > 1)
        def _():
            idx_copy(1, 1).start()

    @pl.when(i + 1 < nb)
    def _():
        idx_copy(i + 1, nslot).wait()

        @pl.when(i + 1 < n_active)
        def _():
            issue_gather(nslot)

    prev = blk_e_ref[jnp.maximum(i - 1, 0)]
    changed = jnp.logical_or(i == 0, blk_e_ref[i] != prev)

    @pl.when(changed)
    def _():
        for c in range(4):
            cols = slice(c * 512, (c + 1) * 512)
            wgu_bf[:, cols] = wgu_ref[0, 0, :, cols].astype(jnp.bfloat16)
        for c in range(2):
            cols = slice(c * 512, (c + 1) * 512)
            wd_bf[:, cols] = wd_ref[0, 0, :, cols].astype(jnp.bfloat16)

    @pl.when(jnp.logical_and(i >= 2, i - 2 < n_active))
    def _():
        wait_scatter(slot)

    @pl.when(i < n_active)
    def _():
        wait_gather(slot)
        x = xbuf[slot].astype(jnp.bfloat16)
        glu = jnp.dot(x, wgu_bf[:, 0:D_EXPERT], preferred_element_type=jnp.float32) + bgu_ref[0, 0, :, 0:D_EXPERT]
        lin = jnp.dot(x, wgu_bf[:, D_EXPERT:], preferred_element_type=jnp.float32) + bgu_ref[0, 0, :, D_EXPERT:]
        glu = jnp.minimum(glu, SWIGLU_LIMIT)
        lin = jnp.clip(lin, -SWIGLU_LIMIT, SWIGLU_LIMIT)
        act = glu * jax.nn.sigmoid(SWIGLU_ALPHA * glu) * (lin + 1.0)
        ybuf[slot] = jnp.dot(act.astype(jnp.bfloat16), wd_bf[...], preferred_element_type=jnp.float32) + bd_ref[0, 0]
        issue_scatter(slot)

    @pl.when(i + 2 < nb)
    def _():
        idx_copy(i + 2, slot).start()

    @pl.when(i == nb - 1)
    def _():
        @pl.when(i < n_active)
        def _():
            wait_scatter(slot)

        @pl.when(jnp.logical_and(i >= 1, i - 1 < n_active))
        def _():
            wait_scatter(nslot)


def _moe_ffn(layer, blk_e, n_active, tbl, h, w_gu, b_gu, w_down, b_down):
    t = h.shape[0]
    n_blocks = tbl.shape[0]
    depth = w_gu.shape[0]
    grid_spec = pltpu.PrefetchScalarGridSpec(
        num_scalar_prefetch=2,
        grid=(n_blocks,),
        in_specs=[
            pl.BlockSpec(memory_space=pl.ANY),
            pl.BlockSpec(memory_space=pl.ANY),
            pl.BlockSpec((1, 1, D_MODEL, 2 * D_EXPERT), lambda i, be, na: (layer, be[i], 0, 0)),
            pl.BlockSpec((1, 1, 1, 2 * D_EXPERT), lambda i, be, na: (layer, be[i], 0, 0)),
            pl.BlockSpec((1, 1, D_EXPERT, D_MODEL), lambda i, be, na: (layer, be[i], 0, 0)),
            pl.BlockSpec((1, 1, 1, D_MODEL), lambda i, be, na: (layer, be[i], 0, 0)),
        ],
        out_specs=pl.BlockSpec(memory_space=pl.ANY),
        scratch_shapes=[
            pltpu.VMEM((2, EXPERT_ROWS, D_MODEL), jnp.float32),
            pltpu.VMEM((2, EXPERT_ROWS, D_MODEL), jnp.float32),
            pltpu.SMEM((2, 2, EXPERT_ROWS), jnp.int32),
            pltpu.VMEM((D_MODEL, 2 * D_EXPERT), jnp.bfloat16),
            pltpu.VMEM((D_EXPERT, D_MODEL), jnp.bfloat16),
            pltpu.SemaphoreType.DMA((2,)),
            pltpu.SemaphoreType.DMA((2,)),
            pltpu.SemaphoreType.DMA((2,)),
        ],
    )
    y = pl.pallas_call(
        _moe_ffn_kernel,
        grid_spec=grid_spec,
        out_shape=jax.ShapeDtypeStruct(((t + SLOT_PAD_TOKENS) * TOP_K, D_MODEL), jnp.float32),
        compiler_params=pltpu.CompilerParams(dimension_semantics=("arbitrary",), vmem_limit_bytes=VMEM_LIMIT),
        name="moe_ffn",
    )(blk_e, n_active, tbl, h, w_gu, b_gu.reshape(depth, N_EXPERTS, 1, -1), w_down,
      b_down.reshape(depth, N_EXPERTS, 1, -1))
    return y.reshape(t + SLOT_PAD_TOKENS, TOP_K * D_MODEL)


def _combine(x_ref, y_ref, g_ref):
    x = x_ref[...]
    g = g_ref[...]
    for k in range(TOP_K):
        x = x + g[:, k:k + 1] * y_ref[:, k * D_MODEL:(k + 1) * D_MODEL]
    return x


def _combine_kernel(x_ref, y_ref, g_ref, xo_ref):
    xo_ref[...] = _combine(x_ref, y_ref, g_ref)


def _combine_in_proj_kernel(x_ref, y_ref, g_ref, gain_ref, w_ref, xo_ref, o_ref):
    x = _combine(x_ref, y_ref, g_ref)
    xo_ref[...] = x
    h = _rms(x, gain_ref[...]).astype(jnp.bfloat16)
    for n in range(PROJ_WIDTH // 512):
        cols = slice(n * 512, (n + 1) * 512)
        o_ref[:, cols] = jnp.dot(h, w_ref[:, cols], preferred_element_type=jnp.float32).astype(o_ref.dtype)


def _combine_call(x2, y, gates, gain=None, w=None):
    t = x2.shape[0]
    tile = 256
    row = lambda width: pl.BlockSpec((tile, width), lambda i: (i, 0))
    full = lambda shape: pl.BlockSpec(shape, lambda i: (0, 0))
    in_specs = [row(D_MODEL), row(TOP_K * D_MODEL), row(ROUTER_PAD)]
    params = pltpu.CompilerParams(dimension_semantics=("arbitrary",), vmem_limit_bytes=VMEM_LIMIT)
    if w is None:
        return pl.pallas_call(
            _combine_kernel, grid=(t // tile,), in_specs=in_specs, out_specs=row(D_MODEL),
            out_shape=jax.ShapeDtypeStruct((t, D_MODEL), jnp.float32), compiler_params=params, name="combine",
        )(x2, y, gates)
    return pl.pallas_call(
        _combine_in_proj_kernel, grid=(t // tile,),
        in_specs=in_specs + [full((1, D_MODEL)), full((D_MODEL, PROJ_WIDTH))],
        out_specs=[row(D_MODEL), row(PROJ_WIDTH)],
        out_shape=[jax.ShapeDtypeStruct((t, D_MODEL), jnp.float32),
                   jax.ShapeDtypeStruct((t, PROJ_WIDTH), jnp.bfloat16)],
        compiler_params=params, name="combine_in_proj",
    )(x2, y, gates, gain, w)


def _t5_bucket(rel):
    half = T5_BUCKETS // 2
    max_exact = half // 2
    ret = np.where(rel > 0, half, 0)
    n = np.abs(rel)
    nf = np.maximum(n, 1).astype(np.float32)
    large = max_exact + (np.log(nf / max_exact) / math.log(T5_MAX_DIST / max_exact)
                         * (half - max_exact)).astype(np.int32)
    large = np.minimum(large, half - 1)
    return ret + np.where(n < max_exact, n, large)


def _band_ok(n_prev, band):
    q_chunk = np.arange(Q_TILE)[:, None] // CHUNK + n_prev
    k_chunk = np.arange(band)[None, :] // CHUNK
    return (k_chunk <= q_chunk) & (k_chunk >= q_chunk - n_prev)


def _a_bias(t5_bias):
    rel = np.arange(A_BAND)[None, :] - A_PAD - np.arange(Q_TILE)[:, None]
    bias = t5_bias[_t5_bucket(rel)].transpose(2, 0, 1).astype(jnp.float32)
    bias = jnp.where(_band_ok(A_PREV_CHUNKS, A_BAND)[None], bias, NEG_INF)
    return bias.reshape(A_KV_HEADS, A_GROUP * Q_TILE, A_BAND)


def _c_bias(rel_table):
    dist = np.arange(Q_TILE)[:, None] + C_PAD - np.arange(C_BAND)[None, :]
    idx = np.clip(dist, -(CHUNK - 1), C_REL_CLIP) + (CHUNK - 1)
    bias = rel_table[idx].transpose(2, 0, 1).astype(jnp.float32)
    return jnp.where(_band_ok(C_PREV_CHUNKS, C_BAND)[None], bias, NEG_INF)


def _rope_tables(seq):
    half = B_ROPE_DIM // 2
    inv_freq = ROPE_THETA ** (-(jnp.arange(half, dtype=jnp.float32) / half))
    ang = jnp.arange(seq, dtype=jnp.float32)[:, None] * inv_freq[None, :]
    cos, sin = jnp.cos(ang), jnp.sin(ang)
    zeros = jnp.zeros((seq, B_NOPE_DIM), jnp.float32)
    tail = jnp.zeros((seq, B_HEAD_PAD - B_QK_DIM), jnp.float32)
    zh = jnp.zeros((seq, half), jnp.float32)
    cos_t = jnp.concatenate([zeros + 1.0, cos, cos, tail + 1.0], axis=1)
    sa = jnp.concatenate([zeros, -sin, zh, tail], axis=1)
    sb = jnp.concatenate([zeros, zh, sin, tail], axis=1)
    return cos_t, sa, sb


def _regroup_w_in(w):
    splits = np.cumsum((512, 128, 128, 256, 128, 32, 256, 256))
    aq, ak, av, bcq, bckv, bkpe, cq, ck, cv = jnp.split(w, splits.tolist(), axis=1)
    z = lambda n: jnp.zeros((w.shape[0], n), w.dtype)
    return jnp.concatenate([aq, ak, av, bcq, bckv, z(64), bkpe, z(32), cq, ck, cv], axis=1).astype(jnp.bfloat16)


def _pad_heads(w, head_dim, keep):
    rank = w.shape[0]
    w = w.reshape(rank, B_HEADS, head_dim)[:, :, :keep]
    w = jnp.pad(w, ((0, 0), (0, 0), (0, B_HEAD_PAD - keep)))
    return w.reshape(rank, B_HEADS * B_HEAD_PAD).astype(jnp.bfloat16)


def _pad_lanes(g, offset, width):
    return jnp.pad(g, (offset, width - offset - g.shape[0])).reshape(1, width)


def _route(top_idx, t):
    tk = t * TOP_K
    flat_e = top_idx.reshape(-1)
    order = jnp.argsort(flat_e, stable=True).astype(jnp.int32)
    bounds = jnp.searchsorted(flat_e[order], jnp.arange(N_EXPERTS + 1, dtype=jnp.int32), side="left").astype(jnp.int32)
    start, counts = bounds[:-1], bounds[1:] - bounds[:-1]
    nblk = (counts + EXPERT_ROWS - 1) // EXPERT_ROWS
    blk_end = jnp.cumsum(nblk)
    blk_first = blk_end - nblk
    n_blocks = (tk + N_EXPERTS * (EXPERT_ROWS - 1) + EXPERT_ROWS - 1) // EXPERT_ROWS
    blk = jnp.arange(n_blocks, dtype=jnp.int32)
    blk_e = jnp.minimum(jnp.searchsorted(blk_end, blk, side="right"), N_EXPERTS - 1).astype(jnp.int32)
    j = blk - blk_first[blk_e]
    n_valid = jnp.clip(counts[blk_e] - j * EXPERT_ROWS, 0, EXPERT_ROWS)
    r = jnp.arange(EXPERT_ROWS, dtype=jnp.int32)[None, :]
    pos = (start[blk_e] + j * EXPERT_ROWS)[:, None] + r
    slot = order[jnp.clip(pos, 0, tk - 1)]
    valid = r < n_valid[:, None]
    src = jnp.where(valid, slot // TOP_K, 0)
    dst = jnp.where(valid, slot, tk + r)
    tbl = jnp.stack([src, dst], axis=1).astype(jnp.int32)
    n_active = blk_end[-1].astype(jnp.int32).reshape(1)
    return tbl, blk_e, n_active


def kernel(x, attn_norm, w_in, a_q_norm, a_k_norm, a_sinks, t5_bias, b_q_a_norm, b_w_q_b, b_kv_a_norm, b_w_kv_b,
           b_q_norm, b_k_norm, c_q_norm, c_k_norm, c_rel_bias, a_out_norm, b_out_norm, c_out_norm, w_out, ffn_norm,
           w_router, b_router, w_gate_up, b_gate_up, w_down, b_down):
    b, s, d = x.shape
    t = b * s
    depth = w_in.shape[0]
    a_bias = _a_bias(t5_bias)
    cos_t, sin_a, sin_b = _rope_tables(s)
    x2 = x.reshape(t, d)
    proj = _in_proj(x2, attn_norm[0].reshape(1, d), _regroup_w_in(w_in[0]))
    for l in range(depth):
        proj3 = proj.reshape(b, s, PROJ_WIDTH)
        o_a = _attn_a(proj3, a_bias, a_q_norm[l].reshape(1, -1), a_k_norm[l].reshape(1, -1), a_sinks[l])
        kv_w = b_w_kv_b[l].reshape(B_KV_RANK, B_HEADS, B_NOPE_DIM + B_V_DIM)
        wv = kv_w[:, :, B_NOPE_DIM:].reshape(B_KV_RANK, B_HEADS * B_V_DIM).astype(jnp.bfloat16)
        qb, kb, vb = _latent_prep(
            proj, b_q_a_norm[l].reshape(1, -1), b_kv_a_norm[l].reshape(1, -1),
            _pad_heads(b_w_q_b[l], B_QK_DIM, B_QK_DIM), _pad_heads(b_w_kv_b[l], B_NOPE_DIM + B_V_DIM, B_NOPE_DIM), wv,
            _pad_lanes(b_q_norm[l], 0, B_HEAD_PAD), _pad_lanes(b_k_norm[l], 0, B_HEAD_PAD), cos_t, sin_a, sin_b, s)
        o_b = _attn_b(qb.reshape(b, s, -1), kb.reshape(b, s, -1), vb.reshape(b, s, -1))
        o_c = _attn_c(proj3, _c_bias(c_rel_bias[l]), c_q_norm[l].reshape(1, -1), c_k_norm[l].reshape(1, -1))
        wr = jnp.pad(w_router[l], ((0, 0), (0, ROUTER_PAD - N_EXPERTS)))
        br = jnp.pad(b_router[l], (0, ROUTER_PAD - N_EXPERTS)).reshape(1, ROUTER_PAD)
        x2, h, top_idx, gates = _out_proj(
            o_a.reshape(t, -1), o_b.reshape(t, -1), o_c.reshape(t, -1), x2,
            a_out_norm[l].reshape(1, -1), b_out_norm[l].reshape(1, -1), c_out_norm[l].reshape(1, -1),
            w_out[l].astype(jnp.bfloat16), ffn_norm[l].reshape(1, -1), wr, br)
        tbl, blk_e, n_active = _route(top_idx[:, :TOP_K], t)
        y = _moe_ffn(l, blk_e, n_active, tbl, h, w_gate_up, b_gate_up, w_down, b_down)
        if l + 1 < depth:
            x2, proj = _combine_call(x2, y, gates, attn_norm[l + 1].reshape(1, d), _regroup_w_in(w_in[l + 1]))
        else:
            x2 = _combine_call(x2, y, gates)
    return x2.reshape(b, s, d)
```

```python
import functools
import math

import jax
import jax.numpy as jnp
import numpy as np
from jax import lax
from jax.experimental import pallas as pl
from jax.experimental.pallas import tpu as pltpu

D_MODEL = 1024
CHUNK = 64
HEAD_DIM = 64
EPS = 1e-6
NEG_INF = -1e30
A_Q_HEADS = 8
A_KV_HEADS = 2
A_GROUP = 4
A_PREV_CHUNKS = 2
T5_BUCKETS = 32
T5_MAX_DIST = 128
B_HEADS = 4
B_Q_RANK = 256
B_KV_RANK = 128
B_NOPE_DIM = 64
B_ROPE_DIM = 32
B_QK_DIM = 96
B_V_DIM = 64
ROPE_THETA = 10000.0
C_HEADS = 4
C_PREV_CHUNKS = 8
C_REL_CLIP = 256
N_EXPERTS = 32
TOP_K = 4
D_EXPERT = 1024
SWIGLU_LIMIT = 7.0
SWIGLU_ALPHA = 1.702

SEC_AQ = 0
SEC_AKV = 512
SEC_BCQ = 768
SEC_BCKV = 1024
SEC_BKPE = 1152
SEC_CQ = 1280
SEC_CK = 1536
SEC_CV = 1792
PROJ_WIDTH = 2048
B_HEAD_PAD = 128

Q_TILE = 128
A_BAND = Q_TILE + A_PREV_CHUNKS * CHUNK
C_BAND = Q_TILE + C_PREV_CHUNKS * CHUNK
A_PAD = A_PREV_CHUNKS * CHUNK
C_PAD = C_PREV_CHUNKS * CHUNK
B_TILE = 256
ROW_TILE = 512
EXPERT_ROWS = 256
ROUTER_PAD = 128
TOKEN_TILE = 256
LANE_TILES = D_MODEL // 128
VMEM_LIMIT = 48 * 1024 * 1024


def _rms(x, gain):
    return x * lax.rsqrt(jnp.mean(x * x, axis=-1, keepdims=True) + EPS) * gain


def _nt_dot(a, b):
    return lax.dot_general(a, b, (((1,), (1,)), ((), ())), preferred_element_type=jnp.float32)


def _in_proj_kernel(x_ref, g_ref, w_ref, o_ref):
    h = _rms(x_ref[...], g_ref[...]).astype(jnp.bfloat16)
    for n in range(PROJ_WIDTH // 512):
        cols = slice(n * 512, (n + 1) * 512)
        o_ref[:, cols] = jnp.dot(h, w_ref[:, cols], preferred_element_type=jnp.float32).astype(o_ref.dtype)


def _in_proj(x2, gain, w):
    t = x2.shape[0]
    return pl.pallas_call(
        _in_proj_kernel,
        grid=(t // ROW_TILE,),
        in_specs=[
            pl.BlockSpec((ROW_TILE, D_MODEL), lambda i: (i, 0)),
            pl.BlockSpec((1, D_MODEL), lambda i: (0, 0)),
            pl.BlockSpec((D_MODEL, PROJ_WIDTH), lambda i: (0, 0)),
        ],
        out_specs=pl.BlockSpec((ROW_TILE, PROJ_WIDTH), lambda i: (i, 0)),
        out_shape=jax.ShapeDtypeStruct((t, PROJ_WIDTH), jnp.bfloat16),
        compiler_params=pltpu.CompilerParams(dimension_semantics=("arbitrary",), vmem_limit_bytes=VMEM_LIMIT),
        name="in_proj",
    )(x2, gain, w)


def _attn_a_kernel(q_ref, kv_ref, bias_ref, qg_ref, kg_ref, sink_ref, o_ref, k_pad, v_pad):
    i = pl.program_id(1)
    seq = kv_ref.shape[1]

    @pl.when(i == 0)
    def _():
        k_pad[0:A_PAD, :] = jnp.zeros((A_PAD, 128), k_pad.dtype)
        v_pad[0:A_PAD, :] = jnp.zeros((A_PAD, 128), v_pad.dtype)

        def fill(r, carry):
            rows = pl.ds(pl.multiple_of(r * 256, 256), 256)
            dst = pl.ds(pl.multiple_of(r * 256 + A_PAD, 128), 256)
            kv = kv_ref[0, rows, :]
            k = kv[:, 0:128].astype(jnp.float32)
            kn = jnp.concatenate([_rms(k[:, 0:64], kg_ref[...]), _rms(k[:, 64:128], kg_ref[...])], axis=1)
            k_pad[dst, :] = kn.astype(k_pad.dtype)
            v_pad[dst, :] = kv[:, 128:256]
            return carry

        lax.fori_loop(0, seq // 256, fill, 0)

    start = pl.multiple_of(i * Q_TILE, Q_TILE)
    kb = k_pad[pl.ds(start, A_BAND), :]
    vb = v_pad[pl.ds(start, A_BAND), :]
    q = q_ref[0].astype(jnp.float32)
    col = lax.broadcasted_iota(jnp.int32, (A_GROUP * Q_TILE, A_BAND), 1)
    key_ok = col + (start - A_PAD) >= 0
    scale = HEAD_DIM ** -0.5
    for hk in range(A_KV_HEADS):
        qs = []
        sinks = []
        for g in range(A_GROUP):
            h = hk * A_GROUP + g
            qh = _rms(q[:, h * 64:(h + 1) * 64], qg_ref[...]) * scale
            qs.append(qh.astype(jnp.bfloat16))
            sinks.append(jnp.full((Q_TILE, 1), sink_ref[h], jnp.float32))
        qs = jnp.concatenate(qs, axis=0)
        sink = jnp.concatenate(sinks, axis=0)
        s = _nt_dot(qs, kb[:, hk * 64:(hk + 1) * 64]) + bias_ref[hk]
        s = jnp.where(key_ok, s, NEG_INF)
        m = jnp.maximum(jnp.max(s, axis=-1, keepdims=True), sink)
        e = jnp.exp(s - m)
        denom = jnp.sum(e, axis=-1, keepdims=True) + jnp.exp(sink - m)
        o = jnp.dot(e.astype(jnp.bfloat16), vb[:, hk * 64:(hk + 1) * 64], preferred_element_type=jnp.float32)
        o = o / denom
        for g in range(A_GROUP):
            h = hk * A_GROUP + g
            o_ref[0, :, h * 64:(h + 1) * 64] = o[g * Q_TILE:(g + 1) * Q_TILE].astype(o_ref.dtype)


def _attn_a(proj3, bias, q_gain, k_gain, sinks):
    b, s, _ = proj3.shape
    return pl.pallas_call(
        _attn_a_kernel,
        grid=(b, s // Q_TILE),
        in_specs=[
            pl.BlockSpec((1, Q_TILE, 512), lambda bi, i: (bi, i, SEC_AQ // 512)),
            pl.BlockSpec((1, s, 256), lambda bi, i: (bi, 0, SEC_AKV // 256)),
            pl.BlockSpec((A_KV_HEADS, A_GROUP * Q_TILE, A_BAND), lambda bi, i: (0, 0, 0)),
            pl.BlockSpec((1, HEAD_DIM), lambda bi, i: (0, 0)),
            pl.BlockSpec((1, HEAD_DIM), lambda bi, i: (0, 0)),
            pl.BlockSpec(memory_space=pltpu.SMEM),
        ],
        out_specs=pl.BlockSpec((1, Q_TILE, 512), lambda bi, i: (bi, i, 0)),
        out_shape=jax.ShapeDtypeStruct((b, s, 512), jnp.bfloat16),
        scratch_shapes=[pltpu.VMEM((s + A_PAD, 128), jnp.bfloat16), pltpu.VMEM((s + A_PAD, 128), jnp.bfloat16)],
        compiler_params=pltpu.CompilerParams(dimension_semantics=("arbitrary", "arbitrary"),
                                             vmem_limit_bytes=VMEM_LIMIT),
        name="attn_a",
    )(proj3, proj3, bias, q_gain, k_gain, sinks)


def _latent_prep_kernel(cq_ref, ckv_ref, kpe_ref, qag_ref, kvag_ref, wq_ref, wk_ref, wv_ref,
                        qg_ref, kg_ref, cos_ref, sa_ref, sb_ref, qo_ref, ko_ref, vo_ref):
    lane = lax.broadcasted_iota(jnp.int32, (ROW_TILE, B_HEAD_PAD), 1)
    is_nope = lane < B_NOPE_DIM
    is_rope = jnp.logical_and(lane >= B_NOPE_DIM, lane < B_QK_DIM)
    cos = cos_ref[...]
    sa = sa_ref[...]
    sb = sb_ref[...]

    def split_norm(x, gain):
        sq = x * x
        nope = jnp.sum(jnp.where(is_nope, sq, 0.0), axis=-1, keepdims=True) * (1.0 / B_NOPE_DIM)
        rope = jnp.sum(jnp.where(is_rope, sq, 0.0), axis=-1, keepdims=True) * (1.0 / B_ROPE_DIM)
        r = jnp.where(is_nope, lax.rsqrt(nope + EPS), lax.rsqrt(rope + EPS))
        return x * r * gain

    def rotate(x):
        return x * cos + pltpu.roll(x, 112, 1) * sa + pltpu.roll(x, 16, 1) * sb

    cq = _rms(cq_ref[...].astype(jnp.float32), qag_ref[...]).astype(jnp.bfloat16)
    ckv = _rms(ckv_ref[...].astype(jnp.float32), kvag_ref[...]).astype(jnp.bfloat16)
    kpe = kpe_ref[...].astype(jnp.float32)
    kpe_n = rotate(split_norm(kpe, kg_ref[...]))
    kpe_n = jnp.where(is_rope, kpe_n, 0.0)
    vo_ref[...] = jnp.dot(ckv, wv_ref[...], preferred_element_type=jnp.float32).astype(vo_ref.dtype)
    qscale = B_QK_DIM ** -0.5
    for h in range(B_HEADS):
        cols = slice(h * B_HEAD_PAD, (h + 1) * B_HEAD_PAD)
        qh = jnp.dot(cq, wq_ref[:, cols], preferred_element_type=jnp.float32)
        qh = rotate(split_norm(qh, qg_ref[...])) * qscale
        qo_ref[:, cols] = qh.astype(qo_ref.dtype)
        kh = jnp.dot(ckv, wk_ref[:, cols], preferred_element_type=jnp.float32)
        kh = jnp.where(is_nope, split_norm(kh, kg_ref[...]), 0.0) + kpe_n
        ko_ref[:, cols] = kh.astype(ko_ref.dtype)


def _latent_prep(proj, q_a_gain, kv_a_gain, wq, wk, wv, q_gain, k_gain, cos, sa, sb, seq):
    t = proj.shape[0]
    pos_blocks = seq // ROW_TILE
    full = lambda shape: pl.BlockSpec(shape, lambda i: (0, 0))
    tab = pl.BlockSpec((ROW_TILE, B_HEAD_PAD), lambda i: (i % pos_blocks, 0))
    return pl.pallas_call(
        _latent_prep_kernel,
        grid=(t // ROW_TILE,),
        in_specs=[
            pl.BlockSpec((ROW_TILE, 256), lambda i: (i, SEC_BCQ // 256)),
            pl.BlockSpec((ROW_TILE, 128), lambda i: (i, SEC_BCKV // 128)),
            pl.BlockSpec((ROW_TILE, 128), lambda i: (i, SEC_BKPE // 128)),
            full((1, B_Q_RANK)), full((1, B_KV_RANK)),
            full((B_Q_RANK, B_HEADS * B_HEAD_PAD)), full((B_KV_RANK, B_HEADS * B_HEAD_PAD)),
            full((B_KV_RANK, B_HEADS * B_V_DIM)),
            full((1, B_HEAD_PAD)), full((1, B_HEAD_PAD)),
            tab, tab, tab,
        ],
        out_specs=[
            pl.BlockSpec((ROW_TILE, B_HEADS * B_HEAD_PAD), lambda i: (i, 0)),
            pl.BlockSpec((ROW_TILE, B_HEADS * B_HEAD_PAD), lambda i: (i, 0)),
            pl.BlockSpec((ROW_TILE, B_HEADS * B_V_DIM), lambda i: (i, 0)),
        ],
        out_shape=[
            jax.ShapeDtypeStruct((t, B_HEADS * B_HEAD_PAD), jnp.bfloat16),
            jax.ShapeDtypeStruct((t, B_HEADS * B_HEAD_PAD), jnp.bfloat16),
            jax.ShapeDtypeStruct((t, B_HEADS * B_V_DIM), jnp.bfloat16),
        ],
        compiler_params=pltpu.CompilerParams(dimension_semantics=("arbitrary",), vmem_limit_bytes=VMEM_LIMIT),
        name="latent_prep",
    )(proj, proj, proj, q_a_gain, kv_a_gain, wq, wk, wv, q_gain, k_gain, cos, sa, sb)


def _attn_b_kernel(q_ref, k_ref, v_ref, o_ref):
    i = pl.program_id(1)
    row_chunk = lax.broadcasted_iota(jnp.int32, (B_TILE, B_TILE), 0) // CHUNK
    col_chunk = lax.broadcasted_iota(jnp.int32, (B_TILE, B_TILE), 1) // CHUNK
    diag_ok = col_chunk <= row_chunk
    for h in range(B_HEADS):
        qh = q_ref[0, :, h * B_HEAD_PAD:(h + 1) * B_HEAD_PAD]

        def block(j, masked):
            rows = pl.ds(pl.multiple_of(j * B_TILE, B_TILE), B_TILE)
            s = _nt_dot(qh, k_ref[0, rows, h * B_HEAD_PAD:(h + 1) * B_HEAD_PAD])
            if masked:
                s = jnp.where(diag_ok, s, NEG_INF)
            return s, v_ref[0, rows, h * B_V_DIM:(h + 1) * B_V_DIM]

        def update(carry, s, v):
            m, l, acc = carry
            m_new = jnp.maximum(m, jnp.max(s, axis=-1, keepdims=True))
            alpha = jnp.exp(m - m_new)
            e = jnp.exp(s - m_new)
            l = alpha * l + jnp.sum(e, axis=-1, keepdims=True)
            acc = alpha * acc + jnp.dot(e.astype(jnp.bfloat16), v, preferred_element_type=jnp.float32)
            return m_new, l, acc

        def body(j, carry):
            s, v = block(j, False)
            return update(carry, s, v)

        init = (jnp.full((B_TILE, 1), NEG_INF, jnp.float32), jnp.zeros((B_TILE, 1), jnp.float32),
                jnp.zeros((B_TILE, B_V_DIM), jnp.float32))
        carry = lax.fori_loop(0, i, body, init)
        s, v = block(i, True)
        _, l, acc = update(carry, s, v)
        o_ref[0, :, h * B_V_DIM:(h + 1) * B_V_DIM] = (acc / l).astype(o_ref.dtype)


def _attn_b(qb, kb, vb):
    b, s, _ = qb.shape
    return pl.pallas_call(
        _attn_b_kernel,
        grid=(b, s // B_TILE),
        in_specs=[
            pl.BlockSpec((1, B_TILE, B_HEADS * B_HEAD_PAD), lambda bi, i: (bi, i, 0)),
            pl.BlockSpec((1, s, B_HEADS * B_HEAD_PAD), lambda bi, i: (bi, 0, 0)),
            pl.BlockSpec((1, s, B_HEADS * B_V_DIM), lambda bi, i: (bi, 0, 0)),
        ],
        out_specs=pl.BlockSpec((1, B_TILE, B_HEADS * B_V_DIM), lambda bi, i: (bi, i, 0)),
        out_shape=jax.ShapeDtypeStruct((b, s, B_HEADS * B_V_DIM), jnp.bfloat16),
        compiler_params=pltpu.CompilerParams(dimension_semantics=("arbitrary", "arbitrary"),
                                             vmem_limit_bytes=VMEM_LIMIT),
        name="attn_b",
    )(qb, kb, vb)


def _attn_c_kernel(q_ref, k_ref, v_ref, bias_ref, qg_ref, kg_ref, o_ref, k_pad, v_pad):
    i = pl.program_id(1)
    seq = k_ref.shape[1]
    width = C_HEADS * HEAD_DIM

    @pl.when(i == 0)
    def _():
        k_pad[0:C_PAD, :] = jnp.zeros((C_PAD, width), k_pad.dtype)
        v_pad[0:C_PAD, :] = jnp.zeros((C_PAD, width), v_pad.dtype)

        def fill(r, carry):
            rows = pl.ds(pl.multiple_of(r * 256, 256), 256)
            dst = pl.ds(pl.multiple_of(r * 256 + C_PAD, 256), 256)
            k = k_ref[0, rows, :].astype(jnp.float32)
            kn = jnp.concatenate([_rms(k[:, h * 64:(h + 1) * 64], kg_ref[...]) for h in range(C_HEADS)], axis=1)
            k_pad[dst, :] = kn.astype(k_pad.dtype)
            v_pad[dst, :] = v_ref[0, rows, :]
            return carry

        lax.fori_loop(0, seq // 256, fill, 0)

    start = pl.multiple_of(i * Q_TILE, Q_TILE)
    kb = k_pad[pl.ds(start, C_BAND), :]
    vb = v_pad[pl.ds(start, C_BAND), :]
    q = q_ref[0].astype(jnp.float32)
    col = lax.broadcasted_iota(jnp.int32, (Q_TILE, C_BAND), 1)
    key_ok = col + (start - C_PAD) >= 0
    scale = HEAD_DIM ** -0.5
    for h in range(C_HEADS):
        hs = slice(h * 64, (h + 1) * 64)
        qh = (_rms(q[:, hs], qg_ref[...]) * scale).astype(jnp.bfloat16)
        s = _nt_dot(qh, kb[:, hs]) + bias_ref[h]
        s = jnp.where(key_ok, s, NEG_INF)
        m = jnp.max(s, axis=-1, keepdims=True)
        e = jnp.exp(s - m)
        denom = jnp.sum(e, axis=-1, keepdims=True)
        o = jnp.dot(e.astype(jnp.bfloat16), vb[:, hs], preferred_element_type=jnp.float32)
        o_ref[0, :, hs] = (o / denom).astype(o_ref.dtype)


def _attn_c(proj3, bias, q_gain, k_gain):
    b, s, _ = proj3.shape
    width = C_HEADS * HEAD_DIM
    return pl.pallas_call(
        _attn_c_kernel,
        grid=(b, s // Q_TILE),
        in_specs=[
            pl.BlockSpec((1, Q_TILE, width), lambda bi, i: (bi, i, SEC_CQ // width)),
            pl.BlockSpec((1, s, width), lambda bi, i: (bi, 0, SEC_CK // width)),
            pl.BlockSpec((1, s, width), lambda bi, i: (bi, 0, SEC_CV // width)),
            pl.BlockSpec((C_HEADS, Q_TILE, C_BAND), lambda bi, i: (0, 0, 0)),
            pl.BlockSpec((1, HEAD_DIM), lambda bi, i: (0, 0)),
            pl.BlockSpec((1, HEAD_DIM), lambda bi, i: (0, 0)),
        ],
        out_specs=pl.BlockSpec((1, Q_TILE, width), lambda bi, i: (bi, i, 0)),
        out_shape=jax.ShapeDtypeStruct((b, s, width), jnp.bfloat16),
        scratch_shapes=[pltpu.VMEM((s + C_PAD, width), jnp.bfloat16), pltpu.VMEM((s + C_PAD, width), jnp.bfloat16)],
        compiler_params=pltpu.CompilerParams(dimension_semantics=("arbitrary", "arbitrary"),
                                             vmem_limit_bytes=VMEM_LIMIT),
        name="attn_c",
    )(proj3, proj3, proj3, bias, q_gain, k_gain)


def _out_proj_kernel(oa_ref, ob_ref, oc_ref, x_ref, ga_ref, gb_ref, gc_ref, w_ref, gf_ref, wr_ref, br_ref,
                     xo_ref, h_ref, dest_ref, tg_ref, cnt_ref, count):
    i = pl.program_id(0)
    cap = pl.num_programs(0) * TOKEN_TILE

    @pl.when(i == 0)
    def _():
        count[...] = jnp.zeros_like(count)

    acc = x_ref[...]
    parts = ((oa_ref, ga_ref, 0, 512), (ob_ref, gb_ref, 512, 256), (oc_ref, gc_ref, 768, 256))
    for ref, g_ref, off, width in parts:
        o = _rms(ref[...].astype(jnp.float32), g_ref[...]).astype(jnp.bfloat16)
        acc = acc + jnp.dot(o, w_ref[off:off + width, :], preferred_element_type=jnp.float32)
    xo_ref[...] = acc
    h = _rms(acc, gf_ref[...])
    for c in range(LANE_TILES):
        h_ref[:, c, :] = h[:, c * 128:(c + 1) * 128]
    logits = jnp.dot(h, wr_ref[...], preferred_element_type=jnp.float32,
                     precision=lax.Precision.HIGHEST) + br_ref[...]
    lane = lax.broadcasted_iota(jnp.int32, logits.shape, 1)
    work = jnp.where(lane < N_EXPERTS, logits, -jnp.inf)
    gate_out = jnp.zeros(logits.shape, jnp.float32)
    denom = jnp.zeros((logits.shape[0], 1), jnp.float32)
    top = None
    picks = []
    for k in range(TOP_K):
        m = jnp.max(work, axis=-1, keepdims=True)
        idx = jnp.min(jnp.where(work == m, lane, ROUTER_PAD), axis=-1, keepdims=True)
        pick = lane == idx
        work = jnp.where(pick, -jnp.inf, work)
        top = m if top is None else top
        e = jnp.exp(m - top)
        denom = denom + e
        gate_out = jnp.where(lane == k, e, gate_out)
        picks.append((idx, pick))
    tg_ref[...] = gate_out / denom
    chosen = jnp.zeros(logits.shape, jnp.float32)
    for _, pick in picks:
        chosen = jnp.where(pick, 1.0, chosen)
    r_i = lax.broadcasted_iota(jnp.int32, (TOKEN_TILE, TOKEN_TILE), 0)
    c_i = lax.broadcasted_iota(jnp.int32, (TOKEN_TILE, TOKEN_TILE), 1)
    earlier = jnp.where(c_i < r_i, 1.0, 0.0).astype(jnp.bfloat16)
    before = jnp.dot(earlier, chosen.astype(jnp.bfloat16), preferred_element_type=jnp.float32) + count[...]
    dest_out = jnp.zeros(logits.shape, jnp.int32)
    for k, (idx, pick) in enumerate(picks):
        rank = jnp.sum(jnp.where(pick, before, 0.0), axis=-1, keepdims=True).astype(jnp.int32)
        dest_out = jnp.where(lane == k, idx * cap + rank, dest_out)
    dest_ref[...] = dest_out
    count[...] = count[...] + jnp.sum(chosen, axis=0, keepdims=True)
    cnt_ref[...] = count[...].astype(jnp.int32)


def _out_proj(oa, ob, oc, x2, ga, gb, gc, w, gf, wr, br):
    t = x2.shape[0]
    row = lambda width: pl.BlockSpec((TOKEN_TILE, width), lambda i: (i, 0))
    full = lambda shape: pl.BlockSpec(shape, lambda i: (0, 0))
    return pl.pallas_call(
        _out_proj_kernel,
        grid=(t // TOKEN_TILE,),
        in_specs=[row(512), row(256), row(256), row(D_MODEL), full((1, 512)), full((1, 256)), full((1, 256)),
                  full((D_MODEL, D_MODEL)), full((1, D_MODEL)), full((D_MODEL, ROUTER_PAD)), full((1, ROUTER_PAD))],
        out_specs=[row(D_MODEL), pl.BlockSpec((TOKEN_TILE, LANE_TILES, 128), lambda i: (i, 0, 0)),
                   row(ROUTER_PAD), row(ROUTER_PAD), full((1, ROUTER_PAD))],
        out_shape=[jax.ShapeDtypeStruct((t, D_MODEL), jnp.float32),
                   jax.ShapeDtypeStruct((t, LANE_TILES, 128), jnp.float32),
                   jax.ShapeDtypeStruct((t, ROUTER_PAD), jnp.int32),
                   jax.ShapeDtypeStruct((t, ROUTER_PAD), jnp.float32),
                   jax.ShapeDtypeStruct((1, ROUTER_PAD), jnp.int32)],
        scratch_shapes=[pltpu.VMEM((1, ROUTER_PAD), jnp.float32)],
        compiler_params=pltpu.CompilerParams(dimension_semantics=("arbitrary",), vmem_limit_bytes=VMEM_LIMIT),
        name="out_proj",
    )(oa, ob, oc, x2, ga, gb, gc, w, gf, wr, br)


def _dispatch_kernel(dest_hbm, h_ref, x_hbm, idx_smem, isem, dsem):
    i = pl.program_id(0)
    nt = pl.num_programs(0)
    slot = i % 2

    def idx_copy(tile, s):
        return pltpu.make_async_copy(dest_hbm.at[tile], idx_smem.at[s], isem.at[s])

    @pl.when(i == 0)
    def _():
        idx_copy(0, 0).start()

    idx_copy(i, slot).wait()

    @pl.when(i + 1 < nt)
    def _():
        idx_copy(i + 1, 1 - slot).start()

    def body(r, carry):
        for k in range(TOP_K):
            dst = idx_smem[slot, r * TOP_K + k]
            pltpu.make_async_copy(h_ref.at[r], x_hbm.at[dst], dsem).start()
        return carry

    lax.fori_loop(0, TOKEN_TILE, body, 0, unroll=4)
    for _ in range(TOP_K):
        pltpu.make_async_copy(h_ref, x_hbm.at[pl.ds(0, TOKEN_TILE)], dsem).wait()


def _dispatch(dest_tiles, h3, n_rows):
    t = h3.shape[0]
    return pl.pallas_call(
        _dispatch_kernel,
        grid=(t // TOKEN_TILE,),
        in_specs=[pl.BlockSpec(memory_space=pl.ANY),
                  pl.BlockSpec((TOKEN_TILE, LANE_TILES, 128), lambda i: (i, 0, 0))],
        out_specs=pl.BlockSpec(memory_space=pl.ANY),
        out_shape=jax.ShapeDtypeStruct((n_rows, LANE_TILES, 128), jnp.float32),
        scratch_shapes=[pltpu.SMEM((2, TOKEN_TILE * TOP_K), jnp.int32), pltpu.SemaphoreType.DMA((2,)),
                        pltpu.SemaphoreType.DMA],
        compiler_params=pltpu.CompilerParams(dimension_semantics=("arbitrary",), vmem_limit_bytes=VMEM_LIMIT),
        name="dispatch",
    )(dest_tiles, h3)


def _moe_ffn_kernel(blk_e_ref, blk_row_ref, blk_valid_ref, x_ref, wgu_ref, bgu_ref, wd_ref, bd_ref, y_ref,
                    wgu_bf, wd_bf):
    i = pl.program_id(0)
    prev = blk_e_ref[jnp.maximum(i - 1, 0)]
    changed = jnp.logical_or(i == 0, blk_e_ref[i] != prev)
    n_valid = blk_valid_ref[i]

    @pl.when(changed)
    def _():
        for c in range(4):
            cols = slice(c * 512, (c + 1) * 512)
            wgu_bf[:, cols] = wgu_ref[0, 0, :, cols].astype(jnp.bfloat16)
        for c in range(2):
            cols = slice(c * 512, (c + 1) * 512)
            wd_bf[:, cols] = wd_ref[0, 0, :, cols].astype(jnp.bfloat16)

    @pl.when(n_valid > 0)
    def _():
        live = lax.broadcasted_iota(jnp.int32, (EXPERT_ROWS, 128), 0) < n_valid
        x = jnp.concatenate([jnp.where(live, x_ref[:, c, :], 0.0).astype(jnp.bfloat16) for c in range(LANE_TILES)],
                            axis=1)
        glu = jnp.dot(x, wgu_bf[:, 0:D_EXPERT], preferred_element_type=jnp.float32) + bgu_ref[0, 0, :, 0:D_EXPERT]
        lin = jnp.dot(x, wgu_bf[:, D_EXPERT:], preferred_element_type=jnp.float32) + bgu_ref[0, 0, :, D_EXPERT:]
        glu = jnp.minimum(glu, SWIGLU_LIMIT)
        lin = jnp.clip(lin, -SWIGLU_LIMIT, SWIGLU_LIMIT)
        act = glu * jax.nn.sigmoid(SWIGLU_ALPHA * glu) * (lin + 1.0)
        y = jnp.dot(act.astype(jnp.bfloat16), wd_bf[...], preferred_element_type=jnp.float32) + bd_ref[0, 0]
        for c in range(LANE_TILES):
            y_ref[:, c, :] = y[:, c * 128:(c + 1) * 128]

    @pl.when(n_valid <= 0)
    def _():
        y_ref[...] = jnp.zeros_like(y_ref)


def _moe_ffn(layer, blk_e, blk_row, blk_valid, x_rows, w_gu, b_gu, w_down, b_down):
    n_rows = x_rows.shape[0]
    n_blocks = blk_e.shape[0]
    depth = w_gu.shape[0]
    rows = lambda i, be, br, bv: (br[i], 0, 0)
    expert = lambda i, be, br, bv: (layer, be[i], 0, 0)
    grid_spec = pltpu.PrefetchScalarGridSpec(
        num_scalar_prefetch=3,
        grid=(n_blocks,),
        in_specs=[
            pl.BlockSpec((EXPERT_ROWS, LANE_TILES, 128), rows),
            pl.BlockSpec((1, 1, D_MODEL, 2 * D_EXPERT), expert),
            pl.BlockSpec((1, 1, 1, 2 * D_EXPERT), expert),
            pl.BlockSpec((1, 1, D_EXPERT, D_MODEL), expert),
            pl.BlockSpec((1, 1, 1, D_MODEL), expert),
        ],
        out_specs=pl.BlockSpec((EXPERT_ROWS, LANE_TILES, 128), rows),
        scratch_shapes=[pltpu.VMEM((D_MODEL, 2 * D_EXPERT), jnp.bfloat16),
                        pltpu.VMEM((D_EXPERT, D_MODEL), jnp.bfloat16)],
    )
    return pl.pallas_call(
        _moe_ffn_kernel,
        grid_spec=grid_spec,
        out_shape=jax.ShapeDtypeStruct((n_rows, LANE_TILES, 128), jnp.float32),
        compiler_params=pltpu.CompilerParams(dimension_semantics=("arbitrary",), vmem_limit_bytes=VMEM_LIMIT),
        name="moe_ffn",
    )(blk_e, blk_row, blk_valid, x_rows, w_gu, b_gu.reshape(depth, N_EXPERTS, 1, -1), w_down,
      b_down.reshape(depth, N_EXPERTS, 1, -1))


def _combine_kernel(dest_hbm, y_hbm, x_ref, g_ref, *rest, project):
    if project:
        gain_ref, w_ref, xo_ref, o_ref, ybuf, idx_smem, isem, gsem = rest
    else:
        xo_ref, ybuf, idx_smem, isem, gsem = rest
    i = pl.program_id(0)
    nt = pl.num_programs(0)
    slot = i % 2

    def idx_copy(tile, s):
        return pltpu.make_async_copy(dest_hbm.at[tile], idx_smem.at[s], isem.at[s])

    def issue_gather(s):
        def body(r, carry):
            for k in range(TOP_K):
                src = idx_smem[s, r * TOP_K + k]
                pltpu.make_async_copy(y_hbm.at[src], ybuf.at[s, k * TOKEN_TILE + r], gsem.at[s]).start()
            return carry
        lax.fori_loop(0, TOKEN_TILE, body, 0, unroll=4)

    @pl.when(i == 0)
    def _():
        idx_copy(0, 0).start()
        idx_copy(0, 0).wait()
        issue_gather(0)

        @pl.when(nt > 1)
        def _():
            idx_copy(1, 1).start()

    @pl.when(i + 1 < nt)
    def _():
        idx_copy(i + 1, 1 - slot).wait()
        issue_gather(1 - slot)

    pltpu.make_async_copy(y_hbm.at[pl.ds(0, TOP_K * TOKEN_TILE)], ybuf.at[slot], gsem.at[slot]).wait()

    @pl.when(i + 2 < nt)
    def _():
        idx_copy(i + 2, slot).start()

    g = g_ref[...]
    gates = [jnp.broadcast_to(g[:, k:k + 1], (TOKEN_TILE, 128)) for k in range(TOP_K)]
    cols = []
    for c in range(LANE_TILES):
        acc = x_ref[:, c * 128:(c + 1) * 128]
        for k in range(TOP_K):
            acc = acc + gates[k] * ybuf[slot, k * TOKEN_TILE:(k + 1) * TOKEN_TILE, c, :]
        cols.append(acc)
    x = jnp.concatenate(cols, axis=1)
    xo_ref[...] = x
    if project:
        h = _rms(x, gain_ref[...]).astype(jnp.bfloat16)
        for n in range(PROJ_WIDTH // 512):
            sl = slice(n * 512, (n + 1) * 512)
            o_ref[:, sl] = jnp.dot(h, w_ref[:, sl], preferred_element_type=jnp.float32).astype(o_ref.dtype)


def _combine_call(dest_tiles, y_rows, x2, gates, gain=None, w=None):
    t = x2.shape[0]
    project = w is not None
    row = lambda width: pl.BlockSpec((TOKEN_TILE, width), lambda i: (i, 0))
    full = lambda shape: pl.BlockSpec(shape, lambda i: (0, 0))
    in_specs = [pl.BlockSpec(memory_space=pl.ANY), pl.BlockSpec(memory_space=pl.ANY), row(D_MODEL), row(ROUTER_PAD)]
    out_specs = [row(D_MODEL)]
    out_shape = [jax.ShapeDtypeStruct((t, D_MODEL), jnp.float32)]
    args = [dest_tiles, y_rows, x2, gates]
    if project:
        in_specs += [full((1, D_MODEL)), full((D_MODEL, PROJ_WIDTH))]
        out_specs.append(row(PROJ_WIDTH))
        out_shape.append(jax.ShapeDtypeStruct((t, PROJ_WIDTH), jnp.bfloat16))
        args += [gain, w]
    out = pl.pallas_call(
        functools.partial(_combine_kernel, project=project),
        grid=(t // TOKEN_TILE,),
        in_specs=in_specs, out_specs=out_specs, out_shape=out_shape,
        scratch_shapes=[pltpu.VMEM((2, TOP_K * TOKEN_TILE, LANE_TILES, 128), jnp.float32),
                        pltpu.SMEM((2, TOKEN_TILE * TOP_K), jnp.int32),
                        pltpu.SemaphoreType.DMA((2,)), pltpu.SemaphoreType.DMA((2,))],
        compiler_params=pltpu.CompilerParams(dimension_semantics=("arbitrary",), vmem_limit_bytes=VMEM_LIMIT),
        name="combine_in_proj" if project else "combine",
    )(*args)
    return out if project else out[0]


def _t5_bucket(rel):
    half = T5_BUCKETS // 2
    max_exact = half // 2
    ret = np.where(rel > 0, half, 0)
    n = np.abs(rel)
    nf = np.maximum(n, 1).astype(np.float32)
    large = max_exact + (np.log(nf / max_exact) / math.log(T5_MAX_DIST / max_exact)
                         * (half - max_exact)).astype(np.int32)
    large = np.minimum(large, half - 1)
    return ret + np.where(n < max_exact, n, large)


def _band_ok(n_prev, band):
    q_chunk = np.arange(Q_TILE)[:, None] // CHUNK + n_prev
    k_chunk = np.arange(band)[None, :] // CHUNK
    return (k_chunk <= q_chunk) & (k_chunk >= q_chunk - n_prev)


def _a_bias(t5_bias):
    rel = np.arange(A_BAND)[None, :] - A_PAD - np.arange(Q_TILE)[:, None]
    bias = t5_bias[_t5_bucket(rel)].transpose(2, 0, 1).astype(jnp.float32)
    bias = jnp.where(_band_ok(A_PREV_CHUNKS, A_BAND)[None], bias, NEG_INF)
    return bias.reshape(A_KV_HEADS, A_GROUP * Q_TILE, A_BAND)


def _c_bias(rel_table):
    dist = np.arange(Q_TILE)[:, None] + C_PAD - np.arange(C_BAND)[None, :]
    idx = np.clip(dist, -(CHUNK - 1), C_REL_CLIP) + (CHUNK - 1)
    bias = rel_table[idx].transpose(2, 0, 1).astype(jnp.float32)
    return jnp.where(_band_ok(C_PREV_CHUNKS, C_BAND)[None], bias, NEG_INF)


def _rope_tables(seq):
    half = B_ROPE_DIM // 2
    inv_freq = ROPE_THETA ** (-(jnp.arange(half, dtype=jnp.float32) / half))
    ang = jnp.arange(seq, dtype=jnp.float32)[:, None] * inv_freq[None, :]
    cos, sin = jnp.cos(ang), jnp.sin(ang)
    zeros = jnp.zeros((seq, B_NOPE_DIM), jnp.float32)
    tail = jnp.zeros((seq, B_HEAD_PAD - B_QK_DIM), jnp.float32)
    zh = jnp.zeros((seq, half), jnp.float32)
    cos_t = jnp.concatenate([zeros + 1.0, cos, cos, tail + 1.0], axis=1)
    sa = jnp.concatenate([zeros, -sin, zh, tail], axis=1)
    sb = jnp.concatenate([zeros, zh, sin, tail], axis=1)
    return cos_t, sa, sb


def _regroup_w_in(w):
    splits = np.cumsum((512, 128, 128, 256, 128, 32, 256, 256))
    aq, ak, av, bcq, bckv, bkpe, cq, ck, cv = jnp.split(w, splits.tolist(), axis=1)
    z = lambda n: jnp.zeros((w.shape[0], n), w.dtype)
    return jnp.concatenate([aq, ak, av, bcq, bckv, z(64), bkpe, z(32), cq, ck, cv], axis=1).astype(jnp.bfloat16)


def _pad_heads(w, head_dim, keep):
    rank = w.shape[0]
    w = w.reshape(rank, B_HEADS, head_dim)[:, :, :keep]
    w = jnp.pad(w, ((0, 0), (0, 0), (0, B_HEAD_PAD - keep)))
    return w.reshape(rank, B_HEADS * B_HEAD_PAD).astype(jnp.bfloat16)


def _pad_lanes(g, offset, width):
    return jnp.pad(g, (offset, width - offset - g.shape[0])).reshape(1, width)


def _block_table(counts, t):
    tk = t * TOP_K
    blocks_per_expert = t // EXPERT_ROWS
    nblk = (counts + EXPERT_ROWS - 1) // EXPERT_ROWS
    blk_end = jnp.cumsum(nblk)
    blk_first = blk_end - nblk
    n_blocks = (tk + N_EXPERTS * (EXPERT_ROWS - 1) + EXPERT_ROWS - 1) // EXPERT_ROWS
    blk = jnp.arange(n_blocks, dtype=jnp.int32)
    blk_e = jnp.minimum(jnp.searchsorted(blk_end, blk, side="right"), N_EXPERTS - 1).astype(jnp.int32)
    j = blk - blk_first[blk_e]
    blk_valid = jnp.clip(counts[blk_e] - j * EXPERT_ROWS, 0, EXPERT_ROWS).astype(jnp.int32)
    spare = N_EXPERTS * blocks_per_expert
    blk_row = jnp.where(blk_valid > 0, blk_e * blocks_per_expert + j, spare).astype(jnp.int32)
    return blk_e, blk_row, blk_valid


def kernel(x, attn_norm, w_in, a_q_norm, a_k_norm, a_sinks, t5_bias, b_q_a_norm, b_w_q_b, b_kv_a_norm, b_w_kv_b,
           b_q_norm, b_k_norm, c_q_norm, c_k_norm, c_rel_bias, a_out_norm, b_out_norm, c_out_norm, w_out, ffn_norm,
           w_router, b_router, w_gate_up, b_gate_up, w_down, b_down):
    b, s, d = x.shape
    t = b * s
    depth = w_in.shape[0]
    a_bias = _a_bias(t5_bias)
    cos_t, sin_a, sin_b = _rope_tables(s)
    x2 = x.reshape(t, d)
    proj = _in_proj(x2, attn_norm[0].reshape(1, d), _regroup_w_in(w_in[0]))
    for l in range(depth):
        proj3 = proj.reshape(b, s, PROJ_WIDTH)
        o_a = _attn_a(proj3, a_bias, a_q_norm[l].reshape(1, -1), a_k_norm[l].reshape(1, -1), a_sinks[l])
        kv_w = b_w_kv_b[l].reshape(B_KV_RANK, B_HEADS, B_NOPE_DIM + B_V_DIM)
        wv = kv_w[:, :, B_NOPE_DIM:].reshape(B_KV_RANK, B_HEADS * B_V_DIM).astype(jnp.bfloat16)
        qb, kb, vb = _latent_prep(
            proj, b_q_a_norm[l].reshape(1, -1), b_kv_a_norm[l].reshape(1, -1),
            _pad_heads(b_w_q_b[l], B_QK_DIM, B_QK_DIM), _pad_heads(b_w_kv_b[l], B_NOPE_DIM + B_V_DIM, B_NOPE_DIM), wv,
            _pad_lanes(b_q_norm[l], 0, B_HEAD_PAD), _pad_lanes(b_k_norm[l], 0, B_HEAD_PAD), cos_t, sin_a, sin_b, s)
        o_b = _attn_b(qb.reshape(b, s, -1), kb.reshape(b, s, -1), vb.reshape(b, s, -1))
        o_c = _attn_c(proj3, _c_bias(c_rel_bias[l]), c_q_norm[l].reshape(1, -1), c_k_norm[l].reshape(1, -1))
        wr = jnp.pad(w_router[l], ((0, 0), (0, ROUTER_PAD - N_EXPERTS)))
        br = jnp.pad(b_router[l], (0, ROUTER_PAD - N_EXPERTS)).reshape(1, ROUTER_PAD)
        x2, h3, dest, gates, counts = _out_proj(
            o_a.reshape(t, -1), o_b.reshape(t, -1), o_c.reshape(t, -1), x2,
            a_out_norm[l].reshape(1, -1), b_out_norm[l].reshape(1, -1), c_out_norm[l].reshape(1, -1),
            w_out[l].astype(jnp.bfloat16), ffn_norm[l].reshape(1, -1), wr, br)
        dest_tiles = dest[:, :TOP_K].reshape(t // TOKEN_TILE, TOKEN_TILE * TOP_K)
        blk_e, blk_row, blk_valid = _block_table(counts[0, :N_EXPERTS], t)
        n_rows = (N_EXPERTS * (t // EXPERT_ROWS) + 1) * EXPERT_ROWS
        x_rows = _dispatch(dest_tiles, h3, n_rows)
        y_rows = _moe_ffn(l, blk_e, blk_row, blk_valid, x_rows, w_gate_up, b_gate_up, w_down, b_down)
        if l + 1 < depth:
            x2, proj = _combine_call(dest_tiles, y_rows, x2, gates, attn_norm[l + 1].reshape(1, d),
                                     _regroup_w_in(w_in[l + 1]))
        else:
            x2 = _combine_call(dest_tiles, y_rows, x2, gates)
    return x2.reshape(b, s, d)
```

```python
import functools
import math

import jax
import jax.numpy as jnp
import numpy as np
from jax import lax
from jax.experimental import pallas as pl
from jax.experimental.pallas import tpu as pltpu

D_MODEL = 1024
CHUNK = 64
HEAD_DIM = 64
EPS = 1e-6
NEG_INF = -1e30
A_Q_HEADS = 8
A_KV_HEADS = 2
A_GROUP = 4
A_PREV_CHUNKS = 2
T5_BUCKETS = 32
T5_MAX_DIST = 128
B_HEADS = 4
B_Q_RANK = 256
B_KV_RANK = 128
B_NOPE_DIM = 64
B_ROPE_DIM = 32
B_QK_DIM = 96
B_V_DIM = 64
ROPE_THETA = 10000.0
C_HEADS = 4
C_PREV_CHUNKS = 8
C_REL_CLIP = 256
N_EXPERTS = 32
TOP_K = 4
D_EXPERT = 1024
SWIGLU_LIMIT = 7.0
SWIGLU_ALPHA = 1.702

SEC_AQ = 0
SEC_AKV = 512
SEC_BCQ = 768
SEC_BCKV = 1024
SEC_BKPE = 1152
SEC_CQ = 1280
SEC_CK = 1536
SEC_CV = 1792
PROJ_WIDTH = 2048
B_HEAD_PAD = 128

Q_TILE = 128
A_BAND = Q_TILE + A_PREV_CHUNKS * CHUNK
C_BAND = Q_TILE + C_PREV_CHUNKS * CHUNK
A_PAD = A_PREV_CHUNKS * CHUNK
C_PAD = C_PREV_CHUNKS * CHUNK
B_TILE = 256
ROW_TILE = 512
EXPERT_ROWS = 256
ROUTER_PAD = 128
TOKEN_TILE = 256
LANE_TILES = D_MODEL // 128
VMEM_LIMIT = 48 * 1024 * 1024


def _rms(x, gain):
    return x * lax.rsqrt(jnp.mean(x * x, axis=-1, keepdims=True) + EPS) * gain


def _nt_dot(a, b):
    return lax.dot_general(a, b, (((1,), (1,)), ((), ())), preferred_element_type=jnp.float32)


def _load_row_tiles(ref, first_row, n_rows, c):
    return ref[pl.ds(first_row * LANE_TILES + c, n_rows, stride=LANE_TILES), :]


def _store_row_tiles(ref, value):
    for c in range(LANE_TILES):
        ref[pl.ds(c, value.shape[0], stride=LANE_TILES), :] = value[:, c * 128:(c + 1) * 128]


def _row_tile(ref, row):
    return ref.at[pl.ds(pl.multiple_of(row * LANE_TILES, LANE_TILES), LANE_TILES)]


def _in_proj_kernel(x_ref, g_ref, w_ref, o_ref):
    h = _rms(x_ref[...], g_ref[...]).astype(jnp.bfloat16)
    for n in range(PROJ_WIDTH // 512):
        cols = slice(n * 512, (n + 1) * 512)
        o_ref[:, cols] = jnp.dot(h, w_ref[:, cols], preferred_element_type=jnp.float32).astype(o_ref.dtype)


def _in_proj(x2, gain, w):
    t = x2.shape[0]
    return pl.pallas_call(
        _in_proj_kernel,
        grid=(t // ROW_TILE,),
        in_specs=[
            pl.BlockSpec((ROW_TILE, D_MODEL), lambda i: (i, 0)),
            pl.BlockSpec((1, D_MODEL), lambda i: (0, 0)),
            pl.BlockSpec((D_MODEL, PROJ_WIDTH), lambda i: (0, 0)),
        ],
        out_specs=pl.BlockSpec((ROW_TILE, PROJ_WIDTH), lambda i: (i, 0)),
        out_shape=jax.ShapeDtypeStruct((t, PROJ_WIDTH), jnp.bfloat16),
        compiler_params=pltpu.CompilerParams(dimension_semantics=("arbitrary",), vmem_limit_bytes=VMEM_LIMIT),
        name="in_proj",
    )(x2, gain, w)


def _attn_a_kernel(q_ref, kv_ref, bias_ref, qg_ref, kg_ref, sink_ref, o_ref, k_pad, v_pad):
    i = pl.program_id(1)
    seq = kv_ref.shape[1]

    @pl.when(i == 0)
    def _():
        k_pad[0:A_PAD, :] = jnp.zeros((A_PAD, 128), k_pad.dtype)
        v_pad[0:A_PAD, :] = jnp.zeros((A_PAD, 128), v_pad.dtype)

        def fill(r, carry):
            rows = pl.ds(pl.multiple_of(r * 256, 256), 256)
            dst = pl.ds(pl.multiple_of(r * 256 + A_PAD, 128), 256)
            kv = kv_ref[0, rows, :]
            k = kv[:, 0:128].astype(jnp.float32)
            kn = jnp.concatenate([_rms(k[:, 0:64], kg_ref[...]), _rms(k[:, 64:128], kg_ref[...])], axis=1)
            k_pad[dst, :] = kn.astype(k_pad.dtype)
            v_pad[dst, :] = kv[:, 128:256]
            return carry

        lax.fori_loop(0, seq // 256, fill, 0)

    start = pl.multiple_of(i * Q_TILE, Q_TILE)
    kb = k_pad[pl.ds(start, A_BAND), :]
    vb = v_pad[pl.ds(start, A_BAND), :]
    q = q_ref[0].astype(jnp.float32)
    col = lax.broadcasted_iota(jnp.int32, (A_GROUP * Q_TILE, A_BAND), 1)
    key_ok = col + (start - A_PAD) >= 0
    scale = HEAD_DIM ** -0.5
    for hk in range(A_KV_HEADS):
        qs = []
        sinks = []
        for g in range(A_GROUP):
            h = hk * A_GROUP + g
            qh = _rms(q[:, h * 64:(h + 1) * 64], qg_ref[...]) * scale
            qs.append(qh.astype(jnp.bfloat16))
            sinks.append(jnp.full((Q_TILE, 1), sink_ref[h], jnp.float32))
        qs = jnp.concatenate(qs, axis=0)
        sink = jnp.concatenate(sinks, axis=0)
        s = _nt_dot(qs, kb[:, hk * 64:(hk + 1) * 64]) + bias_ref[hk]
        s = jnp.where(key_ok, s, NEG_INF)
        m = jnp.maximum(jnp.max(s, axis=-1, keepdims=True), sink)
        e = jnp.exp(s - m)
        denom = jnp.sum(e, axis=-1, keepdims=True) + jnp.exp(sink - m)
        o = jnp.dot(e.astype(jnp.bfloat16), vb[:, hk * 64:(hk + 1) * 64], preferred_element_type=jnp.float32)
        o = o / denom
        for g in range(A_GROUP):
            h = hk * A_GROUP + g
            o_ref[0, :, h * 64:(h + 1) * 64] = o[g * Q_TILE:(g + 1) * Q_TILE].astype(o_ref.dtype)


def _attn_a(proj3, bias, q_gain, k_gain, sinks):
    b, s, _ = proj3.shape
    return pl.pallas_call(
        _attn_a_kernel,
        grid=(b, s // Q_TILE),
        in_specs=[
            pl.BlockSpec((1, Q_TILE, 512), lambda bi, i: (bi, i, SEC_AQ // 512)),
            pl.BlockSpec((1, s, 256), lambda bi, i: (bi, 0, SEC_AKV // 256)),
            pl.BlockSpec((A_KV_HEADS, A_GROUP * Q_TILE, A_BAND), lambda bi, i: (0, 0, 0)),
            pl.BlockSpec((1, HEAD_DIM), lambda bi, i: (0, 0)),
            pl.BlockSpec((1, HEAD_DIM), lambda bi, i: (0, 0)),
            pl.BlockSpec(memory_space=pltpu.SMEM),
        ],
        out_specs=pl.BlockSpec((1, Q_TILE, 512), lambda bi, i: (bi, i, 0)),
        out_shape=jax.ShapeDtypeStruct((b, s, 512), jnp.bfloat16),
        scratch_shapes=[pltpu.VMEM((s + A_PAD, 128), jnp.bfloat16), pltpu.VMEM((s + A_PAD, 128), jnp.bfloat16)],
        compiler_params=pltpu.CompilerParams(dimension_semantics=("arbitrary", "arbitrary"),
                                             vmem_limit_bytes=VMEM_LIMIT),
        name="attn_a",
    )(proj3, proj3, bias, q_gain, k_gain, sinks)


def _latent_prep_kernel(cq_ref, ckv_ref, kpe_ref, qag_ref, kvag_ref, wq_ref, wk_ref, wv_ref,
                        qg_ref, kg_ref, cos_ref, sa_ref, sb_ref, qo_ref, ko_ref, vo_ref):
    lane = lax.broadcasted_iota(jnp.int32, (ROW_TILE, B_HEAD_PAD), 1)
    is_nope = lane < B_NOPE_DIM
    is_rope = jnp.logical_and(lane >= B_NOPE_DIM, lane < B_QK_DIM)
    cos = cos_ref[...]
    sa = sa_ref[...]
    sb = sb_ref[...]

    def split_norm(x, gain):
        sq = x * x
        nope = jnp.sum(jnp.where(is_nope, sq, 0.0), axis=-1, keepdims=True) * (1.0 / B_NOPE_DIM)
        rope = jnp.sum(jnp.where(is_rope, sq, 0.0), axis=-1, keepdims=True) * (1.0 / B_ROPE_DIM)
        r = jnp.where(is_nope, lax.rsqrt(nope + EPS), lax.rsqrt(rope + EPS))
        return x * r * gain

    def rotate(x):
        return x * cos + pltpu.roll(x, 112, 1) * sa + pltpu.roll(x, 16, 1) * sb

    cq = _rms(cq_ref[...].astype(jnp.float32), qag_ref[...]).astype(jnp.bfloat16)
    ckv = _rms(ckv_ref[...].astype(jnp.float32), kvag_ref[...]).astype(jnp.bfloat16)
    kpe = kpe_ref[...].astype(jnp.float32)
    kpe_n = rotate(split_norm(kpe, kg_ref[...]))
    kpe_n = jnp.where(is_rope, kpe_n, 0.0)
    vo_ref[...] = jnp.dot(ckv, wv_ref[...], preferred_element_type=jnp.float32).astype(vo_ref.dtype)
    qscale = B_QK_DIM ** -0.5
    for h in range(B_HEADS):
        cols = slice(h * B_HEAD_PAD, (h + 1) * B_HEAD_PAD)
        qh = jnp.dot(cq, wq_ref[:, cols], preferred_element_type=jnp.float32)
        qh = rotate(split_norm(qh, qg_ref[...])) * qscale
        qo_ref[:, cols] = qh.astype(qo_ref.dtype)
        kh = jnp.dot(ckv, wk_ref[:, cols], preferred_element_type=jnp.float32)
        kh = jnp.where(is_nope, split_norm(kh, kg_ref[...]), 0.0) + kpe_n
        ko_ref[:, cols] = kh.astype(ko_ref.dtype)


def _latent_prep(proj, q_a_gain, kv_a_gain, wq, wk, wv, q_gain, k_gain, cos, sa, sb, seq):
    t = proj.shape[0]
    pos_blocks = seq // ROW_TILE
    full = lambda shape: pl.BlockSpec(shape, lambda i: (0, 0))
    tab = pl.BlockSpec((ROW_TILE, B_HEAD_PAD), lambda i: (i % pos_blocks, 0))
    return pl.pallas_call(
        _latent_prep_kernel,
        grid=(t // ROW_TILE,),
        in_specs=[
            pl.BlockSpec((ROW_TILE, 256), lambda i: (i, SEC_BCQ // 256)),
            pl.BlockSpec((ROW_TILE, 128), lambda i: (i, SEC_BCKV // 128)),
            pl.BlockSpec((ROW_TILE, 128), lambda i: (i, SEC_BKPE // 128)),
            full((1, B_Q_RANK)), full((1, B_KV_RANK)),
            full((B_Q_RANK, B_HEADS * B_HEAD_PAD)), full((B_KV_RANK, B_HEADS * B_HEAD_PAD)),
            full((B_KV_RANK, B_HEADS * B_V_DIM)),
            full((1, B_HEAD_PAD)), full((1, B_HEAD_PAD)),
            tab, tab, tab,
        ],
        out_specs=[
            pl.BlockSpec((ROW_TILE, B_HEADS * B_HEAD_PAD), lambda i: (i, 0)),
            pl.BlockSpec((ROW_TILE, B_HEADS * B_HEAD_PAD), lambda i: (i, 0)),
            pl.BlockSpec((ROW_TILE, B_HEADS * B_V_DIM), lambda i: (i, 0)),
        ],
        out_shape=[
            jax.ShapeDtypeStruct((t, B_HEADS * B_HEAD_PAD), jnp.bfloat16),
            jax.ShapeDtypeStruct((t, B_HEADS * B_HEAD_PAD), jnp.bfloat16),
            jax.ShapeDtypeStruct((t, B_HEADS * B_V_DIM), jnp.bfloat16),
        ],
        compiler_params=pltpu.CompilerParams(dimension_semantics=("arbitrary",), vmem_limit_bytes=VMEM_LIMIT),
        name="latent_prep",
    )(proj, proj, proj, q_a_gain, kv_a_gain, wq, wk, wv, q_gain, k_gain, cos, sa, sb)


def _attn_b_kernel(q_ref, k_ref, v_ref, o_ref):
    i = pl.program_id(1)
    row_chunk = lax.broadcasted_iota(jnp.int32, (B_TILE, B_TILE), 0) // CHUNK
    col_chunk = lax.broadcasted_iota(jnp.int32, (B_TILE, B_TILE), 1) // CHUNK
    diag_ok = col_chunk <= row_chunk
    for h in range(B_HEADS):
        qh = q_ref[0, :, h * B_HEAD_PAD:(h + 1) * B_HEAD_PAD]

        def block(j, masked):
            rows = pl.ds(pl.multiple_of(j * B_TILE, B_TILE), B_TILE)
            s = _nt_dot(qh, k_ref[0, rows, h * B_HEAD_PAD:(h + 1) * B_HEAD_PAD])
            if masked:
                s = jnp.where(diag_ok, s, NEG_INF)
            return s, v_ref[0, rows, h * B_V_DIM:(h + 1) * B_V_DIM]

        def update(carry, s, v):
            m, l, acc = carry
            m_new = jnp.maximum(m, jnp.max(s, axis=-1, keepdims=True))
            alpha = jnp.exp(m - m_new)
            e = jnp.exp(s - m_new)
            l = alpha * l + jnp.sum(e, axis=-1, keepdims=True)
            acc = alpha * acc + jnp.dot(e.astype(jnp.bfloat16), v, preferred_element_type=jnp.float32)
            return m_new, l, acc

        def body(j, carry):
            s, v = block(j, False)
            return update(carry, s, v)

        init = (jnp.full((B_TILE, 1), NEG_INF, jnp.float32), jnp.zeros((B_TILE, 1), jnp.float32),
                jnp.zeros((B_TILE, B_V_DIM), jnp.float32))
        carry = lax.fori_loop(0, i, body, init)
        s, v = block(i, True)
        _, l, acc = update(carry, s, v)
        o_ref[0, :, h * B_V_DIM:(h + 1) * B_V_DIM] = (acc / l).astype(o_ref.dtype)


def _attn_b(qb, kb, vb):
    b, s, _ = qb.shape
    return pl.pallas_call(
        _attn_b_kernel,
        grid=(b, s // B_TILE),
        in_specs=[
            pl.BlockSpec((1, B_TILE, B_HEADS * B_HEAD_PAD), lambda bi, i: (bi, i, 0)),
            pl.BlockSpec((1, s, B_HEADS * B_HEAD_PAD), lambda bi, i: (bi, 0, 0)),
            pl.BlockSpec((1, s, B_HEADS * B_V_DIM), lambda bi, i: (bi, 0, 0)),
        ],
        out_specs=pl.BlockSpec((1, B_TILE, B_HEADS * B_V_DIM), lambda bi, i: (bi, i, 0)),
        out_shape=jax.ShapeDtypeStruct((b, s, B_HEADS * B_V_DIM), jnp.bfloat16),
        compiler_params=pltpu.CompilerParams(dimension_semantics=("arbitrary", "arbitrary"),
                                             vmem_limit_bytes=VMEM_LIMIT),
        name="attn_b",
    )(qb, kb, vb)


def _attn_c_kernel(q_ref, k_ref, v_ref, bias_ref, qg_ref, kg_ref, o_ref, k_pad, v_pad):
    i = pl.program_id(1)
    seq = k_ref.shape[1]
    width = C_HEADS * HEAD_DIM

    @pl.when(i == 0)
    def _():
        k_pad[0:C_PAD, :] = jnp.zeros((C_PAD, width), k_pad.dtype)
        v_pad[0:C_PAD, :] = jnp.zeros((C_PAD, width), v_pad.dtype)

        def fill(r, carry):
            rows = pl.ds(pl.multiple_of(r * 256, 256), 256)
            dst = pl.ds(pl.multiple_of(r * 256 + C_PAD, 256), 256)
            k = k_ref[0, rows, :].astype(jnp.float32)
            kn = jnp.concatenate([_rms(k[:, h * 64:(h + 1) * 64], kg_ref[...]) for h in range(C_HEADS)], axis=1)
            k_pad[dst, :] = kn.astype(k_pad.dtype)
            v_pad[dst, :] = v_ref[0, rows, :]
            return carry

        lax.fori_loop(0, seq // 256, fill, 0)

    start = pl.multiple_of(i * Q_TILE, Q_TILE)
    kb = k_pad[pl.ds(start, C_BAND), :]
    vb = v_pad[pl.ds(start, C_BAND), :]
    q = q_ref[0].astype(jnp.float32)
    col = lax.broadcasted_iota(jnp.int32, (Q_TILE, C_BAND), 1)
    key_ok = col + (start - C_PAD) >= 0
    scale = HEAD_DIM ** -0.5
    for h in range(C_HEADS):
        hs = slice(h * 64, (h + 1) * 64)
        qh = (_rms(q[:, hs], qg_ref[...]) * scale).astype(jnp.bfloat16)
        s = _nt_dot(qh, kb[:, hs]) + bias_ref[h]
        s = jnp.where(key_ok, s, NEG_INF)
        m = jnp.max(s, axis=-1, keepdims=True)
        e = jnp.exp(s - m)
        denom = jnp.sum(e, axis=-1, keepdims=True)
        o = jnp.dot(e.astype(jnp.bfloat16), vb[:, hs], preferred_element_type=jnp.float32)
        o_ref[0, :, hs] = (o / denom).astype(o_ref.dtype)


def _attn_c(proj3, bias, q_gain, k_gain):
    b, s, _ = proj3.shape
    width = C_HEADS * HEAD_DIM
    return pl.pallas_call(
        _attn_c_kernel,
        grid=(b, s // Q_TILE),
        in_specs=[
            pl.BlockSpec((1, Q_TILE, width), lambda bi, i: (bi, i, SEC_CQ // width)),
            pl.BlockSpec((1, s, width), lambda bi, i: (bi, 0, SEC_CK // width)),
            pl.BlockSpec((1, s, width), lambda bi, i: (bi, 0, SEC_CV // width)),
            pl.BlockSpec((C_HEADS, Q_TILE, C_BAND), lambda bi, i: (0, 0, 0)),
            pl.BlockSpec((1, HEAD_DIM), lambda bi, i: (0, 0)),
            pl.BlockSpec((1, HEAD_DIM), lambda bi, i: (0, 0)),
        ],
        out_specs=pl.BlockSpec((1, Q_TILE, width), lambda bi, i: (bi, i, 0)),
        out_shape=jax.ShapeDtypeStruct((b, s, width), jnp.bfloat16),
        scratch_shapes=[pltpu.VMEM((s + C_PAD, width), jnp.bfloat16), pltpu.VMEM((s + C_PAD, width), jnp.bfloat16)],
        compiler_params=pltpu.CompilerParams(dimension_semantics=("arbitrary", "arbitrary"),
                                             vmem_limit_bytes=VMEM_LIMIT),
        name="attn_c",
    )(proj3, proj3, proj3, bias, q_gain, k_gain)


def _out_proj_kernel(oa_ref, ob_ref, oc_ref, x_ref, ga_ref, gb_ref, gc_ref, w_ref, gf_ref, wr_ref, br_ref,
                     xo_ref, h_ref, dest_ref, tg_ref, cnt_ref, count):
    i = pl.program_id(0)
    cap = pl.num_programs(0) * TOKEN_TILE

    @pl.when(i == 0)
    def _():
        count[...] = jnp.zeros_like(count)

    acc = x_ref[...]
    parts = ((oa_ref, ga_ref, 0, 512), (ob_ref, gb_ref, 512, 256), (oc_ref, gc_ref, 768, 256))
    for ref, g_ref, off, width in parts:
        o = _rms(ref[...].astype(jnp.float32), g_ref[...]).astype(jnp.bfloat16)
        acc = acc + jnp.dot(o, w_ref[off:off + width, :], preferred_element_type=jnp.float32)
    xo_ref[...] = acc
    h = _rms(acc, gf_ref[...])
    _store_row_tiles(h_ref, h)
    logits = jnp.dot(h, wr_ref[...], preferred_element_type=jnp.float32,
                     precision=lax.Precision.HIGHEST) + br_ref[...]
    lane = lax.broadcasted_iota(jnp.int32, logits.shape, 1)
    work = jnp.where(lane < N_EXPERTS, logits, -jnp.inf)
    gate_out = jnp.zeros(logits.shape, jnp.float32)
    denom = jnp.zeros((logits.shape[0], 1), jnp.float32)
    top = None
    picks = []
    for k in range(TOP_K):
        m = jnp.max(work, axis=-1, keepdims=True)
        idx = jnp.min(jnp.where(work == m, lane, ROUTER_PAD), axis=-1, keepdims=True)
        pick = lane == idx
        work = jnp.where(pick, -jnp.inf, work)
        top = m if top is None else top
        e = jnp.exp(m - top)
        denom = denom + e
        gate_out = jnp.where(lane == k, e, gate_out)
        picks.append((idx, pick))
    tg_ref[...] = gate_out / denom
    chosen = jnp.zeros(logits.shape, jnp.float32)
    for _, pick in picks:
        chosen = jnp.where(pick, 1.0, chosen)
    r_i = lax.broadcasted_iota(jnp.int32, (TOKEN_TILE, TOKEN_TILE), 0)
    c_i = lax.broadcasted_iota(jnp.int32, (TOKEN_TILE, TOKEN_TILE), 1)
    earlier = jnp.where(c_i < r_i, 1.0, 0.0).astype(jnp.bfloat16)
    before = jnp.dot(earlier, chosen.astype(jnp.bfloat16), preferred_element_type=jnp.float32) + count[...]
    dest_out = jnp.zeros(logits.shape, jnp.int32)
    for k, (idx, pick) in enumerate(picks):
        rank = jnp.sum(jnp.where(pick, before, 0.0), axis=-1, keepdims=True).astype(jnp.int32)
        dest_out = jnp.where(lane == k, idx * cap + rank, dest_out)
    dest_ref[...] = dest_out
    count[...] = count[...] + jnp.sum(chosen, axis=0, keepdims=True)
    cnt_ref[...] = count[...].astype(jnp.int32)


def _out_proj(oa, ob, oc, x2, ga, gb, gc, w, gf, wr, br):
    t = x2.shape[0]
    row = lambda width: pl.BlockSpec((TOKEN_TILE, width), lambda i: (i, 0))
    full = lambda shape: pl.BlockSpec(shape, lambda i: (0, 0))
    return pl.pallas_call(
        _out_proj_kernel,
        grid=(t // TOKEN_TILE,),
        in_specs=[row(512), row(256), row(256), row(D_MODEL), full((1, 512)), full((1, 256)), full((1, 256)),
                  full((D_MODEL, D_MODEL)), full((1, D_MODEL)), full((D_MODEL, ROUTER_PAD)), full((1, ROUTER_PAD))],
        out_specs=[row(D_MODEL), pl.BlockSpec((TOKEN_TILE * LANE_TILES, 128), lambda i: (i, 0)),
                   row(ROUTER_PAD), row(ROUTER_PAD), full((1, ROUTER_PAD))],
        out_shape=[jax.ShapeDtypeStruct((t, D_MODEL), jnp.float32),
                   jax.ShapeDtypeStruct((t * LANE_TILES, 128), jnp.float32),
                   jax.ShapeDtypeStruct((t, ROUTER_PAD), jnp.int32),
                   jax.ShapeDtypeStruct((t, ROUTER_PAD), jnp.float32),
                   jax.ShapeDtypeStruct((1, ROUTER_PAD), jnp.int32)],
        scratch_shapes=[pltpu.VMEM((1, ROUTER_PAD), jnp.float32)],
        compiler_params=pltpu.CompilerParams(dimension_semantics=("arbitrary",), vmem_limit_bytes=VMEM_LIMIT),
        name="out_proj",
    )(oa, ob, oc, x2, ga, gb, gc, w, gf, wr, br)


def _dispatch_kernel(dest_hbm, h_ref, x_hbm, idx_smem, isem, dsem):
    i = pl.program_id(0)
    nt = pl.num_programs(0)
    slot = i % 2

    def idx_copy(tile, s):
        return pltpu.make_async_copy(dest_hbm.at[tile], idx_smem.at[s], isem.at[s])

    @pl.when(i == 0)
    def _():
        idx_copy(0, 0).start()

    idx_copy(i, slot).wait()

    @pl.when(i + 1 < nt)
    def _():
        idx_copy(i + 1, 1 - slot).start()

    def body(r, carry):
        for k in range(TOP_K):
            dst = idx_smem[slot, r * TOP_K + k]
            pltpu.make_async_copy(_row_tile(h_ref, r), _row_tile(x_hbm, dst), dsem).start()
        return carry

    lax.fori_loop(0, TOKEN_TILE, body, 0, unroll=4)
    for _ in range(TOP_K):
        pltpu.make_async_copy(h_ref, x_hbm.at[pl.ds(0, TOKEN_TILE * LANE_TILES)], dsem).wait()


def _dispatch(dest_tiles, h3, n_rows):
    t = h3.shape[0] // LANE_TILES
    return pl.pallas_call(
        _dispatch_kernel,
        grid=(t // TOKEN_TILE,),
        in_specs=[pl.BlockSpec(memory_space=pl.ANY),
                  pl.BlockSpec((TOKEN_TILE * LANE_TILES, 128), lambda i: (i, 0))],
        out_specs=pl.BlockSpec(memory_space=pl.ANY),
        out_shape=jax.ShapeDtypeStruct((n_rows * LANE_TILES, 128), jnp.float32),
        scratch_shapes=[pltpu.SMEM((2, TOKEN_TILE * TOP_K), jnp.int32), pltpu.SemaphoreType.DMA((2,)),
                        pltpu.SemaphoreType.DMA],
        compiler_params=pltpu.CompilerParams(dimension_semantics=("arbitrary",), vmem_limit_bytes=VMEM_LIMIT),
        name="dispatch",
    )(dest_tiles, h3)


def _moe_ffn_kernel(blk_e_ref, blk_row_ref, blk_valid_ref, x_ref, wgu_ref, bgu_ref, wd_ref, bd_ref, y_ref,
                    wgu_bf, wd_bf):
    i = pl.program_id(0)
    prev = blk_e_ref[jnp.maximum(i - 1, 0)]
    changed = jnp.logical_or(i == 0, blk_e_ref[i] != prev)
    n_valid = blk_valid_ref[i]

    @pl.when(changed)
    def _():
        for c in range(4):
            cols = slice(c * 512, (c + 1) * 512)
            wgu_bf[:, cols] = wgu_ref[0, 0, :, cols].astype(jnp.bfloat16)
        for c in range(2):
            cols = slice(c * 512, (c + 1) * 512)
            wd_bf[:, cols] = wd_ref[0, 0, :, cols].astype(jnp.bfloat16)

    @pl.when(n_valid > 0)
    def _():
        live = lax.broadcasted_iota(jnp.int32, (EXPERT_ROWS, 128), 0) < n_valid
        x = jnp.concatenate([jnp.where(live, _load_row_tiles(x_ref, 0, EXPERT_ROWS, c), 0.0).astype(jnp.bfloat16)
                             for c in range(LANE_TILES)], axis=1)
        glu = jnp.dot(x, wgu_bf[:, 0:D_EXPERT], preferred_element_type=jnp.float32) + bgu_ref[0, 0, :, 0:D_EXPERT]
        lin = jnp.dot(x, wgu_bf[:, D_EXPERT:], preferred_element_type=jnp.float32) + bgu_ref[0, 0, :, D_EXPERT:]
        glu = jnp.minimum(glu, SWIGLU_LIMIT)
        lin = jnp.clip(lin, -SWIGLU_LIMIT, SWIGLU_LIMIT)
        act = glu * jax.nn.sigmoid(SWIGLU_ALPHA * glu) * (lin + 1.0)
        y = jnp.dot(act.astype(jnp.bfloat16), wd_bf[...], preferred_element_type=jnp.float32) + bd_ref[0, 0]
        _store_row_tiles(y_ref, y)

    @pl.when(n_valid <= 0)
    def _():
        y_ref[...] = jnp.zeros_like(y_ref)


def _moe_ffn(layer, blk_e, blk_row, blk_valid, x_rows, w_gu, b_gu, w_down, b_down):
    n_blocks = blk_e.shape[0]
    depth = w_gu.shape[0]
    rows = lambda i, be, br, bv: (br[i], 0)
    expert = lambda i, be, br, bv: (layer, be[i], 0, 0)
    grid_spec = pltpu.PrefetchScalarGridSpec(
        num_scalar_prefetch=3,
        grid=(n_blocks,),
        in_specs=[
            pl.BlockSpec((EXPERT_ROWS * LANE_TILES, 128), rows),
            pl.BlockSpec((1, 1, D_MODEL, 2 * D_EXPERT), expert),
            pl.BlockSpec((1, 1, 1, 2 * D_EXPERT), expert),
            pl.BlockSpec((1, 1, D_EXPERT, D_MODEL), expert),
            pl.BlockSpec((1, 1, 1, D_MODEL), expert),
        ],
        out_specs=pl.BlockSpec((EXPERT_ROWS * LANE_TILES, 128), rows),
        scratch_shapes=[pltpu.VMEM((D_MODEL, 2 * D_EXPERT), jnp.bfloat16),
                        pltpu.VMEM((D_EXPERT, D_MODEL), jnp.bfloat16)],
    )
    return pl.pallas_call(
        _moe_ffn_kernel,
        grid_spec=grid_spec,
        out_shape=jax.ShapeDtypeStruct(x_rows.shape, jnp.float32),
        compiler_params=pltpu.CompilerParams(dimension_semantics=("arbitrary",), vmem_limit_bytes=VMEM_LIMIT),
        name="moe_ffn",
    )(blk_e, blk_row, blk_valid, x_rows, w_gu, b_gu.reshape(depth, N_EXPERTS, 1, -1), w_down,
      b_down.reshape(depth, N_EXPERTS, 1, -1))


def _combine_kernel(dest_hbm, y_hbm, x_ref, g_ref, *rest, project):
    if project:
        gain_ref, w_ref, xo_ref, o_ref, ybuf, idx_smem, isem, gsem = rest
    else:
        xo_ref, ybuf, idx_smem, isem, gsem = rest
    i = pl.program_id(0)
    nt = pl.num_programs(0)
    slot = i % 2

    def idx_copy(tile, s):
        return pltpu.make_async_copy(dest_hbm.at[tile], idx_smem.at[s], isem.at[s])

    def issue_gather(s):
        def body(r, carry):
            for k in range(TOP_K):
                src = idx_smem[s, r * TOP_K + k]
                pltpu.make_async_copy(_row_tile(y_hbm, src), _row_tile(ybuf.at[s], k * TOKEN_TILE + r),
                                      gsem.at[s]).start()
            return carry
        lax.fori_loop(0, TOKEN_TILE, body, 0, unroll=4)

    @pl.when(i == 0)
    def _():
        idx_copy(0, 0).start()
        idx_copy(0, 0).wait()
        issue_gather(0)

        @pl.when(nt > 1)
        def _():
            idx_copy(1, 1).start()

    @pl.when(i + 1 < nt)
    def _():
        idx_copy(i + 1, 1 - slot).wait()
        issue_gather(1 - slot)

    pltpu.make_async_copy(y_hbm.at[pl.ds(0, TOP_K * TOKEN_TILE * LANE_TILES)], ybuf.at[slot], gsem.at[slot]).wait()

    @pl.when(i + 2 < nt)
    def _():
        idx_copy(i + 2, slot).start()

    g = g_ref[...]
    gates = [jnp.broadcast_to(g[:, k:k + 1], (TOKEN_TILE, 128)) for k in range(TOP_K)]
    cols = []
    for c in range(LANE_TILES):
        acc = x_ref[:, c * 128:(c + 1) * 128]
        for k in range(TOP_K):
            acc = acc + gates[k] * _load_row_tiles(ybuf.at[slot], k * TOKEN_TILE, TOKEN_TILE, c)
        cols.append(acc)
    x = jnp.concatenate(cols, axis=1)
    xo_ref[...] = x
    if project:
        h = _rms(x, gain_ref[...]).astype(jnp.bfloat16)
        for n in range(PROJ_WIDTH // 512):
            sl = slice(n * 512, (n + 1) * 512)
            o_ref[:, sl] = jnp.dot(h, w_ref[:, sl], preferred_element_type=jnp.float32).astype(o_ref.dtype)


def _combine_call(dest_tiles, y_rows, x2, gates, gain=None, w=None):
    t = x2.shape[0]
    project = w is not None
    row = lambda width: pl.BlockSpec((TOKEN_TILE, width), lambda i: (i, 0))
    full = lambda shape: pl.BlockSpec(shape, lambda i: (0, 0))
    in_specs = [pl.BlockSpec(memory_space=pl.ANY), pl.BlockSpec(memory_space=pl.ANY), row(D_MODEL), row(ROUTER_PAD)]
    out_specs = [row(D_MODEL)]
    out_shape = [jax.ShapeDtypeStruct((t, D_MODEL), jnp.float32)]
    args = [dest_tiles, y_rows, x2, gates]
    if project:
        in_specs += [full((1, D_MODEL)), full((D_MODEL, PROJ_WIDTH))]
        out_specs.append(row(PROJ_WIDTH))
        out_shape.append(jax.ShapeDtypeStruct((t, PROJ_WIDTH), jnp.bfloat16))
        args += [gain, w]
    out = pl.pallas_call(
        functools.partial(_combine_kernel, project=project),
        grid=(t // TOKEN_TILE,),
        in_specs=in_specs, out_specs=out_specs, out_shape=out_shape,
        scratch_shapes=[pltpu.VMEM((2, TOP_K * TOKEN_TILE * LANE_TILES, 128), jnp.float32),
                        pltpu.SMEM((2, TOKEN_TILE * TOP_K), jnp.int32),
                        pltpu.SemaphoreType.DMA((2,)), pltpu.SemaphoreType.DMA((2,))],
        compiler_params=pltpu.CompilerParams(dimension_semantics=("arbitrary",), vmem_limit_bytes=VMEM_LIMIT),
        name="combine_in_proj" if project else "combine",
    )(*args)
    return out if project else out[0]


def _t5_bucket(rel):
    half = T5_BUCKETS // 2
    max_exact = half // 2
    ret = np.where(rel > 0, half, 0)
    n = np.abs(rel)
    nf = np.maximum(n, 1).astype(np.float32)
    large = max_exact + (np.log(nf / max_exact) / math.log(T5_MAX_DIST / max_exact)
                         * (half - max_exact)).astype(np.int32)
    large = np.minimum(large, half - 1)
    return ret + np.where(n < max_exact, n, large)


def _band_ok(n_prev, band):
    q_chunk = np.arange(Q_TILE)[:, None] // CHUNK + n_prev
    k_chunk = np.arange(band)[None, :] // CHUNK
    return (k_chunk <= q_chunk) & (k_chunk >= q_chunk - n_prev)


def _toeplitz(u, band):
    heads, length = u.shape
    flat = jnp.broadcast_to(u[:, None, :], (heads, Q_TILE, length)).reshape(heads, Q_TILE * length)
    skew = flat[:, :Q_TILE * (length - 1)].reshape(heads, Q_TILE, length - 1)
    return skew[:, :, Q_TILE - 1:Q_TILE - 1 + band]


def _a_bias(t5_bias):
    rel = np.arange(Q_TILE + A_BAND) - (Q_TILE - 1) - A_PAD
    bias = _toeplitz(t5_bias[_t5_bucket(rel)].T.astype(jnp.float32), A_BAND)
    bias = jnp.where(_band_ok(A_PREV_CHUNKS, A_BAND)[None], bias, NEG_INF)
    return bias.reshape(A_KV_HEADS, A_GROUP * Q_TILE, A_BAND)


def _c_bias(rel_table):
    dist = C_PAD + (Q_TILE - 1) - np.arange(Q_TILE + C_BAND)
    idx = np.clip(dist, -(CHUNK - 1), C_REL_CLIP) + (CHUNK - 1)
    bias = _toeplitz(rel_table[idx].T.astype(jnp.float32), C_BAND)
    return jnp.where(_band_ok(C_PREV_CHUNKS, C_BAND)[None], bias, NEG_INF)


def _rope_tables(seq):
    half = B_ROPE_DIM // 2
    inv_freq = ROPE_THETA ** (-(jnp.arange(half, dtype=jnp.float32) / half))
    ang = jnp.arange(seq, dtype=jnp.float32)[:, None] * inv_freq[None, :]
    cos, sin = jnp.cos(ang), jnp.sin(ang)
    zeros = jnp.zeros((seq, B_NOPE_DIM), jnp.float32)
    tail = jnp.zeros((seq, B_HEAD_PAD - B_QK_DIM), jnp.float32)
    zh = jnp.zeros((seq, half), jnp.float32)
    cos_t = jnp.concatenate([zeros + 1.0, cos, cos, tail + 1.0], axis=1)
    sa = jnp.concatenate([zeros, -sin, zh, tail], axis=1)
    sb = jnp.concatenate([zeros, zh, sin, tail], axis=1)
    return cos_t, sa, sb


def _regroup_w_in(w):
    splits = np.cumsum((512, 128, 128, 256, 128, 32, 256, 256))
    aq, ak, av, bcq, bckv, bkpe, cq, ck, cv = jnp.split(w, splits.tolist(), axis=1)
    z = lambda n: jnp.zeros((w.shape[0], n), w.dtype)
    return jnp.concatenate([aq, ak, av, bcq, bckv, z(64), bkpe, z(32), cq, ck, cv], axis=1).astype(jnp.bfloat16)


def _pad_heads(w, head_dim, keep):
    rank = w.shape[0]
    w = w.reshape(rank, B_HEADS, head_dim)[:, :, :keep]
    w = jnp.pad(w, ((0, 0), (0, 0), (0, B_HEAD_PAD - keep)))
    return w.reshape(rank, B_HEADS * B_HEAD_PAD).astype(jnp.bfloat16)


def _pad_lanes(g, offset, width):
    return jnp.pad(g, (offset, width - offset - g.shape[0])).reshape(1, width)


def _block_table(counts, t):
    tk = t * TOP_K
    blocks_per_expert = t // EXPERT_ROWS
    nblk = (counts + EXPERT_ROWS - 1) // EXPERT_ROWS
    blk_end = jnp.cumsum(nblk)
    blk_first = blk_end - nblk
    n_blocks = (tk + N_EXPERTS * (EXPERT_ROWS - 1) + EXPERT_ROWS - 1) // EXPERT_ROWS
    blk = jnp.arange(n_blocks, dtype=jnp.int32)
    blk_e = jnp.minimum(jnp.sum(blk[:, None] >= blk_end[None, :], axis=1), N_EXPERTS - 1).astype(jnp.int32)
    own = blk_e[:, None] == jnp.arange(N_EXPERTS, dtype=jnp.int32)[None, :]
    pick = lambda v: jnp.sum(jnp.where(own, v[None, :], 0), axis=1)
    j = blk - pick(blk_first)
    blk_valid = jnp.clip(pick(counts) - j * EXPERT_ROWS, 0, EXPERT_ROWS).astype(jnp.int32)
    spare = N_EXPERTS * blocks_per_expert
    blk_row = jnp.where(blk_valid > 0, blk_e * blocks_per_expert + j, spare).astype(jnp.int32)
    return blk_e, blk_row, blk_valid


def kernel(x, attn_norm, w_in, a_q_norm, a_k_norm, a_sinks, t5_bias, b_q_a_norm, b_w_q_b, b_kv_a_norm, b_w_kv_b,
           b_q_norm, b_k_norm, c_q_norm, c_k_norm, c_rel_bias, a_out_norm, b_out_norm, c_out_norm, w_out, ffn_norm,
           w_router, b_router, w_gate_up, b_gate_up, w_down, b_down):
    b, s, d = x.shape
    t = b * s
    depth = w_in.shape[0]
    a_bias = _a_bias(t5_bias)
    cos_t, sin_a, sin_b = _rope_tables(s)
    x2 = x.reshape(t, d)
    proj = _in_proj(x2, attn_norm[0].reshape(1, d), _regroup_w_in(w_in[0]))
    for l in range(depth):
        proj3 = proj.reshape(b, s, PROJ_WIDTH)
        o_a = _attn_a(proj3, a_bias, a_q_norm[l].reshape(1, -1), a_k_norm[l].reshape(1, -1), a_sinks[l])
        kv_w = b_w_kv_b[l].reshape(B_KV_RANK, B_HEADS, B_NOPE_DIM + B_V_DIM)
        wv = kv_w[:, :, B_NOPE_DIM:].reshape(B_KV_RANK, B_HEADS * B_V_DIM).astype(jnp.bfloat16)
        qb, kb, vb = _latent_prep(
            proj, b_q_a_norm[l].reshape(1, -1), b_kv_a_norm[l].reshape(1, -1),
            _pad_heads(b_w_q_b[l], B_QK_DIM, B_QK_DIM), _pad_heads(b_w_kv_b[l], B_NOPE_DIM + B_V_DIM, B_NOPE_DIM), wv,
            _pad_lanes(b_q_norm[l], 0, B_HEAD_PAD), _pad_lanes(b_k_norm[l], 0, B_HEAD_PAD), cos_t, sin_a, sin_b, s)
        o_b = _attn_b(qb.reshape(b, s, -1), kb.reshape(b, s, -1), vb.reshape(b, s, -1))
        o_c = _attn_c(proj3, _c_bias(c_rel_bias[l]), c_q_norm[l].reshape(1, -1), c_k_norm[l].reshape(1, -1))
        wr = jnp.pad(w_router[l], ((0, 0), (0, ROUTER_PAD - N_EXPERTS)))
        br = jnp.pad(b_router[l], (0, ROUTER_PAD - N_EXPERTS)).reshape(1, ROUTER_PAD)
        x2, h3, dest, gates, counts = _out_proj(
            o_a.reshape(t, -1), o_b.reshape(t, -1), o_c.reshape(t, -1), x2,
            a_out_norm[l].reshape(1, -1), b_out_norm[l].reshape(1, -1), c_out_norm[l].reshape(1, -1),
            w_out[l].astype(jnp.bfloat16), ffn_norm[l].reshape(1, -1), wr, br)
        dest_tiles = dest[:, :TOP_K].reshape(t // TOKEN_TILE, TOKEN_TILE * TOP_K)
        blk_e, blk_row, blk_valid = _block_table(counts[0, :N_EXPERTS], t)
        n_rows = (N_EXPERTS * (t // EXPERT_ROWS) + 1) * EXPERT_ROWS
        x_rows = _dispatch(dest_tiles, h3, n_rows)
        y_rows = _moe_ffn(l, blk_e, blk_row, blk_valid, x_rows, w_gate_up, b_gate_up, w_down, b_down)
        if l + 1 < depth:
            x2, proj = _combine_call(dest_tiles, y_rows, x2, gates, attn_norm[l + 1].reshape(1, d),
                                     _regroup_w_in(w_in[l + 1]))
        else:
            x2 = _combine_call(dest_tiles, y_rows, x2, gates)
    return x2.reshape(b, s, d)
```

```python
import functools
import math

import jax
import jax.numpy as jnp
import numpy as np
from jax import lax
from jax.experimental import pallas as pl
from jax.experimental.pallas import tpu as pltpu

D_MODEL = 1024
CHUNK = 64
HEAD_DIM = 64
EPS = 1e-6
NEG_INF = -1e30
A_Q_HEADS = 8
A_KV_HEADS = 2
A_GROUP = 4
A_PREV_CHUNKS = 2
T5_BUCKETS = 32
T5_MAX_DIST = 128
B_HEADS = 4
B_Q_RANK = 256
B_KV_RANK = 128
B_NOPE_DIM = 64
B_ROPE_DIM = 32
B_QK_DIM = 96
B_V_DIM = 64
ROPE_THETA = 10000.0
C_HEADS = 4
C_PREV_CHUNKS = 8
C_REL_CLIP = 256
N_EXPERTS = 32
TOP_K = 4
D_EXPERT = 1024
SWIGLU_LIMIT = 7.0
SWIGLU_ALPHA = 1.702

SEC_AQ = 0
SEC_AKV = 512
SEC_BCQ = 768
SEC_BCKV = 1024
SEC_BKPE = 1152
SEC_CQ = 1280
SEC_CK = 1536
SEC_CV = 1792
PROJ_WIDTH = 2048
HEAD_NORM_SLABS = tuple(sec // 128 + j for sec, n in ((SEC_AQ, 4), (SEC_AKV, 1), (SEC_CQ, 2), (SEC_CK, 2))
                        for j in range(n))
B_HEAD_PAD = 128

Q_TILE = 128
A_BAND = Q_TILE + A_PREV_CHUNKS * CHUNK
C_BAND = Q_TILE + C_PREV_CHUNKS * CHUNK
A_PAD = A_PREV_CHUNKS * CHUNK
C_PAD = C_PREV_CHUNKS * CHUNK
B_TILE = 256
ROW_TILE = 512
EXPERT_ROWS = 256
ROUTER_PAD = 128
TOKEN_TILE = 256
LANE_TILES = D_MODEL // 128
VMEM_LIMIT = 48 * 1024 * 1024


def _rms(x, gain):
    return x * lax.rsqrt(jnp.mean(x * x, axis=-1, keepdims=True) + EPS) * gain


def _nt_dot(a, b):
    return lax.dot_general(a, b, (((1,), (1,)), ((), ())), preferred_element_type=jnp.float32)


def _load_row_tiles(ref, first_row, n_rows, c):
    return ref[pl.ds(first_row * LANE_TILES + c, n_rows, stride=LANE_TILES), :]


def _store_row_tiles(ref, value):
    for c in range(LANE_TILES):
        ref[pl.ds(c, value.shape[0], stride=LANE_TILES), :] = value[:, c * 128:(c + 1) * 128]


def _row_tile(ref, row):
    return ref.at[pl.ds(pl.multiple_of(row * LANE_TILES, LANE_TILES), LANE_TILES)]


def _project(h, w_ref, hg_ref, o_ref):
    li = lax.broadcasted_iota(jnp.int32, (128, 128), 0) // HEAD_DIM
    lj = lax.broadcasted_iota(jnp.int32, (128, 128), 1) // HEAD_DIM
    same_head = jnp.where(li == lj, 1.0, 0.0).astype(jnp.bfloat16)
    for n in range(PROJ_WIDTH // 512):
        acc = jnp.dot(h, w_ref[:, n * 512:(n + 1) * 512], preferred_element_type=jnp.float32)
        slabs = []
        for j in range(4):
            slab = n * 4 + j
            a = acc[:, j * 128:(j + 1) * 128]
            if slab in HEAD_NORM_SLABS:
                sq = a * a
                hi = sq.astype(jnp.bfloat16)
                lo = (sq - hi.astype(jnp.float32)).astype(jnp.bfloat16)
                ss = (jnp.dot(hi, same_head, preferred_element_type=jnp.float32)
                      + jnp.dot(lo, same_head, preferred_element_type=jnp.float32))
                a = a * lax.rsqrt(ss * (1.0 / HEAD_DIM) + EPS) * hg_ref[:, slab * 128:(slab + 1) * 128]
            slabs.append(a.astype(o_ref.dtype))
        o_ref[:, n * 512:(n + 1) * 512] = jnp.concatenate(slabs, axis=1)


def _in_proj_kernel(x_ref, g_ref, w_ref, hg_ref, o_ref):
    _project(_rms(x_ref[...], g_ref[...]).astype(jnp.bfloat16), w_ref, hg_ref, o_ref)


def _in_proj(x2, gain, w, head_gain):
    t = x2.shape[0]
    return pl.pallas_call(
        _in_proj_kernel,
        grid=(t // ROW_TILE,),
        in_specs=[
            pl.BlockSpec((ROW_TILE, D_MODEL), lambda i: (i, 0)),
            pl.BlockSpec((1, D_MODEL), lambda i: (0, 0)),
            pl.BlockSpec((D_MODEL, PROJ_WIDTH), lambda i: (0, 0)),
            pl.BlockSpec((1, PROJ_WIDTH), lambda i: (0, 0)),
        ],
        out_specs=pl.BlockSpec((ROW_TILE, PROJ_WIDTH), lambda i: (i, 0)),
        out_shape=jax.ShapeDtypeStruct((t, PROJ_WIDTH), jnp.bfloat16),
        compiler_params=pltpu.CompilerParams(dimension_semantics=("arbitrary",), vmem_limit_bytes=VMEM_LIMIT),
        name="in_proj",
    )(x2, gain, w, head_gain)


def _half_lanes(x, parity):
    lane = lax.broadcasted_iota(jnp.int32, x.shape, x.ndim - 1) % 128
    keep = lane < HEAD_DIM if parity == 0 else lane >= HEAD_DIM
    return jnp.where(keep, x, jnp.zeros_like(x))


def _band_softmax(s, key_ok, sink=None):
    s = jnp.where(key_ok, s, NEG_INF)
    m = jnp.max(s, axis=-1, keepdims=True)
    if sink is not None:
        m = jnp.maximum(m, sink)
    e = jnp.exp(s - m)
    denom = jnp.sum(e, axis=-1, keepdims=True)
    if sink is not None:
        denom = denom + jnp.exp(sink - m)
    return e.astype(jnp.bfloat16), 1.0 / denom


def _attn_a_kernel(q_ref, kv_ref, bias_ref, sink_ref, o_ref, k_pad, v_pad):
    i = pl.program_id(1)
    seq = kv_ref.shape[1]

    @pl.when(i == 0)
    def _():
        for n in range(2 * A_KV_HEADS):
            k_pad[n, 0:A_PAD, :] = jnp.zeros((A_PAD, 128), k_pad.dtype)
            v_pad[n, 0:A_PAD, :] = jnp.zeros((A_PAD, 128), v_pad.dtype)

        def fill(r, carry):
            rows = pl.ds(pl.multiple_of(r * 256, 256), 256)
            dst = pl.ds(pl.multiple_of(r * 256 + A_PAD, 128), 256)
            kv = kv_ref[0, rows, :].astype(jnp.float32)
            for src, pad in ((kv[:, 0:128], k_pad), (kv[:, 128:256], v_pad)):
                swapped = pltpu.roll(src, HEAD_DIM, 1)
                pad[0, dst, :] = _half_lanes(src, 0).astype(pad.dtype)
                pad[1, dst, :] = _half_lanes(swapped, 1).astype(pad.dtype)
                pad[2, dst, :] = _half_lanes(swapped, 0).astype(pad.dtype)
                pad[3, dst, :] = _half_lanes(src, 1).astype(pad.dtype)
            return carry

        lax.fori_loop(0, seq // 256, fill, 0)

    start = pl.multiple_of(i * Q_TILE, Q_TILE)
    band = pl.ds(start, A_BAND)
    col = lax.broadcasted_iota(jnp.int32, (2 * Q_TILE, A_BAND), 1)
    key_ok = col + (start - A_PAD) >= 0
    for hk in range(A_KV_HEADS):
        q2 = jnp.concatenate([q_ref[0, :, (2 * hk) * 128:(2 * hk + 1) * 128],
                              q_ref[0, :, (2 * hk + 1) * 128:(2 * hk + 2) * 128]], axis=0)
        out = None
        for parity in range(2):
            heads = (4 * hk + parity, 4 * hk + 2 + parity)
            sink = jnp.concatenate([jnp.full((Q_TILE, 1), sink_ref[h], jnp.float32) for h in heads], axis=0)
            s = _nt_dot(q2, k_pad[2 * hk + parity, band, :]) + bias_ref[2 * hk + parity]
            e, inv = _band_softmax(s, key_ok, sink)
            o = jnp.dot(e, v_pad[2 * hk + parity, band, :], preferred_element_type=jnp.float32) * inv
            out = o if out is None else out + o
        o_ref[0, :, (2 * hk) * 128:(2 * hk + 1) * 128] = out[0:Q_TILE].astype(o_ref.dtype)
        o_ref[0, :, (2 * hk + 1) * 128:(2 * hk + 2) * 128] = out[Q_TILE:2 * Q_TILE].astype(o_ref.dtype)


def _attn_a(proj3, bias, sinks):
    b, s, _ = proj3.shape
    return pl.pallas_call(
        _attn_a_kernel,
        grid=(b, s // Q_TILE),
        in_specs=[
            pl.BlockSpec((1, Q_TILE, 512), lambda bi, i: (bi, i, SEC_AQ // 512)),
            pl.BlockSpec((1, s, 256), lambda bi, i: (bi, 0, SEC_AKV // 256)),
            pl.BlockSpec((2 * A_KV_HEADS, 2 * Q_TILE, A_BAND), lambda bi, i: (0, 0, 0)),
            pl.BlockSpec(memory_space=pltpu.SMEM),
        ],
        out_specs=pl.BlockSpec((1, Q_TILE, 512), lambda bi, i: (bi, i, 0)),
        out_shape=jax.ShapeDtypeStruct((b, s, 512), jnp.bfloat16),
        scratch_shapes=[pltpu.VMEM((2 * A_KV_HEADS, s + A_PAD, 128), jnp.bfloat16),
                        pltpu.VMEM((2 * A_KV_HEADS, s + A_PAD, 128), jnp.bfloat16)],
        compiler_params=pltpu.CompilerParams(dimension_semantics=("arbitrary", "arbitrary"),
                                             vmem_limit_bytes=VMEM_LIMIT),
        name="attn_a",
    )(proj3, proj3, bias, sinks)


def _latent_prep_kernel(cq_ref, ckv_ref, kpe_ref, qag_ref, kvag_ref, wq_ref, wk_ref, wv_ref,
                        qg_ref, kg_ref, cos_ref, sa_ref, sb_ref, qo_ref, ko_ref, vo_ref):
    lane = lax.broadcasted_iota(jnp.int32, (ROW_TILE, B_HEAD_PAD), 1)
    is_nope = lane < B_NOPE_DIM
    is_rope = jnp.logical_and(lane >= B_NOPE_DIM, lane < B_QK_DIM)
    cos = cos_ref[...]
    sa = sa_ref[...]
    sb = sb_ref[...]

    def split_norm(x, gain):
        sq = x * x
        nope = jnp.sum(jnp.where(is_nope, sq, 0.0), axis=-1, keepdims=True) * (1.0 / B_NOPE_DIM)
        rope = jnp.sum(jnp.where(is_rope, sq, 0.0), axis=-1, keepdims=True) * (1.0 / B_ROPE_DIM)
        r = jnp.where(is_nope, lax.rsqrt(nope + EPS), lax.rsqrt(rope + EPS))
        return x * r * gain

    def rotate(x):
        return x * cos + pltpu.roll(x, 112, 1) * sa + pltpu.roll(x, 16, 1) * sb

    cq = _rms(cq_ref[...].astype(jnp.float32), qag_ref[...]).astype(jnp.bfloat16)
    ckv = _rms(ckv_ref[...].astype(jnp.float32), kvag_ref[...]).astype(jnp.bfloat16)
    kpe = kpe_ref[...].astype(jnp.float32)
    kpe_n = rotate(split_norm(kpe, kg_ref[...]))
    kpe_n = jnp.where(is_rope, kpe_n, 0.0)
    vo_ref[...] = jnp.dot(ckv, wv_ref[...], preferred_element_type=jnp.float32).astype(vo_ref.dtype)
    qscale = B_QK_DIM ** -0.5
    for h in range(B_HEADS):
        cols = slice(h * B_HEAD_PAD, (h + 1) * B_HEAD_PAD)
        qh = jnp.dot(cq, wq_ref[:, cols], preferred_element_type=jnp.float32)
        qh = rotate(split_norm(qh, qg_ref[...])) * qscale
        qo_ref[:, cols] = qh.astype(qo_ref.dtype)
        kh = jnp.dot(ckv, wk_ref[:, cols], preferred_element_type=jnp.float32)
        kh = jnp.where(is_nope, split_norm(kh, kg_ref[...]), 0.0) + kpe_n
        ko_ref[:, cols] = kh.astype(ko_ref.dtype)


def _latent_prep(proj, q_a_gain, kv_a_gain, wq, wk, wv, q_gain, k_gain, cos, sa, sb, seq):
    t = proj.shape[0]
    pos_blocks = seq // ROW_TILE
    full = lambda shape: pl.BlockSpec(shape, lambda i: (0, 0))
    tab = pl.BlockSpec((ROW_TILE, B_HEAD_PAD), lambda i: (i % pos_blocks, 0))
    return pl.pallas_call(
        _latent_prep_kernel,
        grid=(t // ROW_TILE,),
        in_specs=[
            pl.BlockSpec((ROW_TILE, 256), lambda i: (i, SEC_BCQ // 256)),
            pl.BlockSpec((ROW_TILE, 128), lambda i: (i, SEC_BCKV // 128)),
            pl.BlockSpec((ROW_TILE, 128), lambda i: (i, SEC_BKPE // 128)),
            full((1, B_Q_RANK)), full((1, B_KV_RANK)),
            full((B_Q_RANK, B_HEADS * B_HEAD_PAD)), full((B_KV_RANK, B_HEADS * B_HEAD_PAD)),
            full((B_KV_RANK, B_HEADS * B_V_DIM)),
            full((1, B_HEAD_PAD)), full((1, B_HEAD_PAD)),
            tab, tab, tab,
        ],
        out_specs=[
            pl.BlockSpec((ROW_TILE, B_HEADS * B_HEAD_PAD), lambda i: (i, 0)),
            pl.BlockSpec((ROW_TILE, B_HEADS * B_HEAD_PAD), lambda i: (i, 0)),
            pl.BlockSpec((ROW_TILE, B_HEADS * B_V_DIM), lambda i: (i, 0)),
        ],
        out_shape=[
            jax.ShapeDtypeStruct((t, B_HEADS * B_HEAD_PAD), jnp.bfloat16),
            jax.ShapeDtypeStruct((t, B_HEADS * B_HEAD_PAD), jnp.bfloat16),
            jax.ShapeDtypeStruct((t, B_HEADS * B_V_DIM), jnp.bfloat16),
        ],
        compiler_params=pltpu.CompilerParams(dimension_semantics=("arbitrary",), vmem_limit_bytes=VMEM_LIMIT),
        name="latent_prep",
    )(proj, proj, proj, q_a_gain, kv_a_gain, wq, wk, wv, q_gain, k_gain, cos, sa, sb)


def _attn_b_kernel(q_ref, k_ref, v_ref, o_ref):
    i = pl.program_id(1)
    row_chunk = lax.broadcasted_iota(jnp.int32, (B_TILE, B_TILE), 0) // CHUNK
    col_chunk = lax.broadcasted_iota(jnp.int32, (B_TILE, B_TILE), 1) // CHUNK
    diag_ok = col_chunk <= row_chunk

    def step(j, carry, masked):
        rows = pl.ds(pl.multiple_of(j * B_TILE, B_TILE), B_TILE)
        out = []
        for h in range(B_HEADS):
            m, l, acc = carry[h]
            s = _nt_dot(q_ref[0, :, h * B_HEAD_PAD:(h + 1) * B_HEAD_PAD],
                        k_ref[0, rows, h * B_HEAD_PAD:(h + 1) * B_HEAD_PAD])
            if masked:
                s = jnp.where(diag_ok, s, NEG_INF)
            m_new = jnp.maximum(m, jnp.max(s, axis=-1, keepdims=True))
            alpha = jnp.exp(m - m_new)
            e = jnp.exp(s - m_new)
            l = alpha * l + jnp.sum(e, axis=-1, keepdims=True)
            acc = alpha * acc + jnp.dot(e.astype(jnp.bfloat16), v_ref[0, rows, h * B_V_DIM:(h + 1) * B_V_DIM],
                                        preferred_element_type=jnp.float32)
            out.append((m_new, l, acc))
        return tuple(out)

    init = tuple((jnp.full((B_TILE, 1), NEG_INF, jnp.float32), jnp.zeros((B_TILE, 1), jnp.float32),
                  jnp.zeros((B_TILE, B_V_DIM), jnp.float32)) for _ in range(B_HEADS))
    carry = lax.fori_loop(0, i, lambda j, c: step(j, c, False), init)
    carry = step(i, carry, True)
    for h in range(B_HEADS):
        _, l, acc = carry[h]
        o_ref[0, :, h * B_V_DIM:(h + 1) * B_V_DIM] = (acc / l).astype(o_ref.dtype)


def _attn_b(qb, kb, vb):
    b, s, _ = qb.shape
    return pl.pallas_call(
        _attn_b_kernel,
        grid=(b, s // B_TILE),
        in_specs=[
            pl.BlockSpec((1, B_TILE, B_HEADS * B_HEAD_PAD), lambda bi, i: (bi, i, 0)),
            pl.BlockSpec((1, s, B_HEADS * B_HEAD_PAD), lambda bi, i: (bi, 0, 0)),
            pl.BlockSpec((1, s, B_HEADS * B_V_DIM), lambda bi, i: (bi, 0, 0)),
        ],
        out_specs=pl.BlockSpec((1, B_TILE, B_HEADS * B_V_DIM), lambda bi, i: (bi, i, 0)),
        out_shape=jax.ShapeDtypeStruct((b, s, B_HEADS * B_V_DIM), jnp.bfloat16),
        compiler_params=pltpu.CompilerParams(dimension_semantics=("arbitrary", "arbitrary"),
                                             vmem_limit_bytes=VMEM_LIMIT),
        name="attn_b",
    )(qb, kb, vb)


def _attn_c_kernel(q_ref, k_ref, v_ref, bias_ref, o_ref, k_pad, v_pad):
    i = pl.program_id(1)
    seq = k_ref.shape[1]
    width = C_HEADS * HEAD_DIM

    @pl.when(i == 0)
    def _():
        for parity in range(2):
            k_pad[parity, 0:C_PAD, :] = jnp.zeros((C_PAD, width), k_pad.dtype)
            v_pad[parity, 0:C_PAD, :] = jnp.zeros((C_PAD, width), v_pad.dtype)

        def fill(r, carry):
            rows = pl.ds(pl.multiple_of(r * 256, 256), 256)
            dst = pl.ds(pl.multiple_of(r * 256 + C_PAD, 256), 256)
            for src, pad in ((k_ref, k_pad), (v_ref, v_pad)):
                x = src[0, rows, :]
                for parity in range(2):
                    pad[parity, dst, :] = _half_lanes(x, parity)
            return carry

        lax.fori_loop(0, seq // 256, fill, 0)

    start = pl.multiple_of(i * Q_TILE, Q_TILE)
    band = pl.ds(start, C_BAND)
    col = lax.broadcasted_iota(jnp.int32, (Q_TILE, C_BAND), 1)
    key_ok = col + (start - C_PAD) >= 0
    for slab in range(width // 128):
        lanes = slice(slab * 128, (slab + 1) * 128)
        q = q_ref[0, :, lanes]
        out = None
        for parity in range(2):
            s = _nt_dot(q, k_pad[parity, band, lanes]) + bias_ref[2 * slab + parity]
            e, inv = _band_softmax(s, key_ok)
            o = jnp.dot(e, v_pad[parity, band, lanes], preferred_element_type=jnp.float32) * inv
            out = o if out is None else out + o
        o_ref[0, :, lanes] = out.astype(o_ref.dtype)


def _attn_c(proj3, bias):
    b, s, _ = proj3.shape
    width = C_HEADS * HEAD_DIM
    return pl.pallas_call(
        _attn_c_kernel,
        grid=(b, s // Q_TILE),
        in_specs=[
            pl.BlockSpec((1, Q_TILE, width), lambda bi, i: (bi, i, SEC_CQ // width)),
            pl.BlockSpec((1, s, width), lambda bi, i: (bi, 0, SEC_CK // width)),
            pl.BlockSpec((1, s, width), lambda bi, i: (bi, 0, SEC_CV // width)),
            pl.BlockSpec((C_HEADS, Q_TILE, C_BAND), lambda bi, i: (0, 0, 0)),
        ],
        out_specs=pl.BlockSpec((1, Q_TILE, width), lambda bi, i: (bi, i, 0)),
        out_shape=jax.ShapeDtypeStruct((b, s, width), jnp.bfloat16),
        scratch_shapes=[pltpu.VMEM((2, s + C_PAD, width), jnp.bfloat16),
                        pltpu.VMEM((2, s + C_PAD, width), jnp.bfloat16)],
        compiler_params=pltpu.CompilerParams(dimension_semantics=("arbitrary", "arbitrary"),
                                             vmem_limit_bytes=VMEM_LIMIT),
        name="attn_c",
    )(proj3, proj3, proj3, bias)


def _out_proj_kernel(oa_ref, ob_ref, oc_ref, x_ref, ga_ref, gb_ref, gc_ref, w_ref, gf_ref, wr_ref, br_ref,
                     xo_ref, h_ref, dest_ref, tg_ref, cnt_ref, count):
    i = pl.program_id(0)
    cap = pl.num_programs(0) * TOKEN_TILE

    @pl.when(i == 0)
    def _():
        count[...] = jnp.zeros_like(count)

    acc = x_ref[...]
    parts = ((oa_ref, ga_ref, 0, 512), (ob_ref, gb_ref, 512, 256), (oc_ref, gc_ref, 768, 256))
    for ref, g_ref, off, width in parts:
        o = _rms(ref[...].astype(jnp.float32), g_ref[...]).astype(jnp.bfloat16)
        acc = acc + jnp.dot(o, w_ref[off:off + width, :], preferred_element_type=jnp.float32)
    xo_ref[...] = acc
    h = _rms(acc, gf_ref[...])
    _store_row_tiles(h_ref, h)
    logits = jnp.dot(h, wr_ref[...], preferred_element_type=jnp.float32,
                     precision=lax.Precision.HIGHEST) + br_ref[...]
    lane = lax.broadcasted_iota(jnp.int32, logits.shape, 1)
    work = jnp.where(lane < N_EXPERTS, logits, -jnp.inf)
    gate_out = jnp.zeros(logits.shape, jnp.float32)
    denom = jnp.zeros((logits.shape[0], 1), jnp.float32)
    top = None
    picks = []
    for k in range(TOP_K):
        m = jnp.max(work, axis=-1, keepdims=True)
        idx = jnp.min(jnp.where(work == m, lane, ROUTER_PAD), axis=-1, keepdims=True)
        pick = lane == idx
        work = jnp.where(pick, -jnp.inf, work)
        top = m if top is None else top
        e = jnp.exp(m - top)
        denom = denom + e
        gate_out = jnp.where(lane == k, e, gate_out)
        picks.append((idx, pick))
    tg_ref[...] = gate_out / denom
    chosen = jnp.zeros(logits.shape, jnp.float32)
    for _, pick in picks:
        chosen = jnp.where(pick, 1.0, chosen)
    r_i = lax.broadcasted_iota(jnp.int32, (TOKEN_TILE, TOKEN_TILE), 0)
    c_i = lax.broadcasted_iota(jnp.int32, (TOKEN_TILE, TOKEN_TILE), 1)
    earlier = jnp.where(c_i < r_i, 1.0, 0.0).astype(jnp.bfloat16)
    before = jnp.dot(earlier, chosen.astype(jnp.bfloat16), preferred_element_type=jnp.float32) + count[...]
    dest_out = jnp.zeros(logits.shape, jnp.int32)
    for k, (idx, pick) in enumerate(picks):
        rank = jnp.sum(jnp.where(pick, before, 0.0), axis=-1, keepdims=True).astype(jnp.int32)
        dest_out = jnp.where(lane == k, idx * cap + rank, dest_out)
    dest_ref[...] = dest_out
    count[...] = count[...] + jnp.sum(chosen, axis=0, keepdims=True)
    cnt_ref[...] = count[...].astype(jnp.int32)


def _out_proj(oa, ob, oc, x2, ga, gb, gc, w, gf, wr, br):
    t = x2.shape[0]
    row = lambda width: pl.BlockSpec((TOKEN_TILE, width), lambda i: (i, 0))
    full = lambda shape: pl.BlockSpec(shape, lambda i: (0, 0))
    return pl.pallas_call(
        _out_proj_kernel,
        grid=(t // TOKEN_TILE,),
        in_specs=[row(512), row(256), row(256), row(D_MODEL), full((1, 512)), full((1, 256)), full((1, 256)),
                  full((D_MODEL, D_MODEL)), full((1, D_MODEL)), full((D_MODEL, ROUTER_PAD)), full((1, ROUTER_PAD))],
        out_specs=[row(D_MODEL), pl.BlockSpec((TOKEN_TILE * LANE_TILES, 128), lambda i: (i, 0)),
                   row(ROUTER_PAD), row(ROUTER_PAD), full((1, ROUTER_PAD))],
        out_shape=[jax.ShapeDtypeStruct((t, D_MODEL), jnp.float32),
                   jax.ShapeDtypeStruct((t * LANE_TILES, 128), jnp.float32),
                   jax.ShapeDtypeStruct((t, ROUTER_PAD), jnp.int32),
                   jax.ShapeDtypeStruct((t, ROUTER_PAD), jnp.float32),
                   jax.ShapeDtypeStruct((1, ROUTER_PAD), jnp.int32)],
        scratch_shapes=[pltpu.VMEM((1, ROUTER_PAD), jnp.float32)],
        compiler_params=pltpu.CompilerParams(dimension_semantics=("arbitrary",), vmem_limit_bytes=VMEM_LIMIT),
        name="out_proj",
    )(oa, ob, oc, x2, ga, gb, gc, w, gf, wr, br)


def _dispatch_kernel(dest_hbm, h_ref, x_hbm, idx_smem, isem, dsem):
    i = pl.program_id(0)
    nt = pl.num_programs(0)
    slot = i % 2

    def idx_copy(tile, s):
        return pltpu.make_async_copy(dest_hbm.at[tile], idx_smem.at[s], isem.at[s])

    @pl.when(i == 0)
    def _():
        idx_copy(0, 0).start()

    idx_copy(i, slot).wait()

    @pl.when(i + 1 < nt)
    def _():
        idx_copy(i + 1, 1 - slot).start()

    def body(r, carry):
        for k in range(TOP_K):
            dst = idx_smem[slot, r * TOP_K + k]
            pltpu.make_async_copy(_row_tile(h_ref, r), _row_tile(x_hbm, dst), dsem).start()
        return carry

    lax.fori_loop(0, TOKEN_TILE, body, 0, unroll=4)
    for _ in range(TOP_K):
        pltpu.make_async_copy(h_ref, x_hbm.at[pl.ds(0, TOKEN_TILE * LANE_TILES)], dsem).wait()


def _dispatch(dest_tiles, h3, n_rows):
    t = h3.shape[0] // LANE_TILES
    return pl.pallas_call(
        _dispatch_kernel,
        grid=(t // TOKEN_TILE,),
        in_specs=[pl.BlockSpec(memory_space=pl.ANY),
                  pl.BlockSpec((TOKEN_TILE * LANE_TILES, 128), lambda i: (i, 0))],
        out_specs=pl.BlockSpec(memory_space=pl.ANY),
        out_shape=jax.ShapeDtypeStruct((n_rows * LANE_TILES, 128), jnp.float32),
        scratch_shapes=[pltpu.SMEM((2, TOKEN_TILE * TOP_K), jnp.int32), pltpu.SemaphoreType.DMA((2,)),
                        pltpu.SemaphoreType.DMA],
        compiler_params=pltpu.CompilerParams(dimension_semantics=("arbitrary",), vmem_limit_bytes=VMEM_LIMIT),
        name="dispatch",
    )(dest_tiles, h3)


def _moe_ffn_kernel(blk_e_ref, blk_row_ref, blk_valid_ref, x_ref, wgu_ref, bgu_ref, wd_ref, bd_ref, y_ref,
                    wgu_bf, wd_bf):
    i = pl.program_id(0)
    prev = blk_e_ref[jnp.maximum(i - 1, 0)]
    changed = jnp.logical_or(i == 0, blk_e_ref[i] != prev)
    n_valid = blk_valid_ref[i]

    @pl.when(changed)
    def _():
        for c in range(4):
            cols = slice(c * 512, (c + 1) * 512)
            wgu_bf[:, cols] = wgu_ref[0, 0, :, cols].astype(jnp.bfloat16)
        for c in range(2):
            cols = slice(c * 512, (c + 1) * 512)
            wd_bf[:, cols] = wd_ref[0, 0, :, cols].astype(jnp.bfloat16)

    @pl.when(n_valid > 0)
    def _():
        live = lax.broadcasted_iota(jnp.int32, (EXPERT_ROWS, 128), 0) < n_valid
        x = jnp.concatenate([jnp.where(live, _load_row_tiles(x_ref, 0, EXPERT_ROWS, c), 0.0).astype(jnp.bfloat16)
                             for c in range(LANE_TILES)], axis=1)
        glu = jnp.dot(x, wgu_bf[:, 0:D_EXPERT], preferred_element_type=jnp.float32) + bgu_ref[0, 0, :, 0:D_EXPERT]
        lin = jnp.dot(x, wgu_bf[:, D_EXPERT:], preferred_element_type=jnp.float32) + bgu_ref[0, 0, :, D_EXPERT:]
        glu = jnp.minimum(glu, SWIGLU_LIMIT)
        lin = jnp.clip(lin, -SWIGLU_LIMIT, SWIGLU_LIMIT)
        act = glu * jax.nn.sigmoid(SWIGLU_ALPHA * glu) * (lin + 1.0)
        y = jnp.dot(act.astype(jnp.bfloat16), wd_bf[...], preferred_element_type=jnp.float32) + bd_ref[0, 0]
        _store_row_tiles(y_ref, y)

    @pl.when(n_valid <= 0)
    def _():
        y_ref[...] = jnp.zeros_like(y_ref)


def _moe_ffn(layer, blk_e, blk_row, blk_valid, x_rows, w_gu, b_gu, w_down, b_down):
    n_blocks = blk_e.shape[0]
    depth = w_gu.shape[0]
    rows = lambda i, be, br, bv: (br[i], 0)
    expert = lambda i, be, br, bv: (layer, be[i], 0, 0)
    grid_spec = pltpu.PrefetchScalarGridSpec(
        num_scalar_prefetch=3,
        grid=(n_blocks,),
        in_specs=[
            pl.BlockSpec((EXPERT_ROWS * LANE_TILES, 128), rows),
            pl.BlockSpec((1, 1, D_MODEL, 2 * D_EXPERT), expert),
            pl.BlockSpec((1, 1, 1, 2 * D_EXPERT), expert),
            pl.BlockSpec((1, 1, D_EXPERT, D_MODEL), expert),
            pl.BlockSpec((1, 1, 1, D_MODEL), expert),
        ],
        out_specs=pl.BlockSpec((EXPERT_ROWS * LANE_TILES, 128), rows),
        scratch_shapes=[pltpu.VMEM((D_MODEL, 2 * D_EXPERT), jnp.bfloat16),
                        pltpu.VMEM((D_EXPERT, D_MODEL), jnp.bfloat16)],
    )
    return pl.pallas_call(
        _moe_ffn_kernel,
        grid_spec=grid_spec,
        out_shape=jax.ShapeDtypeStruct(x_rows.shape, jnp.float32),
        compiler_params=pltpu.CompilerParams(dimension_semantics=("arbitrary",), vmem_limit_bytes=VMEM_LIMIT),
        name="moe_ffn",
    )(blk_e, blk_row, blk_valid, x_rows, w_gu, b_gu.reshape(depth, N_EXPERTS, 1, -1), w_down,
      b_down.reshape(depth, N_EXPERTS, 1, -1))


def _combine_kernel(dest_hbm, y_hbm, x_ref, g_ref, *rest, project):
    if project:
        gain_ref, w_ref, hg_ref, xo_ref, o_ref, ybuf, idx_smem, isem, gsem = rest
    else:
        xo_ref, ybuf, idx_smem, isem, gsem = rest
    i = pl.program_id(0)
    nt = pl.num_programs(0)
    slot = i % 2

    def idx_copy(tile, s):
        return pltpu.make_async_copy(dest_hbm.at[tile], idx_smem.at[s], isem.at[s])

    def issue_gather(s):
        def body(r, carry):
            for k in range(TOP_K):
                src = idx_smem[s, r * TOP_K + k]
                pltpu.make_async_copy(_row_tile(y_hbm, src), _row_tile(ybuf.at[s], k * TOKEN_TILE + r),
                                      gsem.at[s]).start()
            return carry
        lax.fori_loop(0, TOKEN_TILE, body, 0, unroll=4)

    @pl.when(i == 0)
    def _():
        idx_copy(0, 0).start()
        idx_copy(0, 0).wait()
        issue_gather(0)

        @pl.when(nt > 1)
        def _():
            idx_copy(1, 1).start()

    @pl.when(i + 1 < nt)
    def _():
        idx_copy(i + 1, 1 - slot).wait()
        issue_gather(1 - slot)

    pltpu.make_async_copy(y_hbm.at[pl.ds(0, TOP_K * TOKEN_TILE * LANE_TILES)], ybuf.at[slot], gsem.at[slot]).wait()

    @pl.when(i + 2 < nt)
    def _():
        idx_copy(i + 2, slot).start()

    g = g_ref[...]
    gates = [jnp.broadcast_to(g[:, k:k + 1], (TOKEN_TILE, 128)) for k in range(TOP_K)]
    cols = []
    for c in range(LANE_TILES):
        acc = x_ref[:, c * 128:(c + 1) * 128]
        for k in range(TOP_K):
            acc = acc + gates[k] * _load_row_tiles(ybuf.at[slot], k * TOKEN_TILE, TOKEN_TILE, c)
        cols.append(acc)
    x = jnp.concatenate(cols, axis=1)
    xo_ref[...] = x
    if project:
        _project(_rms(x, gain_ref[...]).astype(jnp.bfloat16), w_ref, hg_ref, o_ref)


def _combine_call(dest_tiles, y_rows, x2, gates, gain=None, w=None, head_gain=None):
    t = x2.shape[0]
    project = w is not None
    row = lambda width: pl.BlockSpec((TOKEN_TILE, width), lambda i: (i, 0))
    full = lambda shape: pl.BlockSpec(shape, lambda i: (0, 0))
    in_specs = [pl.BlockSpec(memory_space=pl.ANY), pl.BlockSpec(memory_space=pl.ANY), row(D_MODEL), row(ROUTER_PAD)]
    out_specs = [row(D_MODEL)]
    out_shape = [jax.ShapeDtypeStruct((t, D_MODEL), jnp.float32)]
    args = [dest_tiles, y_rows, x2, gates]
    if project:
        in_specs += [full((1, D_MODEL)), full((D_MODEL, PROJ_WIDTH)), full((1, PROJ_WIDTH))]
        out_specs.append(row(PROJ_WIDTH))
        out_shape.append(jax.ShapeDtypeStruct((t, PROJ_WIDTH), jnp.bfloat16))
        args += [gain, w, head_gain]
    out = pl.pallas_call(
        functools.partial(_combine_kernel, project=project),
        grid=(t // TOKEN_TILE,),
        in_specs=in_specs, out_specs=out_specs, out_shape=out_shape,
        scratch_shapes=[pltpu.VMEM((2, TOP_K * TOKEN_TILE * LANE_TILES, 128), jnp.float32),
                        pltpu.SMEM((2, TOKEN_TILE * TOP_K), jnp.int32),
                        pltpu.SemaphoreType.DMA((2,)), pltpu.SemaphoreType.DMA((2,))],
        compiler_params=pltpu.CompilerParams(dimension_semantics=("arbitrary",), vmem_limit_bytes=VMEM_LIMIT),
        name="combine_in_proj" if project else "combine",
    )(*args)
    return out if project else out[0]


def _t5_bucket(rel):
    half = T5_BUCKETS // 2
    max_exact = half // 2
    ret = np.where(rel > 0, half, 0)
    n = np.abs(rel)
    nf = np.maximum(n, 1).astype(np.float32)
    large = max_exact + (np.log(nf / max_exact) / math.log(T5_MAX_DIST / max_exact)
                         * (half - max_exact)).astype(np.int32)
    large = np.minimum(large, half - 1)
    return ret + np.where(n < max_exact, n, large)


def _band_ok(n_prev, band):
    q_chunk = np.arange(Q_TILE)[:, None] // CHUNK + n_prev
    k_chunk = np.arange(band)[None, :] // CHUNK
    return (k_chunk <= q_chunk) & (k_chunk >= q_chunk - n_prev)


def _toeplitz(u, band):
    heads, length = u.shape
    flat = jnp.broadcast_to(u[:, None, :], (heads, Q_TILE, length)).reshape(heads, Q_TILE * length)
    skew = flat[:, :Q_TILE * (length - 1)].reshape(heads, Q_TILE, length - 1)
    return skew[:, :, Q_TILE - 1:Q_TILE - 1 + band]


def _a_bias(t5_bias):
    rel = np.arange(Q_TILE + A_BAND) - (Q_TILE - 1) - A_PAD
    bias = _toeplitz(t5_bias[_t5_bucket(rel)].T.astype(jnp.float32), A_BAND)
    bias = jnp.where(_band_ok(A_PREV_CHUNKS, A_BAND)[None], bias, NEG_INF)
    pairs = [jnp.concatenate([bias[4 * hk + parity], bias[4 * hk + 2 + parity]], axis=0)
             for hk in range(A_KV_HEADS) for parity in range(2)]
    return jnp.stack(pairs)


def _c_bias(rel_table):
    dist = C_PAD + (Q_TILE - 1) - np.arange(Q_TILE + C_BAND)
    idx = np.clip(dist, -(CHUNK - 1), C_REL_CLIP) + (CHUNK - 1)
    bias = _toeplitz(rel_table[idx].T.astype(jnp.float32), C_BAND)
    return jnp.where(_band_ok(C_PREV_CHUNKS, C_BAND)[None], bias, NEG_INF)


def _rope_tables(seq):
    half = B_ROPE_DIM // 2
    inv_freq = ROPE_THETA ** (-(jnp.arange(half, dtype=jnp.float32) / half))
    ang = jnp.arange(seq, dtype=jnp.float32)[:, None] * inv_freq[None, :]
    cos, sin = jnp.cos(ang), jnp.sin(ang)
    zeros = jnp.zeros((seq, B_NOPE_DIM), jnp.float32)
    tail = jnp.zeros((seq, B_HEAD_PAD - B_QK_DIM), jnp.float32)
    zh = jnp.zeros((seq, half), jnp.float32)
    cos_t = jnp.concatenate([zeros + 1.0, cos, cos, tail + 1.0], axis=1)
    sa = jnp.concatenate([zeros, -sin, zh, tail], axis=1)
    sb = jnp.concatenate([zeros, zh, sin, tail], axis=1)
    return cos_t, sa, sb


def _regroup_w_in(w):
    splits = np.cumsum((512, 128, 128, 256, 128, 32, 256, 256))
    aq, ak, av, bcq, bckv, bkpe, cq, ck, cv = jnp.split(w, splits.tolist(), axis=1)
    z = lambda n: jnp.zeros((w.shape[0], n), w.dtype)
    return jnp.concatenate([aq, ak, av, bcq, bckv, z(64), bkpe, z(32), cq, ck, cv], axis=1).astype(jnp.bfloat16)


def _head_gain_row(a_q, a_k, c_q, c_k):
    scale = HEAD_DIM ** -0.5
    row = jnp.ones((PROJ_WIDTH,), jnp.float32)
    for sec, heads, gain in ((SEC_AQ, A_Q_HEADS, a_q * scale), (SEC_AKV, A_KV_HEADS, a_k),
                             (SEC_CQ, C_HEADS, c_q * scale), (SEC_CK, C_HEADS, c_k)):
        row = row.at[sec:sec + heads * HEAD_DIM].set(jnp.tile(gain, heads))
    return row.reshape(1, PROJ_WIDTH)


def _pad_heads(w, head_dim, keep):
    rank = w.shape[0]
    w = w.reshape(rank, B_HEADS, head_dim)[:, :, :keep]
    w = jnp.pad(w, ((0, 0), (0, 0), (0, B_HEAD_PAD - keep)))
    return w.reshape(rank, B_HEADS * B_HEAD_PAD).astype(jnp.bfloat16)


def _pad_lanes(g, offset, width):
    return jnp.pad(g, (offset, width - offset - g.shape[0])).reshape(1, width)


def _block_table(counts, t):
    tk = t * TOP_K
    blocks_per_expert = t // EXPERT_ROWS
    nblk = (counts + EXPERT_ROWS - 1) // EXPERT_ROWS
    blk_end = jnp.cumsum(nblk)
    blk_first = blk_end - nblk
    n_blocks = (tk + N_EXPERTS * (EXPERT_ROWS - 1) + EXPERT_ROWS - 1) // EXPERT_ROWS
    blk = jnp.arange(n_blocks, dtype=jnp.int32)
    blk_e = jnp.minimum(jnp.sum(blk[:, None] >= blk_end[None, :], axis=1), N_EXPERTS - 1).astype(jnp.int32)
    own = blk_e[:, None] == jnp.arange(N_EXPERTS, dtype=jnp.int32)[None, :]
    pick = lambda v: jnp.sum(jnp.where(own, v[None, :], 0), axis=1)
    j = blk - pick(blk_first)
    blk_valid = jnp.clip(pick(counts) - j * EXPERT_ROWS, 0, EXPERT_ROWS).astype(jnp.int32)
    spare = N_EXPERTS * blocks_per_expert
    blk_row = jnp.where(blk_valid > 0, blk_e * blocks_per_expert + j, spare).astype(jnp.int32)
    return blk_e, blk_row, blk_valid


def kernel(x, attn_norm, w_in, a_q_norm, a_k_norm, a_sinks, t5_bias, b_q_a_norm, b_w_q_b, b_kv_a_norm, b_w_kv_b,
           b_q_norm, b_k_norm, c_q_norm, c_k_norm, c_rel_bias, a_out_norm, b_out_norm, c_out_norm, w_out, ffn_norm,
           w_router, b_router, w_gate_up, b_gate_up, w_down, b_down):
    b, s, d = x.shape
    t = b * s
    depth = w_in.shape[0]
    a_bias = _a_bias(t5_bias)
    cos_t, sin_a, sin_b = _rope_tables(s)
    x2 = x.reshape(t, d)
    head_gain = lambda l: _head_gain_row(a_q_norm[l], a_k_norm[l], c_q_norm[l], c_k_norm[l])
    proj = _in_proj(x2, attn_norm[0].reshape(1, d), _regroup_w_in(w_in[0]), head_gain(0))
    for l in range(depth):
        proj3 = proj.reshape(b, s, PROJ_WIDTH)
        o_a = _attn_a(proj3, a_bias, a_sinks[l])
        kv_w = b_w_kv_b[l].reshape(B_KV_RANK, B_HEADS, B_NOPE_DIM + B_V_DIM)
        wv = kv_w[:, :, B_NOPE_DIM:].reshape(B_KV_RANK, B_HEADS * B_V_DIM).astype(jnp.bfloat16)
        qb, kb, vb = _latent_prep(
            proj, b_q_a_norm[l].reshape(1, -1), b_kv_a_norm[l].reshape(1, -1),
            _pad_heads(b_w_q_b[l], B_QK_DIM, B_QK_DIM), _pad_heads(b_w_kv_b[l], B_NOPE_DIM + B_V_DIM, B_NOPE_DIM), wv,
            _pad_lanes(b_q_norm[l], 0, B_HEAD_PAD), _pad_lanes(b_k_norm[l], 0, B_HEAD_PAD), cos_t, sin_a, sin_b, s)
        o_b = _attn_b(qb.reshape(b, s, -1), kb.reshape(b, s, -1), vb.reshape(b, s, -1))
        o_c = _attn_c(proj3, _c_bias(c_rel_bias[l]))
        wr = jnp.pad(w_router[l], ((0, 0), (0, ROUTER_PAD - N_EXPERTS)))
        br = jnp.pad(b_router[l], (0, ROUTER_PAD - N_EXPERTS)).reshape(1, ROUTER_PAD)
        x2, h3, dest, gates, counts = _out_proj(
            o_a.reshape(t, -1), o_b.reshape(t, -1), o_c.reshape(t, -1), x2,
            a_out_norm[l].reshape(1, -1), b_out_norm[l].reshape(1, -1), c_out_norm[l].reshape(1, -1),
            w_out[l].astype(jnp.bfloat16), ffn_norm[l].reshape(1, -1), wr, br)
        dest_tiles = dest[:, :TOP_K].reshape(t // TOKEN_TILE, TOKEN_TILE * TOP_K)
        blk_e, blk_row, blk_valid = _block_table(counts[0, :N_EXPERTS], t)
        n_rows = (N_EXPERTS * (t // EXPERT_ROWS) + 1) * EXPERT_ROWS
        x_rows = _dispatch(dest_tiles, h3, n_rows)
        y_rows = _moe_ffn(l, blk_e, blk_row, blk_valid, x_rows, w_gate_up, b_gate_up, w_down, b_down)
        if l + 1 < depth:
            x2, proj = _combine_call(dest_tiles, y_rows, x2, gates, attn_norm[l + 1].reshape(1, d),
                                     _regroup_w_in(w_in[l + 1]), head_gain(l + 1))
        else:
            x2 = _combine_call(dest_tiles, y_rows, x2, gates)
    return x2.reshape(b, s, d)
```

```python
import functools
import math

import jax
import jax.numpy as jnp
import numpy as np
from jax import lax
from jax.experimental import pallas as pl
from jax.experimental.pallas import tpu as pltpu

D_MODEL = 1024
CHUNK = 64
HEAD_DIM = 64
EPS = 1e-6
NEG_INF = -1e30
A_Q_HEADS = 8
A_KV_HEADS = 2
A_GROUP = 4
A_PREV_CHUNKS = 2
T5_BUCKETS = 32
T5_MAX_DIST = 128
B_HEADS = 4
B_Q_RANK = 256
B_KV_RANK = 128
B_NOPE_DIM = 64
B_ROPE_DIM = 32
B_QK_DIM = 96
B_V_DIM = 64
ROPE_THETA = 10000.0
C_HEADS = 4
C_PREV_CHUNKS = 8
C_REL_CLIP = 256
N_EXPERTS = 32
TOP_K = 4
D_EXPERT = 1024
SWIGLU_LIMIT = 7.0
SWIGLU_ALPHA = 1.702

SEC_AQ = 0
SEC_AKV = 512
SEC_BCQ = 768
SEC_BCKV = 1024
SEC_BKPE = 1152
SEC_CQ = 1280
SEC_CK = 1536
SEC_CV = 1792
PROJ_WIDTH = 2048
HEAD_NORM_SLABS = tuple(sec // 128 + j for sec, n in ((SEC_AQ, 4), (SEC_AKV, 1), (SEC_CQ, 2), (SEC_CK, 2))
                        for j in range(n))
B_HEAD_PAD = 128

Q_TILE = 128
A_BAND = Q_TILE + A_PREV_CHUNKS * CHUNK
C_BAND = Q_TILE + C_PREV_CHUNKS * CHUNK
A_PAD = A_PREV_CHUNKS * CHUNK
C_PAD = C_PREV_CHUNKS * CHUNK
B_TILE = 256
ROW_TILE = 512
EXPERT_ROWS = 256
ROUTER_PAD = 128
TOKEN_TILE = 256
LANE_TILES = D_MODEL // 128
VMEM_LIMIT = 48 * 1024 * 1024


def _rms(x, gain):
    return x * lax.rsqrt(jnp.mean(x * x, axis=-1, keepdims=True) + EPS) * gain


def _nt_dot(a, b):
    return lax.dot_general(a, b, (((1,), (1,)), ((), ())), preferred_element_type=jnp.float32)


def _load_row_tiles(ref, first_row, n_rows, c):
    return ref[pl.ds(first_row * LANE_TILES + c, n_rows, stride=LANE_TILES), :]


def _store_row_tiles(ref, value):
    for c in range(LANE_TILES):
        ref[pl.ds(c, value.shape[0], stride=LANE_TILES), :] = value[:, c * 128:(c + 1) * 128]


def _row_tile(ref, row):
    return ref.at[pl.ds(pl.multiple_of(row * LANE_TILES, LANE_TILES), LANE_TILES)]


def _project(h, w_ref, hg_ref, o_ref):
    li = lax.broadcasted_iota(jnp.int32, (128, 128), 0) // HEAD_DIM
    lj = lax.broadcasted_iota(jnp.int32, (128, 128), 1) // HEAD_DIM
    same_head = jnp.where(li == lj, 1.0, 0.0).astype(jnp.bfloat16)
    for n in range(PROJ_WIDTH // 512):
        acc = jnp.dot(h, w_ref[:, n * 512:(n + 1) * 512], preferred_element_type=jnp.float32)
        slabs = []
        for j in range(4):
            slab = n * 4 + j
            a = acc[:, j * 128:(j + 1) * 128]
            if slab in HEAD_NORM_SLABS:
                sq = a * a
                hi = sq.astype(jnp.bfloat16)
                lo = (sq - hi.astype(jnp.float32)).astype(jnp.bfloat16)
                ss = (jnp.dot(hi, same_head, preferred_element_type=jnp.float32)
                      + jnp.dot(lo, same_head, preferred_element_type=jnp.float32))
                a = a * lax.rsqrt(ss * (1.0 / HEAD_DIM) + EPS) * hg_ref[:, slab * 128:(slab + 1) * 128]
            slabs.append(a.astype(o_ref.dtype))
        o_ref[:, n * 512:(n + 1) * 512] = jnp.concatenate(slabs, axis=1)


def _in_proj_kernel(x_ref, g_ref, w_ref, hg_ref, o_ref):
    _project(_rms(x_ref[...], g_ref[...]).astype(jnp.bfloat16), w_ref, hg_ref, o_ref)


def _in_proj(x2, gain, w, head_gain):
    t = x2.shape[0]
    return pl.pallas_call(
        _in_proj_kernel,
        grid=(t // ROW_TILE,),
        in_specs=[
            pl.BlockSpec((ROW_TILE, D_MODEL), lambda i: (i, 0)),
            pl.BlockSpec((1, D_MODEL), lambda i: (0, 0)),
            pl.BlockSpec((D_MODEL, PROJ_WIDTH), lambda i: (0, 0)),
            pl.BlockSpec((1, PROJ_WIDTH), lambda i: (0, 0)),
        ],
        out_specs=pl.BlockSpec((ROW_TILE, PROJ_WIDTH), lambda i: (i, 0)),
        out_shape=jax.ShapeDtypeStruct((t, PROJ_WIDTH), jnp.bfloat16),
        compiler_params=pltpu.CompilerParams(dimension_semantics=("arbitrary",), vmem_limit_bytes=VMEM_LIMIT),
        name="in_proj",
    )(x2, gain, w, head_gain)


def _half_lanes(x, parity, fill=0.0):
    lane = lax.broadcasted_iota(jnp.int32, x.shape, x.ndim - 1) % 128
    keep = lane < HEAD_DIM if parity == 0 else lane >= HEAD_DIM
    return jnp.where(keep, x, jnp.full_like(x, fill))


def _tn_dot(a, b):
    return lax.dot_general(a, b, (((0,), (0,)), ((), ())), preferred_element_type=jnp.float32)


def _softmax_pv(s, key_ok, v_aug, sink=None, value_half=0):
    s = jnp.where(key_ok, s, NEG_INF)
    m = jnp.max(s, axis=0, keepdims=True)
    if sink is not None:
        m = jnp.maximum(m, sink)
    e = jnp.exp(s - m).astype(jnp.bfloat16)
    o = _tn_dot(e, v_aug)
    if sink is not None:
        row = lax.broadcasted_iota(jnp.int32, (8, s.shape[1]), 0)
        e_sink = jnp.where(row == 0, jnp.exp(sink - m), 0.0).astype(jnp.bfloat16)
        first = lax.broadcasted_iota(jnp.int32, (8, 128), 0) == 0
        sel = _half_lanes(jnp.where(first, 1.0, 0.0), 1 - value_half).astype(jnp.bfloat16)
        o = o + _tn_dot(e_sink, sel)
    return o


def _merge_halves(halves):
    lane = lax.broadcasted_iota(jnp.int32, halves[0].shape, 1)
    normed = [o / pltpu.roll(o, HEAD_DIM, 1) for o in halves]
    return jnp.where(lane < HEAD_DIM, normed[0], normed[1])


def _attn_a_kernel(q_ref, kv_ref, bias_ref, sink_ref, o_ref, k_pad, v_pad):
    i = pl.program_id(1)
    seq = kv_ref.shape[1]

    @pl.when(i == 0)
    def _():
        for n in range(2 * A_KV_HEADS):
            k_pad[n, 0:A_PAD, :] = jnp.zeros((A_PAD, 128), k_pad.dtype)
            v_pad[n, 0:A_PAD, :] = jnp.zeros((A_PAD, 128), v_pad.dtype)

        def fill(r, carry):
            rows = pl.ds(pl.multiple_of(r * 256, 256), 256)
            dst = pl.ds(pl.multiple_of(r * 256 + A_PAD, 128), 256)
            kv = kv_ref[0, rows, :].astype(jnp.float32)
            for src, pad, other in ((kv[:, 0:128], k_pad, 0.0), (kv[:, 128:256], v_pad, 1.0)):
                swapped = pltpu.roll(src, HEAD_DIM, 1)
                pad[0, dst, :] = _half_lanes(src, 0, other).astype(pad.dtype)
                pad[1, dst, :] = _half_lanes(swapped, 1, other).astype(pad.dtype)
                pad[2, dst, :] = _half_lanes(swapped, 0, other).astype(pad.dtype)
                pad[3, dst, :] = _half_lanes(src, 1, other).astype(pad.dtype)
            return carry

        lax.fori_loop(0, seq // 256, fill, 0)

    start = pl.multiple_of(i * Q_TILE, Q_TILE)
    band = pl.ds(start, A_BAND)
    key_ok = lax.broadcasted_iota(jnp.int32, (A_BAND, 2 * Q_TILE), 0) + (start - A_PAD) >= 0
    first_slab = lax.broadcasted_iota(jnp.int32, (1, 2 * Q_TILE), 1) < Q_TILE
    for hk in range(A_KV_HEADS):
        q2 = jnp.concatenate([q_ref[0, :, (2 * hk) * 128:(2 * hk + 1) * 128],
                              q_ref[0, :, (2 * hk + 1) * 128:(2 * hk + 2) * 128]], axis=0)
        halves = []
        for parity in range(2):
            sink = jnp.where(first_slab, sink_ref[4 * hk + parity], sink_ref[4 * hk + 2 + parity])
            s = _nt_dot(k_pad[2 * hk + parity, band, :], q2) + bias_ref[2 * hk + parity]
            halves.append(_softmax_pv(s, key_ok, v_pad[2 * hk + parity, band, :], sink, parity))
        out = _merge_halves(halves)
        o_ref[0, :, (2 * hk) * 128:(2 * hk + 1) * 128] = out[0:Q_TILE].astype(o_ref.dtype)
        o_ref[0, :, (2 * hk + 1) * 128:(2 * hk + 2) * 128] = out[Q_TILE:2 * Q_TILE].astype(o_ref.dtype)


def _attn_a(proj3, bias, sinks):
    b, s, _ = proj3.shape
    return pl.pallas_call(
        _attn_a_kernel,
        grid=(b, s // Q_TILE),
        in_specs=[
            pl.BlockSpec((1, Q_TILE, 512), lambda bi, i: (bi, i, SEC_AQ // 512)),
            pl.BlockSpec((1, s, 256), lambda bi, i: (bi, 0, SEC_AKV // 256)),
            pl.BlockSpec((2 * A_KV_HEADS, 2 * Q_TILE, A_BAND), lambda bi, i: (0, 0, 0)),
            pl.BlockSpec(memory_space=pltpu.SMEM),
        ],
        out_specs=pl.BlockSpec((1, Q_TILE, 512), lambda bi, i: (bi, i, 0)),
        out_shape=jax.ShapeDtypeStruct((b, s, 512), jnp.bfloat16),
        scratch_shapes=[pltpu.VMEM((2 * A_KV_HEADS, s + A_PAD, 128), jnp.bfloat16),
                        pltpu.VMEM((2 * A_KV_HEADS, s + A_PAD, 128), jnp.bfloat16)],
        compiler_params=pltpu.CompilerParams(dimension_semantics=("arbitrary", "arbitrary"),
                                             vmem_limit_bytes=VMEM_LIMIT),
        name="attn_a",
    )(proj3, proj3, bias, sinks)


def _latent_prep_kernel(cq_ref, ckv_ref, kpe_ref, qag_ref, kvag_ref, wq_ref, wk_ref, wv_ref,
                        qg_ref, kg_ref, cos_ref, sa_ref, sb_ref, qo_ref, ko_ref, vo_ref):
    lane = lax.broadcasted_iota(jnp.int32, (ROW_TILE, B_HEAD_PAD), 1)
    is_nope = lane < B_NOPE_DIM
    is_rope = jnp.logical_and(lane >= B_NOPE_DIM, lane < B_QK_DIM)
    cos = cos_ref[...]
    sa = sa_ref[...]
    sb = sb_ref[...]

    def split_norm(x, gain):
        sq = x * x
        nope = jnp.sum(jnp.where(is_nope, sq, 0.0), axis=-1, keepdims=True) * (1.0 / B_NOPE_DIM)
        rope = jnp.sum(jnp.where(is_rope, sq, 0.0), axis=-1, keepdims=True) * (1.0 / B_ROPE_DIM)
        r = jnp.where(is_nope, lax.rsqrt(nope + EPS), lax.rsqrt(rope + EPS))
        return x * r * gain

    def rotate(x):
        return x * cos + pltpu.roll(x, 112, 1) * sa + pltpu.roll(x, 16, 1) * sb

    cq = _rms(cq_ref[...].astype(jnp.float32), qag_ref[...]).astype(jnp.bfloat16)
    ckv = _rms(ckv_ref[...].astype(jnp.float32), kvag_ref[...]).astype(jnp.bfloat16)
    kpe = kpe_ref[...].astype(jnp.float32)
    kpe_n = rotate(split_norm(kpe, kg_ref[...]))
    kpe_n = jnp.where(is_rope, kpe_n, 0.0)
    v = jnp.dot(ckv, wv_ref[...], preferred_element_type=jnp.float32)
    for h in range(B_HEADS):
        pair = v[:, (h // 2) * 128:(h // 2 + 1) * 128]
        vo_ref[:, h * 128:(h + 1) * 128] = _half_lanes(pair, h % 2, 1.0).astype(vo_ref.dtype)
    qscale = B_QK_DIM ** -0.5
    for h in range(B_HEADS):
        cols = slice(h * B_HEAD_PAD, (h + 1) * B_HEAD_PAD)
        qh = jnp.dot(cq, wq_ref[:, cols], preferred_element_type=jnp.float32)
        qh = rotate(split_norm(qh, qg_ref[...])) * qscale
        qo_ref[:, cols] = qh.astype(qo_ref.dtype)
        kh = jnp.dot(ckv, wk_ref[:, cols], preferred_element_type=jnp.float32)
        kh = jnp.where(is_nope, split_norm(kh, kg_ref[...]), 0.0) + kpe_n
        ko_ref[:, cols] = kh.astype(ko_ref.dtype)


def _latent_prep(proj, q_a_gain, kv_a_gain, wq, wk, wv, q_gain, k_gain, cos, sa, sb, seq):
    t = proj.shape[0]
    pos_blocks = seq // ROW_TILE
    full = lambda shape: pl.BlockSpec(shape, lambda i: (0, 0))
    tab = pl.BlockSpec((ROW_TILE, B_HEAD_PAD), lambda i: (i % pos_blocks, 0))
    return pl.pallas_call(
        _latent_prep_kernel,
        grid=(t // ROW_TILE,),
        in_specs=[
            pl.BlockSpec((ROW_TILE, 256), lambda i: (i, SEC_BCQ // 256)),
            pl.BlockSpec((ROW_TILE, 128), lambda i: (i, SEC_BCKV // 128)),
            pl.BlockSpec((ROW_TILE, 128), lambda i: (i, SEC_BKPE // 128)),
            full((1, B_Q_RANK)), full((1, B_KV_RANK)),
            full((B_Q_RANK, B_HEADS * B_HEAD_PAD)), full((B_KV_RANK, B_HEADS * B_HEAD_PAD)),
            full((B_KV_RANK, B_HEADS * B_V_DIM)),
            full((1, B_HEAD_PAD)), full((1, B_HEAD_PAD)),
            tab, tab, tab,
        ],
        out_specs=[
            pl.BlockSpec((ROW_TILE, B_HEADS * B_HEAD_PAD), lambda i: (i, 0)),
            pl.BlockSpec((ROW_TILE, B_HEADS * B_HEAD_PAD), lambda i: (i, 0)),
            pl.BlockSpec((ROW_TILE, B_HEADS * B_HEAD_PAD), lambda i: (i, 0)),
        ],
        out_shape=[
            jax.ShapeDtypeStruct((t, B_HEADS * B_HEAD_PAD), jnp.bfloat16),
            jax.ShapeDtypeStruct((t, B_HEADS * B_HEAD_PAD), jnp.bfloat16),
            jax.ShapeDtypeStruct((t, B_HEADS * B_HEAD_PAD), jnp.bfloat16),
        ],
        compiler_params=pltpu.CompilerParams(dimension_semantics=("arbitrary",), vmem_limit_bytes=VMEM_LIMIT),
        name="latent_prep",
    )(proj, proj, proj, q_a_gain, kv_a_gain, wq, wk, wv, q_gain, k_gain, cos, sa, sb)


def _attn_b_kernel(q_ref, k_ref, v_ref, o_ref):
    i = pl.program_id(1)
    query_chunk = lax.broadcasted_iota(jnp.int32, (B_TILE, B_TILE), 0) // CHUNK
    key_chunk = lax.broadcasted_iota(jnp.int32, (B_TILE, B_TILE), 1) // CHUNK
    diag_ok = key_chunk <= query_chunk

    def step(j, carry, masked):
        rows = pl.ds(pl.multiple_of(j * B_TILE, B_TILE), B_TILE)
        out = []
        for h in range(B_HEADS):
            m, acc = carry[h]
            head = slice(h * B_HEAD_PAD, (h + 1) * B_HEAD_PAD)
            s = _nt_dot(q_ref[0, :, head], k_ref[0, rows, head])
            if masked:
                s = jnp.where(diag_ok, s, NEG_INF)
            m_new = jnp.maximum(m, jnp.max(s, axis=-1, keepdims=True))
            e = jnp.exp(s - m_new).astype(jnp.bfloat16)
            acc = jnp.exp(m - m_new) * acc + jnp.dot(e, v_ref[0, rows, head], preferred_element_type=jnp.float32)
            out.append((m_new, acc))
        return tuple(out)

    init = tuple((jnp.full((B_TILE, 1), NEG_INF, jnp.float32), jnp.zeros((B_TILE, B_HEAD_PAD), jnp.float32))
                 for _ in range(B_HEADS))
    carry = lax.fori_loop(0, i, lambda j, c: step(j, c, False), init)
    carry = step(i, carry, True)
    for pair in range(B_HEADS // 2):
        halves = [carry[2 * pair + parity][1] for parity in range(2)]
        o_ref[0, :, pair * 128:(pair + 1) * 128] = _merge_halves(halves).astype(o_ref.dtype)


def _attn_b(qb, kb, vb):
    b, s, _ = qb.shape
    return pl.pallas_call(
        _attn_b_kernel,
        grid=(b, s // B_TILE),
        in_specs=[
            pl.BlockSpec((1, B_TILE, B_HEADS * B_HEAD_PAD), lambda bi, i: (bi, i, 0)),
            pl.BlockSpec((1, s, B_HEADS * B_HEAD_PAD), lambda bi, i: (bi, 0, 0)),
            pl.BlockSpec((1, s, B_HEADS * B_HEAD_PAD), lambda bi, i: (bi, 0, 0)),
        ],
        out_specs=pl.BlockSpec((1, B_TILE, B_HEADS * B_V_DIM), lambda bi, i: (bi, i, 0)),
        out_shape=jax.ShapeDtypeStruct((b, s, B_HEADS * B_V_DIM), jnp.bfloat16),
        compiler_params=pltpu.CompilerParams(dimension_semantics=("arbitrary", "arbitrary"),
                                             vmem_limit_bytes=VMEM_LIMIT),
        name="attn_b",
    )(qb, kb, vb)


def _attn_c_kernel(q_ref, k_ref, v_ref, bias_ref, o_ref, k_pad, v_pad):
    i = pl.program_id(1)
    seq = k_ref.shape[1]
    width = C_HEADS * HEAD_DIM

    @pl.when(i == 0)
    def _():
        for parity in range(2):
            k_pad[parity, 0:C_PAD, :] = jnp.zeros((C_PAD, width), k_pad.dtype)
            v_pad[parity, 0:C_PAD, :] = jnp.zeros((C_PAD, width), v_pad.dtype)

        def fill(r, carry):
            rows = pl.ds(pl.multiple_of(r * 256, 256), 256)
            dst = pl.ds(pl.multiple_of(r * 256 + C_PAD, 256), 256)
            k = k_ref[0, rows, :]
            v = v_ref[0, rows, :]
            for parity in range(2):
                k_pad[parity, dst, :] = _half_lanes(k, parity)
                v_pad[parity, dst, :] = _half_lanes(v, parity, fill=1.0)
            return carry

        lax.fori_loop(0, seq // 256, fill, 0)

    start = pl.multiple_of(i * Q_TILE, Q_TILE)
    band = pl.ds(start, C_BAND)
    key_ok = lax.broadcasted_iota(jnp.int32, (C_BAND, Q_TILE), 0) + (start - C_PAD) >= 0
    for slab in range(width // 128):
        lanes = slice(slab * 128, (slab + 1) * 128)
        q = q_ref[0, :, lanes]
        halves = []
        for parity in range(2):
            s = _nt_dot(k_pad[parity, band, lanes], q) + bias_ref[2 * slab + parity]
            halves.append(_softmax_pv(s, key_ok, v_pad[parity, band, lanes]))
        o_ref[0, :, lanes] = _merge_halves(halves).astype(o_ref.dtype)


def _attn_c(proj3, bias):
    b, s, _ = proj3.shape
    width = C_HEADS * HEAD_DIM
    return pl.pallas_call(
        _attn_c_kernel,
        grid=(b, s // Q_TILE),
        in_specs=[
            pl.BlockSpec((1, Q_TILE, width), lambda bi, i: (bi, i, SEC_CQ // width)),
            pl.BlockSpec((1, s, width), lambda bi, i: (bi, 0, SEC_CK // width)),
            pl.BlockSpec((1, s, width), lambda bi, i: (bi, 0, SEC_CV // width)),
            pl.BlockSpec((C_HEADS, C_BAND, Q_TILE), lambda bi, i: (0, 0, 0)),
        ],
        out_specs=pl.BlockSpec((1, Q_TILE, width), lambda bi, i: (bi, i, 0)),
        out_shape=jax.ShapeDtypeStruct((b, s, width), jnp.bfloat16),
        scratch_shapes=[pltpu.VMEM((2, s + C_PAD, width), jnp.bfloat16),
                        pltpu.VMEM((2, s + C_PAD, width), jnp.bfloat16)],
        compiler_params=pltpu.CompilerParams(dimension_semantics=("arbitrary", "arbitrary"),
                                             vmem_limit_bytes=VMEM_LIMIT),
        name="attn_c",
    )(proj3, proj3, proj3, bias)


def _out_proj_kernel(oa_ref, ob_ref, oc_ref, x_ref, ga_ref, gb_ref, gc_ref, w_ref, gf_ref, wr_ref, br_ref,
                     xo_ref, h_ref, dest_ref, tg_ref, cnt_ref, count):
    i = pl.program_id(0)
    cap = pl.num_programs(0) * TOKEN_TILE

    @pl.when(i == 0)
    def _():
        count[...] = jnp.zeros_like(count)

    acc = x_ref[...]
    parts = ((oa_ref, ga_ref, 0, 512), (ob_ref, gb_ref, 512, 256), (oc_ref, gc_ref, 768, 256))
    for ref, g_ref, off, width in parts:
        o = _rms(ref[...].astype(jnp.float32), g_ref[...]).astype(jnp.bfloat16)
        acc = acc + jnp.dot(o, w_ref[off:off + width, :], preferred_element_type=jnp.float32)
    xo_ref[...] = acc
    h = _rms(acc, gf_ref[...])
    _store_row_tiles(h_ref, h)
    logits = jnp.dot(h, wr_ref[...], preferred_element_type=jnp.float32,
                     precision=lax.Precision.HIGHEST) + br_ref[...]
    lane = lax.broadcasted_iota(jnp.int32, logits.shape, 1)
    work = jnp.where(lane < N_EXPERTS, logits, -jnp.inf)
    gate_out = jnp.zeros(logits.shape, jnp.float32)
    denom = jnp.zeros((logits.shape[0], 1), jnp.float32)
    top = None
    picks = []
    for k in range(TOP_K):
        m = jnp.max(work, axis=-1, keepdims=True)
        idx = jnp.min(jnp.where(work == m, lane, ROUTER_PAD), axis=-1, keepdims=True)
        pick = lane == idx
        work = jnp.where(pick, -jnp.inf, work)
        top = m if top is None else top
        e = jnp.exp(m - top)
        denom = denom + e
        gate_out = jnp.where(lane == k, e, gate_out)
        picks.append((idx, pick))
    tg_ref[...] = gate_out / denom
    chosen = jnp.zeros(logits.shape, jnp.float32)
    for _, pick in picks:
        chosen = jnp.where(pick, 1.0, chosen)
    r_i = lax.broadcasted_iota(jnp.int32, (TOKEN_TILE, TOKEN_TILE), 0)
    c_i = lax.broadcasted_iota(jnp.int32, (TOKEN_TILE, TOKEN_TILE), 1)
    earlier = jnp.where(c_i < r_i, 1.0, 0.0).astype(jnp.bfloat16)
    before = jnp.dot(earlier, chosen.astype(jnp.bfloat16), preferred_element_type=jnp.float32) + count[...]
    dest_out = jnp.zeros(logits.shape, jnp.int32)
    for k, (idx, pick) in enumerate(picks):
        rank = jnp.sum(jnp.where(pick, before, 0.0), axis=-1, keepdims=True).astype(jnp.int32)
        dest_out = jnp.where(lane == k, idx * cap + rank, dest_out)
    dest_ref[...] = dest_out
    count[...] = count[...] + jnp.sum(chosen, axis=0, keepdims=True)
    cnt_ref[...] = count[...].astype(jnp.int32)


def _out_proj(oa, ob, oc, x2, ga, gb, gc, w, gf, wr, br):
    t = x2.shape[0]
    row = lambda width: pl.BlockSpec((TOKEN_TILE, width), lambda i: (i, 0))
    full = lambda shape: pl.BlockSpec(shape, lambda i: (0, 0))
    return pl.pallas_call(
        _out_proj_kernel,
        grid=(t // TOKEN_TILE,),
        in_specs=[row(512), row(256), row(256), row(D_MODEL), full((1, 512)), full((1, 256)), full((1, 256)),
                  full((D_MODEL, D_MODEL)), full((1, D_MODEL)), full((D_MODEL, ROUTER_PAD)), full((1, ROUTER_PAD))],
        out_specs=[row(D_MODEL), pl.BlockSpec((TOKEN_TILE * LANE_TILES, 128), lambda i: (i, 0)),
                   row(ROUTER_PAD), row(ROUTER_PAD), full((1, ROUTER_PAD))],
        out_shape=[jax.ShapeDtypeStruct((t, D_MODEL), jnp.float32),
                   jax.ShapeDtypeStruct((t * LANE_TILES, 128), jnp.float32),
                   jax.ShapeDtypeStruct((t, ROUTER_PAD), jnp.int32),
                   jax.ShapeDtypeStruct((t, ROUTER_PAD), jnp.float32),
                   jax.ShapeDtypeStruct((1, ROUTER_PAD), jnp.int32)],
        scratch_shapes=[pltpu.VMEM((1, ROUTER_PAD), jnp.float32)],
        compiler_params=pltpu.CompilerParams(dimension_semantics=("arbitrary",), vmem_limit_bytes=VMEM_LIMIT),
        name="out_proj",
    )(oa, ob, oc, x2, ga, gb, gc, w, gf, wr, br)


def _dispatch_kernel(dest_hbm, h_ref, x_hbm, idx_smem, isem, dsem):
    i = pl.program_id(0)
    nt = pl.num_programs(0)
    slot = i % 2

    def idx_copy(tile, s):
        return pltpu.make_async_copy(dest_hbm.at[tile], idx_smem.at[s], isem.at[s])

    @pl.when(i == 0)
    def _():
        idx_copy(0, 0).start()

    idx_copy(i, slot).wait()

    @pl.when(i + 1 < nt)
    def _():
        idx_copy(i + 1, 1 - slot).start()

    def body(r, carry):
        for k in range(TOP_K):
            dst = idx_smem[slot, r * TOP_K + k]
            pltpu.make_async_copy(_row_tile(h_ref, r), _row_tile(x_hbm, dst), dsem).start()
        return carry

    lax.fori_loop(0, TOKEN_TILE, body, 0, unroll=4)
    for _ in range(TOP_K):
        pltpu.make_async_copy(h_ref, x_hbm.at[pl.ds(0, TOKEN_TILE * LANE_TILES)], dsem).wait()


def _dispatch(dest_tiles, h3, n_rows):
    t = h3.shape[0] // LANE_TILES
    return pl.pallas_call(
        _dispatch_kernel,
        grid=(t // TOKEN_TILE,),
        in_specs=[pl.BlockSpec(memory_space=pl.ANY),
                  pl.BlockSpec((TOKEN_TILE * LANE_TILES, 128), lambda i: (i, 0))],
        out_specs=pl.BlockSpec(memory_space=pl.ANY),
        out_shape=jax.ShapeDtypeStruct((n_rows * LANE_TILES, 128), jnp.float32),
        scratch_shapes=[pltpu.SMEM((2, TOKEN_TILE * TOP_K), jnp.int32), pltpu.SemaphoreType.DMA((2,)),
                        pltpu.SemaphoreType.DMA],
        compiler_params=pltpu.CompilerParams(dimension_semantics=("arbitrary",), vmem_limit_bytes=VMEM_LIMIT),
        name="dispatch",
    )(dest_tiles, h3)


def _moe_ffn_kernel(blk_e_ref, blk_row_ref, blk_valid_ref, x_ref, wgu_ref, bgu_ref, wd_ref, bd_ref, y_ref,
                    wgu_bf, wd_bf):
    i = pl.program_id(0)
    prev = blk_e_ref[jnp.maximum(i - 1, 0)]
    changed = jnp.logical_or(i == 0, blk_e_ref[i] != prev)
    n_valid = blk_valid_ref[i]

    @pl.when(changed)
    def _():
        for c in range(4):
            cols = slice(c * 512, (c + 1) * 512)
            wgu_bf[:, cols] = wgu_ref[0, 0, :, cols].astype(jnp.bfloat16)
        for c in range(2):
            cols = slice(c * 512, (c + 1) * 512)
            wd_bf[:, cols] = wd_ref[0, 0, :, cols].astype(jnp.bfloat16)

    @pl.when(n_valid > 0)
    def _():
        live = lax.broadcasted_iota(jnp.int32, (EXPERT_ROWS, 128), 0) < n_valid
        x = jnp.concatenate([jnp.where(live, _load_row_tiles(x_ref, 0, EXPERT_ROWS, c), 0.0).astype(jnp.bfloat16)
                             for c in range(LANE_TILES)], axis=1)
        glu = jnp.dot(x, wgu_bf[:, 0:D_EXPERT], preferred_element_type=jnp.float32) + bgu_ref[0, 0, :, 0:D_EXPERT]
        lin = jnp.dot(x, wgu_bf[:, D_EXPERT:], preferred_element_type=jnp.float32) + bgu_ref[0, 0, :, D_EXPERT:]
        glu = jnp.minimum(glu, SWIGLU_LIMIT)
        lin = jnp.clip(lin, -SWIGLU_LIMIT, SWIGLU_LIMIT)
        act = glu * jax.nn.sigmoid(SWIGLU_ALPHA * glu) * (lin + 1.0)
        y = jnp.dot(act.astype(jnp.bfloat16), wd_bf[...], preferred_element_type=jnp.float32) + bd_ref[0, 0]
        _store_row_tiles(y_ref, y)

    @pl.when(n_valid <= 0)
    def _():
        y_ref[...] = jnp.zeros_like(y_ref)


def _moe_ffn(layer, blk_e, blk_row, blk_valid, x_rows, w_gu, b_gu, w_down, b_down):
    n_blocks = blk_e.shape[0]
    depth = w_gu.shape[0]
    rows = lambda i, be, br, bv: (br[i], 0)
    expert = lambda i, be, br, bv: (layer, be[i], 0, 0)
    grid_spec = pltpu.PrefetchScalarGridSpec(
        num_scalar_prefetch=3,
        grid=(n_blocks,),
        in_specs=[
            pl.BlockSpec((EXPERT_ROWS * LANE_TILES, 128), rows),
            pl.BlockSpec((1, 1, D_MODEL, 2 * D_EXPERT), expert),
            pl.BlockSpec((1, 1, 1, 2 * D_EXPERT), expert),
            pl.BlockSpec((1, 1, D_EXPERT, D_MODEL), expert),
            pl.BlockSpec((1, 1, 1, D_MODEL), expert),
        ],
        out_specs=pl.BlockSpec((EXPERT_ROWS * LANE_TILES, 128), rows),
        scratch_shapes=[pltpu.VMEM((D_MODEL, 2 * D_EXPERT), jnp.bfloat16),
                        pltpu.VMEM((D_EXPERT, D_MODEL), jnp.bfloat16)],
    )
    return pl.pallas_call(
        _moe_ffn_kernel,
        grid_spec=grid_spec,
        out_shape=jax.ShapeDtypeStruct(x_rows.shape, jnp.float32),
        compiler_params=pltpu.CompilerParams(dimension_semantics=("arbitrary",), vmem_limit_bytes=VMEM_LIMIT),
        name="moe_ffn",
    )(blk_e, blk_row, blk_valid, x_rows, w_gu, b_gu.reshape(depth, N_EXPERTS, 1, -1), w_down,
      b_down.reshape(depth, N_EXPERTS, 1, -1))


def _combine_kernel(dest_hbm, y_hbm, x_ref, g_ref, *rest, project):
    if project:
        gain_ref, w_ref, hg_ref, xo_ref, o_ref, ybuf, idx_smem, isem, gsem = rest
    else:
        xo_ref, ybuf, idx_smem, isem, gsem = rest
    i = pl.program_id(0)
    nt = pl.num_programs(0)
    slot = i % 2

    def idx_copy(tile, s):
        return pltpu.make_async_copy(dest_hbm.at[tile], idx_smem.at[s], isem.at[s])

    def issue_gather(s):
        def body(r, carry):
            for k in range(TOP_K):
                src = idx_smem[s, r * TOP_K + k]
                pltpu.make_async_copy(_row_tile(y_hbm, src), _row_tile(ybuf.at[s], k * TOKEN_TILE + r),
                                      gsem.at[s]).start()
            return carry
        lax.fori_loop(0, TOKEN_TILE, body, 0, unroll=4)

    @pl.when(i == 0)
    def _():
        idx_copy(0, 0).start()
        idx_copy(0, 0).wait()
        issue_gather(0)

        @pl.when(nt > 1)
        def _():
            idx_copy(1, 1).start()

    @pl.when(i + 1 < nt)
    def _():
        idx_copy(i + 1, 1 - slot).wait()
        issue_gather(1 - slot)

    pltpu.make_async_copy(y_hbm.at[pl.ds(0, TOP_K * TOKEN_TILE * LANE_TILES)], ybuf.at[slot], gsem.at[slot]).wait()

    @pl.when(i + 2 < nt)
    def _():
        idx_copy(i + 2, slot).start()

    g = g_ref[...]
    gates = [jnp.broadcast_to(g[:, k:k + 1], (TOKEN_TILE, 128)) for k in range(TOP_K)]
    cols = []
    for c in range(LANE_TILES):
        acc = x_ref[:, c * 128:(c + 1) * 128]
        for k in range(TOP_K):
            acc = acc + gates[k] * _load_row_tiles(ybuf.at[slot], k * TOKEN_TILE, TOKEN_TILE, c)
        cols.append(acc)
    x = jnp.concatenate(cols, axis=1)
    xo_ref[...] = x
    if project:
        _project(_rms(x, gain_ref[...]).astype(jnp.bfloat16), w_ref, hg_ref, o_ref)


def _combine_call(dest_tiles, y_rows, x2, gates, gain=None, w=None, head_gain=None):
    t = x2.shape[0]
    project = w is not None
    row = lambda width: pl.BlockSpec((TOKEN_TILE, width), lambda i: (i, 0))
    full = lambda shape: pl.BlockSpec(shape, lambda i: (0, 0))
    in_specs = [pl.BlockSpec(memory_space=pl.ANY), pl.BlockSpec(memory_space=pl.ANY), row(D_MODEL), row(ROUTER_PAD)]
    out_specs = [row(D_MODEL)]
    out_shape = [jax.ShapeDtypeStruct((t, D_MODEL), jnp.float32)]
    args = [dest_tiles, y_rows, x2, gates]
    if project:
        in_specs += [full((1, D_MODEL)), full((D_MODEL, PROJ_WIDTH)), full((1, PROJ_WIDTH))]
        out_specs.append(row(PROJ_WIDTH))
        out_shape.append(jax.ShapeDtypeStruct((t, PROJ_WIDTH), jnp.bfloat16))
        args += [gain, w, head_gain]
    out = pl.pallas_call(
        functools.partial(_combine_kernel, project=project),
        grid=(t // TOKEN_TILE,),
        in_specs=in_specs, out_specs=out_specs, out_shape=out_shape,
        scratch_shapes=[pltpu.VMEM((2, TOP_K * TOKEN_TILE * LANE_TILES, 128), jnp.float32),
                        pltpu.SMEM((2, TOKEN_TILE * TOP_K), jnp.int32),
                        pltpu.SemaphoreType.DMA((2,)), pltpu.SemaphoreType.DMA((2,))],
        compiler_params=pltpu.CompilerParams(dimension_semantics=("arbitrary",), vmem_limit_bytes=VMEM_LIMIT),
        name="combine_in_proj" if project else "combine",
    )(*args)
    return out if project else out[0]


def _t5_bucket(rel):
    half = T5_BUCKETS // 2
    max_exact = half // 2
    ret = np.where(rel > 0, half, 0)
    n = np.abs(rel)
    nf = np.maximum(n, 1).astype(np.float32)
    large = max_exact + (np.log(nf / max_exact) / math.log(T5_MAX_DIST / max_exact)
                         * (half - max_exact)).astype(np.int32)
    large = np.minimum(large, half - 1)
    return ret + np.where(n < max_exact, n, large)


def _band_ok(n_prev, band):
    q_chunk = np.arange(Q_TILE)[:, None] // CHUNK + n_prev
    k_chunk = np.arange(band)[None, :] // CHUNK
    return (k_chunk <= q_chunk) & (k_chunk >= q_chunk - n_prev)


def _toeplitz(u, band):
    heads, length = u.shape
    flat = jnp.broadcast_to(u[:, None, :], (heads, Q_TILE, length)).reshape(heads, Q_TILE * length)
    skew = flat[:, :Q_TILE * (length - 1)].reshape(heads, Q_TILE, length - 1)
    return skew[:, :, Q_TILE - 1:Q_TILE - 1 + band]


def _a_bias(t5_bias):
    rel = np.arange(Q_TILE + A_BAND) - (Q_TILE - 1) - A_PAD
    bias = _toeplitz(t5_bias[_t5_bucket(rel)].T.astype(jnp.float32), A_BAND)
    bias = jnp.where(_band_ok(A_PREV_CHUNKS, A_BAND)[None], bias, NEG_INF)
    pairs = [jnp.concatenate([bias[4 * hk + parity], bias[4 * hk + 2 + parity]], axis=0)
             for hk in range(A_KV_HEADS) for parity in range(2)]
    return jnp.stack(pairs).transpose(0, 2, 1)


def _c_bias(rel_table):
    dist = C_PAD + (Q_TILE - 1) - np.arange(Q_TILE + C_BAND)
    idx = np.clip(dist, -(CHUNK - 1), C_REL_CLIP) + (CHUNK - 1)
    bias = _toeplitz(rel_table[idx].T.astype(jnp.float32), C_BAND)
    return jnp.where(_band_ok(C_PREV_CHUNKS, C_BAND)[None], bias, NEG_INF).transpose(0, 2, 1)


def _rope_tables(seq):
    half = B_ROPE_DIM // 2
    inv_freq = ROPE_THETA ** (-(jnp.arange(half, dtype=jnp.float32) / half))
    ang = jnp.arange(seq, dtype=jnp.float32)[:, None] * inv_freq[None, :]
    cos, sin = jnp.cos(ang), jnp.sin(ang)
    zeros = jnp.zeros((seq, B_NOPE_DIM), jnp.float32)
    tail = jnp.zeros((seq, B_HEAD_PAD - B_QK_DIM), jnp.float32)
    zh = jnp.zeros((seq, half), jnp.float32)
    cos_t = jnp.concatenate([zeros + 1.0, cos, cos, tail + 1.0], axis=1)
    sa = jnp.concatenate([zeros, -sin, zh, tail], axis=1)
    sb = jnp.concatenate([zeros, zh, sin, tail], axis=1)
    return cos_t, sa, sb


def _regroup_w_in(w):
    splits = np.cumsum((512, 128, 128, 256, 128, 32, 256, 256))
    aq, ak, av, bcq, bckv, bkpe, cq, ck, cv = jnp.split(w, splits.tolist(), axis=1)
    z = lambda n: jnp.zeros((w.shape[0], n), w.dtype)
    return jnp.concatenate([aq, ak, av, bcq, bckv, z(64), bkpe, z(32), cq, ck, cv], axis=1).astype(jnp.bfloat16)


def _head_gain_row(a_q, a_k, c_q, c_k):
    scale = HEAD_DIM ** -0.5
    row = jnp.ones((PROJ_WIDTH,), jnp.float32)
    for sec, heads, gain in ((SEC_AQ, A_Q_HEADS, a_q * scale), (SEC_AKV, A_KV_HEADS, a_k),
                             (SEC_CQ, C_HEADS, c_q * scale), (SEC_CK, C_HEADS, c_k)):
        row = row.at[sec:sec + heads * HEAD_DIM].set(jnp.tile(gain, heads))
    return row.reshape(1, PROJ_WIDTH)


def _pad_heads(w, head_dim, keep):
    rank = w.shape[0]
    w = w.reshape(rank, B_HEADS, head_dim)[:, :, :keep]
    w = jnp.pad(w, ((0, 0), (0, 0), (0, B_HEAD_PAD - keep)))
    return w.reshape(rank, B_HEADS * B_HEAD_PAD).astype(jnp.bfloat16)


def _pad_lanes(g, offset, width):
    return jnp.pad(g, (offset, width - offset - g.shape[0])).reshape(1, width)


def _block_table(counts, t):
    tk = t * TOP_K
    blocks_per_expert = t // EXPERT_ROWS
    nblk = (counts + EXPERT_ROWS - 1) // EXPERT_ROWS
    blk_end = jnp.cumsum(nblk)
    blk_first = blk_end - nblk
    n_blocks = (tk + N_EXPERTS * (EXPERT_ROWS - 1) + EXPERT_ROWS - 1) // EXPERT_ROWS
    blk = jnp.arange(n_blocks, dtype=jnp.int32)
    blk_e = jnp.minimum(jnp.sum(blk[:, None] >= blk_end[None, :], axis=1), N_EXPERTS - 1).astype(jnp.int32)
    own = blk_e[:, None] == jnp.arange(N_EXPERTS, dtype=jnp.int32)[None, :]
    pick = lambda v: jnp.sum(jnp.where(own, v[None, :], 0), axis=1)
    j = blk - pick(blk_first)
    blk_valid = jnp.clip(pick(counts) - j * EXPERT_ROWS, 0, EXPERT_ROWS).astype(jnp.int32)
    spare = N_EXPERTS * blocks_per_expert
    blk_row = jnp.where(blk_valid > 0, blk_e * blocks_per_expert + j, spare).astype(jnp.int32)
    return blk_e, blk_row, blk_valid


def kernel(x, attn_norm, w_in, a_q_norm, a_k_norm, a_sinks, t5_bias, b_q_a_norm, b_w_q_b, b_kv_a_norm, b_w_kv_b,
           b_q_norm, b_k_norm, c_q_norm, c_k_norm, c_rel_bias, a_out_norm, b_out_norm, c_out_norm, w_out, ffn_norm,
           w_router, b_router, w_gate_up, b_gate_up, w_down, b_down):
    b, s, d = x.shape
    t = b * s
    depth = w_in.shape[0]
    a_bias = _a_bias(t5_bias)
    cos_t, sin_a, sin_b = _rope_tables(s)
    x2 = x.reshape(t, d)
    head_gain = lambda l: _head_gain_row(a_q_norm[l], a_k_norm[l], c_q_norm[l], c_k_norm[l])
    proj = _in_proj(x2, attn_norm[0].reshape(1, d), _regroup_w_in(w_in[0]), head_gain(0))
    for l in range(depth):
        proj3 = proj.reshape(b, s, PROJ_WIDTH)
        o_a = _attn_a(proj3, a_bias, a_sinks[l])
        kv_w = b_w_kv_b[l].reshape(B_KV_RANK, B_HEADS, B_NOPE_DIM + B_V_DIM)
        wv = kv_w[:, :, B_NOPE_DIM:].reshape(B_KV_RANK, B_HEADS * B_V_DIM).astype(jnp.bfloat16)
        qb, kb, vb = _latent_prep(
            proj, b_q_a_norm[l].reshape(1, -1), b_kv_a_norm[l].reshape(1, -1),
            _pad_heads(b_w_q_b[l], B_QK_DIM, B_QK_DIM), _pad_heads(b_w_kv_b[l], B_NOPE_DIM + B_V_DIM, B_NOPE_DIM), wv,
            _pad_lanes(b_q_norm[l], 0, B_HEAD_PAD), _pad_lanes(b_k_norm[l], 0, B_HEAD_PAD), cos_t, sin_a, sin_b, s)
        o_b = _attn_b(qb.reshape(b, s, -1), kb.reshape(b, s, -1), vb.reshape(b, s, -1))
        o_c = _attn_c(proj3, _c_bias(c_rel_bias[l]))
        wr = jnp.pad(w_router[l], ((0, 0), (0, ROUTER_PAD - N_EXPERTS)))
        br = jnp.pad(b_router[l], (0, ROUTER_PAD - N_EXPERTS)).reshape(1, ROUTER_PAD)
        x2, h3, dest, gates, counts = _out_proj(
            o_a.reshape(t, -1), o_b.reshape(t, -1), o_c.reshape(t, -1), x2,
            a_out_norm[l].reshape(1, -1), b_out_norm[l].reshape(1, -1), c_out_norm[l].reshape(1, -1),
            w_out[l].astype(jnp.bfloat16), ffn_norm[l].reshape(1, -1), wr, br)
        dest_tiles = dest[:, :TOP_K].reshape(t // TOKEN_TILE, TOKEN_TILE * TOP_K)
        blk_e, blk_row, blk_valid = _block_table(counts[0, :N_EXPERTS], t)
        n_rows = (N_EXPERTS * (t // EXPERT_ROWS) + 1) * EXPERT_ROWS
        x_rows = _dispatch(dest_tiles, h3, n_rows)
        y_rows = _moe_ffn(l, blk_e, blk_row, blk_valid, x_rows, w_gate_up, b_gate_up, w_down, b_down)
        if l + 1 < depth:
            x2, proj = _combine_call(dest_tiles, y_rows, x2, gates, attn_norm[l + 1].reshape(1, d),
                                     _regroup_w_in(w_in[l + 1]), head_gain(l + 1))
        else:
            x2 = _combine_call(dest_tiles, y_rows, x2, gates)
    return x2.reshape(b, s, d)
```

```python
import functools
import math

import jax
import jax.numpy as jnp
import numpy as np
from jax import lax
from jax.experimental import pallas as pl
from jax.experimental.pallas import tpu as pltpu

D_MODEL = 1024
CHUNK = 64
HEAD_DIM = 64
EPS = 1e-6
NEG_INF = -1e30
A_Q_HEADS = 8
A_KV_HEADS = 2
A_GROUP = 4
A_PREV_CHUNKS = 2
T5_BUCKETS = 32
T5_MAX_DIST = 128
B_HEADS = 4
B_Q_RANK = 256
B_KV_RANK = 128
B_NOPE_DIM = 64
B_ROPE_DIM = 32
B_QK_DIM = 96
B_V_DIM = 64
ROPE_THETA = 10000.0
C_HEADS = 4
C_PREV_CHUNKS = 8
C_REL_CLIP = 256
N_EXPERTS = 32
TOP_K = 4
D_EXPERT = 1024
SWIGLU_LIMIT = 7.0
SWIGLU_ALPHA = 1.702

SEC_AQ = 0
SEC_AKV = 512
SEC_BCQ = 768
SEC_BCKV = 1024
SEC_BKPE = 1152
SEC_CQ = 1280
SEC_CK = 1536
SEC_CV = 1792
PROJ_WIDTH = 2048
HEAD_NORM_SLABS = tuple(sec // 128 + j for sec, n in ((SEC_AQ, 4), (SEC_AKV, 1), (SEC_CQ, 2), (SEC_CK, 2))
                        for j in range(n))
B_HEAD_PAD = 128

Q_TILE = 128
A_BAND = Q_TILE + A_PREV_CHUNKS * CHUNK
C_BAND = Q_TILE + C_PREV_CHUNKS * CHUNK
A_PAD = A_PREV_CHUNKS * CHUNK
C_PAD = C_PREV_CHUNKS * CHUNK
B_TILE = 256
ROW_TILE = 512
EXPERT_ROWS = 256
ROUTER_PAD = 128
TOKEN_TILE = 256
TILE_SLOTS = TOKEN_TILE * TOP_K
LANE_TILES = D_MODEL // 128
VMEM_LIMIT = 48 * 1024 * 1024


def _rms(x, gain):
    return x * lax.rsqrt(jnp.mean(x * x, axis=-1, keepdims=True) + EPS) * gain


def _nt_dot(a, b):
    return lax.dot_general(a, b, (((1,), (1,)), ((), ())), preferred_element_type=jnp.float32)


def _load_row_tiles(ref, first_row, n_rows, c):
    return ref[pl.ds(first_row * LANE_TILES + c, n_rows, stride=LANE_TILES), :]


def _store_row_tiles(ref, value):
    for c in range(LANE_TILES):
        ref[pl.ds(c, value.shape[0], stride=LANE_TILES), :] = value[:, c * 128:(c + 1) * 128]


def _row_tile(ref, row):
    return ref.at[pl.ds(pl.multiple_of(row * LANE_TILES, LANE_TILES), LANE_TILES)]


def _index_slot(idx_smem, s):
    return idx_smem.at[pl.ds(pl.multiple_of(s * TILE_SLOTS, TILE_SLOTS), TILE_SLOTS)]


def _project(h, w_ref, hg_ref, o_ref):
    li = lax.broadcasted_iota(jnp.int32, (128, 128), 0) // HEAD_DIM
    lj = lax.broadcasted_iota(jnp.int32, (128, 128), 1) // HEAD_DIM
    same_head = jnp.where(li == lj, 1.0, 0.0).astype(jnp.bfloat16)
    for n in range(PROJ_WIDTH // 512):
        acc = jnp.dot(h, w_ref[:, n * 512:(n + 1) * 512], preferred_element_type=jnp.float32)
        slabs = []
        for j in range(4):
            slab = n * 4 + j
            a = acc[:, j * 128:(j + 1) * 128]
            if slab in HEAD_NORM_SLABS:
                sq = a * a
                hi = sq.astype(jnp.bfloat16)
                lo = (sq - hi.astype(jnp.float32)).astype(jnp.bfloat16)
                ss = (jnp.dot(hi, same_head, preferred_element_type=jnp.float32)
                      + jnp.dot(lo, same_head, preferred_element_type=jnp.float32))
                a = a * lax.rsqrt(ss * (1.0 / HEAD_DIM) + EPS) * hg_ref[:, slab * 128:(slab + 1) * 128]
            slabs.append(a.astype(o_ref.dtype))
        o_ref[:, n * 512:(n + 1) * 512] = jnp.concatenate(slabs, axis=1)


def _in_proj_kernel(x_ref, g_ref, w_ref, hg_ref, o_ref):
    _project(_rms(x_ref[...], g_ref[...]).astype(jnp.bfloat16), w_ref, hg_ref, o_ref)


def _in_proj(x2, gain, w, head_gain):
    t = x2.shape[0]
    return pl.pallas_call(
        _in_proj_kernel,
        grid=(t // ROW_TILE,),
        in_specs=[
            pl.BlockSpec((ROW_TILE, D_MODEL), lambda i: (i, 0)),
            pl.BlockSpec((1, D_MODEL), lambda i: (0, 0)),
            pl.BlockSpec((D_MODEL, PROJ_WIDTH), lambda i: (0, 0)),
            pl.BlockSpec((1, PROJ_WIDTH), lambda i: (0, 0)),
        ],
        out_specs=pl.BlockSpec((ROW_TILE, PROJ_WIDTH), lambda i: (i, 0)),
        out_shape=jax.ShapeDtypeStruct((t, PROJ_WIDTH), jnp.bfloat16),
        compiler_params=pltpu.CompilerParams(dimension_semantics=("arbitrary",), vmem_limit_bytes=VMEM_LIMIT),
        name="in_proj",
    )(x2, gain, w, head_gain)


def _half_lanes(x, parity, fill=0.0):
    lane = lax.broadcasted_iota(jnp.int32, x.shape, x.ndim - 1) % 128
    keep = lane < HEAD_DIM if parity == 0 else lane >= HEAD_DIM
    return jnp.where(keep, x, jnp.full_like(x, fill))


def _tn_dot(a, b):
    return lax.dot_general(a, b, (((0,), (0,)), ((), ())), preferred_element_type=jnp.float32)


def _softmax_pv(s, key_ok, v_aug, sink=None, value_half=0):
    s = jnp.where(key_ok, s, NEG_INF)
    m = jnp.max(s, axis=0, keepdims=True)
    if sink is not None:
        m = jnp.maximum(m, sink)
    e = jnp.exp(s - m).astype(jnp.bfloat16)
    o = _tn_dot(e, v_aug)
    if sink is not None:
        row = lax.broadcasted_iota(jnp.int32, (8, s.shape[1]), 0)
        e_sink = jnp.where(row == 0, jnp.exp(sink - m), 0.0).astype(jnp.bfloat16)
        first = lax.broadcasted_iota(jnp.int32, (8, 128), 0) == 0
        sel = _half_lanes(jnp.where(first, 1.0, 0.0), 1 - value_half).astype(jnp.bfloat16)
        o = o + _tn_dot(e_sink, sel)
    return o


def _merge_halves(halves):
    lane = lax.broadcasted_iota(jnp.int32, halves[0].shape, 1)
    normed = [o / pltpu.roll(o, HEAD_DIM, 1) for o in halves]
    return jnp.where(lane < HEAD_DIM, normed[0], normed[1])


def _attn_a_kernel(q_ref, kv_ref, bias_ref, sink_ref, o_ref, k_pad, v_pad):
    i = pl.program_id(1)
    seq = kv_ref.shape[1]

    @pl.when(i == 0)
    def _():
        for n in range(2 * A_KV_HEADS):
            k_pad[n, 0:A_PAD, :] = jnp.zeros((A_PAD, 128), k_pad.dtype)
            v_pad[n, 0:A_PAD, :] = jnp.zeros((A_PAD, 128), v_pad.dtype)

        def fill(r, carry):
            rows = pl.ds(pl.multiple_of(r * 256, 256), 256)
            dst = pl.ds(pl.multiple_of(r * 256 + A_PAD, 128), 256)
            kv = kv_ref[0, rows, :].astype(jnp.float32)
            for src, pad, other in ((kv[:, 0:128], k_pad, 0.0), (kv[:, 128:256], v_pad, 1.0)):
                swapped = pltpu.roll(src, HEAD_DIM, 1)
                pad[0, dst, :] = _half_lanes(src, 0, other).astype(pad.dtype)
                pad[1, dst, :] = _half_lanes(swapped, 1, other).astype(pad.dtype)
                pad[2, dst, :] = _half_lanes(swapped, 0, other).astype(pad.dtype)
                pad[3, dst, :] = _half_lanes(src, 1, other).astype(pad.dtype)
            return carry

        lax.fori_loop(0, seq // 256, fill, 0)

    start = pl.multiple_of(i * Q_TILE, Q_TILE)
    band = pl.ds(start, A_BAND)
    key_ok = lax.broadcasted_iota(jnp.int32, (A_BAND, 2 * Q_TILE), 0) + (start - A_PAD) >= 0
    first_slab = lax.broadcasted_iota(jnp.int32, (1, 2 * Q_TILE), 1) < Q_TILE
    for hk in range(A_KV_HEADS):
        q2 = jnp.concatenate([q_ref[0, :, (2 * hk) * 128:(2 * hk + 1) * 128],
                              q_ref[0, :, (2 * hk + 1) * 128:(2 * hk + 2) * 128]], axis=0)
        halves = []
        for parity in range(2):
            sink = jnp.where(first_slab, sink_ref[4 * hk + parity], sink_ref[4 * hk + 2 + parity])
            s = _nt_dot(k_pad[2 * hk + parity, band, :], q2) + bias_ref[2 * hk + parity]
            halves.append(_softmax_pv(s, key_ok, v_pad[2 * hk + parity, band, :], sink, parity))
        out = _merge_halves(halves)
        o_ref[0, :, (2 * hk) * 128:(2 * hk + 1) * 128] = out[0:Q_TILE].astype(o_ref.dtype)
        o_ref[0, :, (2 * hk + 1) * 128:(2 * hk + 2) * 128] = out[Q_TILE:2 * Q_TILE].astype(o_ref.dtype)


def _attn_a(proj3, bias, sinks):
    b, s, _ = proj3.shape
    return pl.pallas_call(
        _attn_a_kernel,
        grid=(b, s // Q_TILE),
        in_specs=[
            pl.BlockSpec((1, Q_TILE, 512), lambda bi, i: (bi, i, SEC_AQ // 512)),
            pl.BlockSpec((1, s, 256), lambda bi, i: (bi, 0, SEC_AKV // 256)),
            pl.BlockSpec((2 * A_KV_HEADS, 2 * Q_TILE, A_BAND), lambda bi, i: (0, 0, 0)),
            pl.BlockSpec(memory_space=pltpu.SMEM),
        ],
        out_specs=pl.BlockSpec((1, Q_TILE, 512), lambda bi, i: (bi, i, 0)),
        out_shape=jax.ShapeDtypeStruct((b, s, 512), jnp.bfloat16),
        scratch_shapes=[pltpu.VMEM((2 * A_KV_HEADS, s + A_PAD, 128), jnp.bfloat16),
                        pltpu.VMEM((2 * A_KV_HEADS, s + A_PAD, 128), jnp.bfloat16)],
        compiler_params=pltpu.CompilerParams(dimension_semantics=("arbitrary", "arbitrary"),
                                             vmem_limit_bytes=VMEM_LIMIT),
        name="attn_a",
    )(proj3, proj3, bias, sinks)


def _latent_prep_kernel(cq_ref, ckv_ref, kpe_ref, qag_ref, kvag_ref, wq_ref, wk_ref, wv_ref,
                        qg_ref, kg_ref, cos_ref, sa_ref, sb_ref, qo_ref, ko_ref, vo_ref):
    lane = lax.broadcasted_iota(jnp.int32, (ROW_TILE, B_HEAD_PAD), 1)
    is_nope = lane < B_NOPE_DIM
    is_rope = jnp.logical_and(lane >= B_NOPE_DIM, lane < B_QK_DIM)
    cos = cos_ref[...]
    sa = sa_ref[...]
    sb = sb_ref[...]

    def split_norm(x, gain):
        sq = x * x
        nope = jnp.sum(jnp.where(is_nope, sq, 0.0), axis=-1, keepdims=True) * (1.0 / B_NOPE_DIM)
        rope = jnp.sum(jnp.where(is_rope, sq, 0.0), axis=-1, keepdims=True) * (1.0 / B_ROPE_DIM)
        r = jnp.where(is_nope, lax.rsqrt(nope + EPS), lax.rsqrt(rope + EPS))
        return x * r * gain

    def rotate(x):
        return x * cos + pltpu.roll(x, 112, 1) * sa + pltpu.roll(x, 16, 1) * sb

    cq = _rms(cq_ref[...].astype(jnp.float32), qag_ref[...]).astype(jnp.bfloat16)
    ckv = _rms(ckv_ref[...].astype(jnp.float32), kvag_ref[...]).astype(jnp.bfloat16)
    kpe = kpe_ref[...].astype(jnp.float32)
    kpe_n = rotate(split_norm(kpe, kg_ref[...]))
    kpe_n = jnp.where(is_rope, kpe_n, 0.0)
    v = jnp.dot(ckv, wv_ref[...], preferred_element_type=jnp.float32)
    for h in range(B_HEADS):
        pair = v[:, (h // 2) * 128:(h // 2 + 1) * 128]
        vo_ref[:, h * 128:(h + 1) * 128] = _half_lanes(pair, h % 2, 1.0).astype(vo_ref.dtype)
    qscale = B_QK_DIM ** -0.5
    for h in range(B_HEADS):
        cols = slice(h * B_HEAD_PAD, (h + 1) * B_HEAD_PAD)
        qh = jnp.dot(cq, wq_ref[:, cols], preferred_element_type=jnp.float32)
        qh = rotate(split_norm(qh, qg_ref[...])) * qscale
        qo_ref[:, cols] = qh.astype(qo_ref.dtype)
        kh = jnp.dot(ckv, wk_ref[:, cols], preferred_element_type=jnp.float32)
        kh = jnp.where(is_nope, split_norm(kh, kg_ref[...]), 0.0) + kpe_n
        ko_ref[:, cols] = kh.astype(ko_ref.dtype)


def _latent_prep(proj, q_a_gain, kv_a_gain, wq, wk, wv, q_gain, k_gain, cos, sa, sb, seq):
    t = proj.shape[0]
    pos_blocks = seq // ROW_TILE
    full = lambda shape: pl.BlockSpec(shape, lambda i: (0, 0))
    tab = pl.BlockSpec((ROW_TILE, B_HEAD_PAD), lambda i: (i % pos_blocks, 0))
    return pl.pallas_call(
        _latent_prep_kernel,
        grid=(t // ROW_TILE,),
        in_specs=[
            pl.BlockSpec((ROW_TILE, 256), lambda i: (i, SEC_BCQ // 256)),
            pl.BlockSpec((ROW_TILE, 128), lambda i: (i, SEC_BCKV // 128)),
            pl.BlockSpec((ROW_TILE, 128), lambda i: (i, SEC_BKPE // 128)),
            full((1, B_Q_RANK)), full((1, B_KV_RANK)),
            full((B_Q_RANK, B_HEADS * B_HEAD_PAD)), full((B_KV_RANK, B_HEADS * B_HEAD_PAD)),
            full((B_KV_RANK, B_HEADS * B_V_DIM)),
            full((1, B_HEAD_PAD)), full((1, B_HEAD_PAD)),
            tab, tab, tab,
        ],
        out_specs=[
            pl.BlockSpec((ROW_TILE, B_HEADS * B_HEAD_PAD), lambda i: (i, 0)),
            pl.BlockSpec((ROW_TILE, B_HEADS * B_HEAD_PAD), lambda i: (i, 0)),
            pl.BlockSpec((ROW_TILE, B_HEADS * B_HEAD_PAD), lambda i: (i, 0)),
        ],
        out_shape=[
            jax.ShapeDtypeStruct((t, B_HEADS * B_HEAD_PAD), jnp.bfloat16),
            jax.ShapeDtypeStruct((t, B_HEADS * B_HEAD_PAD), jnp.bfloat16),
            jax.ShapeDtypeStruct((t, B_HEADS * B_HEAD_PAD), jnp.bfloat16),
        ],
        compiler_params=pltpu.CompilerParams(dimension_semantics=("arbitrary",), vmem_limit_bytes=VMEM_LIMIT),
        name="latent_prep",
    )(proj, proj, proj, q_a_gain, kv_a_gain, wq, wk, wv, q_gain, k_gain, cos, sa, sb)


def _attn_b_kernel(q_ref, k_ref, v_ref, o_ref):
    i = pl.program_id(1)
    query_chunk = lax.broadcasted_iota(jnp.int32, (B_TILE, B_TILE), 0) // CHUNK
    key_chunk = lax.broadcasted_iota(jnp.int32, (B_TILE, B_TILE), 1) // CHUNK
    diag_ok = key_chunk <= query_chunk

    def step(j, carry, masked):
        rows = pl.ds(pl.multiple_of(j * B_TILE, B_TILE), B_TILE)
        out = []
        for h in range(B_HEADS):
            m, acc = carry[h]
            head = slice(h * B_HEAD_PAD, (h + 1) * B_HEAD_PAD)
            s = _nt_dot(q_ref[0, :, head], k_ref[0, rows, head])
            if masked:
                s = jnp.where(diag_ok, s, NEG_INF)
            m_new = jnp.maximum(m, jnp.max(s, axis=-1, keepdims=True))
            e = jnp.exp(s - m_new).astype(jnp.bfloat16)
            acc = jnp.exp(m - m_new) * acc + jnp.dot(e, v_ref[0, rows, head], preferred_element_type=jnp.float32)
            out.append((m_new, acc))
        return tuple(out)

    init = tuple((jnp.full((B_TILE, 1), NEG_INF, jnp.float32), jnp.zeros((B_TILE, B_HEAD_PAD), jnp.float32))
                 for _ in range(B_HEADS))
    carry = lax.fori_loop(0, i, lambda j, c: step(j, c, False), init)
    carry = step(i, carry, True)
    for pair in range(B_HEADS // 2):
        halves = [carry[2 * pair + parity][1] for parity in range(2)]
        o_ref[0, :, pair * 128:(pair + 1) * 128] = _merge_halves(halves).astype(o_ref.dtype)


def _attn_b(qb, kb, vb):
    b, s, _ = qb.shape
    return pl.pallas_call(
        _attn_b_kernel,
        grid=(b, s // B_TILE),
        in_specs=[
            pl.BlockSpec((1, B_TILE, B_HEADS * B_HEAD_PAD), lambda bi, i: (bi, i, 0)),
            pl.BlockSpec((1, s, B_HEADS * B_HEAD_PAD), lambda bi, i: (bi, 0, 0)),
            pl.BlockSpec((1, s, B_HEADS * B_HEAD_PAD), lambda bi, i: (bi, 0, 0)),
        ],
        out_specs=pl.BlockSpec((1, B_TILE, B_HEADS * B_V_DIM), lambda bi, i: (bi, i, 0)),
        out_shape=jax.ShapeDtypeStruct((b, s, B_HEADS * B_V_DIM), jnp.bfloat16),
        compiler_params=pltpu.CompilerParams(dimension_semantics=("arbitrary", "arbitrary"),
                                             vmem_limit_bytes=VMEM_LIMIT),
        name="attn_b",
    )(qb, kb, vb)


def _attn_c_kernel(q_ref, k_ref, v_ref, bias_ref, o_ref, k_pad, v_pad):
    i = pl.program_id(1)
    seq = k_ref.shape[1]
    width = C_HEADS * HEAD_DIM

    @pl.when(i == 0)
    def _():
        for parity in range(2):
            k_pad[parity, 0:C_PAD, :] = jnp.zeros((C_PAD, width), k_pad.dtype)
            v_pad[parity, 0:C_PAD, :] = jnp.zeros((C_PAD, width), v_pad.dtype)

        def fill(r, carry):
            rows = pl.ds(pl.multiple_of(r * 256, 256), 256)
            dst = pl.ds(pl.multiple_of(r * 256 + C_PAD, 256), 256)
            k = k_ref[0, rows, :]
            v = v_ref[0, rows, :]
            for parity in range(2):
                k_pad[parity, dst, :] = _half_lanes(k, parity)
                v_pad[parity, dst, :] = _half_lanes(v, parity, fill=1.0)
            return carry

        lax.fori_loop(0, seq // 256, fill, 0)

    start = pl.multiple_of(i * Q_TILE, Q_TILE)
    band = pl.ds(start, C_BAND)
    key_ok = lax.broadcasted_iota(jnp.int32, (C_BAND, Q_TILE), 0) + (start - C_PAD) >= 0
    for slab in range(width // 128):
        lanes = slice(slab * 128, (slab + 1) * 128)
        q = q_ref[0, :, lanes]
        halves = []
        for parity in range(2):
            s = _nt_dot(k_pad[parity, band, lanes], q) + bias_ref[2 * slab + parity]
            halves.append(_softmax_pv(s, key_ok, v_pad[parity, band, lanes]))
        o_ref[0, :, lanes] = _merge_halves(halves).astype(o_ref.dtype)


def _attn_c(proj3, bias):
    b, s, _ = proj3.shape
    width = C_HEADS * HEAD_DIM
    return pl.pallas_call(
        _attn_c_kernel,
        grid=(b, s // Q_TILE),
        in_specs=[
            pl.BlockSpec((1, Q_TILE, width), lambda bi, i: (bi, i, SEC_CQ // width)),
            pl.BlockSpec((1, s, width), lambda bi, i: (bi, 0, SEC_CK // width)),
            pl.BlockSpec((1, s, width), lambda bi, i: (bi, 0, SEC_CV // width)),
            pl.BlockSpec((C_HEADS, C_BAND, Q_TILE), lambda bi, i: (0, 0, 0)),
        ],
        out_specs=pl.BlockSpec((1, Q_TILE, width), lambda bi, i: (bi, i, 0)),
        out_shape=jax.ShapeDtypeStruct((b, s, width), jnp.bfloat16),
        scratch_shapes=[pltpu.VMEM((2, s + C_PAD, width), jnp.bfloat16),
                        pltpu.VMEM((2, s + C_PAD, width), jnp.bfloat16)],
        compiler_params=pltpu.CompilerParams(dimension_semantics=("arbitrary", "arbitrary"),
                                             vmem_limit_bytes=VMEM_LIMIT),
        name="attn_c",
    )(proj3, proj3, proj3, bias)


def _out_proj_kernel(oa_ref, ob_ref, oc_ref, x_ref, ga_ref, gb_ref, gc_ref, w_ref, gf_ref, wr_ref, br_ref,
                     xo_ref, h_ref, dest_ref, tg_ref, cnt_ref, count):
    i = pl.program_id(0)
    cap = pl.num_programs(0) * TOKEN_TILE

    @pl.when(i == 0)
    def _():
        count[...] = jnp.zeros_like(count)

    acc = x_ref[...]
    parts = ((oa_ref, ga_ref, 0, 512), (ob_ref, gb_ref, 512, 256), (oc_ref, gc_ref, 768, 256))
    for ref, g_ref, off, width in parts:
        o = _rms(ref[...].astype(jnp.float32), g_ref[...]).astype(jnp.bfloat16)
        acc = acc + jnp.dot(o, w_ref[off:off + width, :], preferred_element_type=jnp.float32)
    xo_ref[...] = acc
    h = _rms(acc, gf_ref[...])
    _store_row_tiles(h_ref, h)
    logits = jnp.dot(h, wr_ref[...], preferred_element_type=jnp.float32,
                     precision=lax.Precision.HIGHEST) + br_ref[...]
    lane = lax.broadcasted_iota(jnp.int32, logits.shape, 1)
    work = jnp.where(lane < N_EXPERTS, logits, -jnp.inf)
    gate_out = jnp.zeros(logits.shape, jnp.float32)
    denom = jnp.zeros((logits.shape[0], 1), jnp.float32)
    top = None
    picks = []
    for k in range(TOP_K):
        m = jnp.max(work, axis=-1, keepdims=True)
        idx = jnp.min(jnp.where(work == m, lane, ROUTER_PAD), axis=-1, keepdims=True)
        pick = lane == idx
        work = jnp.where(pick, -jnp.inf, work)
        top = m if top is None else top
        e = jnp.exp(m - top)
        denom = denom + e
        gate_out = jnp.where(lane == k, e, gate_out)
        picks.append((idx, pick))
    tg_ref[...] = gate_out / denom
    chosen = jnp.zeros(logits.shape, jnp.float32)
    for _, pick in picks:
        chosen = jnp.where(pick, 1.0, chosen)
    r_i = lax.broadcasted_iota(jnp.int32, (TOKEN_TILE, TOKEN_TILE), 0)
    c_i = lax.broadcasted_iota(jnp.int32, (TOKEN_TILE, TOKEN_TILE), 1)
    earlier = jnp.where(c_i < r_i, 1.0, 0.0).astype(jnp.bfloat16)
    before = jnp.dot(earlier, chosen.astype(jnp.bfloat16), preferred_element_type=jnp.float32) + count[...]
    dest_out = jnp.zeros(logits.shape, jnp.int32)
    for k, (idx, pick) in enumerate(picks):
        rank = jnp.sum(jnp.where(pick, before, 0.0), axis=-1, keepdims=True).astype(jnp.int32)
        dest_out = jnp.where(lane == k, idx * cap + rank, dest_out)
    dest_ref[...] = dest_out
    count[...] = count[...] + jnp.sum(chosen, axis=0, keepdims=True)
    cnt_ref[...] = count[...].astype(jnp.int32)


def _out_proj(oa, ob, oc, x2, ga, gb, gc, w, gf, wr, br):
    t = x2.shape[0]
    row = lambda width: pl.BlockSpec((TOKEN_TILE, width), lambda i: (i, 0))
    full = lambda shape: pl.BlockSpec(shape, lambda i: (0, 0))
    return pl.pallas_call(
        _out_proj_kernel,
        grid=(t // TOKEN_TILE,),
        in_specs=[row(512), row(256), row(256), row(D_MODEL), full((1, 512)), full((1, 256)), full((1, 256)),
                  full((D_MODEL, D_MODEL)), full((1, D_MODEL)), full((D_MODEL, ROUTER_PAD)), full((1, ROUTER_PAD))],
        out_specs=[row(D_MODEL), pl.BlockSpec((TOKEN_TILE * LANE_TILES, 128), lambda i: (i, 0)),
                   row(ROUTER_PAD), row(ROUTER_PAD), full((1, ROUTER_PAD))],
        out_shape=[jax.ShapeDtypeStruct((t, D_MODEL), jnp.float32),
                   jax.ShapeDtypeStruct((t * LANE_TILES, 128), jnp.float32),
                   jax.ShapeDtypeStruct((t, ROUTER_PAD), jnp.int32),
                   jax.ShapeDtypeStruct((t, ROUTER_PAD), jnp.float32),
                   jax.ShapeDtypeStruct((1, ROUTER_PAD), jnp.int32)],
        scratch_shapes=[pltpu.VMEM((1, ROUTER_PAD), jnp.float32)],
        compiler_params=pltpu.CompilerParams(dimension_semantics=("arbitrary",), vmem_limit_bytes=VMEM_LIMIT),
        name="out_proj",
    )(oa, ob, oc, x2, ga, gb, gc, w, gf, wr, br)


def _dispatch_kernel(dest_hbm, h_ref, x_hbm, idx_smem, isem, dsem):
    i = pl.program_id(0)
    nt = pl.num_programs(0)
    slot = i % 2

    def idx_copy(tile, s):
        return pltpu.make_async_copy(dest_hbm.at[tile], _index_slot(idx_smem, s), isem.at[s])

    @pl.when(i == 0)
    def _():
        idx_copy(0, 0).start()

    idx_copy(i, slot).wait()

    @pl.when(i + 1 < nt)
    def _():
        idx_copy(i + 1, 1 - slot).start()

    base = slot * TILE_SLOTS

    def body(r, carry):
        for k in range(TOP_K):
            dst = idx_smem[base + r * TOP_K + k]
            pltpu.make_async_copy(_row_tile(h_ref, r), _row_tile(x_hbm, dst), dsem).start(priority=k % 2)
        return carry

    lax.fori_loop(0, TOKEN_TILE, body, 0, unroll=4)
    for _ in range(TOP_K):
        pltpu.make_async_copy(h_ref, x_hbm.at[pl.ds(0, TOKEN_TILE * LANE_TILES)], dsem).wait()


def _dispatch(dest_tiles, h3, n_rows):
    t = h3.shape[0] // LANE_TILES
    return pl.pallas_call(
        _dispatch_kernel,
        grid=(t // TOKEN_TILE,),
        in_specs=[pl.BlockSpec(memory_space=pl.ANY),
                  pl.BlockSpec((TOKEN_TILE * LANE_TILES, 128), lambda i: (i, 0))],
        out_specs=pl.BlockSpec(memory_space=pl.ANY),
        out_shape=jax.ShapeDtypeStruct((n_rows * LANE_TILES, 128), jnp.float32),
        scratch_shapes=[pltpu.SMEM((2 * TILE_SLOTS,), jnp.int32), pltpu.SemaphoreType.DMA((2,)),
                        pltpu.SemaphoreType.DMA],
        compiler_params=pltpu.CompilerParams(dimension_semantics=("arbitrary",), vmem_limit_bytes=VMEM_LIMIT),
        name="dispatch",
    )(dest_tiles, h3)


def _moe_ffn_kernel(blk_e_ref, blk_row_ref, blk_valid_ref, blk_first_ref, blk_next_ref, blk_slot_ref,
                    x_ref, wgu_hbm, bgu_ref, wd_hbm, bd_ref, y_ref, wgu_f32, wd_f32, wgu_bf, wd_bf, gsem, dsem,
                    *, layer):
    i = pl.program_id(0)
    n_valid = blk_valid_ref[i]

    def weight_copies(e, s):
        return (pltpu.make_async_copy(wgu_hbm.at[layer, e], wgu_f32.at[s], gsem.at[s]),
                pltpu.make_async_copy(wd_hbm.at[layer, e], wd_f32.at[s], dsem.at[s]))

    @pl.when(i == 0)
    def _():
        for copy in weight_copies(blk_e_ref[0], blk_slot_ref[0]):
            copy.start()

    @pl.when(blk_first_ref[i] == 1)
    def _():
        s = blk_slot_ref[i]
        for copy in weight_copies(blk_e_ref[i], s):
            copy.wait()
        nxt = blk_next_ref[i]

        @pl.when(nxt >= 0)
        def _():
            for copy in weight_copies(nxt, 1 - s):
                copy.start()

        for c in range(4):
            cols = slice(c * 512, (c + 1) * 512)
            wgu_bf[:, cols] = wgu_f32[s, :, cols].astype(jnp.bfloat16)
        for c in range(2):
            cols = slice(c * 512, (c + 1) * 512)
            wd_bf[:, cols] = wd_f32[s, :, cols].astype(jnp.bfloat16)

    @pl.when(n_valid > 0)
    def _():
        live = lax.broadcasted_iota(jnp.int32, (EXPERT_ROWS, 128), 0) < n_valid
        x = jnp.concatenate([jnp.where(live, _load_row_tiles(x_ref, 0, EXPERT_ROWS, c), 0.0).astype(jnp.bfloat16)
                             for c in range(LANE_TILES)], axis=1)
        glu = jnp.dot(x, wgu_bf[:, 0:D_EXPERT], preferred_element_type=jnp.float32) + bgu_ref[0, 0, :, 0:D_EXPERT]
        lin = jnp.dot(x, wgu_bf[:, D_EXPERT:], preferred_element_type=jnp.float32) + bgu_ref[0, 0, :, D_EXPERT:]
        glu = jnp.minimum(glu, SWIGLU_LIMIT)
        lin = jnp.clip(lin, -SWIGLU_LIMIT, SWIGLU_LIMIT)
        act = glu * jax.nn.sigmoid(SWIGLU_ALPHA * glu) * (lin + 1.0)
        y = jnp.dot(act.astype(jnp.bfloat16), wd_bf[...], preferred_element_type=jnp.float32) + bd_ref[0, 0]
        _store_row_tiles(y_ref, y)

    @pl.when(n_valid <= 0)
    def _():
        y_ref[...] = jnp.zeros_like(y_ref)


def _moe_ffn(layer, table, x_rows, w_gu, b_gu, w_down, b_down):
    n_blocks = table[0].shape[0]
    depth = w_gu.shape[0]
    rows = lambda i, be, br, *_: (br[i], 0)
    expert = lambda i, be, *_: (layer, be[i], 0, 0)
    grid_spec = pltpu.PrefetchScalarGridSpec(
        num_scalar_prefetch=len(table),
        grid=(n_blocks,),
        in_specs=[
            pl.BlockSpec((EXPERT_ROWS * LANE_TILES, 128), rows),
            pl.BlockSpec(memory_space=pl.ANY),
            pl.BlockSpec((1, 1, 1, 2 * D_EXPERT), expert),
            pl.BlockSpec(memory_space=pl.ANY),
            pl.BlockSpec((1, 1, 1, D_MODEL), expert),
        ],
        out_specs=pl.BlockSpec((EXPERT_ROWS * LANE_TILES, 128), rows),
        scratch_shapes=[pltpu.VMEM((2, D_MODEL, 2 * D_EXPERT), jnp.float32),
                        pltpu.VMEM((2, D_EXPERT, D_MODEL), jnp.float32),
                        pltpu.VMEM((D_MODEL, 2 * D_EXPERT), jnp.bfloat16),
                        pltpu.VMEM((D_EXPERT, D_MODEL), jnp.bfloat16),
                        pltpu.SemaphoreType.DMA((2,)), pltpu.SemaphoreType.DMA((2,))],
    )
    return pl.pallas_call(
        functools.partial(_moe_ffn_kernel, layer=layer),
        grid_spec=grid_spec,
        out_shape=jax.ShapeDtypeStruct(x_rows.shape, jnp.float32),
        compiler_params=pltpu.CompilerParams(dimension_semantics=("arbitrary",), vmem_limit_bytes=VMEM_LIMIT),
        name="moe_ffn",
    )(*table, x_rows, w_gu, b_gu.reshape(depth, N_EXPERTS, 1, -1), w_down, b_down.reshape(depth, N_EXPERTS, 1, -1))


def _combine_kernel(dest_hbm, y_hbm, x_ref, g_ref, *rest, project):
    if project:
        gain_ref, w_ref, hg_ref, xo_ref, o_ref, ybuf, idx_smem, isem, gsem = rest
    else:
        xo_ref, ybuf, idx_smem, isem, gsem = rest
    i = pl.program_id(0)
    nt = pl.num_programs(0)
    slot = i % 2

    def idx_copy(tile, s):
        return pltpu.make_async_copy(dest_hbm.at[tile], _index_slot(idx_smem, s), isem.at[s])

    def issue_gather(s):
        base = s * TILE_SLOTS

        def body(r, carry):
            for k in range(TOP_K):
                src = idx_smem[base + r * TOP_K + k]
                pltpu.make_async_copy(_row_tile(y_hbm, src), _row_tile(ybuf.at[s], k * TOKEN_TILE + r),
                                      gsem.at[s]).start(priority=k % 2)
            return carry
        lax.fori_loop(0, TOKEN_TILE, body, 0, unroll=4)

    @pl.when(i == 0)
    def _():
        idx_copy(0, 0).start()
        idx_copy(0, 0).wait()
        issue_gather(0)

        @pl.when(nt > 1)
        def _():
            idx_copy(1, 1).start()

    @pl.when(i + 1 < nt)
    def _():
        idx_copy(i + 1, 1 - slot).wait()
        issue_gather(1 - slot)

    pltpu.make_async_copy(y_hbm.at[pl.ds(0, TOP_K * TOKEN_TILE * LANE_TILES)], ybuf.at[slot], gsem.at[slot]).wait()

    @pl.when(i + 2 < nt)
    def _():
        idx_copy(i + 2, slot).start()

    g = g_ref[...]
    gates = [jnp.broadcast_to(g[:, k:k + 1], (TOKEN_TILE, 128)) for k in range(TOP_K)]
    cols = []
    for c in range(LANE_TILES):
        acc = x_ref[:, c * 128:(c + 1) * 128]
        for k in range(TOP_K):
            acc = acc + gates[k] * _load_row_tiles(ybuf.at[slot], k * TOKEN_TILE, TOKEN_TILE, c)
        cols.append(acc)
    x = jnp.concatenate(cols, axis=1)
    xo_ref[...] = x
    if project:
        _project(_rms(x, gain_ref[...]).astype(jnp.bfloat16), w_ref, hg_ref, o_ref)


def _combine_call(dest_tiles, y_rows, x2, gates, gain=None, w=None, head_gain=None):
    t = x2.shape[0]
    project = w is not None
    row = lambda width: pl.BlockSpec((TOKEN_TILE, width), lambda i: (i, 0))
    full = lambda shape: pl.BlockSpec(shape, lambda i: (0, 0))
    in_specs = [pl.BlockSpec(memory_space=pl.ANY), pl.BlockSpec(memory_space=pl.ANY), row(D_MODEL), row(ROUTER_PAD)]
    out_specs = [row(D_MODEL)]
    out_shape = [jax.ShapeDtypeStruct((t, D_MODEL), jnp.float32)]
    args = [dest_tiles, y_rows, x2, gates]
    if project:
        in_specs += [full((1, D_MODEL)), full((D_MODEL, PROJ_WIDTH)), full((1, PROJ_WIDTH))]
        out_specs.append(row(PROJ_WIDTH))
        out_shape.append(jax.ShapeDtypeStruct((t, PROJ_WIDTH), jnp.bfloat16))
        args += [gain, w, head_gain]
    out = pl.pallas_call(
        functools.partial(_combine_kernel, project=project),
        grid=(t // TOKEN_TILE,),
        in_specs=in_specs, out_specs=out_specs, out_shape=out_shape,
        scratch_shapes=[pltpu.VMEM((2, TOP_K * TOKEN_TILE * LANE_TILES, 128), jnp.float32),
                        pltpu.SMEM((2 * TILE_SLOTS,), jnp.int32),
                        pltpu.SemaphoreType.DMA((2,)), pltpu.SemaphoreType.DMA((2,))],
        compiler_params=pltpu.CompilerParams(dimension_semantics=("arbitrary",), vmem_limit_bytes=VMEM_LIMIT),
        name="combine_in_proj" if project else "combine",
    )(*args)
    return out if project else out[0]


def _t5_bucket(rel):
    half = T5_BUCKETS // 2
    max_exact = half // 2
    ret = np.where(rel > 0, half, 0)
    n = np.abs(rel)
    nf = np.maximum(n, 1).astype(np.float32)
    large = max_exact + (np.log(nf / max_exact) / math.log(T5_MAX_DIST / max_exact)
                         * (half - max_exact)).astype(np.int32)
    large = np.minimum(large, half - 1)
    return ret + np.where(n < max_exact, n, large)


def _band_ok(n_prev, band):
    q_chunk = np.arange(Q_TILE)[:, None] // CHUNK + n_prev
    k_chunk = np.arange(band)[None, :] // CHUNK
    return (k_chunk <= q_chunk) & (k_chunk >= q_chunk - n_prev)


def _toeplitz(u, band):
    heads, length = u.shape
    flat = jnp.broadcast_to(u[:, None, :], (heads, Q_TILE, length)).reshape(heads, Q_TILE * length)
    skew = flat[:, :Q_TILE * (length - 1)].reshape(heads, Q_TILE, length - 1)
    return skew[:, :, Q_TILE - 1:Q_TILE - 1 + band]


def _a_bias(t5_bias):
    rel = np.arange(Q_TILE + A_BAND) - (Q_TILE - 1) - A_PAD
    bias = _toeplitz(t5_bias[_t5_bucket(rel)].T.astype(jnp.float32), A_BAND)
    bias = jnp.where(_band_ok(A_PREV_CHUNKS, A_BAND)[None], bias, NEG_INF)
    pairs = [jnp.concatenate([bias[4 * hk + parity], bias[4 * hk + 2 + parity]], axis=0)
             for hk in range(A_KV_HEADS) for parity in range(2)]
    return jnp.stack(pairs).transpose(0, 2, 1)


def _c_bias(rel_table):
    dist = C_PAD + (Q_TILE - 1) - np.arange(Q_TILE + C_BAND)
    idx = np.clip(dist, -(CHUNK - 1), C_REL_CLIP) + (CHUNK - 1)
    bias = _toeplitz(rel_table[idx].T.astype(jnp.float32), C_BAND)
    return jnp.where(_band_ok(C_PREV_CHUNKS, C_BAND)[None], bias, NEG_INF).transpose(0, 2, 1)


def _rope_tables(seq):
    half = B_ROPE_DIM // 2
    inv_freq = ROPE_THETA ** (-(jnp.arange(half, dtype=jnp.float32) / half))
    ang = jnp.arange(seq, dtype=jnp.float32)[:, None] * inv_freq[None, :]
    cos, sin = jnp.cos(ang), jnp.sin(ang)
    zeros = jnp.zeros((seq, B_NOPE_DIM), jnp.float32)
    tail = jnp.zeros((seq, B_HEAD_PAD - B_QK_DIM), jnp.float32)
    zh = jnp.zeros((seq, half), jnp.float32)
    cos_t = jnp.concatenate([zeros + 1.0, cos, cos, tail + 1.0], axis=1)
    sa = jnp.concatenate([zeros, -sin, zh, tail], axis=1)
    sb = jnp.concatenate([zeros, zh, sin, tail], axis=1)
    return cos_t, sa, sb


def _regroup_w_in(w):
    splits = np.cumsum((512, 128, 128, 256, 128, 32, 256, 256))
    aq, ak, av, bcq, bckv, bkpe, cq, ck, cv = jnp.split(w, splits.tolist(), axis=1)
    z = lambda n: jnp.zeros((w.shape[0], n), w.dtype)
    return jnp.concatenate([aq, ak, av, bcq, bckv, z(64), bkpe, z(32), cq, ck, cv], axis=1).astype(jnp.bfloat16)


def _head_gain_row(a_q, a_k, c_q, c_k):
    scale = HEAD_DIM ** -0.5
    row = jnp.ones((PROJ_WIDTH,), jnp.float32)
    for sec, heads, gain in ((SEC_AQ, A_Q_HEADS, a_q * scale), (SEC_AKV, A_KV_HEADS, a_k),
                             (SEC_CQ, C_HEADS, c_q * scale), (SEC_CK, C_HEADS, c_k)):
        row = row.at[sec:sec + heads * HEAD_DIM].set(jnp.tile(gain, heads))
    return row.reshape(1, PROJ_WIDTH)


def _pad_heads(w, head_dim, keep):
    rank = w.shape[0]
    w = w.reshape(rank, B_HEADS, head_dim)[:, :, :keep]
    w = jnp.pad(w, ((0, 0), (0, 0), (0, B_HEAD_PAD - keep)))
    return w.reshape(rank, B_HEADS * B_HEAD_PAD).astype(jnp.bfloat16)


def _pad_lanes(g, offset, width):
    return jnp.pad(g, (offset, width - offset - g.shape[0])).reshape(1, width)


def _block_table(counts, t):
    tk = t * TOP_K
    blocks_per_expert = t // EXPERT_ROWS
    nblk = (counts + EXPERT_ROWS - 1) // EXPERT_ROWS
    blk_end = jnp.cumsum(nblk)
    blk_first = blk_end - nblk
    n_blocks = (tk + N_EXPERTS * (EXPERT_ROWS - 1) + EXPERT_ROWS - 1) // EXPERT_ROWS
    blk = jnp.arange(n_blocks, dtype=jnp.int32)
    blk_e = jnp.minimum(jnp.sum(blk[:, None] >= blk_end[None, :], axis=1), N_EXPERTS - 1).astype(jnp.int32)
    own = blk_e[:, None] == jnp.arange(N_EXPERTS, dtype=jnp.int32)[None, :]
    pick = lambda v: jnp.sum(jnp.where(own, v[None, :], 0), axis=1)
    j = blk - pick(blk_first)
    blk_valid = jnp.clip(pick(counts) - j * EXPERT_ROWS, 0, EXPERT_ROWS).astype(jnp.int32)
    spare = N_EXPERTS * blocks_per_expert
    blk_row = jnp.where(blk_valid > 0, blk_e * blocks_per_expert + j, spare).astype(jnp.int32)
    experts = jnp.arange(N_EXPERTS, dtype=jnp.int32)
    used = counts > 0
    later = jnp.logical_and(experts[None, :] > experts[:, None], used[None, :])
    nxt = jnp.min(jnp.where(later, experts[None, :], N_EXPERTS), axis=1)
    nxt = jnp.where(nxt < N_EXPERTS, nxt, -1)
    ordinal = jnp.cumsum(used.astype(jnp.int32)) - 1
    blk_first = jnp.logical_and(j == 0, blk_valid > 0).astype(jnp.int32)
    blk_next = pick(nxt).astype(jnp.int32)
    blk_slot = (pick(ordinal) % 2).astype(jnp.int32)
    return blk_e, blk_row, blk_valid, blk_first, blk_next, blk_slot


def kernel(x, attn_norm, w_in, a_q_norm, a_k_norm, a_sinks, t5_bias, b_q_a_norm, b_w_q_b, b_kv_a_norm, b_w_kv_b,
           b_q_norm, b_k_norm, c_q_norm, c_k_norm, c_rel_bias, a_out_norm, b_out_norm, c_out_norm, w_out, ffn_norm,
           w_router, b_router, w_gate_up, b_gate_up, w_down, b_down):
    b, s, d = x.shape
    t = b * s
    depth = w_in.shape[0]
    a_bias = _a_bias(t5_bias)
    cos_t, sin_a, sin_b = _rope_tables(s)
    x2 = x.reshape(t, d)
    head_gain = lambda l: _head_gain_row(a_q_norm[l], a_k_norm[l], c_q_norm[l], c_k_norm[l])
    proj = _in_proj(x2, attn_norm[0].reshape(1, d), _regroup_w_in(w_in[0]), head_gain(0))
    for l in range(depth):
        proj3 = proj.reshape(b, s, PROJ_WIDTH)
        o_a = _attn_a(proj3, a_bias, a_sinks[l])
        kv_w = b_w_kv_b[l].reshape(B_KV_RANK, B_HEADS, B_NOPE_DIM + B_V_DIM)
        wv = kv_w[:, :, B_NOPE_DIM:].reshape(B_KV_RANK, B_HEADS * B_V_DIM).astype(jnp.bfloat16)
        qb, kb, vb = _latent_prep(
            proj, b_q_a_norm[l].reshape(1, -1), b_kv_a_norm[l].reshape(1, -1),
            _pad_heads(b_w_q_b[l], B_QK_DIM, B_QK_DIM), _pad_heads(b_w_kv_b[l], B_NOPE_DIM + B_V_DIM, B_NOPE_DIM), wv,
            _pad_lanes(b_q_norm[l], 0, B_HEAD_PAD), _pad_lanes(b_k_norm[l], 0, B_HEAD_PAD), cos_t, sin_a, sin_b, s)
        o_b = _attn_b(qb.reshape(b, s, -1), kb.reshape(b, s, -1), vb.reshape(b, s, -1))
        o_c = _attn_c(proj3, _c_bias(c_rel_bias[l]))
        wr = jnp.pad(w_router[l], ((0, 0), (0, ROUTER_PAD - N_EXPERTS)))
        br = jnp.pad(b_router[l], (0, ROUTER_PAD - N_EXPERTS)).reshape(1, ROUTER_PAD)
        x2, h3, dest, gates, counts = _out_proj(
            o_a.reshape(t, -1), o_b.reshape(t, -1), o_c.reshape(t, -1), x2,
            a_out_norm[l].reshape(1, -1), b_out_norm[l].reshape(1, -1), c_out_norm[l].reshape(1, -1),
            w_out[l].astype(jnp.bfloat16), ffn_norm[l].reshape(1, -1), wr, br)
        dest_tiles = dest[:, :TOP_K].reshape(t // TOKEN_TILE, TOKEN_TILE * TOP_K)
        table = _block_table(counts[0, :N_EXPERTS], t)
        n_rows = (N_EXPERTS * (t // EXPERT_ROWS) + 1) * EXPERT_ROWS
        x_rows = _dispatch(dest_tiles, h3, n_rows)
        y_rows = _moe_ffn(l, table, x_rows, w_gate_up, b_gate_up, w_down, b_down)
        if l + 1 < depth:
            x2, proj = _combine_call(dest_tiles, y_rows, x2, gates, attn_norm[l + 1].reshape(1, d),
                                     _regroup_w_in(w_in[l + 1]), head_gain(l + 1))
        else:
            x2 = _combine_call(dest_tiles, y_rows, x2, gates)
    return x2.reshape(b, s, d)
```

```python
import functools
import math

import jax
import jax.numpy as jnp
import numpy as np
from jax import lax
from jax.experimental import pallas as pl
from jax.experimental.pallas import tpu as pltpu

D_MODEL = 1024
CHUNK = 64
HEAD_DIM = 64
EPS = 1e-6
NEG_INF = -1e30
A_Q_HEADS = 8
A_KV_HEADS = 2
A_GROUP = 4
A_PREV_CHUNKS = 2
T5_BUCKETS = 32
T5_MAX_DIST = 128
B_HEADS = 4
B_Q_RANK = 256
B_KV_RANK = 128
B_NOPE_DIM = 64
B_ROPE_DIM = 32
B_QK_DIM = 96
B_V_DIM = 64
ROPE_THETA = 10000.0
C_HEADS = 4
C_PREV_CHUNKS = 8
C_REL_CLIP = 256
N_EXPERTS = 32
TOP_K = 4
D_EXPERT = 1024
SWIGLU_LIMIT = 7.0
SWIGLU_ALPHA = 1.702

SEC_AQ = 0
SEC_AKV = 512
SEC_BCQ = 768
SEC_BCKV = 1024
SEC_BKPE = 1152
SEC_CQ = 1280
SEC_CK = 1536
SEC_CV = 1792
PROJ_WIDTH = 2048
HEAD_NORM_SLABS = tuple(sec // 128 + j for sec, n in ((SEC_AQ, 4), (SEC_AKV, 1), (SEC_CQ, 2), (SEC_CK, 2))
                        for j in range(n))
B_HEAD_PAD = 128

Q_TILE = 128
A_BAND = Q_TILE + A_PREV_CHUNKS * CHUNK
C_BAND = Q_TILE + C_PREV_CHUNKS * CHUNK
A_PAD = A_PREV_CHUNKS * CHUNK
C_PAD = C_PREV_CHUNKS * CHUNK
B_TILE = 512
ROW_TILE = 512
EXPERT_ROWS = 256
ROUTER_PAD = 128
TOKEN_TILE = 256
TILE_SLOTS = TOKEN_TILE * TOP_K
LANE_TILES = D_MODEL // 128
VMEM_LIMIT = 48 * 1024 * 1024


def _rms(x, gain):
    return x * lax.rsqrt(jnp.mean(x * x, axis=-1, keepdims=True) + EPS) * gain


def _nt_dot(a, b):
    return lax.dot_general(a, b, (((1,), (1,)), ((), ())), preferred_element_type=jnp.float32)


def _load_row_tiles(ref, first_row, n_rows, c):
    return ref[pl.ds(first_row * LANE_TILES + c, n_rows, stride=LANE_TILES), :]


def _store_row_tiles(ref, value):
    for c in range(LANE_TILES):
        ref[pl.ds(c, value.shape[0], stride=LANE_TILES), :] = value[:, c * 128:(c + 1) * 128]


def _row_tile(ref, row):
    return ref.at[pl.ds(pl.multiple_of(row * LANE_TILES, LANE_TILES), LANE_TILES)]


def _index_slot(idx_smem, s):
    return idx_smem.at[pl.ds(pl.multiple_of(s * TILE_SLOTS, TILE_SLOTS), TILE_SLOTS)]


def _project(h, w_ref, hg_ref, o_ref):
    li = lax.broadcasted_iota(jnp.int32, (128, 128), 0) // HEAD_DIM
    lj = lax.broadcasted_iota(jnp.int32, (128, 128), 1) // HEAD_DIM
    same_head = jnp.where(li == lj, 1.0, 0.0).astype(jnp.bfloat16)
    for n in range(PROJ_WIDTH // 512):
        acc = jnp.dot(h, w_ref[:, n * 512:(n + 1) * 512], preferred_element_type=jnp.float32)
        slabs = []
        for j in range(4):
            slab = n * 4 + j
            a = acc[:, j * 128:(j + 1) * 128]
            if slab in HEAD_NORM_SLABS:
                sq = a * a
                hi = sq.astype(jnp.bfloat16)
                lo = (sq - hi.astype(jnp.float32)).astype(jnp.bfloat16)
                ss = (jnp.dot(hi, same_head, preferred_element_type=jnp.float32)
                      + jnp.dot(lo, same_head, preferred_element_type=jnp.float32))
                a = a * lax.rsqrt(ss * (1.0 / HEAD_DIM) + EPS) * hg_ref[:, slab * 128:(slab + 1) * 128]
            slabs.append(a.astype(o_ref.dtype))
        o_ref[:, n * 512:(n + 1) * 512] = jnp.concatenate(slabs, axis=1)


def _in_proj_kernel(x_ref, g_ref, w_ref, hg_ref, o_ref):
    _project(_rms(x_ref[...], g_ref[...]).astype(jnp.bfloat16), w_ref, hg_ref, o_ref)


def _in_proj(x2, gain, w, head_gain):
    t = x2.shape[0]
    return pl.pallas_call(
        _in_proj_kernel,
        grid=(t // ROW_TILE,),
        in_specs=[
            pl.BlockSpec((ROW_TILE, D_MODEL), lambda i: (i, 0)),
            pl.BlockSpec((1, D_MODEL), lambda i: (0, 0)),
            pl.BlockSpec((D_MODEL, PROJ_WIDTH), lambda i: (0, 0)),
            pl.BlockSpec((1, PROJ_WIDTH), lambda i: (0, 0)),
        ],
        out_specs=pl.BlockSpec((ROW_TILE, PROJ_WIDTH), lambda i: (i, 0)),
        out_shape=jax.ShapeDtypeStruct((t, PROJ_WIDTH), jnp.bfloat16),
        compiler_params=pltpu.CompilerParams(dimension_semantics=("arbitrary",), vmem_limit_bytes=VMEM_LIMIT),
        name="in_proj",
    )(x2, gain, w, head_gain)


def _half_lanes(x, parity, fill=0.0):
    lane = lax.broadcasted_iota(jnp.int32, x.shape, x.ndim - 1) % 128
    keep = lane < HEAD_DIM if parity == 0 else lane >= HEAD_DIM
    return jnp.where(keep, x, jnp.full_like(x, fill))


def _tn_dot(a, b):
    return lax.dot_general(a, b, (((0,), (0,)), ((), ())), preferred_element_type=jnp.float32)


def _softmax_pv(s, key_ok, v_aug, sink=None, value_half=0):
    s = jnp.where(key_ok, s, NEG_INF)
    m = jnp.max(s, axis=0, keepdims=True)
    if sink is not None:
        m = jnp.maximum(m, sink)
    e = jnp.exp(s - m).astype(jnp.bfloat16)
    o = _tn_dot(e, v_aug)
    if sink is not None:
        row = lax.broadcasted_iota(jnp.int32, (8, s.shape[1]), 0)
        e_sink = jnp.where(row == 0, jnp.exp(sink - m), 0.0).astype(jnp.bfloat16)
        first = lax.broadcasted_iota(jnp.int32, (8, 128), 0) == 0
        sel = _half_lanes(jnp.where(first, 1.0, 0.0), 1 - value_half).astype(jnp.bfloat16)
        o = o + _tn_dot(e_sink, sel)
    return o


def _merge_halves(halves):
    lane = lax.broadcasted_iota(jnp.int32, halves[0].shape, 1)
    normed = [o / pltpu.roll(o, HEAD_DIM, 1) for o in halves]
    return jnp.where(lane < HEAD_DIM, normed[0], normed[1])


def _attn_a_kernel(q_ref, kv_ref, bias_ref, sink_ref, o_ref, k_pad, v_pad):
    i = pl.program_id(1)
    seq = kv_ref.shape[1]

    @pl.when(i == 0)
    def _():
        for n in range(2 * A_KV_HEADS):
            k_pad[n, 0:A_PAD, :] = jnp.zeros((A_PAD, 128), k_pad.dtype)
            v_pad[n, 0:A_PAD, :] = jnp.zeros((A_PAD, 128), v_pad.dtype)

        def fill(r, carry):
            rows = pl.ds(pl.multiple_of(r * 256, 256), 256)
            dst = pl.ds(pl.multiple_of(r * 256 + A_PAD, 128), 256)
            kv = kv_ref[0, rows, :].astype(jnp.float32)
            for src, pad, other in ((kv[:, 0:128], k_pad, 0.0), (kv[:, 128:256], v_pad, 1.0)):
                swapped = pltpu.roll(src, HEAD_DIM, 1)
                pad[0, dst, :] = _half_lanes(src, 0, other).astype(pad.dtype)
                pad[1, dst, :] = _half_lanes(swapped, 1, other).astype(pad.dtype)
                pad[2, dst, :] = _half_lanes(swapped, 0, other).astype(pad.dtype)
                pad[3, dst, :] = _half_lanes(src, 1, other).astype(pad.dtype)
            return carry

        lax.fori_loop(0, seq // 256, fill, 0)

    start = pl.multiple_of(i * Q_TILE, Q_TILE)
    band = pl.ds(start, A_BAND)
    key_ok = lax.broadcasted_iota(jnp.int32, (A_BAND, 2 * Q_TILE), 0) + (start - A_PAD) >= 0
    first_slab = lax.broadcasted_iota(jnp.int32, (1, 2 * Q_TILE), 1) < Q_TILE
    for hk in range(A_KV_HEADS):
        q2 = jnp.concatenate([q_ref[0, :, (2 * hk) * 128:(2 * hk + 1) * 128],
                              q_ref[0, :, (2 * hk + 1) * 128:(2 * hk + 2) * 128]], axis=0)
        halves = []
        for parity in range(2):
            sink = jnp.where(first_slab, sink_ref[4 * hk + parity], sink_ref[4 * hk + 2 + parity])
            s = _nt_dot(k_pad[2 * hk + parity, band, :], q2) + bias_ref[2 * hk + parity]
            halves.append(_softmax_pv(s, key_ok, v_pad[2 * hk + parity, band, :], sink, parity))
        out = _merge_halves(halves)
        o_ref[0, :, (2 * hk) * 128:(2 * hk + 1) * 128] = out[0:Q_TILE].astype(o_ref.dtype)
        o_ref[0, :, (2 * hk + 1) * 128:(2 * hk + 2) * 128] = out[Q_TILE:2 * Q_TILE].astype(o_ref.dtype)


def _attn_a(proj3, bias, sinks):
    b, s, _ = proj3.shape
    return pl.pallas_call(
        _attn_a_kernel,
        grid=(b, s // Q_TILE),
        in_specs=[
            pl.BlockSpec((1, Q_TILE, 512), lambda bi, i: (bi, i, SEC_AQ // 512)),
            pl.BlockSpec((1, s, 256), lambda bi, i: (bi, 0, SEC_AKV // 256)),
            pl.BlockSpec((2 * A_KV_HEADS, 2 * Q_TILE, A_BAND), lambda bi, i: (0, 0, 0)),
            pl.BlockSpec(memory_space=pltpu.SMEM),
        ],
        out_specs=pl.BlockSpec((1, Q_TILE, 512), lambda bi, i: (bi, i, 0)),
        out_shape=jax.ShapeDtypeStruct((b, s, 512), jnp.bfloat16),
        scratch_shapes=[pltpu.VMEM((2 * A_KV_HEADS, s + A_PAD, 128), jnp.bfloat16),
                        pltpu.VMEM((2 * A_KV_HEADS, s + A_PAD, 128), jnp.bfloat16)],
        compiler_params=pltpu.CompilerParams(dimension_semantics=("arbitrary", "arbitrary"),
                                             vmem_limit_bytes=VMEM_LIMIT),
        name="attn_a",
    )(proj3, proj3, bias, sinks)


def _latent_prep_kernel(cq_ref, ckv_ref, kpe_ref, qag_ref, kvag_ref, wq_ref, wk_ref, wv_ref,
                        qg_ref, kg_ref, cos_ref, sa_ref, sb_ref, qo_ref, ko_ref, vo_ref):
    lane = lax.broadcasted_iota(jnp.int32, (ROW_TILE, B_HEAD_PAD), 1)
    is_nope = lane < B_NOPE_DIM
    is_rope = jnp.logical_and(lane >= B_NOPE_DIM, lane < B_QK_DIM)
    cos = cos_ref[...]
    sa = sa_ref[...]
    sb = sb_ref[...]

    def split_norm(x, gain):
        sq = x * x
        nope = jnp.sum(jnp.where(is_nope, sq, 0.0), axis=-1, keepdims=True) * (1.0 / B_NOPE_DIM)
        rope = jnp.sum(jnp.where(is_rope, sq, 0.0), axis=-1, keepdims=True) * (1.0 / B_ROPE_DIM)
        r = jnp.where(is_nope, lax.rsqrt(nope + EPS), lax.rsqrt(rope + EPS))
        return x * r * gain

    def rotate(x):
        return x * cos + pltpu.roll(x, 112, 1) * sa + pltpu.roll(x, 16, 1) * sb

    cq = _rms(cq_ref[...].astype(jnp.float32), qag_ref[...]).astype(jnp.bfloat16)
    ckv = _rms(ckv_ref[...].astype(jnp.float32), kvag_ref[...]).astype(jnp.bfloat16)
    kpe = kpe_ref[...].astype(jnp.float32)
    kpe_n = rotate(split_norm(kpe, kg_ref[...]))
    kpe_n = jnp.where(is_rope, kpe_n, 0.0)
    v = jnp.dot(ckv, wv_ref[...], preferred_element_type=jnp.float32)
    for h in range(B_HEADS):
        pair = v[:, (h // 2) * 128:(h // 2 + 1) * 128]
        vo_ref[:, h * 128:(h + 1) * 128] = _half_lanes(pair, h % 2, 1.0).astype(vo_ref.dtype)
    qscale = B_QK_DIM ** -0.5
    for h in range(B_HEADS):
        cols = slice(h * B_HEAD_PAD, (h + 1) * B_HEAD_PAD)
        qh = jnp.dot(cq, wq_ref[:, cols], preferred_element_type=jnp.float32)
        qh = rotate(split_norm(qh, qg_ref[...])) * qscale
        qo_ref[:, cols] = qh.astype(qo_ref.dtype)
        kh = jnp.dot(ckv, wk_ref[:, cols], preferred_element_type=jnp.float32)
        kh = jnp.where(is_nope, split_norm(kh, kg_ref[...]), 0.0) + kpe_n
        ko_ref[:, cols] = kh.astype(ko_ref.dtype)


def _latent_prep(proj, q_a_gain, kv_a_gain, wq, wk, wv, q_gain, k_gain, cos, sa, sb, seq):
    t = proj.shape[0]
    pos_blocks = seq // ROW_TILE
    full = lambda shape: pl.BlockSpec(shape, lambda i: (0, 0))
    tab = pl.BlockSpec((ROW_TILE, B_HEAD_PAD), lambda i: (i % pos_blocks, 0))
    return pl.pallas_call(
        _latent_prep_kernel,
        grid=(t // ROW_TILE,),
        in_specs=[
            pl.BlockSpec((ROW_TILE, 256), lambda i: (i, SEC_BCQ // 256)),
            pl.BlockSpec((ROW_TILE, 128), lambda i: (i, SEC_BCKV // 128)),
            pl.BlockSpec((ROW_TILE, 128), lambda i: (i, SEC_BKPE // 128)),
            full((1, B_Q_RANK)), full((1, B_KV_RANK)),
            full((B_Q_RANK, B_HEADS * B_HEAD_PAD)), full((B_KV_RANK, B_HEADS * B_HEAD_PAD)),
            full((B_KV_RANK, B_HEADS * B_V_DIM)),
            full((1, B_HEAD_PAD)), full((1, B_HEAD_PAD)),
            tab, tab, tab,
        ],
        out_specs=[
            pl.BlockSpec((ROW_TILE, B_HEADS * B_HEAD_PAD), lambda i: (i, 0)),
            pl.BlockSpec((ROW_TILE, B_HEADS * B_HEAD_PAD), lambda i: (i, 0)),
            pl.BlockSpec((ROW_TILE, B_HEADS * B_HEAD_PAD), lambda i: (i, 0)),
        ],
        out_shape=[
            jax.ShapeDtypeStruct((t, B_HEADS * B_HEAD_PAD), jnp.bfloat16),
            jax.ShapeDtypeStruct((t, B_HEADS * B_HEAD_PAD), jnp.bfloat16),
            jax.ShapeDtypeStruct((t, B_HEADS * B_HEAD_PAD), jnp.bfloat16),
        ],
        compiler_params=pltpu.CompilerParams(dimension_semantics=("arbitrary",), vmem_limit_bytes=VMEM_LIMIT),
        name="latent_prep",
    )(proj, proj, proj, q_a_gain, kv_a_gain, wq, wk, wv, q_gain, k_gain, cos, sa, sb)


def _attn_b_kernel(q_ref, k_ref, v_ref, o_ref):
    i = pl.program_id(1)
    query_chunk = lax.broadcasted_iota(jnp.int32, (B_TILE, B_TILE), 0) // CHUNK
    key_chunk = lax.broadcasted_iota(jnp.int32, (B_TILE, B_TILE), 1) // CHUNK
    diag_ok = key_chunk <= query_chunk

    def step(j, carry, masked):
        rows = pl.ds(pl.multiple_of(j * B_TILE, B_TILE), B_TILE)
        out = []
        for h in range(B_HEADS):
            m, acc = carry[h]
            head = slice(h * B_HEAD_PAD, (h + 1) * B_HEAD_PAD)
            s = _nt_dot(q_ref[0, :, head], k_ref[0, rows, head])
            if masked:
                s = jnp.where(diag_ok, s, NEG_INF)
            m_new = jnp.maximum(m, jnp.max(s, axis=-1, keepdims=True))
            e = jnp.exp(s - m_new).astype(jnp.bfloat16)
            acc = jnp.exp(m - m_new) * acc + jnp.dot(e, v_ref[0, rows, head], preferred_element_type=jnp.float32)
            out.append((m_new, acc))
        return tuple(out)

    init = tuple((jnp.full((B_TILE, 1), NEG_INF, jnp.float32), jnp.zeros((B_TILE, B_HEAD_PAD), jnp.float32))
                 for _ in range(B_HEADS))
    carry = lax.fori_loop(0, i, lambda j, c: step(j, c, False), init)
    carry = step(i, carry, True)
    for pair in range(B_HEADS // 2):
        halves = [carry[2 * pair + parity][1] for parity in range(2)]
        o_ref[0, :, pair * 128:(pair + 1) * 128] = _merge_halves(halves).astype(o_ref.dtype)


def _attn_b(qb, kb, vb):
    b, s, _ = qb.shape
    return pl.pallas_call(
        _attn_b_kernel,
        grid=(b, s // B_TILE),
        in_specs=[
            pl.BlockSpec((1, B_TILE, B_HEADS * B_HEAD_PAD), lambda bi, i: (bi, i, 0)),
            pl.BlockSpec((1, s, B_HEADS * B_HEAD_PAD), lambda bi, i: (bi, 0, 0)),
            pl.BlockSpec((1, s, B_HEADS * B_HEAD_PAD), lambda bi, i: (bi, 0, 0)),
        ],
        out_specs=pl.BlockSpec((1, B_TILE, B_HEADS * B_V_DIM), lambda bi, i: (bi, i, 0)),
        out_shape=jax.ShapeDtypeStruct((b, s, B_HEADS * B_V_DIM), jnp.bfloat16),
        compiler_params=pltpu.CompilerParams(dimension_semantics=("arbitrary", "arbitrary"),
                                             vmem_limit_bytes=VMEM_LIMIT),
        name="attn_b",
    )(qb, kb, vb)


def _attn_c_kernel(q_ref, k_ref, v_ref, bias_ref, o_ref, k_pad, v_pad):
    i = pl.program_id(1)
    seq = k_ref.shape[1]
    width = C_HEADS * HEAD_DIM

    @pl.when(i == 0)
    def _():
        for parity in range(2):
            k_pad[parity, 0:C_PAD, :] = jnp.zeros((C_PAD, width), k_pad.dtype)
            v_pad[parity, 0:C_PAD, :] = jnp.zeros((C_PAD, width), v_pad.dtype)

        def fill(r, carry):
            rows = pl.ds(pl.multiple_of(r * 256, 256), 256)
            dst = pl.ds(pl.multiple_of(r * 256 + C_PAD, 256), 256)
            k = k_ref[0, rows, :]
            v = v_ref[0, rows, :]
            for parity in range(2):
                k_pad[parity, dst, :] = _half_lanes(k, parity)
                v_pad[parity, dst, :] = _half_lanes(v, parity, fill=1.0)
            return carry

        lax.fori_loop(0, seq // 256, fill, 0)

    start = pl.multiple_of(i * Q_TILE, Q_TILE)
    band = pl.ds(start, C_BAND)
    key_ok = lax.broadcasted_iota(jnp.int32, (C_BAND, Q_TILE), 0) + (start - C_PAD) >= 0
    for slab in range(width // 128):
        lanes = slice(slab * 128, (slab + 1) * 128)
        q = q_ref[0, :, lanes]
        halves = []
        for parity in range(2):
            s = _nt_dot(k_pad[parity, band, lanes], q) + bias_ref[2 * slab + parity]
            halves.append(_softmax_pv(s, key_ok, v_pad[parity, band, lanes]))
        o_ref[0, :, lanes] = _merge_halves(halves).astype(o_ref.dtype)


def _attn_c(proj3, bias):
    b, s, _ = proj3.shape
    width = C_HEADS * HEAD_DIM
    return pl.pallas_call(
        _attn_c_kernel,
        grid=(b, s // Q_TILE),
        in_specs=[
            pl.BlockSpec((1, Q_TILE, width), lambda bi, i: (bi, i, SEC_CQ // width)),
            pl.BlockSpec((1, s, width), lambda bi, i: (bi, 0, SEC_CK // width)),
            pl.BlockSpec((1, s, width), lambda bi, i: (bi, 0, SEC_CV // width)),
            pl.BlockSpec((C_HEADS, C_BAND, Q_TILE), lambda bi, i: (0, 0, 0)),
        ],
        out_specs=pl.BlockSpec((1, Q_TILE, width), lambda bi, i: (bi, i, 0)),
        out_shape=jax.ShapeDtypeStruct((b, s, width), jnp.bfloat16),
        scratch_shapes=[pltpu.VMEM((2, s + C_PAD, width), jnp.bfloat16),
                        pltpu.VMEM((2, s + C_PAD, width), jnp.bfloat16)],
        compiler_params=pltpu.CompilerParams(dimension_semantics=("arbitrary", "arbitrary"),
                                             vmem_limit_bytes=VMEM_LIMIT),
        name="attn_c",
    )(proj3, proj3, proj3, bias)


def _out_proj_kernel(oa_ref, ob_ref, oc_ref, x_ref, ga_ref, gb_ref, gc_ref, w_ref, gf_ref, wr_ref, br_ref,
                     xo_ref, h_ref, dest_ref, tg_ref, cnt_ref, count):
    i = pl.program_id(0)
    cap = pl.num_programs(0) * TOKEN_TILE

    @pl.when(i == 0)
    def _():
        count[...] = jnp.zeros_like(count)

    acc = x_ref[...]
    parts = ((oa_ref, ga_ref, 0, 512), (ob_ref, gb_ref, 512, 256), (oc_ref, gc_ref, 768, 256))
    for ref, g_ref, off, width in parts:
        o = _rms(ref[...].astype(jnp.float32), g_ref[...]).astype(jnp.bfloat16)
        acc = acc + jnp.dot(o, w_ref[off:off + width, :], preferred_element_type=jnp.float32)
    xo_ref[...] = acc
    h = _rms(acc, gf_ref[...])
    _store_row_tiles(h_ref, h)
    h_hi = h.astype(jnp.bfloat16)
    h_lo = (h - h_hi.astype(jnp.float32)).astype(jnp.bfloat16)
    logits = (jnp.dot(h_hi, wr_ref[0], preferred_element_type=jnp.float32)
              + jnp.dot(h_lo, wr_ref[0], preferred_element_type=jnp.float32)
              + jnp.dot(h_hi, wr_ref[1], preferred_element_type=jnp.float32)) + br_ref[...]
    lane = lax.broadcasted_iota(jnp.int32, logits.shape, 1)
    work = jnp.where(lane < N_EXPERTS, logits, -jnp.inf)
    gate_out = jnp.zeros(logits.shape, jnp.float32)
    denom = jnp.zeros((logits.shape[0], 1), jnp.float32)
    top = None
    picks = []
    for k in range(TOP_K):
        m = jnp.max(work, axis=-1, keepdims=True)
        idx = jnp.min(jnp.where(work == m, lane, ROUTER_PAD), axis=-1, keepdims=True)
        pick = lane == idx
        work = jnp.where(pick, -jnp.inf, work)
        top = m if top is None else top
        e = jnp.exp(m - top)
        denom = denom + e
        gate_out = jnp.where(lane == k, e, gate_out)
        picks.append((idx, pick))
    tg_ref[...] = gate_out / denom
    chosen = jnp.zeros(logits.shape, jnp.float32)
    for _, pick in picks:
        chosen = jnp.where(pick, 1.0, chosen)
    r_i = lax.broadcasted_iota(jnp.int32, (TOKEN_TILE, TOKEN_TILE), 0)
    c_i = lax.broadcasted_iota(jnp.int32, (TOKEN_TILE, TOKEN_TILE), 1)
    earlier = jnp.where(c_i < r_i, 1.0, 0.0).astype(jnp.bfloat16)
    before = jnp.dot(earlier, chosen.astype(jnp.bfloat16), preferred_element_type=jnp.float32) + count[...]
    dest_out = jnp.zeros(logits.shape, jnp.int32)
    for k, (idx, pick) in enumerate(picks):
        rank = jnp.sum(jnp.where(pick, before, 0.0), axis=-1, keepdims=True).astype(jnp.int32)
        dest_out = jnp.where(lane == k, idx * cap + rank, dest_out)
    dest_ref[...] = dest_out
    count[...] = count[...] + jnp.sum(chosen, axis=0, keepdims=True)
    cnt_ref[...] = count[...].astype(jnp.int32)


def _out_proj(oa, ob, oc, x2, ga, gb, gc, w, gf, wr, br):
    t = x2.shape[0]
    row = lambda width: pl.BlockSpec((TOKEN_TILE, width), lambda i: (i, 0))
    full = lambda shape: pl.BlockSpec(shape, lambda i: (0, 0))
    return pl.pallas_call(
        _out_proj_kernel,
        grid=(t // TOKEN_TILE,),
        in_specs=[row(512), row(256), row(256), row(D_MODEL), full((1, 512)), full((1, 256)), full((1, 256)),
                  full((D_MODEL, D_MODEL)), full((1, D_MODEL)),
                  pl.BlockSpec((2, D_MODEL, ROUTER_PAD), lambda i: (0, 0, 0)), full((1, ROUTER_PAD))],
        out_specs=[row(D_MODEL), pl.BlockSpec((TOKEN_TILE * LANE_TILES, 128), lambda i: (i, 0)),
                   row(ROUTER_PAD), row(ROUTER_PAD), full((1, ROUTER_PAD))],
        out_shape=[jax.ShapeDtypeStruct((t, D_MODEL), jnp.float32),
                   jax.ShapeDtypeStruct((t * LANE_TILES, 128), jnp.float32),
                   jax.ShapeDtypeStruct((t, ROUTER_PAD), jnp.int32),
                   jax.ShapeDtypeStruct((t, ROUTER_PAD), jnp.float32),
                   jax.ShapeDtypeStruct((1, ROUTER_PAD), jnp.int32)],
        scratch_shapes=[pltpu.VMEM((1, ROUTER_PAD), jnp.float32)],
        compiler_params=pltpu.CompilerParams(dimension_semantics=("arbitrary",), vmem_limit_bytes=VMEM_LIMIT),
        name="out_proj",
    )(oa, ob, oc, x2, ga, gb, gc, w, gf, wr, br)


def _dispatch_kernel(dest_hbm, h_ref, x_hbm, idx_smem, isem, dsem):
    i = pl.program_id(0)
    nt = pl.num_programs(0)
    slot = i % 2

    def idx_copy(tile, s):
        return pltpu.make_async_copy(dest_hbm.at[tile], _index_slot(idx_smem, s), isem.at[s])

    @pl.when(i == 0)
    def _():
        idx_copy(0, 0).start()

    idx_copy(i, slot).wait()

    @pl.when(i + 1 < nt)
    def _():
        idx_copy(i + 1, 1 - slot).start()

    base = slot * TILE_SLOTS

    def body(r, carry):
        for k in range(TOP_K):
            dst = idx_smem[base + r * TOP_K + k]
            pltpu.make_async_copy(_row_tile(h_ref, r), _row_tile(x_hbm, dst), dsem).start(priority=k % 2)
        return carry

    lax.fori_loop(0, TOKEN_TILE, body, 0, unroll=4)
    for _ in range(TOP_K):
        pltpu.make_async_copy(h_ref, x_hbm.at[pl.ds(0, TOKEN_TILE * LANE_TILES)], dsem).wait()


def _dispatch(dest_tiles, h3, n_rows):
    t = h3.shape[0] // LANE_TILES
    return pl.pallas_call(
        _dispatch_kernel,
        grid=(t // TOKEN_TILE,),
        in_specs=[pl.BlockSpec(memory_space=pl.ANY),
                  pl.BlockSpec((TOKEN_TILE * LANE_TILES, 128), lambda i: (i, 0))],
        out_specs=pl.BlockSpec(memory_space=pl.ANY),
        out_shape=jax.ShapeDtypeStruct((n_rows * LANE_TILES, 128), jnp.float32),
        scratch_shapes=[pltpu.SMEM((2 * TILE_SLOTS,), jnp.int32), pltpu.SemaphoreType.DMA((2,)),
                        pltpu.SemaphoreType.DMA],
        compiler_params=pltpu.CompilerParams(dimension_semantics=("arbitrary",), vmem_limit_bytes=VMEM_LIMIT),
        name="dispatch",
    )(dest_tiles, h3)


def _moe_ffn_kernel(blk_e_ref, blk_row_ref, blk_valid_ref, blk_first_ref, blk_next_ref, blk_slot_ref,
                    x_ref, wgu_hbm, bgu_ref, wd_hbm, bd_ref, y_ref, wgu_f32, wd_f32, wgu_bf, wd_bf, gsem, dsem,
                    *, layer):
    i = pl.program_id(0)
    n_valid = blk_valid_ref[i]

    def weight_copies(e, s):
        return (pltpu.make_async_copy(wgu_hbm.at[layer, e], wgu_f32.at[s], gsem.at[s]),
                pltpu.make_async_copy(wd_hbm.at[layer, e], wd_f32.at[s], dsem.at[s]))

    @pl.when(i == 0)
    def _():
        for copy in weight_copies(blk_e_ref[0], blk_slot_ref[0]):
            copy.start()

    @pl.when(blk_first_ref[i] == 1)
    def _():
        s = blk_slot_ref[i]
        for copy in weight_copies(blk_e_ref[i], s):
            copy.wait()
        nxt = blk_next_ref[i]

        @pl.when(nxt >= 0)
        def _():
            for copy in weight_copies(nxt, 1 - s):
                copy.start()

        for c in range(4):
            cols = slice(c * 512, (c + 1) * 512)
            wgu_bf[:, cols] = wgu_f32[s, :, cols].astype(jnp.bfloat16)
        for c in range(2):
            cols = slice(c * 512, (c + 1) * 512)
            wd_bf[:, cols] = wd_f32[s, :, cols].astype(jnp.bfloat16)

    @pl.when(n_valid > 0)
    def _():
        live = lax.broadcasted_iota(jnp.int32, (EXPERT_ROWS, 128), 0) < n_valid
        x = jnp.concatenate([jnp.where(live, _load_row_tiles(x_ref, 0, EXPERT_ROWS, c), 0.0).astype(jnp.bfloat16)
                             for c in range(LANE_TILES)], axis=1)
        glu = jnp.dot(x, wgu_bf[:, 0:D_EXPERT], preferred_element_type=jnp.float32) + bgu_ref[0, 0, :, 0:D_EXPERT]
        lin = jnp.dot(x, wgu_bf[:, D_EXPERT:], preferred_element_type=jnp.float32) + bgu_ref[0, 0, :, D_EXPERT:]
        glu = jnp.minimum(glu, SWIGLU_LIMIT)
        lin = jnp.clip(lin, -SWIGLU_LIMIT, SWIGLU_LIMIT)
        act = glu * jax.nn.sigmoid(SWIGLU_ALPHA * glu) * (lin + 1.0)
        y = jnp.dot(act.astype(jnp.bfloat16), wd_bf[...], preferred_element_type=jnp.float32) + bd_ref[0, 0]
        _store_row_tiles(y_ref, y)

    @pl.when(n_valid <= 0)
    def _():
        y_ref[...] = jnp.zeros_like(y_ref)


def _moe_ffn(layer, table, x_rows, w_gu, b_gu, w_down, b_down):
    n_blocks = table[0].shape[0]
    depth = w_gu.shape[0]
    rows = lambda i, be, br, *_: (br[i], 0)
    expert = lambda i, be, *_: (layer, be[i], 0, 0)
    grid_spec = pltpu.PrefetchScalarGridSpec(
        num_scalar_prefetch=len(table),
        grid=(n_blocks,),
        in_specs=[
            pl.BlockSpec((EXPERT_ROWS * LANE_TILES, 128), rows),
            pl.BlockSpec(memory_space=pl.ANY),
            pl.BlockSpec((1, 1, 1, 2 * D_EXPERT), expert),
            pl.BlockSpec(memory_space=pl.ANY),
            pl.BlockSpec((1, 1, 1, D_MODEL), expert),
        ],
        out_specs=pl.BlockSpec((EXPERT_ROWS * LANE_TILES, 128), rows),
        scratch_shapes=[pltpu.VMEM((2, D_MODEL, 2 * D_EXPERT), jnp.float32),
                        pltpu.VMEM((2, D_EXPERT, D_MODEL), jnp.float32),
                        pltpu.VMEM((D_MODEL, 2 * D_EXPERT), jnp.bfloat16),
                        pltpu.VMEM((D_EXPERT, D_MODEL), jnp.bfloat16),
                        pltpu.SemaphoreType.DMA((2,)), pltpu.SemaphoreType.DMA((2,))],
    )
    return pl.pallas_call(
        functools.partial(_moe_ffn_kernel, layer=layer),
        grid_spec=grid_spec,
        out_shape=jax.ShapeDtypeStruct(x_rows.shape, jnp.float32),
        compiler_params=pltpu.CompilerParams(dimension_semantics=("arbitrary",), vmem_limit_bytes=VMEM_LIMIT),
        name="moe_ffn",
    )(*table, x_rows, w_gu, b_gu.reshape(depth, N_EXPERTS, 1, -1), w_down, b_down.reshape(depth, N_EXPERTS, 1, -1))


def _combine_kernel(dest_hbm, y_hbm, x_ref, g_ref, *rest, project):
    if project:
        gain_ref, w_ref, hg_ref, xo_ref, o_ref, ybuf, idx_smem, isem, gsem = rest
    else:
        xo_ref, ybuf, idx_smem, isem, gsem = rest
    i = pl.program_id(0)
    nt = pl.num_programs(0)
    slot = i % 2

    def idx_copy(tile, s):
        return pltpu.make_async_copy(dest_hbm.at[tile], _index_slot(idx_smem, s), isem.at[s])

    def issue_gather(s):
        base = s * TILE_SLOTS

        def body(r, carry):
            for k in range(TOP_K):
                src = idx_smem[base + r * TOP_K + k]
                pltpu.make_async_copy(_row_tile(y_hbm, src), _row_tile(ybuf.at[s], k * TOKEN_TILE + r),
                                      gsem.at[s]).start(priority=k % 2)
            return carry
        lax.fori_loop(0, TOKEN_TILE, body, 0, unroll=4)

    @pl.when(i == 0)
    def _():
        idx_copy(0, 0).start()
        idx_copy(0, 0).wait()
        issue_gather(0)

        @pl.when(nt > 1)
        def _():
            idx_copy(1, 1).start()

    @pl.when(i + 1 < nt)
    def _():
        idx_copy(i + 1, 1 - slot).wait()
        issue_gather(1 - slot)

    pltpu.make_async_copy(y_hbm.at[pl.ds(0, TOP_K * TOKEN_TILE * LANE_TILES)], ybuf.at[slot], gsem.at[slot]).wait()

    @pl.when(i + 2 < nt)
    def _():
        idx_copy(i + 2, slot).start()

    g = g_ref[...]
    gates = [jnp.broadcast_to(g[:, k:k + 1], (TOKEN_TILE, 128)) for k in range(TOP_K)]
    cols = []
    for c in range(LANE_TILES):
        acc = x_ref[:, c * 128:(c + 1) * 128]
        for k in range(TOP_K):
            acc = acc + gates[k] * _load_row_tiles(ybuf.at[slot], k * TOKEN_TILE, TOKEN_TILE, c)
        cols.append(acc)
    x = jnp.concatenate(cols, axis=1)
    xo_ref[...] = x
    if project:
        _project(_rms(x, gain_ref[...]).astype(jnp.bfloat16), w_ref, hg_ref, o_ref)


def _combine_call(dest_tiles, y_rows, x2, gates, gain=None, w=None, head_gain=None):
    t = x2.shape[0]
    project = w is not None
    row = lambda width: pl.BlockSpec((TOKEN_TILE, width), lambda i: (i, 0))
    full = lambda shape: pl.BlockSpec(shape, lambda i: (0, 0))
    in_specs = [pl.BlockSpec(memory_space=pl.ANY), pl.BlockSpec(memory_space=pl.ANY), row(D_MODEL), row(ROUTER_PAD)]
    out_specs = [row(D_MODEL)]
    out_shape = [jax.ShapeDtypeStruct((t, D_MODEL), jnp.float32)]
    args = [dest_tiles, y_rows, x2, gates]
    if project:
        in_specs += [full((1, D_MODEL)), full((D_MODEL, PROJ_WIDTH)), full((1, PROJ_WIDTH))]
        out_specs.append(row(PROJ_WIDTH))
        out_shape.append(jax.ShapeDtypeStruct((t, PROJ_WIDTH), jnp.bfloat16))
        args += [gain, w, head_gain]
    out = pl.pallas_call(
        functools.partial(_combine_kernel, project=project),
        grid=(t // TOKEN_TILE,),
        in_specs=in_specs, out_specs=out_specs, out_shape=out_shape,
        scratch_shapes=[pltpu.VMEM((2, TOP_K * TOKEN_TILE * LANE_TILES, 128), jnp.float32),
                        pltpu.SMEM((2 * TILE_SLOTS,), jnp.int32),
                        pltpu.SemaphoreType.DMA((2,)), pltpu.SemaphoreType.DMA((2,))],
        compiler_params=pltpu.CompilerParams(dimension_semantics=("arbitrary",), vmem_limit_bytes=VMEM_LIMIT),
        name="combine_in_proj" if project else "combine",
    )(*args)
    return out if project else out[0]


def _t5_bucket(rel):
    half = T5_BUCKETS // 2
    max_exact = half // 2
    ret = np.where(rel > 0, half, 0)
    n = np.abs(rel)
    nf = np.maximum(n, 1).astype(np.float32)
    large = max_exact + (np.log(nf / max_exact) / math.log(T5_MAX_DIST / max_exact)
                         * (half - max_exact)).astype(np.int32)
    large = np.minimum(large, half - 1)
    return ret + np.where(n < max_exact, n, large)


def _band_ok(n_prev, band):
    q_chunk = np.arange(Q_TILE)[:, None] // CHUNK + n_prev
    k_chunk = np.arange(band)[None, :] // CHUNK
    return (k_chunk <= q_chunk) & (k_chunk >= q_chunk - n_prev)


def _toeplitz(u, band):
    heads, length = u.shape
    flat = jnp.broadcast_to(u[:, None, :], (heads, Q_TILE, length)).reshape(heads, Q_TILE * length)
    skew = flat[:, :Q_TILE * (length - 1)].reshape(heads, Q_TILE, length - 1)
    return skew[:, :, Q_TILE - 1:Q_TILE - 1 + band]


def _a_bias(t5_bias):
    rel = np.arange(Q_TILE + A_BAND) - (Q_TILE - 1) - A_PAD
    bias = _toeplitz(t5_bias[_t5_bucket(rel)].T.astype(jnp.float32), A_BAND)
    bias = jnp.where(_band_ok(A_PREV_CHUNKS, A_BAND)[None], bias, NEG_INF)
    pairs = [jnp.concatenate([bias[4 * hk + parity], bias[4 * hk + 2 + parity]], axis=0)
             for hk in range(A_KV_HEADS) for parity in range(2)]
    return jnp.stack(pairs).transpose(0, 2, 1)


def _c_bias(rel_table):
    dist = C_PAD + (Q_TILE - 1) - np.arange(Q_TILE + C_BAND)
    idx = np.clip(dist, -(CHUNK - 1), C_REL_CLIP) + (CHUNK - 1)
    bias = _toeplitz(rel_table[idx].T.astype(jnp.float32), C_BAND)
    return jnp.where(_band_ok(C_PREV_CHUNKS, C_BAND)[None], bias, NEG_INF).transpose(0, 2, 1)


def _rope_tables(seq):
    half = B_ROPE_DIM // 2
    inv_freq = ROPE_THETA ** (-(jnp.arange(half, dtype=jnp.float32) / half))
    ang = jnp.arange(seq, dtype=jnp.float32)[:, None] * inv_freq[None, :]
    cos, sin = jnp.cos(ang), jnp.sin(ang)
    zeros = jnp.zeros((seq, B_NOPE_DIM), jnp.float32)
    tail = jnp.zeros((seq, B_HEAD_PAD - B_QK_DIM), jnp.float32)
    zh = jnp.zeros((seq, half), jnp.float32)
    cos_t = jnp.concatenate([zeros + 1.0, cos, cos, tail + 1.0], axis=1)
    sa = jnp.concatenate([zeros, -sin, zh, tail], axis=1)
    sb = jnp.concatenate([zeros, zh, sin, tail], axis=1)
    return cos_t, sa, sb


def _regroup_w_in(w):
    splits = np.cumsum((512, 128, 128, 256, 128, 32, 256, 256))
    aq, ak, av, bcq, bckv, bkpe, cq, ck, cv = jnp.split(w, splits.tolist(), axis=1)
    z = lambda n: jnp.zeros((w.shape[0], n), w.dtype)
    return jnp.concatenate([aq, ak, av, bcq, bckv, z(64), bkpe, z(32), cq, ck, cv], axis=1).astype(jnp.bfloat16)


def _head_gain_row(a_q, a_k, c_q, c_k):
    scale = HEAD_DIM ** -0.5
    row = jnp.ones((PROJ_WIDTH,), jnp.float32)
    for sec, heads, gain in ((SEC_AQ, A_Q_HEADS, a_q * scale), (SEC_AKV, A_KV_HEADS, a_k),
                             (SEC_CQ, C_HEADS, c_q * scale), (SEC_CK, C_HEADS, c_k)):
        row = row.at[sec:sec + heads * HEAD_DIM].set(jnp.tile(gain, heads))
    return row.reshape(1, PROJ_WIDTH)


def _pad_heads(w, head_dim, keep):
    rank = w.shape[0]
    w = w.reshape(rank, B_HEADS, head_dim)[:, :, :keep]
    w = jnp.pad(w, ((0, 0), (0, 0), (0, B_HEAD_PAD - keep)))
    return w.reshape(rank, B_HEADS * B_HEAD_PAD).astype(jnp.bfloat16)


def _pad_lanes(g, offset, width):
    return jnp.pad(g, (offset, width - offset - g.shape[0])).reshape(1, width)


def _block_table(counts, t):
    tk = t * TOP_K
    blocks_per_expert = t // EXPERT_ROWS
    nblk = (counts + EXPERT_ROWS - 1) // EXPERT_ROWS
    blk_end = jnp.cumsum(nblk)
    blk_first = blk_end - nblk
    n_blocks = (tk + N_EXPERTS * (EXPERT_ROWS - 1) + EXPERT_ROWS - 1) // EXPERT_ROWS
    blk = jnp.arange(n_blocks, dtype=jnp.int32)
    blk_e = jnp.minimum(jnp.sum(blk[:, None] >= blk_end[None, :], axis=1), N_EXPERTS - 1).astype(jnp.int32)
    own = blk_e[:, None] == jnp.arange(N_EXPERTS, dtype=jnp.int32)[None, :]
    pick = lambda v: jnp.sum(jnp.where(own, v[None, :], 0), axis=1)
    j = blk - pick(blk_first)
    blk_valid = jnp.clip(pick(counts) - j * EXPERT_ROWS, 0, EXPERT_ROWS).astype(jnp.int32)
    spare = N_EXPERTS * blocks_per_expert
    blk_row = jnp.where(blk_valid > 0, blk_e * blocks_per_expert + j, spare).astype(jnp.int32)
    experts = jnp.arange(N_EXPERTS, dtype=jnp.int32)
    used = counts > 0
    later = jnp.logical_and(experts[None, :] > experts[:, None], used[None, :])
    nxt = jnp.min(jnp.where(later, experts[None, :], N_EXPERTS), axis=1)
    nxt = jnp.where(nxt < N_EXPERTS, nxt, -1)
    ordinal = jnp.cumsum(used.astype(jnp.int32)) - 1
    blk_first = jnp.logical_and(j == 0, blk_valid > 0).astype(jnp.int32)
    blk_next = pick(nxt).astype(jnp.int32)
    blk_slot = (pick(ordinal) % 2).astype(jnp.int32)
    return blk_e, blk_row, blk_valid, blk_first, blk_next, blk_slot


def kernel(x, attn_norm, w_in, a_q_norm, a_k_norm, a_sinks, t5_bias, b_q_a_norm, b_w_q_b, b_kv_a_norm, b_w_kv_b,
           b_q_norm, b_k_norm, c_q_norm, c_k_norm, c_rel_bias, a_out_norm, b_out_norm, c_out_norm, w_out, ffn_norm,
           w_router, b_router, w_gate_up, b_gate_up, w_down, b_down):
    b, s, d = x.shape
    t = b * s
    depth = w_in.shape[0]
    a_bias = _a_bias(t5_bias)
    cos_t, sin_a, sin_b = _rope_tables(s)
    x2 = x.reshape(t, d)
    head_gain = lambda l: _head_gain_row(a_q_norm[l], a_k_norm[l], c_q_norm[l], c_k_norm[l])
    proj = _in_proj(x2, attn_norm[0].reshape(1, d), _regroup_w_in(w_in[0]), head_gain(0))
    for l in range(depth):
        proj3 = proj.reshape(b, s, PROJ_WIDTH)
        o_a = _attn_a(proj3, a_bias, a_sinks[l])
        kv_w = b_w_kv_b[l].reshape(B_KV_RANK, B_HEADS, B_NOPE_DIM + B_V_DIM)
        wv = kv_w[:, :, B_NOPE_DIM:].reshape(B_KV_RANK, B_HEADS * B_V_DIM).astype(jnp.bfloat16)
        qb, kb, vb = _latent_prep(
            proj, b_q_a_norm[l].reshape(1, -1), b_kv_a_norm[l].reshape(1, -1),
            _pad_heads(b_w_q_b[l], B_QK_DIM, B_QK_DIM), _pad_heads(b_w_kv_b[l], B_NOPE_DIM + B_V_DIM, B_NOPE_DIM), wv,
            _pad_lanes(b_q_norm[l], 0, B_HEAD_PAD), _pad_lanes(b_k_norm[l], 0, B_HEAD_PAD), cos_t, sin_a, sin_b, s)
        o_b = _attn_b(qb.reshape(b, s, -1), kb.reshape(b, s, -1), vb.reshape(b, s, -1))
        o_c = _attn_c(proj3, _c_bias(c_rel_bias[l]))
        wr = jnp.pad(w_router[l], ((0, 0), (0, ROUTER_PAD - N_EXPERTS)))
        wr_hi = wr.astype(jnp.bfloat16)
        wr = jnp.stack([wr_hi, (wr - wr_hi.astype(jnp.float32)).astype(jnp.bfloat16)])
        br = jnp.pad(b_router[l], (0, ROUTER_PAD - N_EXPERTS)).reshape(1, ROUTER_PAD)
        x2, h3, dest, gates, counts = _out_proj(
            o_a.reshape(t, -1), o_b.reshape(t, -1), o_c.reshape(t, -1), x2,
            a_out_norm[l].reshape(1, -1), b_out_norm[l].reshape(1, -1), c_out_norm[l].reshape(1, -1),
            w_out[l].astype(jnp.bfloat16), ffn_norm[l].reshape(1, -1), wr, br)
        dest_tiles = dest[:, :TOP_K].reshape(t // TOKEN_TILE, TOKEN_TILE * TOP_K)
        table = _block_table(counts[0, :N_EXPERTS], t)
        n_rows = (N_EXPERTS * (t // EXPERT_ROWS) + 1) * EXPERT_ROWS
        x_rows = _dispatch(dest_tiles, h3, n_rows)
        y_rows = _moe_ffn(l, table, x_rows, w_gate_up, b_gate_up, w_down, b_down)
        if l + 1 < depth:
            x2, proj = _combine_call(dest_tiles, y_rows, x2, gates, attn_norm[l + 1].reshape(1, d),
                                     _regroup_w_in(w_in[l + 1]), head_gain(l + 1))
        else:
            x2 = _combine_call(dest_tiles, y_rows, x2, gates)
    return x2.reshape(b, s, d)
```

```python
import functools
import math

import jax
import jax.numpy as jnp
import numpy as np
from jax import lax
from jax.experimental import pallas as pl
from jax.experimental.pallas import tpu as pltpu

D_MODEL = 1024
CHUNK = 64
HEAD_DIM = 64
EPS = 1e-6
NEG_INF = -1e30
A_Q_HEADS = 8
A_KV_HEADS = 2
A_GROUP = 4
A_PREV_CHUNKS = 2
T5_BUCKETS = 32
T5_MAX_DIST = 128
B_HEADS = 4
B_Q_RANK = 256
B_KV_RANK = 128
B_NOPE_DIM = 64
B_ROPE_DIM = 32
B_QK_DIM = 96
B_V_DIM = 64
ROPE_THETA = 10000.0
C_HEADS = 4
C_PREV_CHUNKS = 8
C_REL_CLIP = 256
N_EXPERTS = 32
TOP_K = 4
D_EXPERT = 1024
SWIGLU_LIMIT = 7.0
SWIGLU_ALPHA = 1.702

SEC_AQ = 0
SEC_AKV = 512
SEC_BCQ = 768
SEC_BCKV = 1024
SEC_BKPE = 1152
SEC_CQ = 1280
SEC_CK = 1536
SEC_CV = 1792
PROJ_WIDTH = 2048
HEAD_NORM_SLABS = tuple(sec // 128 + j for sec, n in ((SEC_AQ, 4), (SEC_AKV, 1), (SEC_CQ, 2), (SEC_CK, 2))
                        for j in range(n))
B_HEAD_PAD = 128

Q_TILE = 128
A_BAND = Q_TILE + A_PREV_CHUNKS * CHUNK
C_BAND = Q_TILE + C_PREV_CHUNKS * CHUNK
A_PAD = A_PREV_CHUNKS * CHUNK
C_PAD = C_PREV_CHUNKS * CHUNK
B_TILE = 512
ROW_TILE = 512
EXPERT_ROWS = 256
ROUTER_PAD = 128
TOKEN_TILE = 256
TILE_SLOTS = TOKEN_TILE * TOP_K
LANE_TILES = D_MODEL // 128
VMEM_LIMIT = 48 * 1024 * 1024


def _rms(x, gain):
    return x * lax.rsqrt(jnp.mean(x * x, axis=-1, keepdims=True) + EPS) * gain


def _nt_dot(a, b):
    return lax.dot_general(a, b, (((1,), (1,)), ((), ())), preferred_element_type=jnp.float32)


def _load_row_tiles(ref, first_row, n_rows, c):
    return ref[pl.ds(first_row * LANE_TILES + c, n_rows, stride=LANE_TILES), :]


def _store_row_tiles(ref, value):
    for c in range(LANE_TILES):
        ref[pl.ds(c, value.shape[0], stride=LANE_TILES), :] = value[:, c * 128:(c + 1) * 128]


def _row_tile(ref, row):
    return ref.at[pl.ds(pl.multiple_of(row * LANE_TILES, LANE_TILES), LANE_TILES)]


def _index_slot(idx_smem, s):
    return idx_smem.at[pl.ds(pl.multiple_of(s * TILE_SLOTS, TILE_SLOTS), TILE_SLOTS)]


def _project(h, w_ref, hg_ref, o_ref):
    li = lax.broadcasted_iota(jnp.int32, (128, 128), 0) // HEAD_DIM
    lj = lax.broadcasted_iota(jnp.int32, (128, 128), 1) // HEAD_DIM
    same_head = jnp.where(li == lj, 1.0, 0.0).astype(jnp.bfloat16)
    for n in range(PROJ_WIDTH // 512):
        acc = jnp.dot(h, w_ref[:, n * 512:(n + 1) * 512], preferred_element_type=jnp.float32)
        slabs = []
        for j in range(4):
            slab = n * 4 + j
            a = acc[:, j * 128:(j + 1) * 128]
            if slab in HEAD_NORM_SLABS:
                sq = a * a
                hi = sq.astype(jnp.bfloat16)
                lo = (sq - hi.astype(jnp.float32)).astype(jnp.bfloat16)
                ss = (jnp.dot(hi, same_head, preferred_element_type=jnp.float32)
                      + jnp.dot(lo, same_head, preferred_element_type=jnp.float32))
                a = a * lax.rsqrt(ss * (1.0 / HEAD_DIM) + EPS) * hg_ref[:, slab * 128:(slab + 1) * 128]
            slabs.append(a.astype(o_ref.dtype))
        o_ref[:, n * 512:(n + 1) * 512] = jnp.concatenate(slabs, axis=1)


def _in_proj_kernel(x_ref, g_ref, w_ref, hg_ref, o_ref):
    _project(_rms(x_ref[...], g_ref[...]).astype(jnp.bfloat16), w_ref, hg_ref, o_ref)


def _in_proj(x2, gain, w, head_gain):
    t = x2.shape[0]
    return pl.pallas_call(
        _in_proj_kernel,
        grid=(t // ROW_TILE,),
        in_specs=[
            pl.BlockSpec((ROW_TILE, D_MODEL), lambda i: (i, 0)),
            pl.BlockSpec((1, D_MODEL), lambda i: (0, 0)),
            pl.BlockSpec((D_MODEL, PROJ_WIDTH), lambda i: (0, 0)),
            pl.BlockSpec((1, PROJ_WIDTH), lambda i: (0, 0)),
        ],
        out_specs=pl.BlockSpec((ROW_TILE, PROJ_WIDTH), lambda i: (i, 0)),
        out_shape=jax.ShapeDtypeStruct((t, PROJ_WIDTH), jnp.bfloat16),
        compiler_params=pltpu.CompilerParams(dimension_semantics=("arbitrary",), vmem_limit_bytes=VMEM_LIMIT),
        name="in_proj",
    )(x2, gain, w, head_gain)


def _half_lanes(x, parity, fill=0.0):
    lane = lax.broadcasted_iota(jnp.int32, x.shape, x.ndim - 1) % 128
    keep = lane < HEAD_DIM if parity == 0 else lane >= HEAD_DIM
    return jnp.where(keep, x, jnp.full_like(x, fill))


def _tn_dot(a, b):
    return lax.dot_general(a, b, (((0,), (0,)), ((), ())), preferred_element_type=jnp.float32)


def _softmax_pv(s, key_ok, v_aug, sink=None, value_half=0):
    s = jnp.where(key_ok, s, NEG_INF)
    m = jnp.max(s, axis=0, keepdims=True)
    if sink is not None:
        m = jnp.maximum(m, sink)
    e = jnp.exp(s - m).astype(jnp.bfloat16)
    o = _tn_dot(e, v_aug)
    if sink is not None:
        row = lax.broadcasted_iota(jnp.int32, (8, s.shape[1]), 0)
        e_sink = jnp.where(row == 0, jnp.exp(sink - m), 0.0).astype(jnp.bfloat16)
        first = lax.broadcasted_iota(jnp.int32, (8, 128), 0) == 0
        sel = _half_lanes(jnp.where(first, 1.0, 0.0), 1 - value_half).astype(jnp.bfloat16)
        o = o + _tn_dot(e_sink, sel)
    return o


def _merge_halves(halves):
    lane = lax.broadcasted_iota(jnp.int32, halves[0].shape, 1)
    normed = [o / pltpu.roll(o, HEAD_DIM, 1) for o in halves]
    return jnp.where(lane < HEAD_DIM, normed[0], normed[1])


def _attn_a_kernel(q_ref, kv_ref, bias_ref, sink_ref, o_ref, k_pad, v_pad):
    i = pl.program_id(1)
    seq = kv_ref.shape[1]

    @pl.when(i == 0)
    def _():
        for n in range(2 * A_KV_HEADS):
            k_pad[n, 0:A_PAD, :] = jnp.zeros((A_PAD, 128), k_pad.dtype)
            v_pad[n, 0:A_PAD, :] = jnp.zeros((A_PAD, 128), v_pad.dtype)

        def fill(r, carry):
            rows = pl.ds(pl.multiple_of(r * 256, 256), 256)
            dst = pl.ds(pl.multiple_of(r * 256 + A_PAD, 128), 256)
            kv = kv_ref[0, rows, :].astype(jnp.float32)
            for src, pad, other in ((kv[:, 0:128], k_pad, 0.0), (kv[:, 128:256], v_pad, 1.0)):
                swapped = pltpu.roll(src, HEAD_DIM, 1)
                pad[0, dst, :] = _half_lanes(src, 0, other).astype(pad.dtype)
                pad[1, dst, :] = _half_lanes(swapped, 1, other).astype(pad.dtype)
                pad[2, dst, :] = _half_lanes(swapped, 0, other).astype(pad.dtype)
                pad[3, dst, :] = _half_lanes(src, 1, other).astype(pad.dtype)
            return carry

        lax.fori_loop(0, seq // 256, fill, 0)

    start = pl.multiple_of(i * Q_TILE, Q_TILE)
    band = pl.ds(start, A_BAND)
    key_ok = lax.broadcasted_iota(jnp.int32, (A_BAND, 2 * Q_TILE), 0) + (start - A_PAD) >= 0
    first_slab = lax.broadcasted_iota(jnp.int32, (1, 2 * Q_TILE), 1) < Q_TILE
    for hk in range(A_KV_HEADS):
        q2 = jnp.concatenate([q_ref[0, :, (2 * hk) * 128:(2 * hk + 1) * 128],
                              q_ref[0, :, (2 * hk + 1) * 128:(2 * hk + 2) * 128]], axis=0)
        halves = []
        for parity in range(2):
            sink = jnp.where(first_slab, sink_ref[4 * hk + parity], sink_ref[4 * hk + 2 + parity])
            s = _nt_dot(k_pad[2 * hk + parity, band, :], q2) + bias_ref[2 * hk + parity]
            halves.append(_softmax_pv(s, key_ok, v_pad[2 * hk + parity, band, :], sink, parity))
        out = _merge_halves(halves)
        o_ref[0, :, (2 * hk) * 128:(2 * hk + 1) * 128] = out[0:Q_TILE].astype(o_ref.dtype)
        o_ref[0, :, (2 * hk + 1) * 128:(2 * hk + 2) * 128] = out[Q_TILE:2 * Q_TILE].astype(o_ref.dtype)


def _attn_a(proj3, bias, sinks):
    b, s, _ = proj3.shape
    return pl.pallas_call(
        _attn_a_kernel,
        grid=(b, s // Q_TILE),
        in_specs=[
            pl.BlockSpec((1, Q_TILE, 512), lambda bi, i: (bi, i, SEC_AQ // 512)),
            pl.BlockSpec((1, s, 256), lambda bi, i: (bi, 0, SEC_AKV // 256)),
            pl.BlockSpec((2 * A_KV_HEADS, 2 * Q_TILE, A_BAND), lambda bi, i: (0, 0, 0)),
            pl.BlockSpec(memory_space=pltpu.SMEM),
        ],
        out_specs=pl.BlockSpec((1, Q_TILE, 512), lambda bi, i: (bi, i, 0)),
        out_shape=jax.ShapeDtypeStruct((b, s, 512), jnp.bfloat16),
        scratch_shapes=[pltpu.VMEM((2 * A_KV_HEADS, s + A_PAD, 128), jnp.bfloat16),
                        pltpu.VMEM((2 * A_KV_HEADS, s + A_PAD, 128), jnp.bfloat16)],
        compiler_params=pltpu.CompilerParams(dimension_semantics=("arbitrary", "arbitrary"),
                                             vmem_limit_bytes=VMEM_LIMIT),
        name="attn_a",
    )(proj3, proj3, bias, sinks)


def _latent_prep_kernel(cq_ref, ckv_ref, kpe_ref, qag_ref, kvag_ref, wq_ref, wk_ref, wv_ref,
                        qg_ref, kg_ref, cos_ref, sa_ref, sb_ref, qo_ref, ko_ref, vo_ref):
    lane = lax.broadcasted_iota(jnp.int32, (ROW_TILE, B_HEAD_PAD), 1)
    is_nope = lane < B_NOPE_DIM
    is_rope = jnp.logical_and(lane >= B_NOPE_DIM, lane < B_QK_DIM)
    cos = cos_ref[...]
    sa = sa_ref[...]
    sb = sb_ref[...]

    def split_norm(x, gain):
        sq = x * x
        nope = jnp.sum(jnp.where(is_nope, sq, 0.0), axis=-1, keepdims=True) * (1.0 / B_NOPE_DIM)
        rope = jnp.sum(jnp.where(is_rope, sq, 0.0), axis=-1, keepdims=True) * (1.0 / B_ROPE_DIM)
        r = jnp.where(is_nope, lax.rsqrt(nope + EPS), lax.rsqrt(rope + EPS))
        return x * r * gain

    def rotate(x):
        return x * cos + pltpu.roll(x, 112, 1) * sa + pltpu.roll(x, 16, 1) * sb

    cq = _rms(cq_ref[...].astype(jnp.float32), qag_ref[...]).astype(jnp.bfloat16)
    ckv = _rms(ckv_ref[...].astype(jnp.float32), kvag_ref[...]).astype(jnp.bfloat16)
    kpe = kpe_ref[...].astype(jnp.float32)
    kpe_n = rotate(split_norm(kpe, kg_ref[...]))
    kpe_n = jnp.where(is_rope, kpe_n, 0.0)
    v = jnp.dot(ckv, wv_ref[...], preferred_element_type=jnp.float32)
    for h in range(B_HEADS):
        pair = v[:, (h // 2) * 128:(h // 2 + 1) * 128]
        vo_ref[:, h * 128:(h + 1) * 128] = _half_lanes(pair, h % 2, 1.0).astype(vo_ref.dtype)
    qscale = B_QK_DIM ** -0.5
    for h in range(B_HEADS):
        cols = slice(h * B_HEAD_PAD, (h + 1) * B_HEAD_PAD)
        qh = jnp.dot(cq, wq_ref[:, cols], preferred_element_type=jnp.float32)
        qh = rotate(split_norm(qh, qg_ref[...])) * qscale
        qo_ref[:, cols] = qh.astype(qo_ref.dtype)
        kh = jnp.dot(ckv, wk_ref[:, cols], preferred_element_type=jnp.float32)
        kh = jnp.where(is_nope, split_norm(kh, kg_ref[...]), 0.0) + kpe_n
        ko_ref[:, cols] = kh.astype(ko_ref.dtype)


def _latent_prep(proj, q_a_gain, kv_a_gain, wq, wk, wv, q_gain, k_gain, cos, sa, sb, seq):
    t = proj.shape[0]
    pos_blocks = seq // ROW_TILE
    full = lambda shape: pl.BlockSpec(shape, lambda i: (0, 0))
    tab = pl.BlockSpec((ROW_TILE, B_HEAD_PAD), lambda i: (i % pos_blocks, 0))
    return pl.pallas_call(
        _latent_prep_kernel,
        grid=(t // ROW_TILE,),
        in_specs=[
            pl.BlockSpec((ROW_TILE, 256), lambda i: (i, SEC_BCQ // 256)),
            pl.BlockSpec((ROW_TILE, 128), lambda i: (i, SEC_BCKV // 128)),
            pl.BlockSpec((ROW_TILE, 128), lambda i: (i, SEC_BKPE // 128)),
            full((1, B_Q_RANK)), full((1, B_KV_RANK)),
            full((B_Q_RANK, B_HEADS * B_HEAD_PAD)), full((B_KV_RANK, B_HEADS * B_HEAD_PAD)),
            full((B_KV_RANK, B_HEADS * B_V_DIM)),
            full((1, B_HEAD_PAD)), full((1, B_HEAD_PAD)),
            tab, tab, tab,
        ],
        out_specs=[
            pl.BlockSpec((ROW_TILE, B_HEADS * B_HEAD_PAD), lambda i: (i, 0)),
            pl.BlockSpec((ROW_TILE, B_HEADS * B_HEAD_PAD), lambda i: (i, 0)),
            pl.BlockSpec((ROW_TILE, B_HEADS * B_HEAD_PAD), lambda i: (i, 0)),
        ],
        out_shape=[
            jax.ShapeDtypeStruct((t, B_HEADS * B_HEAD_PAD), jnp.bfloat16),
            jax.ShapeDtypeStruct((t, B_HEADS * B_HEAD_PAD), jnp.bfloat16),
            jax.ShapeDtypeStruct((t, B_HEADS * B_HEAD_PAD), jnp.bfloat16),
        ],
        compiler_params=pltpu.CompilerParams(dimension_semantics=("arbitrary",), vmem_limit_bytes=VMEM_LIMIT),
        name="latent_prep",
    )(proj, proj, proj, q_a_gain, kv_a_gain, wq, wk, wv, q_gain, k_gain, cos, sa, sb)


def _attn_b_kernel(q_ref, k_ref, v_ref, o_ref):
    i = pl.program_id(1)
    query_chunk = lax.broadcasted_iota(jnp.int32, (B_TILE, B_TILE), 0) // CHUNK
    key_chunk = lax.broadcasted_iota(jnp.int32, (B_TILE, B_TILE), 1) // CHUNK
    diag_ok = key_chunk <= query_chunk

    def step(j, carry, masked):
        rows = pl.ds(pl.multiple_of(j * B_TILE, B_TILE), B_TILE)
        out = []
        for h in range(B_HEADS):
            m, acc = carry[h]
            head = slice(h * B_HEAD_PAD, (h + 1) * B_HEAD_PAD)
            s = _nt_dot(q_ref[0, :, head], k_ref[0, rows, head])
            if masked:
                s = jnp.where(diag_ok, s, NEG_INF)
            m_new = jnp.maximum(m, jnp.max(s, axis=-1, keepdims=True))
            e = jnp.exp(s - m_new).astype(jnp.bfloat16)
            acc = jnp.exp(m - m_new) * acc + jnp.dot(e, v_ref[0, rows, head], preferred_element_type=jnp.float32)
            out.append((m_new, acc))
        return tuple(out)

    init = tuple((jnp.full((B_TILE, 1), NEG_INF, jnp.float32), jnp.zeros((B_TILE, B_HEAD_PAD), jnp.float32))
                 for _ in range(B_HEADS))
    carry = lax.fori_loop(0, i, lambda j, c: step(j, c, False), init)
    carry = step(i, carry, True)
    for pair in range(B_HEADS // 2):
        halves = [carry[2 * pair + parity][1] for parity in range(2)]
        o_ref[0, :, pair * 128:(pair + 1) * 128] = _merge_halves(halves).astype(o_ref.dtype)


def _attn_b(qb, kb, vb):
    b, s, _ = qb.shape
    return pl.pallas_call(
        _attn_b_kernel,
        grid=(b, s // B_TILE),
        in_specs=[
            pl.BlockSpec((1, B_TILE, B_HEADS * B_HEAD_PAD), lambda bi, i: (bi, i, 0)),
            pl.BlockSpec((1, s, B_HEADS * B_HEAD_PAD), lambda bi, i: (bi, 0, 0)),
            pl.BlockSpec((1, s, B_HEADS * B_HEAD_PAD), lambda bi, i: (bi, 0, 0)),
        ],
        out_specs=pl.BlockSpec((1, B_TILE, B_HEADS * B_V_DIM), lambda bi, i: (bi, i, 0)),
        out_shape=jax.ShapeDtypeStruct((b, s, B_HEADS * B_V_DIM), jnp.bfloat16),
        compiler_params=pltpu.CompilerParams(dimension_semantics=("arbitrary", "arbitrary"),
                                             vmem_limit_bytes=VMEM_LIMIT),
        name="attn_b",
    )(qb, kb, vb)


def _attn_c_kernel(q_ref, k_ref, v_ref, bias_ref, o_ref, k_pad, v_pad):
    i = pl.program_id(1)
    seq = k_ref.shape[1]
    width = C_HEADS * HEAD_DIM

    @pl.when(i == 0)
    def _():
        for parity in range(2):
            k_pad[parity, 0:C_PAD, :] = jnp.zeros((C_PAD, width), k_pad.dtype)
            v_pad[parity, 0:C_PAD, :] = jnp.zeros((C_PAD, width), v_pad.dtype)

        def fill(r, carry):
            rows = pl.ds(pl.multiple_of(r * 256, 256), 256)
            dst = pl.ds(pl.multiple_of(r * 256 + C_PAD, 256), 256)
            k = k_ref[0, rows, :]
            v = v_ref[0, rows, :]
            for parity in range(2):
                k_pad[parity, dst, :] = _half_lanes(k, parity)
                v_pad[parity, dst, :] = _half_lanes(v, parity, fill=1.0)
            return carry

        lax.fori_loop(0, seq // 256, fill, 0)

    start = pl.multiple_of(i * Q_TILE, Q_TILE)
    band = pl.ds(start, C_BAND)
    key_ok = lax.broadcasted_iota(jnp.int32, (C_BAND, Q_TILE), 0) + (start - C_PAD) >= 0
    for slab in range(width // 128):
        lanes = slice(slab * 128, (slab + 1) * 128)
        q = q_ref[0, :, lanes]
        halves = []
        for parity in range(2):
            s = _nt_dot(k_pad[parity, band, lanes], q) + bias_ref[2 * slab + parity]
            halves.append(_softmax_pv(s, key_ok, v_pad[parity, band, lanes]))
        o_ref[0, :, lanes] = _merge_halves(halves).astype(o_ref.dtype)


def _attn_c(proj3, bias):
    b, s, _ = proj3.shape
    width = C_HEADS * HEAD_DIM
    return pl.pallas_call(
        _attn_c_kernel,
        grid=(b, s // Q_TILE),
        in_specs=[
            pl.BlockSpec((1, Q_TILE, width), lambda bi, i: (bi, i, SEC_CQ // width)),
            pl.BlockSpec((1, s, width), lambda bi, i: (bi, 0, SEC_CK // width)),
            pl.BlockSpec((1, s, width), lambda bi, i: (bi, 0, SEC_CV // width)),
            pl.BlockSpec((C_HEADS, C_BAND, Q_TILE), lambda bi, i: (0, 0, 0)),
        ],
        out_specs=pl.BlockSpec((1, Q_TILE, width), lambda bi, i: (bi, i, 0)),
        out_shape=jax.ShapeDtypeStruct((b, s, width), jnp.bfloat16),
        scratch_shapes=[pltpu.VMEM((2, s + C_PAD, width), jnp.bfloat16),
                        pltpu.VMEM((2, s + C_PAD, width), jnp.bfloat16)],
        compiler_params=pltpu.CompilerParams(dimension_semantics=("arbitrary", "arbitrary"),
                                             vmem_limit_bytes=VMEM_LIMIT),
        name="attn_c",
    )(proj3, proj3, proj3, bias)


def _out_proj_kernel(oa_ref, ob_ref, oc_ref, x_ref, ga_ref, gb_ref, gc_ref, w_ref, gf_ref, wr_ref, br_ref,
                     xo_ref, x_hbm, dest_ref, tg_ref, cnt_ref, count, hbuf, dvm, idx_smem, isem, dsem):
    i = pl.program_id(0)
    nt = pl.num_programs(0)
    cap = nt * TOKEN_TILE
    slot = i % 2

    def index_copies(s):
        return [pltpu.make_async_copy(
            dvm.at[s, k], idx_smem.at[pl.ds(pl.multiple_of(s * TILE_SLOTS + k * TOKEN_TILE, TOKEN_TILE), TOKEN_TILE)],
            isem.at[s]) for k in range(TOP_K)]

    def issue_rows(s):
        for copy in index_copies(s):
            copy.wait()
        base = s * TILE_SLOTS

        def body(r, carry):
            for k in range(TOP_K):
                dst = idx_smem[base + k * TOKEN_TILE + r]
                pltpu.make_async_copy(_row_tile(hbuf.at[s], r), _row_tile(x_hbm, dst), dsem.at[s]).start(priority=k % 2)
            return carry

        lax.fori_loop(0, TOKEN_TILE, body, 0, unroll=4)

    def wait_rows(s):
        for _ in range(TOP_K):
            pltpu.make_async_copy(hbuf.at[s], x_hbm.at[pl.ds(0, TOKEN_TILE * LANE_TILES)], dsem.at[s]).wait()

    @pl.when(i == 0)
    def _():
        count[...] = jnp.zeros_like(count)

    @pl.when(i >= 2)
    def _():
        wait_rows(slot)

    @pl.when(i >= 1)
    def _():
        issue_rows(1 - slot)

    acc = x_ref[...]
    parts = ((oa_ref, ga_ref, 0, 512), (ob_ref, gb_ref, 512, 256), (oc_ref, gc_ref, 768, 256))
    for ref, g_ref, off, width in parts:
        o = _rms(ref[...].astype(jnp.float32), g_ref[...]).astype(jnp.bfloat16)
        acc = acc + jnp.dot(o, w_ref[off:off + width, :], preferred_element_type=jnp.float32)
    xo_ref[...] = acc
    h = _rms(acc, gf_ref[...])
    _store_row_tiles(hbuf.at[slot], h)
    h_hi = h.astype(jnp.bfloat16)
    h_lo = (h - h_hi.astype(jnp.float32)).astype(jnp.bfloat16)
    logits = (jnp.dot(h_hi, wr_ref[0], preferred_element_type=jnp.float32)
              + jnp.dot(h_lo, wr_ref[0], preferred_element_type=jnp.float32)
              + jnp.dot(h_hi, wr_ref[1], preferred_element_type=jnp.float32)) + br_ref[...]
    lane = lax.broadcasted_iota(jnp.int32, logits.shape, 1)
    work = jnp.where(lane < N_EXPERTS, logits, -jnp.inf)
    gate_out = jnp.zeros(logits.shape, jnp.float32)
    denom = jnp.zeros((logits.shape[0], 1), jnp.float32)
    top = None
    picks = []
    for k in range(TOP_K):
        m = jnp.max(work, axis=-1, keepdims=True)
        idx = jnp.min(jnp.where(work == m, lane, ROUTER_PAD), axis=-1, keepdims=True)
        pick = lane == idx
        work = jnp.where(pick, -jnp.inf, work)
        top = m if top is None else top
        e = jnp.exp(m - top)
        denom = denom + e
        gate_out = jnp.where(lane == k, e, gate_out)
        picks.append((idx, pick))
    tg_ref[...] = gate_out / denom
    chosen = jnp.zeros(logits.shape, jnp.float32)
    for _, pick in picks:
        chosen = jnp.where(pick, 1.0, chosen)
    r_i = lax.broadcasted_iota(jnp.int32, (TOKEN_TILE, TOKEN_TILE), 0)
    c_i = lax.broadcasted_iota(jnp.int32, (TOKEN_TILE, TOKEN_TILE), 1)
    earlier = jnp.where(c_i < r_i, 1.0, 0.0).astype(jnp.bfloat16)
    before = jnp.dot(earlier, chosen.astype(jnp.bfloat16), preferred_element_type=jnp.float32) + count[...]
    dest_out = jnp.zeros(logits.shape, jnp.int32)
    for k, (idx, pick) in enumerate(picks):
        rank = jnp.sum(jnp.where(pick, before, 0.0), axis=-1, keepdims=True).astype(jnp.int32)
        dest_out = jnp.where(lane == k, idx * cap + rank, dest_out)
    dest_ref[...] = dest_out
    count[...] = count[...] + jnp.sum(chosen, axis=0, keepdims=True)
    cnt_ref[...] = count[...].astype(jnp.int32)
    dvm[slot] = dest_out.T[0:8, :]
    for copy in index_copies(slot):
        copy.start()

    @pl.when(i == nt - 1)
    def _():
        issue_rows(slot)

        @pl.when(nt >= 2)
        def _():
            wait_rows(1 - slot)

        wait_rows(slot)


def _out_proj(oa, ob, oc, x2, ga, gb, gc, w, gf, wr, br, n_rows):
    t = x2.shape[0]
    row = lambda width: pl.BlockSpec((TOKEN_TILE, width), lambda i: (i, 0))
    full = lambda shape: pl.BlockSpec(shape, lambda i: (0, 0))
    return pl.pallas_call(
        _out_proj_kernel,
        grid=(t // TOKEN_TILE,),
        in_specs=[row(512), row(256), row(256), row(D_MODEL), full((1, 512)), full((1, 256)), full((1, 256)),
                  full((D_MODEL, D_MODEL)), full((1, D_MODEL)),
                  pl.BlockSpec((2, D_MODEL, ROUTER_PAD), lambda i: (0, 0, 0)), full((1, ROUTER_PAD))],
        out_specs=[row(D_MODEL), pl.BlockSpec(memory_space=pl.ANY),
                   row(ROUTER_PAD), row(ROUTER_PAD), full((1, ROUTER_PAD))],
        out_shape=[jax.ShapeDtypeStruct((t, D_MODEL), jnp.float32),
                   jax.ShapeDtypeStruct((n_rows * LANE_TILES, 128), jnp.float32),
                   jax.ShapeDtypeStruct((t, ROUTER_PAD), jnp.int32),
                   jax.ShapeDtypeStruct((t, ROUTER_PAD), jnp.float32),
                   jax.ShapeDtypeStruct((1, ROUTER_PAD), jnp.int32)],
        scratch_shapes=[pltpu.VMEM((1, ROUTER_PAD), jnp.float32),
                        pltpu.VMEM((2, TOKEN_TILE * LANE_TILES, 128), jnp.float32),
                        pltpu.VMEM((2, 8, TOKEN_TILE), jnp.int32),
                        pltpu.SMEM((2 * TILE_SLOTS,), jnp.int32),
                        pltpu.SemaphoreType.DMA((2,)), pltpu.SemaphoreType.DMA((2,))],
        compiler_params=pltpu.CompilerParams(dimension_semantics=("arbitrary",), vmem_limit_bytes=VMEM_LIMIT),
        name="out_proj",
    )(oa, ob, oc, x2, ga, gb, gc, w, gf, wr, br)


def _moe_ffn_kernel(blk_e_ref, blk_row_ref, blk_valid_ref, blk_first_ref, blk_next_ref, blk_slot_ref,
                    x_ref, wgu_hbm, bgu_ref, wd_hbm, bd_ref, y_ref, wgu_f32, wd_f32, wgu_bf, wd_bf, gsem, dsem,
                    *, layer):
    i = pl.program_id(0)
    n_valid = blk_valid_ref[i]

    def weight_copies(e, s):
        return (pltpu.make_async_copy(wgu_hbm.at[layer, e], wgu_f32.at[s], gsem.at[s]),
                pltpu.make_async_copy(wd_hbm.at[layer, e], wd_f32.at[s], dsem.at[s]))

    @pl.when(i == 0)
    def _():
        for copy in weight_copies(blk_e_ref[0], blk_slot_ref[0]):
            copy.start()

    @pl.when(blk_first_ref[i] == 1)
    def _():
        s = blk_slot_ref[i]
        for copy in weight_copies(blk_e_ref[i], s):
            copy.wait()
        nxt = blk_next_ref[i]

        @pl.when(nxt >= 0)
        def _():
            for copy in weight_copies(nxt, 1 - s):
                copy.start()

        for c in range(4):
            cols = slice(c * 512, (c + 1) * 512)
            wgu_bf[:, cols] = wgu_f32[s, :, cols].astype(jnp.bfloat16)
        for c in range(2):
            cols = slice(c * 512, (c + 1) * 512)
            wd_bf[:, cols] = wd_f32[s, :, cols].astype(jnp.bfloat16)

    @pl.when(n_valid > 0)
    def _():
        live = lax.broadcasted_iota(jnp.int32, (EXPERT_ROWS, 128), 0) < n_valid
        x = jnp.concatenate([jnp.where(live, _load_row_tiles(x_ref, 0, EXPERT_ROWS, c), 0.0).astype(jnp.bfloat16)
                             for c in range(LANE_TILES)], axis=1)
        glu = jnp.dot(x, wgu_bf[:, 0:D_EXPERT], preferred_element_type=jnp.float32) + bgu_ref[0, 0, :, 0:D_EXPERT]
        lin = jnp.dot(x, wgu_bf[:, D_EXPERT:], preferred_element_type=jnp.float32) + bgu_ref[0, 0, :, D_EXPERT:]
        glu = jnp.minimum(glu, SWIGLU_LIMIT)
        lin = jnp.clip(lin, -SWIGLU_LIMIT, SWIGLU_LIMIT)
        act = glu * jax.nn.sigmoid(SWIGLU_ALPHA * glu) * (lin + 1.0)
        y = jnp.dot(act.astype(jnp.bfloat16), wd_bf[...], preferred_element_type=jnp.float32) + bd_ref[0, 0]
        _store_row_tiles(y_ref, y)

    @pl.when(n_valid <= 0)
    def _():
        y_ref[...] = jnp.zeros_like(y_ref)


def _moe_ffn(layer, table, x_rows, w_gu, b_gu, w_down, b_down):
    n_blocks = table[0].shape[0]
    depth = w_gu.shape[0]
    rows = lambda i, be, br, *_: (br[i], 0)
    expert = lambda i, be, *_: (layer, be[i], 0, 0)
    grid_spec = pltpu.PrefetchScalarGridSpec(
        num_scalar_prefetch=len(table),
        grid=(n_blocks,),
        in_specs=[
            pl.BlockSpec((EXPERT_ROWS * LANE_TILES, 128), rows),
            pl.BlockSpec(memory_space=pl.ANY),
            pl.BlockSpec((1, 1, 1, 2 * D_EXPERT), expert),
            pl.BlockSpec(memory_space=pl.ANY),
            pl.BlockSpec((1, 1, 1, D_MODEL), expert),
        ],
        out_specs=pl.BlockSpec((EXPERT_ROWS * LANE_TILES, 128), rows),
        scratch_shapes=[pltpu.VMEM((2, D_MODEL, 2 * D_EXPERT), jnp.float32),
                        pltpu.VMEM((2, D_EXPERT, D_MODEL), jnp.float32),
                        pltpu.VMEM((D_MODEL, 2 * D_EXPERT), jnp.bfloat16),
                        pltpu.VMEM((D_EXPERT, D_MODEL), jnp.bfloat16),
                        pltpu.SemaphoreType.DMA((2,)), pltpu.SemaphoreType.DMA((2,))],
    )
    return pl.pallas_call(
        functools.partial(_moe_ffn_kernel, layer=layer),
        grid_spec=grid_spec,
        out_shape=jax.ShapeDtypeStruct(x_rows.shape, jnp.float32),
        compiler_params=pltpu.CompilerParams(dimension_semantics=("arbitrary",), vmem_limit_bytes=VMEM_LIMIT),
        name="moe_ffn",
    )(*table, x_rows, w_gu, b_gu.reshape(depth, N_EXPERTS, 1, -1), w_down, b_down.reshape(depth, N_EXPERTS, 1, -1))


def _combine_kernel(dest_hbm, y_hbm, x_ref, g_ref, *rest, project):
    if project:
        gain_ref, w_ref, hg_ref, xo_ref, o_ref, ybuf, idx_smem, isem, gsem = rest
    else:
        xo_ref, ybuf, idx_smem, isem, gsem = rest
    i = pl.program_id(0)
    nt = pl.num_programs(0)
    slot = i % 2

    def idx_copy(tile, s):
        return pltpu.make_async_copy(dest_hbm.at[tile], _index_slot(idx_smem, s), isem.at[s])

    def issue_gather(s):
        base = s * TILE_SLOTS

        def body(r, carry):
            for k in range(TOP_K):
                src = idx_smem[base + r * TOP_K + k]
                pltpu.make_async_copy(_row_tile(y_hbm, src), _row_tile(ybuf.at[s], k * TOKEN_TILE + r),
                                      gsem.at[s]).start(priority=k % 2)
            return carry
        lax.fori_loop(0, TOKEN_TILE, body, 0, unroll=4)

    @pl.when(i == 0)
    def _():
        idx_copy(0, 0).start()
        idx_copy(0, 0).wait()
        issue_gather(0)

        @pl.when(nt > 1)
        def _():
            idx_copy(1, 1).start()

    @pl.when(i + 1 < nt)
    def _():
        idx_copy(i + 1, 1 - slot).wait()
        issue_gather(1 - slot)

    pltpu.make_async_copy(y_hbm.at[pl.ds(0, TOP_K * TOKEN_TILE * LANE_TILES)], ybuf.at[slot], gsem.at[slot]).wait()

    @pl.when(i + 2 < nt)
    def _():
        idx_copy(i + 2, slot).start()

    g = g_ref[...]
    gates = [jnp.broadcast_to(g[:, k:k + 1], (TOKEN_TILE, 128)) for k in range(TOP_K)]
    cols = []
    for c in range(LANE_TILES):
        acc = x_ref[:, c * 128:(c + 1) * 128]
        for k in range(TOP_K):
            acc = acc + gates[k] * _load_row_tiles(ybuf.at[slot], k * TOKEN_TILE, TOKEN_TILE, c)
        cols.append(acc)
    x = jnp.concatenate(cols, axis=1)
    xo_ref[...] = x
    if project:
        _project(_rms(x, gain_ref[...]).astype(jnp.bfloat16), w_ref, hg_ref, o_ref)


def _combine_call(dest_tiles, y_rows, x2, gates, gain=None, w=None, head_gain=None):
    t = x2.shape[0]
    project = w is not None
    row = lambda width: pl.BlockSpec((TOKEN_TILE, width), lambda i: (i, 0))
    full = lambda shape: pl.BlockSpec(shape, lambda i: (0, 0))
    in_specs = [pl.BlockSpec(memory_space=pl.ANY), pl.BlockSpec(memory_space=pl.ANY), row(D_MODEL), row(ROUTER_PAD)]
    out_specs = [row(D_MODEL)]
    out_shape = [jax.ShapeDtypeStruct((t, D_MODEL), jnp.float32)]
    args = [dest_tiles, y_rows, x2, gates]
    if project:
        in_specs += [full((1, D_MODEL)), full((D_MODEL, PROJ_WIDTH)), full((1, PROJ_WIDTH))]
        out_specs.append(row(PROJ_WIDTH))
        out_shape.append(jax.ShapeDtypeStruct((t, PROJ_WIDTH), jnp.bfloat16))
        args += [gain, w, head_gain]
    out = pl.pallas_call(
        functools.partial(_combine_kernel, project=project),
        grid=(t // TOKEN_TILE,),
        in_specs=in_specs, out_specs=out_specs, out_shape=out_shape,
        scratch_shapes=[pltpu.VMEM((2, TOP_K * TOKEN_TILE * LANE_TILES, 128), jnp.float32),
                        pltpu.SMEM((2 * TILE_SLOTS,), jnp.int32),
                        pltpu.SemaphoreType.DMA((2,)), pltpu.SemaphoreType.DMA((2,))],
        compiler_params=pltpu.CompilerParams(dimension_semantics=("arbitrary",), vmem_limit_bytes=VMEM_LIMIT),
        name="combine_in_proj" if project else "combine",
    )(*args)
    return out if project else out[0]


def _t5_bucket(rel):
    half = T5_BUCKETS // 2
    max_exact = half // 2
    ret = np.where(rel > 0, half, 0)
    n = np.abs(rel)
    nf = np.maximum(n, 1).astype(np.float32)
    large = max_exact + (np.log(nf / max_exact) / math.log(T5_MAX_DIST / max_exact)
                         * (half - max_exact)).astype(np.int32)
    large = np.minimum(large, half - 1)
    return ret + np.where(n < max_exact, n, large)


def _band_ok(n_prev, band):
    q_chunk = np.arange(Q_TILE)[:, None] // CHUNK + n_prev
    k_chunk = np.arange(band)[None, :] // CHUNK
    return (k_chunk <= q_chunk) & (k_chunk >= q_chunk - n_prev)


def _toeplitz(u, band):
    heads, length = u.shape
    flat = jnp.broadcast_to(u[:, None, :], (heads, Q_TILE, length)).reshape(heads, Q_TILE * length)
    skew = flat[:, :Q_TILE * (length - 1)].reshape(heads, Q_TILE, length - 1)
    return skew[:, :, Q_TILE - 1:Q_TILE - 1 + band]


def _a_bias(t5_bias):
    rel = np.arange(Q_TILE + A_BAND) - (Q_TILE - 1) - A_PAD
    bias = _toeplitz(t5_bias[_t5_bucket(rel)].T.astype(jnp.float32), A_BAND)
    bias = jnp.where(_band_ok(A_PREV_CHUNKS, A_BAND)[None], bias, NEG_INF)
    pairs = [jnp.concatenate([bias[4 * hk + parity], bias[4 * hk + 2 + parity]], axis=0)
             for hk in range(A_KV_HEADS) for parity in range(2)]
    return jnp.stack(pairs).transpose(0, 2, 1)


def _c_bias(rel_table):
    dist = C_PAD + (Q_TILE - 1) - np.arange(Q_TILE + C_BAND)
    idx = np.clip(dist, -(CHUNK - 1), C_REL_CLIP) + (CHUNK - 1)
    bias = _toeplitz(rel_table[idx].T.astype(jnp.float32), C_BAND)
    return jnp.where(_band_ok(C_PREV_CHUNKS, C_BAND)[None], bias, NEG_INF).transpose(0, 2, 1)


def _rope_tables(seq):
    half = B_ROPE_DIM // 2
    inv_freq = ROPE_THETA ** (-(jnp.arange(half, dtype=jnp.float32) / half))
    ang = jnp.arange(seq, dtype=jnp.float32)[:, None] * inv_freq[None, :]
    cos, sin = jnp.cos(ang), jnp.sin(ang)
    zeros = jnp.zeros((seq, B_NOPE_DIM), jnp.float32)
    tail = jnp.zeros((seq, B_HEAD_PAD - B_QK_DIM), jnp.float32)
    zh = jnp.zeros((seq, half), jnp.float32)
    cos_t = jnp.concatenate([zeros + 1.0, cos, cos, tail + 1.0], axis=1)
    sa = jnp.concatenate([zeros, -sin, zh, tail], axis=1)
    sb = jnp.concatenate([zeros, zh, sin, tail], axis=1)
    return cos_t, sa, sb


def _regroup_w_in(w):
    splits = np.cumsum((512, 128, 128, 256, 128, 32, 256, 256))
    aq, ak, av, bcq, bckv, bkpe, cq, ck, cv = jnp.split(w, splits.tolist(), axis=1)
    z = lambda n: jnp.zeros((w.shape[0], n), w.dtype)
    return jnp.concatenate([aq, ak, av, bcq, bckv, z(64), bkpe, z(32), cq, ck, cv], axis=1).astype(jnp.bfloat16)


def _head_gain_row(a_q, a_k, c_q, c_k):
    scale = HEAD_DIM ** -0.5
    row = jnp.ones((PROJ_WIDTH,), jnp.float32)
    for sec, heads, gain in ((SEC_AQ, A_Q_HEADS, a_q * scale), (SEC_AKV, A_KV_HEADS, a_k),
                             (SEC_CQ, C_HEADS, c_q * scale), (SEC_CK, C_HEADS, c_k)):
        row = row.at[sec:sec + heads * HEAD_DIM].set(jnp.tile(gain, heads))
    return row.reshape(1, PROJ_WIDTH)


def _pad_heads(w, head_dim, keep):
    rank = w.shape[0]
    w = w.reshape(rank, B_HEADS, head_dim)[:, :, :keep]
    w = jnp.pad(w, ((0, 0), (0, 0), (0, B_HEAD_PAD - keep)))
    return w.reshape(rank, B_HEADS * B_HEAD_PAD).astype(jnp.bfloat16)


def _pad_lanes(g, offset, width):
    return jnp.pad(g, (offset, width - offset - g.shape[0])).reshape(1, width)


def _block_table(counts, t):
    tk = t * TOP_K
    blocks_per_expert = t // EXPERT_ROWS
    nblk = (counts + EXPERT_ROWS - 1) // EXPERT_ROWS
    blk_end = jnp.cumsum(nblk)
    blk_first = blk_end - nblk
    n_blocks = (tk + N_EXPERTS * (EXPERT_ROWS - 1) + EXPERT_ROWS - 1) // EXPERT_ROWS
    blk = jnp.arange(n_blocks, dtype=jnp.int32)
    blk_e = jnp.minimum(jnp.sum(blk[:, None] >= blk_end[None, :], axis=1), N_EXPERTS - 1).astype(jnp.int32)
    own = blk_e[:, None] == jnp.arange(N_EXPERTS, dtype=jnp.int32)[None, :]
    pick = lambda v: jnp.sum(jnp.where(own, v[None, :], 0), axis=1)
    j = blk - pick(blk_first)
    blk_valid = jnp.clip(pick(counts) - j * EXPERT_ROWS, 0, EXPERT_ROWS).astype(jnp.int32)
    spare = N_EXPERTS * blocks_per_expert
    blk_row = jnp.where(blk_valid > 0, blk_e * blocks_per_expert + j, spare).astype(jnp.int32)
    experts = jnp.arange(N_EXPERTS, dtype=jnp.int32)
    used = counts > 0
    later = jnp.logical_and(experts[None, :] > experts[:, None], used[None, :])
    nxt = jnp.min(jnp.where(later, experts[None, :], N_EXPERTS), axis=1)
    nxt = jnp.where(nxt < N_EXPERTS, nxt, -1)
    ordinal = jnp.cumsum(used.astype(jnp.int32)) - 1
    blk_first = jnp.logical_and(j == 0, blk_valid > 0).astype(jnp.int32)
    blk_next = pick(nxt).astype(jnp.int32)
    blk_slot = (pick(ordinal) % 2).astype(jnp.int32)
    return blk_e, blk_row, blk_valid, blk_first, blk_next, blk_slot


def kernel(x, attn_norm, w_in, a_q_norm, a_k_norm, a_sinks, t5_bias, b_q_a_norm, b_w_q_b, b_kv_a_norm, b_w_kv_b,
           b_q_norm, b_k_norm, c_q_norm, c_k_norm, c_rel_bias, a_out_norm, b_out_norm, c_out_norm, w_out, ffn_norm,
           w_router, b_router, w_gate_up, b_gate_up, w_down, b_down):
    b, s, d = x.shape
    t = b * s
    depth = w_in.shape[0]
    a_bias = _a_bias(t5_bias)
    cos_t, sin_a, sin_b = _rope_tables(s)
    x2 = x.reshape(t, d)
    head_gain = lambda l: _head_gain_row(a_q_norm[l], a_k_norm[l], c_q_norm[l], c_k_norm[l])
    proj = _in_proj(x2, attn_norm[0].reshape(1, d), _regroup_w_in(w_in[0]), head_gain(0))
    for l in range(depth):
        proj3 = proj.reshape(b, s, PROJ_WIDTH)
        o_a = _attn_a(proj3, a_bias, a_sinks[l])
        kv_w = b_w_kv_b[l].reshape(B_KV_RANK, B_HEADS, B_NOPE_DIM + B_V_DIM)
        wv = kv_w[:, :, B_NOPE_DIM:].reshape(B_KV_RANK, B_HEADS * B_V_DIM).astype(jnp.bfloat16)
        qb, kb, vb = _latent_prep(
            proj, b_q_a_norm[l].reshape(1, -1), b_kv_a_norm[l].reshape(1, -1),
            _pad_heads(b_w_q_b[l], B_QK_DIM, B_QK_DIM), _pad_heads(b_w_kv_b[l], B_NOPE_DIM + B_V_DIM, B_NOPE_DIM), wv,
            _pad_lanes(b_q_norm[l], 0, B_HEAD_PAD), _pad_lanes(b_k_norm[l], 0, B_HEAD_PAD), cos_t, sin_a, sin_b, s)
        o_b = _attn_b(qb.reshape(b, s, -1), kb.reshape(b, s, -1), vb.reshape(b, s, -1))
        o_c = _attn_c(proj3, _c_bias(c_rel_bias[l]))
        wr = jnp.pad(w_router[l], ((0, 0), (0, ROUTER_PAD - N_EXPERTS)))
        wr_hi = wr.astype(jnp.bfloat16)
        wr = jnp.stack([wr_hi, (wr - wr_hi.astype(jnp.float32)).astype(jnp.bfloat16)])
        br = jnp.pad(b_router[l], (0, ROUTER_PAD - N_EXPERTS)).reshape(1, ROUTER_PAD)
        n_rows = (N_EXPERTS * (t // EXPERT_ROWS) + 1) * EXPERT_ROWS
        x2, x_rows, dest, gates, counts = _out_proj(
            o_a.reshape(t, -1), o_b.reshape(t, -1), o_c.reshape(t, -1), x2,
            a_out_norm[l].reshape(1, -1), b_out_norm[l].reshape(1, -1), c_out_norm[l].reshape(1, -1),
            w_out[l].astype(jnp.bfloat16), ffn_norm[l].reshape(1, -1), wr, br, n_rows)
        dest_tiles = dest[:, :TOP_K].reshape(t // TOKEN_TILE, TOKEN_TILE * TOP_K)
        table = _block_table(counts[0, :N_EXPERTS], t)
        y_rows = _moe_ffn(l, table, x_rows, w_gate_up, b_gate_up, w_down, b_down)
        if l + 1 < depth:
            x2, proj = _combine_call(dest_tiles, y_rows, x2, gates, attn_norm[l + 1].reshape(1, d),
                                     _regroup_w_in(w_in[l + 1]), head_gain(l + 1))
        else:
            x2 = _combine_call(dest_tiles, y_rows, x2, gates)
    return x2.reshape(b, s, d)
```

```python
import functools
import math

import jax
import jax.numpy as jnp
import numpy as np
from jax import lax
from jax.experimental import pallas as pl
from jax.experimental.pallas import tpu as pltpu

D_MODEL = 1024
CHUNK = 64
HEAD_DIM = 64
EPS = 1e-6
NEG_INF = -1e30
A_Q_HEADS = 8
A_KV_HEADS = 2
A_GROUP = 4
A_PREV_CHUNKS = 2
T5_BUCKETS = 32
T5_MAX_DIST = 128
B_HEADS = 4
B_Q_RANK = 256
B_KV_RANK = 128
B_NOPE_DIM = 64
B_ROPE_DIM = 32
B_QK_DIM = 96
B_V_DIM = 64
ROPE_THETA = 10000.0
C_HEADS = 4
C_PREV_CHUNKS = 8
C_REL_CLIP = 256
N_EXPERTS = 32
TOP_K = 4
D_EXPERT = 1024
SWIGLU_LIMIT = 7.0
SWIGLU_ALPHA = 1.702

SEC_AQ = 0
SEC_AKV = 512
SEC_BCQ = 768
SEC_BCKV = 1024
SEC_BKPE = 1152
SEC_CQ = 1280
SEC_CK = 1536
SEC_CV = 1792
PROJ_WIDTH = 2048
HEAD_NORM_SLABS = tuple(sec // 128 + j for sec, n in ((SEC_AQ, 4), (SEC_AKV, 1), (SEC_CQ, 2), (SEC_CK, 2))
                        for j in range(n))
B_HEAD_PAD = 128

Q_TILE = 128
A_BAND = Q_TILE + A_PREV_CHUNKS * CHUNK
C_BAND = Q_TILE + C_PREV_CHUNKS * CHUNK
A_PAD = A_PREV_CHUNKS * CHUNK
C_PAD = C_PREV_CHUNKS * CHUNK
B_TILE = 512
ROW_TILE = 512
EXPERT_ROWS = 256
ROUTER_PAD = 128
TOKEN_TILE = 256
TILE_SLOTS = TOKEN_TILE * TOP_K
ROW_BUFFERS = 3
SPARE_ROWS = TILE_SLOTS
LANE_TILES = D_MODEL // 128
VMEM_LIMIT = 48 * 1024 * 1024


def _rms(x, gain):
    return x * lax.rsqrt(jnp.mean(x * x, axis=-1, keepdims=True) + EPS) * gain


def _nt_dot(a, b):
    return lax.dot_general(a, b, (((1,), (1,)), ((), ())), preferred_element_type=jnp.float32)


def _load_row_tiles(ref, first_row, n_rows, c):
    return ref[pl.ds(first_row * LANE_TILES + c, n_rows, stride=LANE_TILES), :]


def _store_row_tiles(ref, value):
    for c in range(LANE_TILES):
        ref[pl.ds(c, value.shape[0], stride=LANE_TILES), :] = value[:, c * 128:(c + 1) * 128]


def _row_tile(ref, row):
    return ref.at[pl.ds(pl.multiple_of(row * LANE_TILES, LANE_TILES), LANE_TILES)]


def _index_slot(idx_smem, s):
    return idx_smem.at[pl.ds(pl.multiple_of(s * TILE_SLOTS, TILE_SLOTS), TILE_SLOTS)]


def _project(h, w_ref, hg_ref, o_ref):
    li = lax.broadcasted_iota(jnp.int32, (128, 128), 0) // HEAD_DIM
    lj = lax.broadcasted_iota(jnp.int32, (128, 128), 1) // HEAD_DIM
    same_head = jnp.where(li == lj, 1.0, 0.0).astype(jnp.bfloat16)
    for n in range(PROJ_WIDTH // 512):
        acc = jnp.dot(h, w_ref[:, n * 512:(n + 1) * 512], preferred_element_type=jnp.float32)
        slabs = []
        for j in range(4):
            slab = n * 4 + j
            a = acc[:, j * 128:(j + 1) * 128]
            if slab in HEAD_NORM_SLABS:
                sq = a * a
                hi = sq.astype(jnp.bfloat16)
                lo = (sq - hi.astype(jnp.float32)).astype(jnp.bfloat16)
                ss = (jnp.dot(hi, same_head, preferred_element_type=jnp.float32)
                      + jnp.dot(lo, same_head, preferred_element_type=jnp.float32))
                a = a * lax.rsqrt(ss * (1.0 / HEAD_DIM) + EPS) * hg_ref[:, slab * 128:(slab + 1) * 128]
            slabs.append(a.astype(o_ref.dtype))
        o_ref[:, n * 512:(n + 1) * 512] = jnp.concatenate(slabs, axis=1)


def _in_proj_kernel(x_ref, g_ref, w_ref, hg_ref, o_ref):
    _project(_rms(x_ref[...], g_ref[...]).astype(jnp.bfloat16), w_ref, hg_ref, o_ref)


def _in_proj(x2, gain, w, head_gain):
    t = x2.shape[0]
    return pl.pallas_call(
        _in_proj_kernel,
        grid=(t // ROW_TILE,),
        in_specs=[
            pl.BlockSpec((ROW_TILE, D_MODEL), lambda i: (i, 0)),
            pl.BlockSpec((1, D_MODEL), lambda i: (0, 0)),
            pl.BlockSpec((D_MODEL, PROJ_WIDTH), lambda i: (0, 0)),
            pl.BlockSpec((1, PROJ_WIDTH), lambda i: (0, 0)),
        ],
        out_specs=pl.BlockSpec((ROW_TILE, PROJ_WIDTH), lambda i: (i, 0)),
        out_shape=jax.ShapeDtypeStruct((t, PROJ_WIDTH), jnp.bfloat16),
        compiler_params=pltpu.CompilerParams(dimension_semantics=("arbitrary",), vmem_limit_bytes=VMEM_LIMIT),
        name="in_proj",
    )(x2, gain, w, head_gain)


def _half_lanes(x, parity, fill=0.0):
    lane = lax.broadcasted_iota(jnp.int32, x.shape, x.ndim - 1) % 128
    keep = lane < HEAD_DIM if parity == 0 else lane >= HEAD_DIM
    return jnp.where(keep, x, jnp.full_like(x, fill))


def _tn_dot(a, b):
    return lax.dot_general(a, b, (((0,), (0,)), ((), ())), preferred_element_type=jnp.float32)


def _softmax_pv(s, key_ok, v_aug, sink=None, value_half=0):
    s = jnp.where(key_ok, s, NEG_INF)
    m = jnp.max(s, axis=0, keepdims=True)
    if sink is not None:
        m = jnp.maximum(m, sink)
    e = jnp.exp(s - m).astype(jnp.bfloat16)
    o = _tn_dot(e, v_aug)
    if sink is not None:
        row = lax.broadcasted_iota(jnp.int32, (8, s.shape[1]), 0)
        e_sink = jnp.where(row == 0, jnp.exp(sink - m), 0.0).astype(jnp.bfloat16)
        first = lax.broadcasted_iota(jnp.int32, (8, 128), 0) == 0
        sel = _half_lanes(jnp.where(first, 1.0, 0.0), 1 - value_half).astype(jnp.bfloat16)
        o = o + _tn_dot(e_sink, sel)
    return o


def _merge_halves(halves):
    lane = lax.broadcasted_iota(jnp.int32, halves[0].shape, 1)
    normed = [o / pltpu.roll(o, HEAD_DIM, 1) for o in halves]
    return jnp.where(lane < HEAD_DIM, normed[0], normed[1])


def _attn_a_kernel(q_ref, kv_ref, bias_ref, sink_ref, o_ref, k_pad, v_pad):
    i = pl.program_id(1)
    seq = kv_ref.shape[1]

    @pl.when(i == 0)
    def _():
        for n in range(2 * A_KV_HEADS):
            k_pad[n, 0:A_PAD, :] = jnp.zeros((A_PAD, 128), k_pad.dtype)
            v_pad[n, 0:A_PAD, :] = jnp.zeros((A_PAD, 128), v_pad.dtype)

        def fill(r, carry):
            rows = pl.ds(pl.multiple_of(r * 256, 256), 256)
            dst = pl.ds(pl.multiple_of(r * 256 + A_PAD, 128), 256)
            kv = kv_ref[0, rows, :].astype(jnp.float32)
            for src, pad, other in ((kv[:, 0:128], k_pad, 0.0), (kv[:, 128:256], v_pad, 1.0)):
                swapped = pltpu.roll(src, HEAD_DIM, 1)
                pad[0, dst, :] = _half_lanes(src, 0, other).astype(pad.dtype)
                pad[1, dst, :] = _half_lanes(swapped, 1, other).astype(pad.dtype)
                pad[2, dst, :] = _half_lanes(swapped, 0, other).astype(pad.dtype)
                pad[3, dst, :] = _half_lanes(src, 1, other).astype(pad.dtype)
            return carry

        lax.fori_loop(0, seq // 256, fill, 0)

    start = pl.multiple_of(i * Q_TILE, Q_TILE)
    band = pl.ds(start, A_BAND)
    key_ok = lax.broadcasted_iota(jnp.int32, (A_BAND, 2 * Q_TILE), 0) + (start - A_PAD) >= 0
    first_slab = lax.broadcasted_iota(jnp.int32, (1, 2 * Q_TILE), 1) < Q_TILE
    for hk in range(A_KV_HEADS):
        q2 = jnp.concatenate([q_ref[0, :, (2 * hk) * 128:(2 * hk + 1) * 128],
                              q_ref[0, :, (2 * hk + 1) * 128:(2 * hk + 2) * 128]], axis=0)
        halves = []
        for parity in range(2):
            sink = jnp.where(first_slab, sink_ref[4 * hk + parity], sink_ref[4 * hk + 2 + parity])
            s = _nt_dot(k_pad[2 * hk + parity, band, :], q2) + bias_ref[2 * hk + parity]
            halves.append(_softmax_pv(s, key_ok, v_pad[2 * hk + parity, band, :], sink, parity))
        out = _merge_halves(halves)
        o_ref[0, :, (2 * hk) * 128:(2 * hk + 1) * 128] = out[0:Q_TILE].astype(o_ref.dtype)
        o_ref[0, :, (2 * hk + 1) * 128:(2 * hk + 2) * 128] = out[Q_TILE:2 * Q_TILE].astype(o_ref.dtype)


def _attn_a(proj3, bias, sinks):
    b, s, _ = proj3.shape
    return pl.pallas_call(
        _attn_a_kernel,
        grid=(b, s // Q_TILE),
        in_specs=[
            pl.BlockSpec((1, Q_TILE, 512), lambda bi, i: (bi, i, SEC_AQ // 512)),
            pl.BlockSpec((1, s, 256), lambda bi, i: (bi, 0, SEC_AKV // 256)),
            pl.BlockSpec((2 * A_KV_HEADS, 2 * Q_TILE, A_BAND), lambda bi, i: (0, 0, 0)),
            pl.BlockSpec(memory_space=pltpu.SMEM),
        ],
        out_specs=pl.BlockSpec((1, Q_TILE, 512), lambda bi, i: (bi, i, 0)),
        out_shape=jax.ShapeDtypeStruct((b, s, 512), jnp.bfloat16),
        scratch_shapes=[pltpu.VMEM((2 * A_KV_HEADS, s + A_PAD, 128), jnp.bfloat16),
                        pltpu.VMEM((2 * A_KV_HEADS, s + A_PAD, 128), jnp.bfloat16)],
        compiler_params=pltpu.CompilerParams(dimension_semantics=("arbitrary", "arbitrary"),
                                             vmem_limit_bytes=VMEM_LIMIT),
        name="attn_a",
    )(proj3, proj3, bias, sinks)


def _latent_prep_kernel(cq_ref, ckv_ref, kpe_ref, qag_ref, kvag_ref, wq_ref, wk_ref, wv_ref,
                        qg_ref, kg_ref, cos_ref, sa_ref, sb_ref, qo_ref, ko_ref, vo_ref):
    lane = lax.broadcasted_iota(jnp.int32, (ROW_TILE, B_HEAD_PAD), 1)
    is_nope = lane < B_NOPE_DIM
    is_rope = jnp.logical_and(lane >= B_NOPE_DIM, lane < B_QK_DIM)
    cos = cos_ref[...]
    sa = sa_ref[...]
    sb = sb_ref[...]

    def split_norm(x, gain):
        sq = x * x
        nope = jnp.sum(jnp.where(is_nope, sq, 0.0), axis=-1, keepdims=True) * (1.0 / B_NOPE_DIM)
        rope = jnp.sum(jnp.where(is_rope, sq, 0.0), axis=-1, keepdims=True) * (1.0 / B_ROPE_DIM)
        r = jnp.where(is_nope, lax.rsqrt(nope + EPS), lax.rsqrt(rope + EPS))
        return x * r * gain

    def rotate(x):
        return x * cos + pltpu.roll(x, 112, 1) * sa + pltpu.roll(x, 16, 1) * sb

    cq = _rms(cq_ref[...].astype(jnp.float32), qag_ref[...]).astype(jnp.bfloat16)
    ckv = _rms(ckv_ref[...].astype(jnp.float32), kvag_ref[...]).astype(jnp.bfloat16)
    kpe = kpe_ref[...].astype(jnp.float32)
    kpe_n = rotate(split_norm(kpe, kg_ref[...]))
    kpe_n = jnp.where(is_rope, kpe_n, 0.0)
    v = jnp.dot(ckv, wv_ref[...], preferred_element_type=jnp.float32)
    for h in range(B_HEADS):
        pair = v[:, (h // 2) * 128:(h // 2 + 1) * 128]
        vo_ref[:, h * 128:(h + 1) * 128] = _half_lanes(pair, h % 2, 1.0).astype(vo_ref.dtype)
    qscale = B_QK_DIM ** -0.5
    for h in range(B_HEADS):
        cols = slice(h * B_HEAD_PAD, (h + 1) * B_HEAD_PAD)
        qh = jnp.dot(cq, wq_ref[:, cols], preferred_element_type=jnp.float32)
        qh = rotate(split_norm(qh, qg_ref[...])) * qscale
        qo_ref[:, cols] = qh.astype(qo_ref.dtype)
        kh = jnp.dot(ckv, wk_ref[:, cols], preferred_element_type=jnp.float32)
        kh = jnp.where(is_nope, split_norm(kh, kg_ref[...]), 0.0) + kpe_n
        ko_ref[:, cols] = kh.astype(ko_ref.dtype)


def _latent_prep(proj, q_a_gain, kv_a_gain, wq, wk, wv, q_gain, k_gain, cos, sa, sb, seq):
    t = proj.shape[0]
    pos_blocks = seq // ROW_TILE
    full = lambda shape: pl.BlockSpec(shape, lambda i: (0, 0))
    tab = pl.BlockSpec((ROW_TILE, B_HEAD_PAD), lambda i: (i % pos_blocks, 0))
    return pl.pallas_call(
        _latent_prep_kernel,
        grid=(t // ROW_TILE,),
        in_specs=[
            pl.BlockSpec((ROW_TILE, 256), lambda i: (i, SEC_BCQ // 256)),
            pl.BlockSpec((ROW_TILE, 128), lambda i: (i, SEC_BCKV // 128)),
            pl.BlockSpec((ROW_TILE, 128), lambda i: (i, SEC_BKPE // 128)),
            full((1, B_Q_RANK)), full((1, B_KV_RANK)),
            full((B_Q_RANK, B_HEADS * B_HEAD_PAD)), full((B_KV_RANK, B_HEADS * B_HEAD_PAD)),
            full((B_KV_RANK, B_HEADS * B_V_DIM)),
            full((1, B_HEAD_PAD)), full((1, B_HEAD_PAD)),
            tab, tab, tab,
        ],
        out_specs=[
            pl.BlockSpec((ROW_TILE, B_HEADS * B_HEAD_PAD), lambda i: (i, 0)),
            pl.BlockSpec((ROW_TILE, B_HEADS * B_HEAD_PAD), lambda i: (i, 0)),
            pl.BlockSpec((ROW_TILE, B_HEADS * B_HEAD_PAD), lambda i: (i, 0)),
        ],
        out_shape=[
            jax.ShapeDtypeStruct((t, B_HEADS * B_HEAD_PAD), jnp.bfloat16),
            jax.ShapeDtypeStruct((t, B_HEADS * B_HEAD_PAD), jnp.bfloat16),
            jax.ShapeDtypeStruct((t, B_HEADS * B_HEAD_PAD), jnp.bfloat16),
        ],
        compiler_params=pltpu.CompilerParams(dimension_semantics=("arbitrary",), vmem_limit_bytes=VMEM_LIMIT),
        name="latent_prep",
    )(proj, proj, proj, q_a_gain, kv_a_gain, wq, wk, wv, q_gain, k_gain, cos, sa, sb)


def _attn_b_kernel(q_ref, k_ref, v_ref, o_ref):
    i = pl.program_id(1)
    query_chunk = lax.broadcasted_iota(jnp.int32, (B_TILE, B_TILE), 0) // CHUNK
    key_chunk = lax.broadcasted_iota(jnp.int32, (B_TILE, B_TILE), 1) // CHUNK
    diag_ok = key_chunk <= query_chunk

    def step(j, carry, masked):
        rows = pl.ds(pl.multiple_of(j * B_TILE, B_TILE), B_TILE)
        out = []
        for h in range(B_HEADS):
            m, acc = carry[h]
            head = slice(h * B_HEAD_PAD, (h + 1) * B_HEAD_PAD)
            s = _nt_dot(q_ref[0, :, head], k_ref[0, rows, head])
            if masked:
                s = jnp.where(diag_ok, s, NEG_INF)
            m_new = jnp.maximum(m, jnp.max(s, axis=-1, keepdims=True))
            e = jnp.exp(s - m_new).astype(jnp.bfloat16)
            acc = jnp.exp(m - m_new) * acc + jnp.dot(e, v_ref[0, rows, head], preferred_element_type=jnp.float32)
            out.append((m_new, acc))
        return tuple(out)

    init = tuple((jnp.full((B_TILE, 1), NEG_INF, jnp.float32), jnp.zeros((B_TILE, B_HEAD_PAD), jnp.float32))
                 for _ in range(B_HEADS))
    carry = lax.fori_loop(0, i, lambda j, c: step(j, c, False), init)
    carry = step(i, carry, True)
    for pair in range(B_HEADS // 2):
        halves = [carry[2 * pair + parity][1] for parity in range(2)]
        o_ref[0, :, pair * 128:(pair + 1) * 128] = _merge_halves(halves).astype(o_ref.dtype)


def _attn_b(qb, kb, vb):
    b, s, _ = qb.shape
    return pl.pallas_call(
        _attn_b_kernel,
        grid=(b, s // B_TILE),
        in_specs=[
            pl.BlockSpec((1, B_TILE, B_HEADS * B_HEAD_PAD), lambda bi, i: (bi, i, 0)),
            pl.BlockSpec((1, s, B_HEADS * B_HEAD_PAD), lambda bi, i: (bi, 0, 0)),
            pl.BlockSpec((1, s, B_HEADS * B_HEAD_PAD), lambda bi, i: (bi, 0, 0)),
        ],
        out_specs=pl.BlockSpec((1, B_TILE, B_HEADS * B_V_DIM), lambda bi, i: (bi, i, 0)),
        out_shape=jax.ShapeDtypeStruct((b, s, B_HEADS * B_V_DIM), jnp.bfloat16),
        compiler_params=pltpu.CompilerParams(dimension_semantics=("arbitrary", "arbitrary"),
                                             vmem_limit_bytes=VMEM_LIMIT),
        name="attn_b",
    )(qb, kb, vb)


def _attn_c_kernel(q_ref, k_ref, v_ref, bias_ref, o_ref, k_pad, v_pad):
    i = pl.program_id(1)
    seq = k_ref.shape[1]
    width = C_HEADS * HEAD_DIM

    @pl.when(i == 0)
    def _():
        for parity in range(2):
            k_pad[parity, 0:C_PAD, :] = jnp.zeros((C_PAD, width), k_pad.dtype)
            v_pad[parity, 0:C_PAD, :] = jnp.zeros((C_PAD, width), v_pad.dtype)

        def fill(r, carry):
            rows = pl.ds(pl.multiple_of(r * 256, 256), 256)
            dst = pl.ds(pl.multiple_of(r * 256 + C_PAD, 256), 256)
            k = k_ref[0, rows, :]
            v = v_ref[0, rows, :]
            for parity in range(2):
                k_pad[parity, dst, :] = _half_lanes(k, parity)
                v_pad[parity, dst, :] = _half_lanes(v, parity, fill=1.0)
            return carry

        lax.fori_loop(0, seq // 256, fill, 0)

    start = pl.multiple_of(i * Q_TILE, Q_TILE)
    band = pl.ds(start, C_BAND)
    key_ok = lax.broadcasted_iota(jnp.int32, (C_BAND, Q_TILE), 0) + (start - C_PAD) >= 0
    for slab in range(width // 128):
        lanes = slice(slab * 128, (slab + 1) * 128)
        q = q_ref[0, :, lanes]
        halves = []
        for parity in range(2):
            s = _nt_dot(k_pad[parity, band, lanes], q) + bias_ref[2 * slab + parity]
            halves.append(_softmax_pv(s, key_ok, v_pad[parity, band, lanes]))
        o_ref[0, :, lanes] = _merge_halves(halves).astype(o_ref.dtype)


def _attn_c(proj3, bias):
    b, s, _ = proj3.shape
    width = C_HEADS * HEAD_DIM
    return pl.pallas_call(
        _attn_c_kernel,
        grid=(b, s // Q_TILE),
        in_specs=[
            pl.BlockSpec((1, Q_TILE, width), lambda bi, i: (bi, i, SEC_CQ // width)),
            pl.BlockSpec((1, s, width), lambda bi, i: (bi, 0, SEC_CK // width)),
            pl.BlockSpec((1, s, width), lambda bi, i: (bi, 0, SEC_CV // width)),
            pl.BlockSpec((C_HEADS, C_BAND, Q_TILE), lambda bi, i: (0, 0, 0)),
        ],
        out_specs=pl.BlockSpec((1, Q_TILE, width), lambda bi, i: (bi, i, 0)),
        out_shape=jax.ShapeDtypeStruct((b, s, width), jnp.bfloat16),
        scratch_shapes=[pltpu.VMEM((2, s + C_PAD, width), jnp.bfloat16),
                        pltpu.VMEM((2, s + C_PAD, width), jnp.bfloat16)],
        compiler_params=pltpu.CompilerParams(dimension_semantics=("arbitrary", "arbitrary"),
                                             vmem_limit_bytes=VMEM_LIMIT),
        name="attn_c",
    )(proj3, proj3, proj3, bias)


def _out_proj_kernel(oa_ref, ob_ref, oc_ref, x_ref, ga_ref, gb_ref, gc_ref, w_ref, gf_ref, wr_ref, br_ref,
                     xo_ref, x_hbm, dest_ref, tg_ref, cnt_ref, count, hbuf, dvm, idx_smem, isem, dsem):
    i = pl.program_id(0)
    nt = pl.num_programs(0)
    cap = nt * TOKEN_TILE
    slot = i % ROW_BUFFERS
    prev = (i + ROW_BUFFERS - 1) % ROW_BUFFERS
    spare_row = x_hbm.shape[0] // LANE_TILES - TILE_SLOTS

    def index_copies(s):
        return [pltpu.make_async_copy(
            dvm.at[s, k], idx_smem.at[pl.ds(pl.multiple_of(s * TILE_SLOTS + k * TOKEN_TILE, TOKEN_TILE), TOKEN_TILE)],
            isem.at[s]) for k in range(TOP_K)]

    def row_copy(s, base, r, k):
        dst = idx_smem[base + k * TOKEN_TILE + r]
        return pltpu.make_async_copy(_row_tile(hbuf.at[s], r), _row_tile(x_hbm, dst), dsem.at[s])

    def wait_rows(s):
        for _ in range(TOP_K):
            pltpu.make_async_copy(hbuf.at[s], x_hbm.at[pl.ds(0, TOKEN_TILE * LANE_TILES)], dsem.at[s]).wait()

    @pl.when(i == 0)
    def _():
        count[...] = jnp.zeros_like(count)
        hbuf[ROW_BUFFERS - 1] = jnp.zeros(hbuf.shape[1:], hbuf.dtype)

        def fill(j, carry):
            idx_smem[(ROW_BUFFERS - 1) * TILE_SLOTS + j] = spare_row + j
            return carry

        lax.fori_loop(0, TILE_SLOTS, fill, 0)

    @pl.when(i >= ROW_BUFFERS - 1)
    def _():
        wait_rows(slot)

    @pl.when(i >= 1)
    def _():
        for copy in index_copies(prev):
            copy.wait()

    prev_base = prev * TILE_SLOTS
    for r in range(TOKEN_TILE):
        for k in range(TOP_K):
            row_copy(prev, prev_base, r, k).start(priority=k % 2)

    acc = x_ref[...]
    parts = ((oa_ref, ga_ref, 0, 512), (ob_ref, gb_ref, 512, 256), (oc_ref, gc_ref, 768, 256))
    for ref, g_ref, off, width in parts:
        o = _rms(ref[...].astype(jnp.float32), g_ref[...]).astype(jnp.bfloat16)
        acc = acc + jnp.dot(o, w_ref[off:off + width, :], preferred_element_type=jnp.float32)
    xo_ref[...] = acc
    h = _rms(acc, gf_ref[...])
    _store_row_tiles(hbuf.at[slot], h)
    h_hi = h.astype(jnp.bfloat16)
    h_lo = (h - h_hi.astype(jnp.float32)).astype(jnp.bfloat16)
    logits = (jnp.dot(h_hi, wr_ref[0], preferred_element_type=jnp.float32)
              + jnp.dot(h_lo, wr_ref[0], preferred_element_type=jnp.float32)
              + jnp.dot(h_hi, wr_ref[1], preferred_element_type=jnp.float32)) + br_ref[...]
    lane = lax.broadcasted_iota(jnp.int32, logits.shape, 1)
    work = jnp.where(lane < N_EXPERTS, logits, -jnp.inf)
    gate_out = jnp.zeros(logits.shape, jnp.float32)
    denom = jnp.zeros((logits.shape[0], 1), jnp.float32)
    top = None
    picks = []
    for k in range(TOP_K):
        m = jnp.max(work, axis=-1, keepdims=True)
        idx = jnp.min(jnp.where(work == m, lane, ROUTER_PAD), axis=-1, keepdims=True)
        pick = lane == idx
        work = jnp.where(pick, -jnp.inf, work)
        top = m if top is None else top
        e = jnp.exp(m - top)
        denom = denom + e
        gate_out = jnp.where(lane == k, e, gate_out)
        picks.append((idx, pick))
    tg_ref[...] = gate_out / denom
    chosen = jnp.zeros(logits.shape, jnp.float32)
    for _, pick in picks:
        chosen = jnp.where(pick, 1.0, chosen)
    r_i = lax.broadcasted_iota(jnp.int32, (TOKEN_TILE, TOKEN_TILE), 0)
    c_i = lax.broadcasted_iota(jnp.int32, (TOKEN_TILE, TOKEN_TILE), 1)
    earlier = jnp.where(c_i < r_i, 1.0, 0.0).astype(jnp.bfloat16)
    before = jnp.dot(earlier, chosen.astype(jnp.bfloat16), preferred_element_type=jnp.float32) + count[...]
    dest_out = jnp.zeros(logits.shape, jnp.int32)
    for k, (idx, pick) in enumerate(picks):
        rank = jnp.sum(jnp.where(pick, before, 0.0), axis=-1, keepdims=True).astype(jnp.int32)
        dest_out = jnp.where(lane == k, idx * cap + rank, dest_out)
    dest_ref[...] = dest_out
    count[...] = count[...] + jnp.sum(chosen, axis=0, keepdims=True)
    cnt_ref[...] = count[...].astype(jnp.int32)
    dvm[slot] = dest_out.T[0:8, :]
    for copy in index_copies(slot):
        copy.start()

    @pl.when(i == nt - 1)
    def _():
        for copy in index_copies(slot):
            copy.wait()
        base = slot * TILE_SLOTS

        def body(r, carry):
            for k in range(TOP_K):
                row_copy(slot, base, r, k).start(priority=k % 2)
            return carry

        lax.fori_loop(0, TOKEN_TILE, body, 0, unroll=4)
        for s in range(ROW_BUFFERS):
            wait_rows(s)


def _out_proj(oa, ob, oc, x2, ga, gb, gc, w, gf, wr, br, n_rows):
    t = x2.shape[0]
    assert t % TOKEN_TILE == 0 and t // TOKEN_TILE >= ROW_BUFFERS and n_rows == N_EXPERTS * t + SPARE_ROWS
    row = lambda width: pl.BlockSpec((TOKEN_TILE, width), lambda i: (i, 0))
    full = lambda shape: pl.BlockSpec(shape, lambda i: (0, 0))
    return pl.pallas_call(
        _out_proj_kernel,
        grid=(t // TOKEN_TILE,),
        in_specs=[row(512), row(256), row(256), row(D_MODEL), full((1, 512)), full((1, 256)), full((1, 256)),
                  full((D_MODEL, D_MODEL)), full((1, D_MODEL)),
                  pl.BlockSpec((2, D_MODEL, ROUTER_PAD), lambda i: (0, 0, 0)), full((1, ROUTER_PAD))],
        out_specs=[row(D_MODEL), pl.BlockSpec(memory_space=pl.ANY),
                   row(ROUTER_PAD), row(ROUTER_PAD), full((1, ROUTER_PAD))],
        out_shape=[jax.ShapeDtypeStruct((t, D_MODEL), jnp.float32),
                   jax.ShapeDtypeStruct((n_rows * LANE_TILES, 128), jnp.float32),
                   jax.ShapeDtypeStruct((t, ROUTER_PAD), jnp.int32),
                   jax.ShapeDtypeStruct((t, ROUTER_PAD), jnp.float32),
                   jax.ShapeDtypeStruct((1, ROUTER_PAD), jnp.int32)],
        scratch_shapes=[pltpu.VMEM((1, ROUTER_PAD), jnp.float32),
                        pltpu.VMEM((ROW_BUFFERS, TOKEN_TILE * LANE_TILES, 128), jnp.float32),
                        pltpu.VMEM((ROW_BUFFERS, 8, TOKEN_TILE), jnp.int32),
                        pltpu.SMEM((ROW_BUFFERS * TILE_SLOTS,), jnp.int32),
                        pltpu.SemaphoreType.DMA((ROW_BUFFERS,)), pltpu.SemaphoreType.DMA((ROW_BUFFERS,))],
        compiler_params=pltpu.CompilerParams(dimension_semantics=("arbitrary",), vmem_limit_bytes=VMEM_LIMIT),
        name="out_proj",
    )(oa, ob, oc, x2, ga, gb, gc, w, gf, wr, br)


def _moe_ffn_kernel(blk_e_ref, blk_row_ref, blk_valid_ref, blk_first_ref, blk_next_ref, blk_slot_ref,
                    x_ref, wgu_hbm, bgu_ref, wd_hbm, bd_ref, y_ref, wgu_f32, wd_f32, wgu_bf, wd_bf, gsem, dsem,
                    *, layer):
    i = pl.program_id(0)
    n_valid = blk_valid_ref[i]

    def weight_copies(e, s):
        return (pltpu.make_async_copy(wgu_hbm.at[layer, e], wgu_f32.at[s], gsem.at[s]),
                pltpu.make_async_copy(wd_hbm.at[layer, e], wd_f32.at[s], dsem.at[s]))

    @pl.when(i == 0)
    def _():
        for copy in weight_copies(blk_e_ref[0], blk_slot_ref[0]):
            copy.start()

    @pl.when(blk_first_ref[i] == 1)
    def _():
        s = blk_slot_ref[i]
        for copy in weight_copies(blk_e_ref[i], s):
            copy.wait()
        nxt = blk_next_ref[i]

        @pl.when(nxt >= 0)
        def _():
            for copy in weight_copies(nxt, 1 - s):
                copy.start()

        for c in range(4):
            cols = slice(c * 512, (c + 1) * 512)
            wgu_bf[:, cols] = wgu_f32[s, :, cols].astype(jnp.bfloat16)
        for c in range(2):
            cols = slice(c * 512, (c + 1) * 512)
            wd_bf[:, cols] = wd_f32[s, :, cols].astype(jnp.bfloat16)

    @pl.when(n_valid > 0)
    def _():
        live = lax.broadcasted_iota(jnp.int32, (EXPERT_ROWS, 128), 0) < n_valid
        x = jnp.concatenate([jnp.where(live, _load_row_tiles(x_ref, 0, EXPERT_ROWS, c), 0.0).astype(jnp.bfloat16)
                             for c in range(LANE_TILES)], axis=1)
        glu = jnp.dot(x, wgu_bf[:, 0:D_EXPERT], preferred_element_type=jnp.float32) + bgu_ref[0, 0, :, 0:D_EXPERT]
        lin = jnp.dot(x, wgu_bf[:, D_EXPERT:], preferred_element_type=jnp.float32) + bgu_ref[0, 0, :, D_EXPERT:]
        glu = jnp.minimum(glu, SWIGLU_LIMIT)
        lin = jnp.clip(lin, -SWIGLU_LIMIT, SWIGLU_LIMIT)
        act = glu * jax.nn.sigmoid(SWIGLU_ALPHA * glu) * (lin + 1.0)
        y = jnp.dot(act.astype(jnp.bfloat16), wd_bf[...], preferred_element_type=jnp.float32) + bd_ref[0, 0]
        _store_row_tiles(y_ref, y)

    @pl.when(n_valid <= 0)
    def _():
        y_ref[...] = jnp.zeros_like(y_ref)


def _moe_ffn(layer, table, x_rows, w_gu, b_gu, w_down, b_down):
    n_blocks = table[0].shape[0]
    depth = w_gu.shape[0]
    rows = lambda i, be, br, *_: (br[i], 0)
    expert = lambda i, be, *_: (layer, be[i], 0, 0)
    grid_spec = pltpu.PrefetchScalarGridSpec(
        num_scalar_prefetch=len(table),
        grid=(n_blocks,),
        in_specs=[
            pl.BlockSpec((EXPERT_ROWS * LANE_TILES, 128), rows),
            pl.BlockSpec(memory_space=pl.ANY),
            pl.BlockSpec((1, 1, 1, 2 * D_EXPERT), expert),
            pl.BlockSpec(memory_space=pl.ANY),
            pl.BlockSpec((1, 1, 1, D_MODEL), expert),
        ],
        out_specs=pl.BlockSpec((EXPERT_ROWS * LANE_TILES, 128), rows),
        scratch_shapes=[pltpu.VMEM((2, D_MODEL, 2 * D_EXPERT), jnp.float32),
                        pltpu.VMEM((2, D_EXPERT, D_MODEL), jnp.float32),
                        pltpu.VMEM((D_MODEL, 2 * D_EXPERT), jnp.bfloat16),
                        pltpu.VMEM((D_EXPERT, D_MODEL), jnp.bfloat16),
                        pltpu.SemaphoreType.DMA((2,)), pltpu.SemaphoreType.DMA((2,))],
    )
    return pl.pallas_call(
        functools.partial(_moe_ffn_kernel, layer=layer),
        grid_spec=grid_spec,
        out_shape=jax.ShapeDtypeStruct(x_rows.shape, jnp.float32),
        compiler_params=pltpu.CompilerParams(dimension_semantics=("arbitrary",), vmem_limit_bytes=VMEM_LIMIT),
        name="moe_ffn",
    )(*table, x_rows, w_gu, b_gu.reshape(depth, N_EXPERTS, 1, -1), w_down, b_down.reshape(depth, N_EXPERTS, 1, -1))


def _combine_kernel(dest_hbm, y_hbm, x_ref, g_ref, *rest, project):
    if project:
        gain_ref, w_ref, hg_ref, xo_ref, o_ref, ybuf, idx_smem, isem, gsem = rest
    else:
        xo_ref, ybuf, idx_smem, isem, gsem = rest
    i = pl.program_id(0)
    nt = pl.num_programs(0)
    slot = i % 2

    def idx_copy(tile, s):
        return pltpu.make_async_copy(dest_hbm.at[tile], _index_slot(idx_smem, s), isem.at[s])

    def issue_gather(s):
        base = s * TILE_SLOTS

        def body(r, carry):
            for k in range(TOP_K):
                src = idx_smem[base + r * TOP_K + k]
                pltpu.make_async_copy(_row_tile(y_hbm, src), _row_tile(ybuf.at[s], k * TOKEN_TILE + r),
                                      gsem.at[s]).start(priority=k % 2)
            return carry
        lax.fori_loop(0, TOKEN_TILE, body, 0, unroll=4)

    @pl.when(i == 0)
    def _():
        idx_copy(0, 0).start()
        idx_copy(0, 0).wait()
        issue_gather(0)

        @pl.when(nt > 1)
        def _():
            idx_copy(1, 1).start()

    @pl.when(i + 1 < nt)
    def _():
        idx_copy(i + 1, 1 - slot).wait()
        issue_gather(1 - slot)

    pltpu.make_async_copy(y_hbm.at[pl.ds(0, TOP_K * TOKEN_TILE * LANE_TILES)], ybuf.at[slot], gsem.at[slot]).wait()

    @pl.when(i + 2 < nt)
    def _():
        idx_copy(i + 2, slot).start()

    g = g_ref[...]
    gates = [jnp.broadcast_to(g[:, k:k + 1], (TOKEN_TILE, 128)) for k in range(TOP_K)]
    cols = []
    for c in range(LANE_TILES):
        acc = x_ref[:, c * 128:(c + 1) * 128]
        for k in range(TOP_K):
            acc = acc + gates[k] * _load_row_tiles(ybuf.at[slot], k * TOKEN_TILE, TOKEN_TILE, c)
        cols.append(acc)
    x = jnp.concatenate(cols, axis=1)
    xo_ref[...] = x
    if project:
        _project(_rms(x, gain_ref[...]).astype(jnp.bfloat16), w_ref, hg_ref, o_ref)


def _combine_call(dest_tiles, y_rows, x2, gates, gain=None, w=None, head_gain=None):
    t = x2.shape[0]
    project = w is not None
    row = lambda width: pl.BlockSpec((TOKEN_TILE, width), lambda i: (i, 0))
    full = lambda shape: pl.BlockSpec(shape, lambda i: (0, 0))
    in_specs = [pl.BlockSpec(memory_space=pl.ANY), pl.BlockSpec(memory_space=pl.ANY), row(D_MODEL), row(ROUTER_PAD)]
    out_specs = [row(D_MODEL)]
    out_shape = [jax.ShapeDtypeStruct((t, D_MODEL), jnp.float32)]
    args = [dest_tiles, y_rows, x2, gates]
    if project:
        in_specs += [full((1, D_MODEL)), full((D_MODEL, PROJ_WIDTH)), full((1, PROJ_WIDTH))]
        out_specs.append(row(PROJ_WIDTH))
        out_shape.append(jax.ShapeDtypeStruct((t, PROJ_WIDTH), jnp.bfloat16))
        args += [gain, w, head_gain]
    out = pl.pallas_call(
        functools.partial(_combine_kernel, project=project),
        grid=(t // TOKEN_TILE,),
        in_specs=in_specs, out_specs=out_specs, out_shape=out_shape,
        scratch_shapes=[pltpu.VMEM((2, TOP_K * TOKEN_TILE * LANE_TILES, 128), jnp.float32),
                        pltpu.SMEM((2 * TILE_SLOTS,), jnp.int32),
                        pltpu.SemaphoreType.DMA((2,)), pltpu.SemaphoreType.DMA((2,))],
        compiler_params=pltpu.CompilerParams(dimension_semantics=("arbitrary",), vmem_limit_bytes=VMEM_LIMIT),
        name="combine_in_proj" if project else "combine",
    )(*args)
    return out if project else out[0]


def _t5_bucket(rel):
    half = T5_BUCKETS // 2
    max_exact = half // 2
    ret = np.where(rel > 0, half, 0)
    n = np.abs(rel)
    nf = np.maximum(n, 1).astype(np.float32)
    large = max_exact + (np.log(nf / max_exact) / math.log(T5_MAX_DIST / max_exact)
                         * (half - max_exact)).astype(np.int32)
    large = np.minimum(large, half - 1)
    return ret + np.where(n < max_exact, n, large)


def _band_ok(n_prev, band):
    q_chunk = np.arange(Q_TILE)[:, None] // CHUNK + n_prev
    k_chunk = np.arange(band)[None, :] // CHUNK
    return (k_chunk <= q_chunk) & (k_chunk >= q_chunk - n_prev)


def _toeplitz(u, band):
    heads, length = u.shape
    flat = jnp.broadcast_to(u[:, None, :], (heads, Q_TILE, length)).reshape(heads, Q_TILE * length)
    skew = flat[:, :Q_TILE * (length - 1)].reshape(heads, Q_TILE, length - 1)
    return skew[:, :, Q_TILE - 1:Q_TILE - 1 + band]


def _a_bias(t5_bias):
    rel = np.arange(Q_TILE + A_BAND) - (Q_TILE - 1) - A_PAD
    bias = _toeplitz(t5_bias[_t5_bucket(rel)].T.astype(jnp.float32), A_BAND)
    bias = jnp.where(_band_ok(A_PREV_CHUNKS, A_BAND)[None], bias, NEG_INF)
    pairs = [jnp.concatenate([bias[4 * hk + parity], bias[4 * hk + 2 + parity]], axis=0)
             for hk in range(A_KV_HEADS) for parity in range(2)]
    return jnp.stack(pairs).transpose(0, 2, 1)


def _c_bias(rel_table):
    dist = C_PAD + (Q_TILE - 1) - np.arange(Q_TILE + C_BAND)
    idx = np.clip(dist, -(CHUNK - 1), C_REL_CLIP) + (CHUNK - 1)
    bias = _toeplitz(rel_table[idx].T.astype(jnp.float32), C_BAND)
    return jnp.where(_band_ok(C_PREV_CHUNKS, C_BAND)[None], bias, NEG_INF).transpose(0, 2, 1)


def _rope_tables(seq):
    half = B_ROPE_DIM // 2
    inv_freq = ROPE_THETA ** (-(jnp.arange(half, dtype=jnp.float32) / half))
    ang = jnp.arange(seq, dtype=jnp.float32)[:, None] * inv_freq[None, :]
    cos, sin = jnp.cos(ang), jnp.sin(ang)
    zeros = jnp.zeros((seq, B_NOPE_DIM), jnp.float32)
    tail = jnp.zeros((seq, B_HEAD_PAD - B_QK_DIM), jnp.float32)
    zh = jnp.zeros((seq, half), jnp.float32)
    cos_t = jnp.concatenate([zeros + 1.0, cos, cos, tail + 1.0], axis=1)
    sa = jnp.concatenate([zeros, -sin, zh, tail], axis=1)
    sb = jnp.concatenate([zeros, zh, sin, tail], axis=1)
    return cos_t, sa, sb


def _regroup_w_in(w):
    splits = np.cumsum((512, 128, 128, 256, 128, 32, 256, 256))
    aq, ak, av, bcq, bckv, bkpe, cq, ck, cv = jnp.split(w, splits.tolist(), axis=1)
    z = lambda n: jnp.zeros((w.shape[0], n), w.dtype)
    return jnp.concatenate([aq, ak, av, bcq, bckv, z(64), bkpe, z(32), cq, ck, cv], axis=1).astype(jnp.bfloat16)


def _head_gain_row(a_q, a_k, c_q, c_k):
    scale = HEAD_DIM ** -0.5
    row = jnp.ones((PROJ_WIDTH,), jnp.float32)
    for sec, heads, gain in ((SEC_AQ, A_Q_HEADS, a_q * scale), (SEC_AKV, A_KV_HEADS, a_k),
                             (SEC_CQ, C_HEADS, c_q * scale), (SEC_CK, C_HEADS, c_k)):
        row = row.at[sec:sec + heads * HEAD_DIM].set(jnp.tile(gain, heads))
    return row.reshape(1, PROJ_WIDTH)


def _pad_heads(w, head_dim, keep):
    rank = w.shape[0]
    w = w.reshape(rank, B_HEADS, head_dim)[:, :, :keep]
    w = jnp.pad(w, ((0, 0), (0, 0), (0, B_HEAD_PAD - keep)))
    return w.reshape(rank, B_HEADS * B_HEAD_PAD).astype(jnp.bfloat16)


def _pad_lanes(g, offset, width):
    return jnp.pad(g, (offset, width - offset - g.shape[0])).reshape(1, width)


def _block_table(counts, t):
    tk = t * TOP_K
    blocks_per_expert = t // EXPERT_ROWS
    nblk = (counts + EXPERT_ROWS - 1) // EXPERT_ROWS
    blk_end = jnp.cumsum(nblk)
    blk_first = blk_end - nblk
    n_blocks = (tk + N_EXPERTS * (EXPERT_ROWS - 1) + EXPERT_ROWS - 1) // EXPERT_ROWS
    blk = jnp.arange(n_blocks, dtype=jnp.int32)
    blk_e = jnp.minimum(jnp.sum(blk[:, None] >= blk_end[None, :], axis=1), N_EXPERTS - 1).astype(jnp.int32)
    own = blk_e[:, None] == jnp.arange(N_EXPERTS, dtype=jnp.int32)[None, :]
    pick = lambda v: jnp.sum(jnp.where(own, v[None, :], 0), axis=1)
    j = blk - pick(blk_first)
    blk_valid = jnp.clip(pick(counts) - j * EXPERT_ROWS, 0, EXPERT_ROWS).astype(jnp.int32)
    spare = N_EXPERTS * blocks_per_expert
    blk_row = jnp.where(blk_valid > 0, blk_e * blocks_per_expert + j, spare).astype(jnp.int32)
    experts = jnp.arange(N_EXPERTS, dtype=jnp.int32)
    used = counts > 0
    later = jnp.logical_and(experts[None, :] > experts[:, None], used[None, :])
    nxt = jnp.min(jnp.where(later, experts[None, :], N_EXPERTS), axis=1)
    nxt = jnp.where(nxt < N_EXPERTS, nxt, -1)
    ordinal = jnp.cumsum(used.astype(jnp.int32)) - 1
    blk_first = jnp.logical_and(j == 0, blk_valid > 0).astype(jnp.int32)
    blk_next = pick(nxt).astype(jnp.int32)
    blk_slot = (pick(ordinal) % 2).astype(jnp.int32)
    return blk_e, blk_row, blk_valid, blk_first, blk_next, blk_slot


def kernel(x, attn_norm, w_in, a_q_norm, a_k_norm, a_sinks, t5_bias, b_q_a_norm, b_w_q_b, b_kv_a_norm, b_w_kv_b,
           b_q_norm, b_k_norm, c_q_norm, c_k_norm, c_rel_bias, a_out_norm, b_out_norm, c_out_norm, w_out, ffn_norm,
           w_router, b_router, w_gate_up, b_gate_up, w_down, b_down):
    b, s, d = x.shape
    t = b * s
    depth = w_in.shape[0]
    a_bias = _a_bias(t5_bias)
    cos_t, sin_a, sin_b = _rope_tables(s)
    x2 = x.reshape(t, d)
    head_gain = lambda l: _head_gain_row(a_q_norm[l], a_k_norm[l], c_q_norm[l], c_k_norm[l])
    proj = _in_proj(x2, attn_norm[0].reshape(1, d), _regroup_w_in(w_in[0]), head_gain(0))
    for l in range(depth):
        proj3 = proj.reshape(b, s, PROJ_WIDTH)
        o_a = _attn_a(proj3, a_bias, a_sinks[l])
        kv_w = b_w_kv_b[l].reshape(B_KV_RANK, B_HEADS, B_NOPE_DIM + B_V_DIM)
        wv = kv_w[:, :, B_NOPE_DIM:].reshape(B_KV_RANK, B_HEADS * B_V_DIM).astype(jnp.bfloat16)
        qb, kb, vb = _latent_prep(
            proj, b_q_a_norm[l].reshape(1, -1), b_kv_a_norm[l].reshape(1, -1),
            _pad_heads(b_w_q_b[l], B_QK_DIM, B_QK_DIM), _pad_heads(b_w_kv_b[l], B_NOPE_DIM + B_V_DIM, B_NOPE_DIM), wv,
            _pad_lanes(b_q_norm[l], 0, B_HEAD_PAD), _pad_lanes(b_k_norm[l], 0, B_HEAD_PAD), cos_t, sin_a, sin_b, s)
        o_b = _attn_b(qb.reshape(b, s, -1), kb.reshape(b, s, -1), vb.reshape(b, s, -1))
        o_c = _attn_c(proj3, _c_bias(c_rel_bias[l]))
        wr = jnp.pad(w_router[l], ((0, 0), (0, ROUTER_PAD - N_EXPERTS)))
        wr_hi = wr.astype(jnp.bfloat16)
        wr = jnp.stack([wr_hi, (wr - wr_hi.astype(jnp.float32)).astype(jnp.bfloat16)])
        br = jnp.pad(b_router[l], (0, ROUTER_PAD - N_EXPERTS)).reshape(1, ROUTER_PAD)
        n_rows = N_EXPERTS * t + SPARE_ROWS
        x2, x_rows, dest, gates, counts = _out_proj(
            o_a.reshape(t, -1), o_b.reshape(t, -1), o_c.reshape(t, -1), x2,
            a_out_norm[l].reshape(1, -1), b_out_norm[l].reshape(1, -1), c_out_norm[l].reshape(1, -1),
            w_out[l].astype(jnp.bfloat16), ffn_norm[l].reshape(1, -1), wr, br, n_rows)
        dest_tiles = dest[:, :TOP_K].reshape(t // TOKEN_TILE, TOKEN_TILE * TOP_K)
        table = _block_table(counts[0, :N_EXPERTS], t)
        y_rows = _moe_ffn(l, table, x_rows, w_gate_up, b_gate_up, w_down, b_down)
        if l + 1 < depth:
            x2, proj = _combine_call(dest_tiles, y_rows, x2, gates, attn_norm[l + 1].reshape(1, d),
                                     _regroup_w_in(w_in[l + 1]), head_gain(l + 1))
        else:
            x2 = _combine_call(dest_tiles, y_rows, x2, gates)
    return x2.reshape(b, s, d)
```

```python
import functools
import math

import jax
import jax.numpy as jnp
import numpy as np
from jax import lax
from jax.experimental import pallas as pl
from jax.experimental.pallas import tpu as pltpu

D_MODEL = 1024
CHUNK = 64
HEAD_DIM = 64
EPS = 1e-6
NEG_INF = -1e30
A_Q_HEADS = 8
A_KV_HEADS = 2
A_GROUP = 4
A_PREV_CHUNKS = 2
T5_BUCKETS = 32
T5_MAX_DIST = 128
B_HEADS = 4
B_Q_RANK = 256
B_KV_RANK = 128
B_NOPE_DIM = 64
B_ROPE_DIM = 32
B_QK_DIM = 96
B_V_DIM = 64
ROPE_THETA = 10000.0
C_HEADS = 4
C_PREV_CHUNKS = 8
C_REL_CLIP = 256
N_EXPERTS = 32
TOP_K = 4
D_EXPERT = 1024
SWIGLU_LIMIT = 7.0
SWIGLU_ALPHA = 1.702

SEC_AQ = 0
SEC_AKV = 512
SEC_BCQ = 768
SEC_BCKV = 1024
SEC_BKPE = 1152
SEC_CQ = 1280
SEC_CK = 1536
SEC_CV = 1792
PROJ_WIDTH = 2048
HEAD_NORM_SLABS = tuple(sec // 128 + j for sec, n in ((SEC_AQ, 4), (SEC_AKV, 1), (SEC_CQ, 2), (SEC_CK, 2))
                        for j in range(n))
B_HEAD_PAD = 128

Q_TILE = 128
A_BAND = Q_TILE + A_PREV_CHUNKS * CHUNK
C_BAND = Q_TILE + C_PREV_CHUNKS * CHUNK
A_PAD = A_PREV_CHUNKS * CHUNK
C_PAD = C_PREV_CHUNKS * CHUNK
B_TILE = 512
ROW_TILE = 512
EXPERT_ROWS = 256
ROUTER_PAD = 128
TOKEN_TILE = 256
TILE_SLOTS = TOKEN_TILE * TOP_K
ROW_BUFFERS = 3
SPARE_ROWS = TILE_SLOTS
LANE_TILES = D_MODEL // 128
VMEM_LIMIT = 48 * 1024 * 1024


def _rms(x, gain):
    return x * lax.rsqrt(jnp.mean(x * x, axis=-1, keepdims=True) + EPS) * gain


def _nt_dot(a, b):
    return lax.dot_general(a, b, (((1,), (1,)), ((), ())), preferred_element_type=jnp.float32)


def _load_row_tiles(ref, first_row, n_rows, c):
    return ref[pl.ds(first_row * LANE_TILES + c, n_rows, stride=LANE_TILES), :]


def _store_row_tiles(ref, value):
    for c in range(LANE_TILES):
        ref[pl.ds(c, value.shape[0], stride=LANE_TILES), :] = value[:, c * 128:(c + 1) * 128]


def _row_tile(ref, row):
    return ref.at[pl.ds(pl.multiple_of(row * LANE_TILES, LANE_TILES), LANE_TILES)]


def _index_slot(idx_smem, s):
    return idx_smem.at[pl.ds(pl.multiple_of(s * TILE_SLOTS, TILE_SLOTS), TILE_SLOTS)]


def _project(h, w_ref, hg_ref, o_ref):
    li = lax.broadcasted_iota(jnp.int32, (128, 128), 0) // HEAD_DIM
    lj = lax.broadcasted_iota(jnp.int32, (128, 128), 1) // HEAD_DIM
    same_head = jnp.where(li == lj, 1.0, 0.0).astype(jnp.bfloat16)
    for n in range(PROJ_WIDTH // 512):
        acc = jnp.dot(h, w_ref[:, n * 512:(n + 1) * 512], preferred_element_type=jnp.float32)
        slabs = []
        for j in range(4):
            slab = n * 4 + j
            a = acc[:, j * 128:(j + 1) * 128]
            if slab in HEAD_NORM_SLABS:
                sq = a * a
                hi = sq.astype(jnp.bfloat16)
                lo = (sq - hi.astype(jnp.float32)).astype(jnp.bfloat16)
                ss = (jnp.dot(hi, same_head, preferred_element_type=jnp.float32)
                      + jnp.dot(lo, same_head, preferred_element_type=jnp.float32))
                a = a * lax.rsqrt(ss * (1.0 / HEAD_DIM) + EPS) * hg_ref[:, slab * 128:(slab + 1) * 128]
            slabs.append(a.astype(o_ref.dtype))
        o_ref[:, n * 512:(n + 1) * 512] = jnp.concatenate(slabs, axis=1)


def _in_proj_kernel(x_ref, g_ref, w_ref, hg_ref, o_ref):
    _project(_rms(x_ref[...], g_ref[...]).astype(jnp.bfloat16), w_ref, hg_ref, o_ref)


def _in_proj(x2, gain, w, head_gain):
    t = x2.shape[0]
    return pl.pallas_call(
        _in_proj_kernel,
        grid=(t // ROW_TILE,),
        in_specs=[
            pl.BlockSpec((ROW_TILE, D_MODEL), lambda i: (i, 0)),
            pl.BlockSpec((1, D_MODEL), lambda i: (0, 0)),
            pl.BlockSpec((D_MODEL, PROJ_WIDTH), lambda i: (0, 0)),
            pl.BlockSpec((1, PROJ_WIDTH), lambda i: (0, 0)),
        ],
        out_specs=pl.BlockSpec((ROW_TILE, PROJ_WIDTH), lambda i: (i, 0)),
        out_shape=jax.ShapeDtypeStruct((t, PROJ_WIDTH), jnp.bfloat16),
        compiler_params=pltpu.CompilerParams(dimension_semantics=("arbitrary",), vmem_limit_bytes=VMEM_LIMIT),
        name="in_proj",
    )(x2, gain, w, head_gain)


def _half_lanes(x, parity, fill=0.0):
    lane = lax.broadcasted_iota(jnp.int32, x.shape, x.ndim - 1) % 128
    keep = lane < HEAD_DIM if parity == 0 else lane >= HEAD_DIM
    return jnp.where(keep, x, jnp.full_like(x, fill))


def _tn_dot(a, b):
    return lax.dot_general(a, b, (((0,), (0,)), ((), ())), preferred_element_type=jnp.float32)


def _softmax_pv(s, key_ok, v_aug, sink=None, value_half=0):
    s = jnp.where(key_ok, s, NEG_INF)
    m = jnp.max(s, axis=0, keepdims=True)
    if sink is not None:
        m = jnp.maximum(m, sink)
    e = jnp.exp(s - m).astype(jnp.bfloat16)
    o = _tn_dot(e, v_aug)
    if sink is not None:
        row = lax.broadcasted_iota(jnp.int32, (8, s.shape[1]), 0)
        e_sink = jnp.where(row == 0, jnp.exp(sink - m), 0.0).astype(jnp.bfloat16)
        first = lax.broadcasted_iota(jnp.int32, (8, 128), 0) == 0
        sel = _half_lanes(jnp.where(first, 1.0, 0.0), 1 - value_half).astype(jnp.bfloat16)
        o = o + _tn_dot(e_sink, sel)
    return o


def _merge_halves(halves):
    lane = lax.broadcasted_iota(jnp.int32, halves[0].shape, 1)
    normed = [o / pltpu.roll(o, HEAD_DIM, 1) for o in halves]
    return jnp.where(lane < HEAD_DIM, normed[0], normed[1])


def _attn_a_kernel(q_ref, kv_ref, bias_ref, sink_ref, o_ref, k_pad, v_pad):
    i = pl.program_id(1)
    seq = kv_ref.shape[1]

    @pl.when(i == 0)
    def _():
        for n in range(2 * A_KV_HEADS):
            k_pad[n, 0:A_PAD, :] = jnp.zeros((A_PAD, 128), k_pad.dtype)
            v_pad[n, 0:A_PAD, :] = jnp.zeros((A_PAD, 128), v_pad.dtype)

        def fill(r, carry):
            rows = pl.ds(pl.multiple_of(r * 256, 256), 256)
            dst = pl.ds(pl.multiple_of(r * 256 + A_PAD, 128), 256)
            kv = kv_ref[0, rows, :].astype(jnp.float32)
            for src, pad, other in ((kv[:, 0:128], k_pad, 0.0), (kv[:, 128:256], v_pad, 1.0)):
                swapped = pltpu.roll(src, HEAD_DIM, 1)
                pad[0, dst, :] = _half_lanes(src, 0, other).astype(pad.dtype)
                pad[1, dst, :] = _half_lanes(swapped, 1, other).astype(pad.dtype)
                pad[2, dst, :] = _half_lanes(swapped, 0, other).astype(pad.dtype)
                pad[3, dst, :] = _half_lanes(src, 1, other).astype(pad.dtype)
            return carry

        lax.fori_loop(0, seq // 256, fill, 0)

    start = pl.multiple_of(i * Q_TILE, Q_TILE)
    band = pl.ds(start, A_BAND)
    key_ok = lax.broadcasted_iota(jnp.int32, (A_BAND, 2 * Q_TILE), 0) + (start - A_PAD) >= 0
    first_slab = lax.broadcasted_iota(jnp.int32, (1, 2 * Q_TILE), 1) < Q_TILE
    for hk in range(A_KV_HEADS):
        q2 = jnp.concatenate([q_ref[0, :, (2 * hk) * 128:(2 * hk + 1) * 128],
                              q_ref[0, :, (2 * hk + 1) * 128:(2 * hk + 2) * 128]], axis=0)
        halves = []
        for parity in range(2):
            sink = jnp.where(first_slab, sink_ref[4 * hk + parity], sink_ref[4 * hk + 2 + parity])
            s = _nt_dot(k_pad[2 * hk + parity, band, :], q2) + bias_ref[2 * hk + parity]
            halves.append(_softmax_pv(s, key_ok, v_pad[2 * hk + parity, band, :], sink, parity))
        out = _merge_halves(halves)
        o_ref[0, :, (2 * hk) * 128:(2 * hk + 1) * 128] = out[0:Q_TILE].astype(o_ref.dtype)
        o_ref[0, :, (2 * hk + 1) * 128:(2 * hk + 2) * 128] = out[Q_TILE:2 * Q_TILE].astype(o_ref.dtype)


def _attn_a(proj3, bias, sinks):
    b, s, _ = proj3.shape
    return pl.pallas_call(
        _attn_a_kernel,
        grid=(b, s // Q_TILE),
        in_specs=[
            pl.BlockSpec((1, Q_TILE, 512), lambda bi, i: (bi, i, SEC_AQ // 512)),
            pl.BlockSpec((1, s, 256), lambda bi, i: (bi, 0, SEC_AKV // 256)),
            pl.BlockSpec((2 * A_KV_HEADS, 2 * Q_TILE, A_BAND), lambda bi, i: (0, 0, 0)),
            pl.BlockSpec(memory_space=pltpu.SMEM),
        ],
        out_specs=pl.BlockSpec((1, Q_TILE, 512), lambda bi, i: (bi, i, 0)),
        out_shape=jax.ShapeDtypeStruct((b, s, 512), jnp.bfloat16),
        scratch_shapes=[pltpu.VMEM((2 * A_KV_HEADS, s + A_PAD, 128), jnp.bfloat16),
                        pltpu.VMEM((2 * A_KV_HEADS, s + A_PAD, 128), jnp.bfloat16)],
        compiler_params=pltpu.CompilerParams(dimension_semantics=("arbitrary", "arbitrary"),
                                             vmem_limit_bytes=VMEM_LIMIT),
        name="attn_a",
    )(proj3, proj3, bias, sinks)


def _latent_prep_kernel(cq_ref, ckv_ref, kpe_ref, qag_ref, kvag_ref, wq_ref, wk_ref, wv_ref,
                        qg_ref, kg_ref, cos_ref, sa_ref, sb_ref, qo_ref, ko_ref, vo_ref):
    lane = lax.broadcasted_iota(jnp.int32, (ROW_TILE, B_HEAD_PAD), 1)
    is_nope = lane < B_NOPE_DIM
    is_rope = jnp.logical_and(lane >= B_NOPE_DIM, lane < B_QK_DIM)
    cos = cos_ref[...]
    sa = sa_ref[...]
    sb = sb_ref[...]

    def split_norm(x, gain):
        sq = x * x
        nope = jnp.sum(jnp.where(is_nope, sq, 0.0), axis=-1, keepdims=True) * (1.0 / B_NOPE_DIM)
        rope = jnp.sum(jnp.where(is_rope, sq, 0.0), axis=-1, keepdims=True) * (1.0 / B_ROPE_DIM)
        r = jnp.where(is_nope, lax.rsqrt(nope + EPS), lax.rsqrt(rope + EPS))
        return x * r * gain

    def rotate(x):
        return x * cos + pltpu.roll(x, 112, 1) * sa + pltpu.roll(x, 16, 1) * sb

    cq = _rms(cq_ref[...].astype(jnp.float32), qag_ref[...]).astype(jnp.bfloat16)
    ckv = _rms(ckv_ref[...].astype(jnp.float32), kvag_ref[...]).astype(jnp.bfloat16)
    kpe = kpe_ref[...].astype(jnp.float32)
    kpe_n = rotate(split_norm(kpe, kg_ref[...]))
    kpe_n = jnp.where(is_rope, kpe_n, 0.0)
    v = jnp.dot(ckv, wv_ref[...], preferred_element_type=jnp.float32)
    for h in range(B_HEADS):
        pair = v[:, (h // 2) * 128:(h // 2 + 1) * 128]
        vo_ref[:, h * 128:(h + 1) * 128] = _half_lanes(pair, h % 2, 1.0).astype(vo_ref.dtype)
    qscale = B_QK_DIM ** -0.5
    for h in range(B_HEADS):
        cols = slice(h * B_HEAD_PAD, (h + 1) * B_HEAD_PAD)
        qh = jnp.dot(cq, wq_ref[:, cols], preferred_element_type=jnp.float32)
        qh = rotate(split_norm(qh, qg_ref[...])) * qscale
        qo_ref[:, cols] = qh.astype(qo_ref.dtype)
        kh = jnp.dot(ckv, wk_ref[:, cols], preferred_element_type=jnp.float32)
        kh = jnp.where(is_nope, split_norm(kh, kg_ref[...]), 0.0) + kpe_n
        ko_ref[:, cols] = kh.astype(ko_ref.dtype)


def _latent_prep(proj, q_a_gain, kv_a_gain, wq, wk, wv, q_gain, k_gain, cos, sa, sb, seq):
    t = proj.shape[0]
    pos_blocks = seq // ROW_TILE
    full = lambda shape: pl.BlockSpec(shape, lambda i: (0, 0))
    tab = pl.BlockSpec((ROW_TILE, B_HEAD_PAD), lambda i: (i % pos_blocks, 0))
    return pl.pallas_call(
        _latent_prep_kernel,
        grid=(t // ROW_TILE,),
        in_specs=[
            pl.BlockSpec((ROW_TILE, 256), lambda i: (i, SEC_BCQ // 256)),
            pl.BlockSpec((ROW_TILE, 128), lambda i: (i, SEC_BCKV // 128)),
            pl.BlockSpec((ROW_TILE, 128), lambda i: (i, SEC_BKPE // 128)),
            full((1, B_Q_RANK)), full((1, B_KV_RANK)),
            full((B_Q_RANK, B_HEADS * B_HEAD_PAD)), full((B_KV_RANK, B_HEADS * B_HEAD_PAD)),
            full((B_KV_RANK, B_HEADS * B_V_DIM)),
            full((1, B_HEAD_PAD)), full((1, B_HEAD_PAD)),
            tab, tab, tab,
        ],
        out_specs=[
            pl.BlockSpec((ROW_TILE, B_HEADS * B_HEAD_PAD), lambda i: (i, 0)),
            pl.BlockSpec((ROW_TILE, B_HEADS * B_HEAD_PAD), lambda i: (i, 0)),
            pl.BlockSpec((ROW_TILE, B_HEADS * B_HEAD_PAD), lambda i: (i, 0)),
        ],
        out_shape=[
            jax.ShapeDtypeStruct((t, B_HEADS * B_HEAD_PAD), jnp.bfloat16),
            jax.ShapeDtypeStruct((t, B_HEADS * B_HEAD_PAD), jnp.bfloat16),
            jax.ShapeDtypeStruct((t, B_HEADS * B_HEAD_PAD), jnp.bfloat16),
        ],
        compiler_params=pltpu.CompilerParams(dimension_semantics=("arbitrary",), vmem_limit_bytes=VMEM_LIMIT),
        name="latent_prep",
    )(proj, proj, proj, q_a_gain, kv_a_gain, wq, wk, wv, q_gain, k_gain, cos, sa, sb)


def _attn_b_kernel(q_ref, k_ref, v_ref, o_ref):
    i = pl.program_id(1)
    query_chunk = lax.broadcasted_iota(jnp.int32, (B_TILE, B_TILE), 0) // CHUNK
    key_chunk = lax.broadcasted_iota(jnp.int32, (B_TILE, B_TILE), 1) // CHUNK
    diag_ok = key_chunk <= query_chunk

    def step(j, carry, masked):
        rows = pl.ds(pl.multiple_of(j * B_TILE, B_TILE), B_TILE)
        out = []
        for h in range(B_HEADS):
            m, acc = carry[h]
            head = slice(h * B_HEAD_PAD, (h + 1) * B_HEAD_PAD)
            s = _nt_dot(q_ref[0, :, head], k_ref[0, rows, head])
            if masked:
                s = jnp.where(diag_ok, s, NEG_INF)
            m_new = jnp.maximum(m, jnp.max(s, axis=-1, keepdims=True))
            e = jnp.exp(s - m_new).astype(jnp.bfloat16)
            acc = jnp.exp(m - m_new) * acc + jnp.dot(e, v_ref[0, rows, head], preferred_element_type=jnp.float32)
            out.append((m_new, acc))
        return tuple(out)

    init = tuple((jnp.full((B_TILE, 1), NEG_INF, jnp.float32), jnp.zeros((B_TILE, B_HEAD_PAD), jnp.float32))
                 for _ in range(B_HEADS))
    carry = lax.fori_loop(0, i, lambda j, c: step(j, c, False), init)
    carry = step(i, carry, True)
    for pair in range(B_HEADS // 2):
        halves = [carry[2 * pair + parity][1] for parity in range(2)]
        o_ref[0, :, pair * 128:(pair + 1) * 128] = _merge_halves(halves).astype(o_ref.dtype)


def _attn_b(qb, kb, vb):
    b, s, _ = qb.shape
    return pl.pallas_call(
        _attn_b_kernel,
        grid=(b, s // B_TILE),
        in_specs=[
            pl.BlockSpec((1, B_TILE, B_HEADS * B_HEAD_PAD), lambda bi, i: (bi, i, 0)),
            pl.BlockSpec((1, s, B_HEADS * B_HEAD_PAD), lambda bi, i: (bi, 0, 0)),
            pl.BlockSpec((1, s, B_HEADS * B_HEAD_PAD), lambda bi, i: (bi, 0, 0)),
        ],
        out_specs=pl.BlockSpec((1, B_TILE, B_HEADS * B_V_DIM), lambda bi, i: (bi, i, 0)),
        out_shape=jax.ShapeDtypeStruct((b, s, B_HEADS * B_V_DIM), jnp.bfloat16),
        compiler_params=pltpu.CompilerParams(dimension_semantics=("arbitrary", "arbitrary"),
                                             vmem_limit_bytes=VMEM_LIMIT),
        name="attn_b",
    )(qb, kb, vb)


def _attn_c_kernel(q_ref, k_ref, v_ref, bias_ref, o_ref, k_pad, v_pad):
    i = pl.program_id(1)
    seq = k_ref.shape[1]
    width = C_HEADS * HEAD_DIM

    @pl.when(i == 0)
    def _():
        for parity in range(2):
            k_pad[parity, 0:C_PAD, :] = jnp.zeros((C_PAD, width), k_pad.dtype)
            v_pad[parity, 0:C_PAD, :] = jnp.zeros((C_PAD, width), v_pad.dtype)

        def fill(r, carry):
            rows = pl.ds(pl.multiple_of(r * 256, 256), 256)
            dst = pl.ds(pl.multiple_of(r * 256 + C_PAD, 256), 256)
            k = k_ref[0, rows, :]
            v = v_ref[0, rows, :]
            for parity in range(2):
                k_pad[parity, dst, :] = _half_lanes(k, parity)
                v_pad[parity, dst, :] = _half_lanes(v, parity, fill=1.0)
            return carry

        lax.fori_loop(0, seq // 256, fill, 0)

    start = pl.multiple_of(i * Q_TILE, Q_TILE)
    band = pl.ds(start, C_BAND)
    key_ok = lax.broadcasted_iota(jnp.int32, (C_BAND, Q_TILE), 0) + (start - C_PAD) >= 0
    for slab in range(width // 128):
        lanes = slice(slab * 128, (slab + 1) * 128)
        q = q_ref[0, :, lanes]
        halves = []
        for parity in range(2):
            s = _nt_dot(k_pad[parity, band, lanes], q) + bias_ref[2 * slab + parity]
            halves.append(_softmax_pv(s, key_ok, v_pad[parity, band, lanes]))
        o_ref[0, :, lanes] = _merge_halves(halves).astype(o_ref.dtype)


def _attn_c(proj3, bias):
    b, s, _ = proj3.shape
    width = C_HEADS * HEAD_DIM
    return pl.pallas_call(
        _attn_c_kernel,
        grid=(b, s // Q_TILE),
        in_specs=[
            pl.BlockSpec((1, Q_TILE, width), lambda bi, i: (bi, i, SEC_CQ // width)),
            pl.BlockSpec((1, s, width), lambda bi, i: (bi, 0, SEC_CK // width)),
            pl.BlockSpec((1, s, width), lambda bi, i: (bi, 0, SEC_CV // width)),
            pl.BlockSpec((C_HEADS, C_BAND, Q_TILE), lambda bi, i: (0, 0, 0)),
        ],
        out_specs=pl.BlockSpec((1, Q_TILE, width), lambda bi, i: (bi, i, 0)),
        out_shape=jax.ShapeDtypeStruct((b, s, width), jnp.bfloat16),
        scratch_shapes=[pltpu.VMEM((2, s + C_PAD, width), jnp.bfloat16),
                        pltpu.VMEM((2, s + C_PAD, width), jnp.bfloat16)],
        compiler_params=pltpu.CompilerParams(dimension_semantics=("arbitrary", "arbitrary"),
                                             vmem_limit_bytes=VMEM_LIMIT),
        name="attn_c",
    )(proj3, proj3, proj3, bias)


def _out_proj_kernel(oa_ref, ob_ref, oc_ref, x_ref, ga_ref, gb_ref, gc_ref, w_ref, gf_ref, wr_ref, br_ref,
                     xo_ref, x_hbm, dest_ref, tg_ref, cnt_ref, count, hbuf, dvm, idx_smem, isem, dsem):
    i = pl.program_id(0)
    nt = pl.num_programs(0)
    cap = nt * TOKEN_TILE
    slot = i % ROW_BUFFERS
    prev = (i + ROW_BUFFERS - 1) % ROW_BUFFERS
    spare_row = x_hbm.shape[0] // LANE_TILES - TILE_SLOTS

    def index_copies(s):
        return [pltpu.make_async_copy(
            dvm.at[s, k], idx_smem.at[pl.ds(pl.multiple_of(s * TILE_SLOTS + k * TOKEN_TILE, TOKEN_TILE), TOKEN_TILE)],
            isem.at[s]) for k in range(TOP_K)]

    def row_copy(s, base, r, k):
        dst = idx_smem[base + k * TOKEN_TILE + r]
        return pltpu.make_async_copy(_row_tile(hbuf.at[s], r), _row_tile(x_hbm, dst), dsem.at[s])

    def wait_rows(s):
        for _ in range(TOP_K):
            pltpu.make_async_copy(hbuf.at[s], x_hbm.at[pl.ds(0, TOKEN_TILE * LANE_TILES)], dsem.at[s]).wait()

    @pl.when(i == 0)
    def _():
        count[...] = jnp.zeros_like(count)
        hbuf[ROW_BUFFERS - 1] = jnp.zeros(hbuf.shape[1:], hbuf.dtype)

        def fill(j, carry):
            idx_smem[(ROW_BUFFERS - 1) * TILE_SLOTS + j] = spare_row + j
            return carry

        lax.fori_loop(0, TILE_SLOTS, fill, 0)

    @pl.when(i >= ROW_BUFFERS - 1)
    def _():
        wait_rows(slot)

    @pl.when(i >= 1)
    def _():
        for copy in index_copies(prev):
            copy.wait()

    prev_base = prev * TILE_SLOTS
    for r in range(TOKEN_TILE):
        for k in range(TOP_K):
            row_copy(prev, prev_base, r, k).start(priority=k % 2)

    acc = x_ref[...]
    parts = ((oa_ref, ga_ref, 0, 512), (ob_ref, gb_ref, 512, 256), (oc_ref, gc_ref, 768, 256))
    for ref, g_ref, off, width in parts:
        o = _rms(ref[...].astype(jnp.float32), g_ref[...]).astype(jnp.bfloat16)
        acc = acc + jnp.dot(o, w_ref[off:off + width, :], preferred_element_type=jnp.float32)
    xo_ref[...] = acc
    h = _rms(acc, gf_ref[...])
    _store_row_tiles(hbuf.at[slot], h)
    h_hi = h.astype(jnp.bfloat16)
    h_lo = (h - h_hi.astype(jnp.float32)).astype(jnp.bfloat16)
    logits = (jnp.dot(h_hi, wr_ref[0], preferred_element_type=jnp.float32)
              + jnp.dot(h_lo, wr_ref[0], preferred_element_type=jnp.float32)
              + jnp.dot(h_hi, wr_ref[1], preferred_element_type=jnp.float32)) + br_ref[...]
    lane = lax.broadcasted_iota(jnp.int32, logits.shape, 1)
    work = jnp.where(lane < N_EXPERTS, logits, -jnp.inf)
    gate_out = jnp.zeros(logits.shape, jnp.float32)
    denom = jnp.zeros((logits.shape[0], 1), jnp.float32)
    top = None
    picks = []
    for k in range(TOP_K):
        m = jnp.max(work, axis=-1, keepdims=True)
        idx = jnp.min(jnp.where(work == m, lane, ROUTER_PAD), axis=-1, keepdims=True)
        pick = lane == idx
        work = jnp.where(pick, -jnp.inf, work)
        top = m if top is None else top
        e = jnp.exp(m - top)
        denom = denom + e
        gate_out = jnp.where(lane == k, e, gate_out)
        picks.append((idx, pick))
    tg_ref[...] = gate_out / denom
    chosen = jnp.zeros(logits.shape, jnp.float32)
    for _, pick in picks:
        chosen = jnp.where(pick, 1.0, chosen)
    r_i = lax.broadcasted_iota(jnp.int32, (TOKEN_TILE, TOKEN_TILE), 0)
    c_i = lax.broadcasted_iota(jnp.int32, (TOKEN_TILE, TOKEN_TILE), 1)
    earlier = jnp.where(c_i < r_i, 1.0, 0.0).astype(jnp.bfloat16)
    before = jnp.dot(earlier, chosen.astype(jnp.bfloat16), preferred_element_type=jnp.float32) + count[...]
    dest_out = jnp.zeros(logits.shape, jnp.int32)
    for k, (idx, pick) in enumerate(picks):
        rank = jnp.sum(jnp.where(pick, before, 0.0), axis=-1, keepdims=True).astype(jnp.int32)
        dest_out = jnp.where(lane == k, idx * cap + rank, dest_out)
    dest_ref[...] = dest_out
    count[...] = count[...] + jnp.sum(chosen, axis=0, keepdims=True)
    cnt_ref[...] = count[...].astype(jnp.int32)
    dvm[slot] = dest_out.T[0:8, :]
    for copy in index_copies(slot):
        copy.start()

    @pl.when(i == nt - 1)
    def _():
        for copy in index_copies(slot):
            copy.wait()
        base = slot * TILE_SLOTS

        def body(r, carry):
            for k in range(TOP_K):
                row_copy(slot, base, r, k).start(priority=k % 2)
            return carry

        lax.fori_loop(0, TOKEN_TILE, body, 0, unroll=4)
        for s in range(ROW_BUFFERS):
            wait_rows(s)


def _out_proj(oa, ob, oc, x2, ga, gb, gc, w, gf, wr, br, n_rows):
    t = x2.shape[0]
    assert t % TOKEN_TILE == 0 and t // TOKEN_TILE >= ROW_BUFFERS and n_rows == N_EXPERTS * t + SPARE_ROWS
    row = lambda width: pl.BlockSpec((TOKEN_TILE, width), lambda i: (i, 0))
    full = lambda shape: pl.BlockSpec(shape, lambda i: (0, 0))
    return pl.pallas_call(
        _out_proj_kernel,
        grid=(t // TOKEN_TILE,),
        in_specs=[row(512), row(256), row(256), row(D_MODEL), full((1, 512)), full((1, 256)), full((1, 256)),
                  full((D_MODEL, D_MODEL)), full((1, D_MODEL)),
                  pl.BlockSpec((2, D_MODEL, ROUTER_PAD), lambda i: (0, 0, 0)), full((1, ROUTER_PAD))],
        out_specs=[row(D_MODEL), pl.BlockSpec(memory_space=pl.ANY),
                   row(ROUTER_PAD), row(ROUTER_PAD), full((1, ROUTER_PAD))],
        out_shape=[jax.ShapeDtypeStruct((t, D_MODEL), jnp.float32),
                   jax.ShapeDtypeStruct((n_rows * LANE_TILES, 128), jnp.float32),
                   jax.ShapeDtypeStruct((t, ROUTER_PAD), jnp.int32),
                   jax.ShapeDtypeStruct((t, ROUTER_PAD), jnp.float32),
                   jax.ShapeDtypeStruct((1, ROUTER_PAD), jnp.int32)],
        scratch_shapes=[pltpu.VMEM((1, ROUTER_PAD), jnp.float32),
                        pltpu.VMEM((ROW_BUFFERS, TOKEN_TILE * LANE_TILES, 128), jnp.float32),
                        pltpu.VMEM((ROW_BUFFERS, 8, TOKEN_TILE), jnp.int32),
                        pltpu.SMEM((ROW_BUFFERS * TILE_SLOTS,), jnp.int32),
                        pltpu.SemaphoreType.DMA((ROW_BUFFERS,)), pltpu.SemaphoreType.DMA((ROW_BUFFERS,))],
        compiler_params=pltpu.CompilerParams(dimension_semantics=("arbitrary",), vmem_limit_bytes=VMEM_LIMIT),
        name="out_proj",
    )(oa, ob, oc, x2, ga, gb, gc, w, gf, wr, br)


def _moe_ffn_kernel(blk_e_ref, blk_row_ref, blk_valid_ref, blk_first_ref, blk_next_ref, blk_slot_ref,
                    x_ref, wgu_hbm, bgu_ref, wd_hbm, bd_ref, y_ref, wgu_f32, wd_f32, wgu_bf, wd_bf, gsem, dsem,
                    *, layer):
    i = pl.program_id(0)
    n_valid = blk_valid_ref[i]

    def weight_copies(e, s):
        return (pltpu.make_async_copy(wgu_hbm.at[layer, e], wgu_f32.at[s], gsem.at[s]),
                pltpu.make_async_copy(wd_hbm.at[layer, e], wd_f32.at[s], dsem.at[s]))

    @pl.when(i == 0)
    def _():
        for copy in weight_copies(blk_e_ref[0], blk_slot_ref[0]):
            copy.start()

    @pl.when(blk_first_ref[i] == 1)
    def _():
        s = blk_slot_ref[i]
        for copy in weight_copies(blk_e_ref[i], s):
            copy.wait()
        nxt = blk_next_ref[i]

        @pl.when(nxt >= 0)
        def _():
            for copy in weight_copies(nxt, 1 - s):
                copy.start()

        for c in range(4):
            cols = slice(c * 512, (c + 1) * 512)
            wgu_bf[:, cols] = wgu_f32[s, :, cols].astype(jnp.bfloat16)
        for c in range(2):
            cols = slice(c * 512, (c + 1) * 512)
            wd_bf[:, cols] = wd_f32[s, :, cols].astype(jnp.bfloat16)

    @pl.when(n_valid > 0)
    def _():
        live = lax.broadcasted_iota(jnp.int32, (EXPERT_ROWS, 128), 0) < n_valid
        x = jnp.concatenate([jnp.where(live, _load_row_tiles(x_ref, 0, EXPERT_ROWS, c), 0.0).astype(jnp.bfloat16)
                             for c in range(LANE_TILES)], axis=1)
        glu = jnp.dot(x, wgu_bf[:, 0:D_EXPERT], preferred_element_type=jnp.float32) + bgu_ref[0, 0, :, 0:D_EXPERT]
        lin = jnp.dot(x, wgu_bf[:, D_EXPERT:], preferred_element_type=jnp.float32) + bgu_ref[0, 0, :, D_EXPERT:]
        glu = jnp.minimum(glu, SWIGLU_LIMIT)
        lin = jnp.clip(lin, -SWIGLU_LIMIT, SWIGLU_LIMIT)
        act = glu * jax.nn.sigmoid(SWIGLU_ALPHA * glu) * (lin + 1.0)
        y = jnp.dot(act.astype(jnp.bfloat16), wd_bf[...], preferred_element_type=jnp.float32) + bd_ref[0, 0]
        _store_row_tiles(y_ref, y)

    @pl.when(n_valid <= 0)
    def _():
        y_ref[...] = jnp.zeros_like(y_ref)


def _moe_ffn(layer, table, x_rows, w_gu, b_gu, w_down, b_down):
    n_blocks = table[0].shape[0]
    depth = w_gu.shape[0]
    rows = lambda i, be, br, *_: (br[i], 0)
    expert = lambda i, be, *_: (layer, be[i], 0, 0)
    grid_spec = pltpu.PrefetchScalarGridSpec(
        num_scalar_prefetch=len(table),
        grid=(n_blocks,),
        in_specs=[
            pl.BlockSpec((EXPERT_ROWS * LANE_TILES, 128), rows),
            pl.BlockSpec(memory_space=pl.ANY),
            pl.BlockSpec((1, 1, 1, 2 * D_EXPERT), expert),
            pl.BlockSpec(memory_space=pl.ANY),
            pl.BlockSpec((1, 1, 1, D_MODEL), expert),
        ],
        out_specs=pl.BlockSpec((EXPERT_ROWS * LANE_TILES, 128), rows),
        scratch_shapes=[pltpu.VMEM((2, D_MODEL, 2 * D_EXPERT), jnp.float32),
                        pltpu.VMEM((2, D_EXPERT, D_MODEL), jnp.float32),
                        pltpu.VMEM((D_MODEL, 2 * D_EXPERT), jnp.bfloat16),
                        pltpu.VMEM((D_EXPERT, D_MODEL), jnp.bfloat16),
                        pltpu.SemaphoreType.DMA((2,)), pltpu.SemaphoreType.DMA((2,))],
    )
    return pl.pallas_call(
        functools.partial(_moe_ffn_kernel, layer=layer),
        grid_spec=grid_spec,
        out_shape=jax.ShapeDtypeStruct(x_rows.shape, jnp.float32),
        compiler_params=pltpu.CompilerParams(dimension_semantics=("arbitrary",), vmem_limit_bytes=VMEM_LIMIT),
        name="moe_ffn",
    )(*table, x_rows, w_gu, b_gu.reshape(depth, N_EXPERTS, 1, -1), w_down, b_down.reshape(depth, N_EXPERTS, 1, -1))


def _combine_kernel(dest_hbm, y_hbm, x_ref, g_ref, *rest, project):
    if project:
        gain_ref, w_ref, hg_ref, xo_ref, o_ref, ybuf0, ybuf1, idx_smem, isem, gsem = rest
    else:
        xo_ref, ybuf0, ybuf1, idx_smem, isem, gsem = rest
    ybufs = (ybuf0, ybuf1)
    i = pl.program_id(0)
    nt = pl.num_programs(0)

    def idx_copy(tile, s):
        return pltpu.make_async_copy(dest_hbm.at[jnp.minimum(tile, nt - 1)], _index_slot(idx_smem, s), isem.at[s])

    def row_copy(s, r, k):
        src = idx_smem[s * TILE_SLOTS + r * TOP_K + k]
        return pltpu.make_async_copy(_row_tile(y_hbm, src), _row_tile(ybufs[s], k * TOKEN_TILE + r), gsem.at[s])

    def wait_gather(s):
        pltpu.make_async_copy(y_hbm.at[pl.ds(0, TILE_SLOTS * LANE_TILES)], ybufs[s], gsem.at[s]).wait()

    @pl.when(i == 0)
    def _():
        idx_copy(0, 0).start()
        idx_copy(0, 0).wait()

        def body(r, carry):
            for k in range(TOP_K):
                row_copy(0, r, k).start(priority=k % 2)
            return carry

        lax.fori_loop(0, TOKEN_TILE, body, 0, unroll=4)
        idx_copy(1, 1).start()

    def step(slot):
        wait_gather(slot)
        idx_copy(i + 1, 1 - slot).wait()
        for r in range(TOKEN_TILE):
            for k in range(TOP_K):
                row_copy(1 - slot, r, k).start(priority=k % 2)
        idx_copy(i + 2, slot).start()

        g = g_ref[...]
        gates = [jnp.broadcast_to(g[:, k:k + 1], (TOKEN_TILE, 128)) for k in range(TOP_K)]
        cols = []
        for c in range(LANE_TILES):
            acc = x_ref[:, c * 128:(c + 1) * 128]
            for k in range(TOP_K):
                acc = acc + gates[k] * _load_row_tiles(ybufs[slot], k * TOKEN_TILE, TOKEN_TILE, c)
            cols.append(acc)
        x = jnp.concatenate(cols, axis=1)
        xo_ref[...] = x
        if project:
            _project(_rms(x, gain_ref[...]).astype(jnp.bfloat16), w_ref, hg_ref, o_ref)

        @pl.when(i == nt - 1)
        def _():
            wait_gather(1 - slot)
            idx_copy(i + 2, slot).wait()

    for parity in range(2):
        pl.when(i % 2 == parity)(functools.partial(step, parity))


def _combine_call(dest_tiles, y_rows, x2, gates, gain=None, w=None, head_gain=None):
    t = x2.shape[0]
    project = w is not None
    row = lambda width: pl.BlockSpec((TOKEN_TILE, width), lambda i: (i, 0))
    full = lambda shape: pl.BlockSpec(shape, lambda i: (0, 0))
    in_specs = [pl.BlockSpec(memory_space=pl.ANY), pl.BlockSpec(memory_space=pl.ANY), row(D_MODEL), row(ROUTER_PAD)]
    out_specs = [row(D_MODEL)]
    out_shape = [jax.ShapeDtypeStruct((t, D_MODEL), jnp.float32)]
    args = [dest_tiles, y_rows, x2, gates]
    if project:
        in_specs += [full((1, D_MODEL)), full((D_MODEL, PROJ_WIDTH)), full((1, PROJ_WIDTH))]
        out_specs.append(row(PROJ_WIDTH))
        out_shape.append(jax.ShapeDtypeStruct((t, PROJ_WIDTH), jnp.bfloat16))
        args += [gain, w, head_gain]
    out = pl.pallas_call(
        functools.partial(_combine_kernel, project=project),
        grid=(t // TOKEN_TILE,),
        in_specs=in_specs, out_specs=out_specs, out_shape=out_shape,
        scratch_shapes=[pltpu.VMEM((TILE_SLOTS * LANE_TILES, 128), jnp.float32),
                        pltpu.VMEM((TILE_SLOTS * LANE_TILES, 128), jnp.float32),
                        pltpu.SMEM((2 * TILE_SLOTS,), jnp.int32),
                        pltpu.SemaphoreType.DMA((2,)), pltpu.SemaphoreType.DMA((2,))],
        compiler_params=pltpu.CompilerParams(dimension_semantics=("arbitrary",), vmem_limit_bytes=VMEM_LIMIT),
        name="combine_in_proj" if project else "combine",
    )(*args)
    return out if project else out[0]


def _t5_bucket(rel):
    half = T5_BUCKETS // 2
    max_exact = half // 2
    ret = np.where(rel > 0, half, 0)
    n = np.abs(rel)
    nf = np.maximum(n, 1).astype(np.float32)
    large = max_exact + (np.log(nf / max_exact) / math.log(T5_MAX_DIST / max_exact)
                         * (half - max_exact)).astype(np.int32)
    large = np.minimum(large, half - 1)
    return ret + np.where(n < max_exact, n, large)


def _band_ok(n_prev, band):
    q_chunk = np.arange(Q_TILE)[:, None] // CHUNK + n_prev
    k_chunk = np.arange(band)[None, :] // CHUNK
    return (k_chunk <= q_chunk) & (k_chunk >= q_chunk - n_prev)


def _toeplitz(u, band):
    heads, length = u.shape
    flat = jnp.broadcast_to(u[:, None, :], (heads, Q_TILE, length)).reshape(heads, Q_TILE * length)
    skew = flat[:, :Q_TILE * (length - 1)].reshape(heads, Q_TILE, length - 1)
    return skew[:, :, Q_TILE - 1:Q_TILE - 1 + band]


def _a_bias(t5_bias):
    rel = np.arange(Q_TILE + A_BAND) - (Q_TILE - 1) - A_PAD
    bias = _toeplitz(t5_bias[_t5_bucket(rel)].T.astype(jnp.float32), A_BAND)
    bias = jnp.where(_band_ok(A_PREV_CHUNKS, A_BAND)[None], bias, NEG_INF)
    pairs = [jnp.concatenate([bias[4 * hk + parity], bias[4 * hk + 2 + parity]], axis=0)
             for hk in range(A_KV_HEADS) for parity in range(2)]
    return jnp.stack(pairs).transpose(0, 2, 1)


def _c_bias(rel_table):
    dist = C_PAD + (Q_TILE - 1) - np.arange(Q_TILE + C_BAND)
    idx = np.clip(dist, -(CHUNK - 1), C_REL_CLIP) + (CHUNK - 1)
    bias = _toeplitz(rel_table[idx].T.astype(jnp.float32), C_BAND)
    return jnp.where(_band_ok(C_PREV_CHUNKS, C_BAND)[None], bias, NEG_INF).transpose(0, 2, 1)


def _rope_tables(seq):
    half = B_ROPE_DIM // 2
    inv_freq = ROPE_THETA ** (-(jnp.arange(half, dtype=jnp.float32) / half))
    ang = jnp.arange(seq, dtype=jnp.float32)[:, None] * inv_freq[None, :]
    cos, sin = jnp.cos(ang), jnp.sin(ang)
    zeros = jnp.zeros((seq, B_NOPE_DIM), jnp.float32)
    tail = jnp.zeros((seq, B_HEAD_PAD - B_QK_DIM), jnp.float32)
    zh = jnp.zeros((seq, half), jnp.float32)
    cos_t = jnp.concatenate([zeros + 1.0, cos, cos, tail + 1.0], axis=1)
    sa = jnp.concatenate([zeros, -sin, zh, tail], axis=1)
    sb = jnp.concatenate([zeros, zh, sin, tail], axis=1)
    return cos_t, sa, sb


def _regroup_w_in(w):
    splits = np.cumsum((512, 128, 128, 256, 128, 32, 256, 256))
    aq, ak, av, bcq, bckv, bkpe, cq, ck, cv = jnp.split(w, splits.tolist(), axis=1)
    z = lambda n: jnp.zeros((w.shape[0], n), w.dtype)
    return jnp.concatenate([aq, ak, av, bcq, bckv, z(64), bkpe, z(32), cq, ck, cv], axis=1).astype(jnp.bfloat16)


def _head_gain_row(a_q, a_k, c_q, c_k):
    scale = HEAD_DIM ** -0.5
    row = jnp.ones((PROJ_WIDTH,), jnp.float32)
    for sec, heads, gain in ((SEC_AQ, A_Q_HEADS, a_q * scale), (SEC_AKV, A_KV_HEADS, a_k),
                             (SEC_CQ, C_HEADS, c_q * scale), (SEC_CK, C_HEADS, c_k)):
        row = row.at[sec:sec + heads * HEAD_DIM].set(jnp.tile(gain, heads))
    return row.reshape(1, PROJ_WIDTH)


def _pad_heads(w, head_dim, keep):
    rank = w.shape[0]
    w = w.reshape(rank, B_HEADS, head_dim)[:, :, :keep]
    w = jnp.pad(w, ((0, 0), (0, 0), (0, B_HEAD_PAD - keep)))
    return w.reshape(rank, B_HEADS * B_HEAD_PAD).astype(jnp.bfloat16)


def _pad_lanes(g, offset, width):
    return jnp.pad(g, (offset, width - offset - g.shape[0])).reshape(1, width)


def _block_table(counts, t):
    tk = t * TOP_K
    blocks_per_expert = t // EXPERT_ROWS
    nblk = (counts + EXPERT_ROWS - 1) // EXPERT_ROWS
    blk_end = jnp.cumsum(nblk)
    blk_first = blk_end - nblk
    n_blocks = (tk + N_EXPERTS * (EXPERT_ROWS - 1) + EXPERT_ROWS - 1) // EXPERT_ROWS
    blk = jnp.arange(n_blocks, dtype=jnp.int32)
    blk_e = jnp.minimum(jnp.sum(blk[:, None] >= blk_end[None, :], axis=1), N_EXPERTS - 1).astype(jnp.int32)
    own = blk_e[:, None] == jnp.arange(N_EXPERTS, dtype=jnp.int32)[None, :]
    pick = lambda v: jnp.sum(jnp.where(own, v[None, :], 0), axis=1)
    j = blk - pick(blk_first)
    blk_valid = jnp.clip(pick(counts) - j * EXPERT_ROWS, 0, EXPERT_ROWS).astype(jnp.int32)
    spare = N_EXPERTS * blocks_per_expert
    blk_row = jnp.where(blk_valid > 0, blk_e * blocks_per_expert + j, spare).astype(jnp.int32)
    experts = jnp.arange(N_EXPERTS, dtype=jnp.int32)
    used = counts > 0
    later = jnp.logical_and(experts[None, :] > experts[:, None], used[None, :])
    nxt = jnp.min(jnp.where(later, experts[None, :], N_EXPERTS), axis=1)
    nxt = jnp.where(nxt < N_EXPERTS, nxt, -1)
    ordinal = jnp.cumsum(used.astype(jnp.int32)) - 1
    blk_first = jnp.logical_and(j == 0, blk_valid > 0).astype(jnp.int32)
    blk_next = pick(nxt).astype(jnp.int32)
    blk_slot = (pick(ordinal) % 2).astype(jnp.int32)
    return blk_e, blk_row, blk_valid, blk_first, blk_next, blk_slot


def kernel(x, attn_norm, w_in, a_q_norm, a_k_norm, a_sinks, t5_bias, b_q_a_norm, b_w_q_b, b_kv_a_norm, b_w_kv_b,
           b_q_norm, b_k_norm, c_q_norm, c_k_norm, c_rel_bias, a_out_norm, b_out_norm, c_out_norm, w_out, ffn_norm,
           w_router, b_router, w_gate_up, b_gate_up, w_down, b_down):
    b, s, d = x.shape
    t = b * s
    depth = w_in.shape[0]
    a_bias = _a_bias(t5_bias)
    cos_t, sin_a, sin_b = _rope_tables(s)
    x2 = x.reshape(t, d)
    head_gain = lambda l: _head_gain_row(a_q_norm[l], a_k_norm[l], c_q_norm[l], c_k_norm[l])
    proj = _in_proj(x2, attn_norm[0].reshape(1, d), _regroup_w_in(w_in[0]), head_gain(0))
    for l in range(depth):
        proj3 = proj.reshape(b, s, PROJ_WIDTH)
        o_a = _attn_a(proj3, a_bias, a_sinks[l])
        kv_w = b_w_kv_b[l].reshape(B_KV_RANK, B_HEADS, B_NOPE_DIM + B_V_DIM)
        wv = kv_w[:, :, B_NOPE_DIM:].reshape(B_KV_RANK, B_HEADS * B_V_DIM).astype(jnp.bfloat16)
        qb, kb, vb = _latent_prep(
            proj, b_q_a_norm[l].reshape(1, -1), b_kv_a_norm[l].reshape(1, -1),
            _pad_heads(b_w_q_b[l], B_QK_DIM, B_QK_DIM), _pad_heads(b_w_kv_b[l], B_NOPE_DIM + B_V_DIM, B_NOPE_DIM), wv,
            _pad_lanes(b_q_norm[l], 0, B_HEAD_PAD), _pad_lanes(b_k_norm[l], 0, B_HEAD_PAD), cos_t, sin_a, sin_b, s)
        o_b = _attn_b(qb.reshape(b, s, -1), kb.reshape(b, s, -1), vb.reshape(b, s, -1))
        o_c = _attn_c(proj3, _c_bias(c_rel_bias[l]))
        wr = jnp.pad(w_router[l], ((0, 0), (0, ROUTER_PAD - N_EXPERTS)))
        wr_hi = wr.astype(jnp.bfloat16)
        wr = jnp.stack([wr_hi, (wr - wr_hi.astype(jnp.float32)).astype(jnp.bfloat16)])
        br = jnp.pad(b_router[l], (0, ROUTER_PAD - N_EXPERTS)).reshape(1, ROUTER_PAD)
        n_rows = N_EXPERTS * t + SPARE_ROWS
        x2, x_rows, dest, gates, counts = _out_proj(
            o_a.reshape(t, -1), o_b.reshape(t, -1), o_c.reshape(t, -1), x2,
            a_out_norm[l].reshape(1, -1), b_out_norm[l].reshape(1, -1), c_out_norm[l].reshape(1, -1),
            w_out[l].astype(jnp.bfloat16), ffn_norm[l].reshape(1, -1), wr, br, n_rows)
        dest_tiles = dest[:, :TOP_K].reshape(t // TOKEN_TILE, TOKEN_TILE * TOP_K)
        table = _block_table(counts[0, :N_EXPERTS], t)
        y_rows = _moe_ffn(l, table, x_rows, w_gate_up, b_gate_up, w_down, b_down)
        if l + 1 < depth:
            x2, proj = _combine_call(dest_tiles, y_rows, x2, gates, attn_norm[l + 1].reshape(1, d),
                                     _regroup_w_in(w_in[l + 1]), head_gain(l + 1))
        else:
            x2 = _combine_call(dest_tiles, y_rows, x2, gates)
    return x2.reshape(b, s, d)
```

```python
import functools
import math

import jax
import jax.numpy as jnp
import numpy as np
from jax import lax
from jax.experimental import pallas as pl
from jax.experimental.pallas import tpu as pltpu

D_MODEL = 1024
CHUNK = 64
HEAD_DIM = 64
EPS = 1e-6
NEG_INF = -1e30
A_Q_HEADS = 8
A_KV_HEADS = 2
A_GROUP = 4
A_PREV_CHUNKS = 2
T5_BUCKETS = 32
T5_MAX_DIST = 128
B_HEADS = 4
B_Q_RANK = 256
B_KV_RANK = 128
B_NOPE_DIM = 64
B_ROPE_DIM = 32
B_QK_DIM = 96
B_V_DIM = 64
ROPE_THETA = 10000.0
C_HEADS = 4
C_PREV_CHUNKS = 8
C_REL_CLIP = 256
N_EXPERTS = 32
TOP_K = 4
D_EXPERT = 1024
SWIGLU_LIMIT = 7.0
SWIGLU_ALPHA = 1.702

SEC_AQ = 0
SEC_AKV = 512
SEC_BCQ = 768
SEC_BCKV = 1024
SEC_BKPE = 1152
SEC_CQ = 1280
SEC_CK = 1536
SEC_CV = 1792
PROJ_WIDTH = 2048
HEAD_NORM_SLABS = tuple(sec // 128 + j for sec, n in ((SEC_AQ, 4), (SEC_AKV, 1), (SEC_CQ, 2), (SEC_CK, 2))
                        for j in range(n))
B_HEAD_PAD = 128

Q_TILE = 128
A_TILES_PER_STEP = 1
C_TILES_PER_STEP = 4
A_STEP_ROWS = Q_TILE * A_TILES_PER_STEP
C_STEP_ROWS = Q_TILE * C_TILES_PER_STEP
A_BAND = Q_TILE + A_PREV_CHUNKS * CHUNK
C_BAND = Q_TILE + C_PREV_CHUNKS * CHUNK
A_PAD = A_PREV_CHUNKS * CHUNK
C_PAD = C_PREV_CHUNKS * CHUNK
B_TILE = 512
ROW_TILE = 512
EXPERT_ROWS = 256
ROUTER_PAD = 128
TOKEN_TILE = 256
TILE_SLOTS = TOKEN_TILE * TOP_K
ROW_BUFFERS = 3
SPARE_ROWS = TILE_SLOTS
LANE_TILES = D_MODEL // 128
VMEM_LIMIT = 48 * 1024 * 1024


def _rms(x, gain):
    return x * lax.rsqrt(jnp.mean(x * x, axis=-1, keepdims=True) + EPS) * gain


def _nt_dot(a, b):
    return lax.dot_general(a, b, (((1,), (1,)), ((), ())), preferred_element_type=jnp.float32)


def _load_row_tiles(ref, first_row, n_rows, c):
    return ref[pl.ds(first_row * LANE_TILES + c, n_rows, stride=LANE_TILES), :]


def _store_row_tiles(ref, value):
    for c in range(LANE_TILES):
        ref[pl.ds(c, value.shape[0], stride=LANE_TILES), :] = value[:, c * 128:(c + 1) * 128]


def _row_tile(ref, row):
    return ref.at[pl.ds(pl.multiple_of(row * LANE_TILES, LANE_TILES), LANE_TILES)]


def _index_slot(idx_smem, s):
    return idx_smem.at[pl.ds(pl.multiple_of(s * TILE_SLOTS, TILE_SLOTS), TILE_SLOTS)]


def _project(h, w_ref, hg_ref, o_ref):
    li = lax.broadcasted_iota(jnp.int32, (128, 128), 0) // HEAD_DIM
    lj = lax.broadcasted_iota(jnp.int32, (128, 128), 1) // HEAD_DIM
    same_head = jnp.where(li == lj, 1.0, 0.0).astype(jnp.bfloat16)
    for n in range(PROJ_WIDTH // 512):
        acc = jnp.dot(h, w_ref[:, n * 512:(n + 1) * 512], preferred_element_type=jnp.float32)
        slabs = []
        for j in range(4):
            slab = n * 4 + j
            a = acc[:, j * 128:(j + 1) * 128]
            if slab in HEAD_NORM_SLABS:
                sq = a * a
                hi = sq.astype(jnp.bfloat16)
                lo = (sq - hi.astype(jnp.float32)).astype(jnp.bfloat16)
                ss = (jnp.dot(hi, same_head, preferred_element_type=jnp.float32)
                      + jnp.dot(lo, same_head, preferred_element_type=jnp.float32))
                a = a * lax.rsqrt(ss * (1.0 / HEAD_DIM) + EPS) * hg_ref[:, slab * 128:(slab + 1) * 128]
            slabs.append(a.astype(o_ref.dtype))
        o_ref[:, n * 512:(n + 1) * 512] = jnp.concatenate(slabs, axis=1)


def _in_proj_kernel(x_ref, g_ref, w_ref, hg_ref, o_ref):
    _project(_rms(x_ref[...], g_ref[...]).astype(jnp.bfloat16), w_ref, hg_ref, o_ref)


def _in_proj(x2, gain, w, head_gain):
    t = x2.shape[0]
    return pl.pallas_call(
        _in_proj_kernel,
        grid=(t // ROW_TILE,),
        in_specs=[
            pl.BlockSpec((ROW_TILE, D_MODEL), lambda i: (i, 0)),
            pl.BlockSpec((1, D_MODEL), lambda i: (0, 0)),
            pl.BlockSpec((D_MODEL, PROJ_WIDTH), lambda i: (0, 0)),
            pl.BlockSpec((1, PROJ_WIDTH), lambda i: (0, 0)),
        ],
        out_specs=pl.BlockSpec((ROW_TILE, PROJ_WIDTH), lambda i: (i, 0)),
        out_shape=jax.ShapeDtypeStruct((t, PROJ_WIDTH), jnp.bfloat16),
        compiler_params=pltpu.CompilerParams(dimension_semantics=("arbitrary",), vmem_limit_bytes=VMEM_LIMIT),
        name="in_proj",
    )(x2, gain, w, head_gain)


def _half_lanes(x, parity, fill=0.0):
    lane = lax.broadcasted_iota(jnp.int32, x.shape, x.ndim - 1) % 128
    keep = lane < HEAD_DIM if parity == 0 else lane >= HEAD_DIM
    return jnp.where(keep, x, jnp.full_like(x, fill))


def _tn_dot(a, b):
    return lax.dot_general(a, b, (((0,), (0,)), ((), ())), preferred_element_type=jnp.float32)


def _softmax_pv(s, key_ok, v_aug, sink=None, value_half=0):
    if key_ok is not None:
        s = jnp.where(key_ok, s, NEG_INF)
    m = jnp.max(s, axis=0, keepdims=True)
    if sink is not None:
        m = jnp.maximum(m, sink)
    e = jnp.exp(s - m).astype(jnp.bfloat16)
    o = _tn_dot(e, v_aug)
    if sink is not None:
        row = lax.broadcasted_iota(jnp.int32, (8, s.shape[1]), 0)
        e_sink = jnp.where(row == 0, jnp.exp(sink - m), 0.0).astype(jnp.bfloat16)
        first = lax.broadcasted_iota(jnp.int32, (8, 128), 0) == 0
        sel = _half_lanes(jnp.where(first, 1.0, 0.0), 1 - value_half).astype(jnp.bfloat16)
        o = o + _tn_dot(e_sink, sel)
    return o


def _merge_halves(halves):
    lane = lax.broadcasted_iota(jnp.int32, halves[0].shape, 1)
    normed = [o / pltpu.roll(o, HEAD_DIM, 1) for o in halves]
    return jnp.where(lane < HEAD_DIM, normed[0], normed[1])


def _attn_a_kernel(q_ref, kv_ref, bias_ref, sink_ref, o_ref, k_pad, v_pad):
    i = pl.program_id(1)
    seq = kv_ref.shape[1]

    @pl.when(i == 0)
    def _():
        for n in range(2 * A_KV_HEADS):
            k_pad[n, 0:A_PAD, :] = jnp.zeros((A_PAD, 128), k_pad.dtype)
            v_pad[n, 0:A_PAD, :] = jnp.zeros((A_PAD, 128), v_pad.dtype)

        def fill(r, carry):
            rows = pl.ds(pl.multiple_of(r * 256, 256), 256)
            dst = pl.ds(pl.multiple_of(r * 256 + A_PAD, 128), 256)
            kv = kv_ref[0, rows, :].astype(jnp.float32)
            for src, pad, other in ((kv[:, 0:128], k_pad, 0.0), (kv[:, 128:256], v_pad, 1.0)):
                swapped = pltpu.roll(src, HEAD_DIM, 1)
                pad[0, dst, :] = _half_lanes(src, 0, other).astype(pad.dtype)
                pad[1, dst, :] = _half_lanes(swapped, 1, other).astype(pad.dtype)
                pad[2, dst, :] = _half_lanes(swapped, 0, other).astype(pad.dtype)
                pad[3, dst, :] = _half_lanes(src, 1, other).astype(pad.dtype)
            return carry

        lax.fori_loop(0, seq // 256, fill, 0)

    first_slab = lax.broadcasted_iota(jnp.int32, (1, 2 * Q_TILE), 1) < Q_TILE

    def tiles(front):
        for sub in range(A_TILES_PER_STEP):
            rows = slice(sub * Q_TILE, (sub + 1) * Q_TILE)
            start = pl.multiple_of((i * A_TILES_PER_STEP + sub) * Q_TILE, Q_TILE)
            band = pl.ds(start, A_BAND)
            key_ok = None
            if front:
                key_ok = lax.broadcasted_iota(jnp.int32, (A_BAND, 2 * Q_TILE), 0) + (start - A_PAD) >= 0
            for hk in range(A_KV_HEADS):
                q2 = jnp.concatenate([q_ref[0, rows, (2 * hk) * 128:(2 * hk + 1) * 128],
                                      q_ref[0, rows, (2 * hk + 1) * 128:(2 * hk + 2) * 128]], axis=0)
                halves = []
                for parity in range(2):
                    sink = jnp.where(first_slab, sink_ref[4 * hk + parity], sink_ref[4 * hk + 2 + parity])
                    s = _nt_dot(k_pad[2 * hk + parity, band, :], q2) + bias_ref[2 * hk + parity]
                    halves.append(_softmax_pv(s, key_ok, v_pad[2 * hk + parity, band, :], sink, parity))
                out = _merge_halves(halves)
                o_ref[0, rows, (2 * hk) * 128:(2 * hk + 1) * 128] = out[0:Q_TILE].astype(o_ref.dtype)
                o_ref[0, rows, (2 * hk + 1) * 128:(2 * hk + 2) * 128] = out[Q_TILE:2 * Q_TILE].astype(o_ref.dtype)

    front_steps = -(-A_PAD // A_STEP_ROWS)
    pl.when(i < front_steps)(functools.partial(tiles, True))
    pl.when(i >= front_steps)(functools.partial(tiles, False))


def _attn_a(proj3, bias, sinks):
    b, s, _ = proj3.shape
    return pl.pallas_call(
        _attn_a_kernel,
        grid=(b, s // A_STEP_ROWS),
        in_specs=[
            pl.BlockSpec((1, A_STEP_ROWS, 512), lambda bi, i: (bi, i, SEC_AQ // 512)),
            pl.BlockSpec((1, s, 256), lambda bi, i: (bi, 0, SEC_AKV // 256)),
            pl.BlockSpec((2 * A_KV_HEADS, 2 * Q_TILE, A_BAND), lambda bi, i: (0, 0, 0)),
            pl.BlockSpec(memory_space=pltpu.SMEM),
        ],
        out_specs=pl.BlockSpec((1, A_STEP_ROWS, 512), lambda bi, i: (bi, i, 0)),
        out_shape=jax.ShapeDtypeStruct((b, s, 512), jnp.bfloat16),
        scratch_shapes=[pltpu.VMEM((2 * A_KV_HEADS, s + A_PAD, 128), jnp.bfloat16),
                        pltpu.VMEM((2 * A_KV_HEADS, s + A_PAD, 128), jnp.bfloat16)],
        compiler_params=pltpu.CompilerParams(dimension_semantics=("arbitrary", "arbitrary"),
                                             vmem_limit_bytes=VMEM_LIMIT),
        name="attn_a",
    )(proj3, proj3, bias, sinks)


def _latent_prep_kernel(cq_ref, ckv_ref, kpe_ref, qag_ref, kvag_ref, wq_ref, wk_ref, wv_ref,
                        qg_ref, kg_ref, cos_ref, sa_ref, sb_ref, qo_ref, ko_ref, vo_ref):
    lane = lax.broadcasted_iota(jnp.int32, (ROW_TILE, B_HEAD_PAD), 1)
    is_nope = lane < B_NOPE_DIM
    is_rope = jnp.logical_and(lane >= B_NOPE_DIM, lane < B_QK_DIM)
    cos = cos_ref[...]
    sa = sa_ref[...]
    sb = sb_ref[...]

    def split_norm(x, gain):
        sq = x * x
        nope = jnp.sum(jnp.where(is_nope, sq, 0.0), axis=-1, keepdims=True) * (1.0 / B_NOPE_DIM)
        rope = jnp.sum(jnp.where(is_rope, sq, 0.0), axis=-1, keepdims=True) * (1.0 / B_ROPE_DIM)
        r = jnp.where(is_nope, lax.rsqrt(nope + EPS), lax.rsqrt(rope + EPS))
        return x * r * gain

    def rotate(x):
        return x * cos + pltpu.roll(x, 112, 1) * sa + pltpu.roll(x, 16, 1) * sb

    cq = _rms(cq_ref[...].astype(jnp.float32), qag_ref[...]).astype(jnp.bfloat16)
    ckv = _rms(ckv_ref[...].astype(jnp.float32), kvag_ref[...]).astype(jnp.bfloat16)
    kpe = kpe_ref[...].astype(jnp.float32)
    kpe_n = rotate(split_norm(kpe, kg_ref[...]))
    kpe_n = jnp.where(is_rope, kpe_n, 0.0)
    v = jnp.dot(ckv, wv_ref[...], preferred_element_type=jnp.float32)
    for h in range(B_HEADS):
        pair = v[:, (h // 2) * 128:(h // 2 + 1) * 128]
        vo_ref[:, h * 128:(h + 1) * 128] = _half_lanes(pair, h % 2, 1.0).astype(vo_ref.dtype)
    qscale = B_QK_DIM ** -0.5
    for h in range(B_HEADS):
        cols = slice(h * B_HEAD_PAD, (h + 1) * B_HEAD_PAD)
        qh = jnp.dot(cq, wq_ref[:, cols], preferred_element_type=jnp.float32)
        qh = rotate(split_norm(qh, qg_ref[...])) * qscale
        qo_ref[:, cols] = qh.astype(qo_ref.dtype)
        kh = jnp.dot(ckv, wk_ref[:, cols], preferred_element_type=jnp.float32)
        kh = jnp.where(is_nope, split_norm(kh, kg_ref[...]), 0.0) + kpe_n
        ko_ref[:, cols] = kh.astype(ko_ref.dtype)


def _latent_prep(proj, q_a_gain, kv_a_gain, wq, wk, wv, q_gain, k_gain, cos, sa, sb, seq):
    t = proj.shape[0]
    pos_blocks = seq // ROW_TILE
    full = lambda shape: pl.BlockSpec(shape, lambda i: (0, 0))
    tab = pl.BlockSpec((ROW_TILE, B_HEAD_PAD), lambda i: (i % pos_blocks, 0))
    return pl.pallas_call(
        _latent_prep_kernel,
        grid=(t // ROW_TILE,),
        in_specs=[
            pl.BlockSpec((ROW_TILE, 256), lambda i: (i, SEC_BCQ // 256)),
            pl.BlockSpec((ROW_TILE, 128), lambda i: (i, SEC_BCKV // 128)),
            pl.BlockSpec((ROW_TILE, 128), lambda i: (i, SEC_BKPE // 128)),
            full((1, B_Q_RANK)), full((1, B_KV_RANK)),
            full((B_Q_RANK, B_HEADS * B_HEAD_PAD)), full((B_KV_RANK, B_HEADS * B_HEAD_PAD)),
            full((B_KV_RANK, B_HEADS * B_V_DIM)),
            full((1, B_HEAD_PAD)), full((1, B_HEAD_PAD)),
            tab, tab, tab,
        ],
        out_specs=[
            pl.BlockSpec((ROW_TILE, B_HEADS * B_HEAD_PAD), lambda i: (i, 0)),
            pl.BlockSpec((ROW_TILE, B_HEADS * B_HEAD_PAD), lambda i: (i, 0)),
            pl.BlockSpec((ROW_TILE, B_HEADS * B_HEAD_PAD), lambda i: (i, 0)),
        ],
        out_shape=[
            jax.ShapeDtypeStruct((t, B_HEADS * B_HEAD_PAD), jnp.bfloat16),
            jax.ShapeDtypeStruct((t, B_HEADS * B_HEAD_PAD), jnp.bfloat16),
            jax.ShapeDtypeStruct((t, B_HEADS * B_HEAD_PAD), jnp.bfloat16),
        ],
        compiler_params=pltpu.CompilerParams(dimension_semantics=("arbitrary",), vmem_limit_bytes=VMEM_LIMIT),
        name="latent_prep",
    )(proj, proj, proj, q_a_gain, kv_a_gain, wq, wk, wv, q_gain, k_gain, cos, sa, sb)


def _attn_b_kernel(q_ref, k_ref, v_ref, o_ref):
    i = pl.program_id(1)
    query_chunk = lax.broadcasted_iota(jnp.int32, (B_TILE, B_TILE), 0) // CHUNK
    key_chunk = lax.broadcasted_iota(jnp.int32, (B_TILE, B_TILE), 1) // CHUNK
    diag_ok = key_chunk <= query_chunk

    def step(j, carry, masked):
        rows = pl.ds(pl.multiple_of(j * B_TILE, B_TILE), B_TILE)
        out = []
        for h in range(B_HEADS):
            m, acc = carry[h]
            head = slice(h * B_HEAD_PAD, (h + 1) * B_HEAD_PAD)
            s = _nt_dot(q_ref[0, :, head], k_ref[0, rows, head])
            if masked:
                s = jnp.where(diag_ok, s, NEG_INF)
            m_new = jnp.maximum(m, jnp.max(s, axis=-1, keepdims=True))
            e = jnp.exp(s - m_new).astype(jnp.bfloat16)
            acc = jnp.exp(m - m_new) * acc + jnp.dot(e, v_ref[0, rows, head], preferred_element_type=jnp.float32)
            out.append((m_new, acc))
        return tuple(out)

    init = tuple((jnp.full((B_TILE, 1), NEG_INF, jnp.float32), jnp.zeros((B_TILE, B_HEAD_PAD), jnp.float32))
                 for _ in range(B_HEADS))
    carry = lax.fori_loop(0, i, lambda j, c: step(j, c, False), init)
    carry = step(i, carry, True)
    for pair in range(B_HEADS // 2):
        halves = [carry[2 * pair + parity][1] for parity in range(2)]
        o_ref[0, :, pair * 128:(pair + 1) * 128] = _merge_halves(halves).astype(o_ref.dtype)


def _attn_b(qb, kb, vb):
    b, s, _ = qb.shape
    return pl.pallas_call(
        _attn_b_kernel,
        grid=(b, s // B_TILE),
        in_specs=[
            pl.BlockSpec((1, B_TILE, B_HEADS * B_HEAD_PAD), lambda bi, i: (bi, i, 0)),
            pl.BlockSpec((1, s, B_HEADS * B_HEAD_PAD), lambda bi, i: (bi, 0, 0)),
            pl.BlockSpec((1, s, B_HEADS * B_HEAD_PAD), lambda bi, i: (bi, 0, 0)),
        ],
        out_specs=pl.BlockSpec((1, B_TILE, B_HEADS * B_V_DIM), lambda bi, i: (bi, i, 0)),
        out_shape=jax.ShapeDtypeStruct((b, s, B_HEADS * B_V_DIM), jnp.bfloat16),
        compiler_params=pltpu.CompilerParams(dimension_semantics=("arbitrary", "arbitrary"),
                                             vmem_limit_bytes=VMEM_LIMIT),
        name="attn_b",
    )(qb, kb, vb)


def _attn_c_kernel(q_ref, k_ref, v_ref, bias_ref, o_ref, k_pad, v_pad):
    i = pl.program_id(1)
    seq = k_ref.shape[1]
    width = C_HEADS * HEAD_DIM

    @pl.when(i == 0)
    def _():
        for parity in range(2):
            k_pad[parity, 0:C_PAD, :] = jnp.zeros((C_PAD, width), k_pad.dtype)
            v_pad[parity, 0:C_PAD, :] = jnp.zeros((C_PAD, width), v_pad.dtype)

        def fill(r, carry):
            rows = pl.ds(pl.multiple_of(r * 256, 256), 256)
            dst = pl.ds(pl.multiple_of(r * 256 + C_PAD, 256), 256)
            k = k_ref[0, rows, :]
            v = v_ref[0, rows, :]
            for parity in range(2):
                k_pad[parity, dst, :] = _half_lanes(k, parity)
                v_pad[parity, dst, :] = _half_lanes(v, parity, fill=1.0)
            return carry

        lax.fori_loop(0, seq // 256, fill, 0)

    def tiles(front):
        for sub in range(C_TILES_PER_STEP):
            rows = slice(sub * Q_TILE, (sub + 1) * Q_TILE)
            start = pl.multiple_of((i * C_TILES_PER_STEP + sub) * Q_TILE, Q_TILE)
            band = pl.ds(start, C_BAND)
            key_ok = None
            if front:
                key_ok = lax.broadcasted_iota(jnp.int32, (C_BAND, Q_TILE), 0) + (start - C_PAD) >= 0
            for slab in range(width // 128):
                lanes = slice(slab * 128, (slab + 1) * 128)
                q = q_ref[0, rows, lanes]
                halves = []
                for parity in range(2):
                    s = _nt_dot(k_pad[parity, band, lanes], q) + bias_ref[2 * slab + parity]
                    halves.append(_softmax_pv(s, key_ok, v_pad[parity, band, lanes]))
                o_ref[0, rows, lanes] = _merge_halves(halves).astype(o_ref.dtype)

    front_steps = -(-C_PAD // C_STEP_ROWS)
    pl.when(i < front_steps)(functools.partial(tiles, True))
    pl.when(i >= front_steps)(functools.partial(tiles, False))


def _attn_c(proj3, bias):
    b, s, _ = proj3.shape
    width = C_HEADS * HEAD_DIM
    return pl.pallas_call(
        _attn_c_kernel,
        grid=(b, s // C_STEP_ROWS),
        in_specs=[
            pl.BlockSpec((1, C_STEP_ROWS, width), lambda bi, i: (bi, i, SEC_CQ // width)),
            pl.BlockSpec((1, s, width), lambda bi, i: (bi, 0, SEC_CK // width)),
            pl.BlockSpec((1, s, width), lambda bi, i: (bi, 0, SEC_CV // width)),
            pl.BlockSpec((C_HEADS, C_BAND, Q_TILE), lambda bi, i: (0, 0, 0)),
        ],
        out_specs=pl.BlockSpec((1, C_STEP_ROWS, width), lambda bi, i: (bi, i, 0)),
        out_shape=jax.ShapeDtypeStruct((b, s, width), jnp.bfloat16),
        scratch_shapes=[pltpu.VMEM((2, s + C_PAD, width), jnp.bfloat16),
                        pltpu.VMEM((2, s + C_PAD, width), jnp.bfloat16)],
        compiler_params=pltpu.CompilerParams(dimension_semantics=("arbitrary", "arbitrary"),
                                             vmem_limit_bytes=VMEM_LIMIT),
        name="attn_c",
    )(proj3, proj3, proj3, bias)


def _out_proj_kernel(oa_ref, ob_ref, oc_ref, x_ref, ga_ref, gb_ref, gc_ref, w_ref, gf_ref, wr_ref, br_ref,
                     xo_ref, x_hbm, dest_ref, tg_ref, cnt_ref, count, hbuf, dvm, idx_smem, isem, dsem):
    i = pl.program_id(0)
    nt = pl.num_programs(0)
    cap = nt * TOKEN_TILE
    slot = i % ROW_BUFFERS
    prev = (i + ROW_BUFFERS - 1) % ROW_BUFFERS
    spare_row = x_hbm.shape[0] // LANE_TILES - TILE_SLOTS

    def index_copies(s):
        return [pltpu.make_async_copy(
            dvm.at[s, k], idx_smem.at[pl.ds(pl.multiple_of(s * TILE_SLOTS + k * TOKEN_TILE, TOKEN_TILE), TOKEN_TILE)],
            isem.at[s]) for k in range(TOP_K)]

    def row_copy(s, base, r, k):
        dst = idx_smem[base + k * TOKEN_TILE + r]
        return pltpu.make_async_copy(_row_tile(hbuf.at[s], r), _row_tile(x_hbm, dst), dsem.at[s])

    def wait_rows(s):
        for _ in range(TOP_K):
            pltpu.make_async_copy(hbuf.at[s], x_hbm.at[pl.ds(0, TOKEN_TILE * LANE_TILES)], dsem.at[s]).wait()

    @pl.when(i == 0)
    def _():
        count[...] = jnp.zeros_like(count)
        hbuf[ROW_BUFFERS - 1] = jnp.zeros(hbuf.shape[1:], hbuf.dtype)

        def fill(j, carry):
            idx_smem[(ROW_BUFFERS - 1) * TILE_SLOTS + j] = spare_row + j
            return carry

        lax.fori_loop(0, TILE_SLOTS, fill, 0)

    @pl.when(i >= ROW_BUFFERS - 1)
    def _():
        wait_rows(slot)

    @pl.when(i >= 1)
    def _():
        for copy in index_copies(prev):
            copy.wait()

    prev_base = prev * TILE_SLOTS
    for r in range(TOKEN_TILE):
        for k in range(TOP_K):
            row_copy(prev, prev_base, r, k).start(priority=k % 2)

    acc = x_ref[...]
    parts = ((oa_ref, ga_ref, 0, 512), (ob_ref, gb_ref, 512, 256), (oc_ref, gc_ref, 768, 256))
    for ref, g_ref, off, width in parts:
        o = _rms(ref[...].astype(jnp.float32), g_ref[...]).astype(jnp.bfloat16)
        acc = acc + jnp.dot(o, w_ref[off:off + width, :], preferred_element_type=jnp.float32)
    xo_ref[...] = acc
    h = _rms(acc, gf_ref[...])
    _store_row_tiles(hbuf.at[slot], h)
    h_hi = h.astype(jnp.bfloat16)
    h_lo = (h - h_hi.astype(jnp.float32)).astype(jnp.bfloat16)
    logits = (jnp.dot(h_hi, wr_ref[0], preferred_element_type=jnp.float32)
              + jnp.dot(h_lo, wr_ref[0], preferred_element_type=jnp.float32)
              + jnp.dot(h_hi, wr_ref[1], preferred_element_type=jnp.float32)) + br_ref[...]
    lane = lax.broadcasted_iota(jnp.int32, logits.shape, 1)
    work = jnp.where(lane < N_EXPERTS, logits, -jnp.inf)
    gate_out = jnp.zeros(logits.shape, jnp.float32)
    denom = jnp.zeros((logits.shape[0], 1), jnp.float32)
    top = None
    picks = []
    for k in range(TOP_K):
        m = jnp.max(work, axis=-1, keepdims=True)
        idx = jnp.min(jnp.where(work == m, lane, ROUTER_PAD), axis=-1, keepdims=True)
        pick = lane == idx
        work = jnp.where(pick, -jnp.inf, work)
        top = m if top is None else top
        e = jnp.exp(m - top)
        denom = denom + e
        gate_out = jnp.where(lane == k, e, gate_out)
        picks.append((idx, pick))
    tg_ref[...] = gate_out / denom
    chosen = jnp.zeros(logits.shape, jnp.float32)
    for _, pick in picks:
        chosen = jnp.where(pick, 1.0, chosen)
    r_i = lax.broadcasted_iota(jnp.int32, (TOKEN_TILE, TOKEN_TILE), 0)
    c_i = lax.broadcasted_iota(jnp.int32, (TOKEN_TILE, TOKEN_TILE), 1)
    earlier = jnp.where(c_i < r_i, 1.0, 0.0).astype(jnp.bfloat16)
    before = jnp.dot(earlier, chosen.astype(jnp.bfloat16), preferred_element_type=jnp.float32) + count[...]
    dest_out = jnp.zeros(logits.shape, jnp.int32)
    for k, (idx, pick) in enumerate(picks):
        rank = jnp.sum(jnp.where(pick, before, 0.0), axis=-1, keepdims=True).astype(jnp.int32)
        dest_out = jnp.where(lane == k, idx * cap + rank, dest_out)
    dest_ref[...] = dest_out
    count[...] = count[...] + jnp.sum(chosen, axis=0, keepdims=True)
    cnt_ref[...] = count[...].astype(jnp.int32)
    dvm[slot] = dest_out.T[0:8, :]
    for copy in index_copies(slot):
        copy.start()

    @pl.when(i == nt - 1)
    def _():
        for copy in index_copies(slot):
            copy.wait()
        base = slot * TILE_SLOTS

        def body(r, carry):
            for k in range(TOP_K):
                row_copy(slot, base, r, k).start(priority=k % 2)
            return carry

        lax.fori_loop(0, TOKEN_TILE, body, 0, unroll=4)
        for s in range(ROW_BUFFERS):
            wait_rows(s)


def _out_proj(oa, ob, oc, x2, ga, gb, gc, w, gf, wr, br, n_rows):
    t = x2.shape[0]
    assert t % TOKEN_TILE == 0 and t // TOKEN_TILE >= ROW_BUFFERS and n_rows == N_EXPERTS * t + SPARE_ROWS
    row = lambda width: pl.BlockSpec((TOKEN_TILE, width), lambda i: (i, 0))
    full = lambda shape: pl.BlockSpec(shape, lambda i: (0, 0))
    return pl.pallas_call(
        _out_proj_kernel,
        grid=(t // TOKEN_TILE,),
        in_specs=[row(512), row(256), row(256), row(D_MODEL), full((1, 512)), full((1, 256)), full((1, 256)),
                  full((D_MODEL, D_MODEL)), full((1, D_MODEL)),
                  pl.BlockSpec((2, D_MODEL, ROUTER_PAD), lambda i: (0, 0, 0)), full((1, ROUTER_PAD))],
        out_specs=[row(D_MODEL), pl.BlockSpec(memory_space=pl.ANY),
                   row(ROUTER_PAD), row(ROUTER_PAD), full((1, ROUTER_PAD))],
        out_shape=[jax.ShapeDtypeStruct((t, D_MODEL), jnp.float32),
                   jax.ShapeDtypeStruct((n_rows * LANE_TILES, 128), jnp.float32),
                   jax.ShapeDtypeStruct((t, ROUTER_PAD), jnp.int32),
                   jax.ShapeDtypeStruct((t, ROUTER_PAD), jnp.float32),
                   jax.ShapeDtypeStruct((1, ROUTER_PAD), jnp.int32)],
        scratch_shapes=[pltpu.VMEM((1, ROUTER_PAD), jnp.float32),
                        pltpu.VMEM((ROW_BUFFERS, TOKEN_TILE * LANE_TILES, 128), jnp.float32),
                        pltpu.VMEM((ROW_BUFFERS, 8, TOKEN_TILE), jnp.int32),
                        pltpu.SMEM((ROW_BUFFERS * TILE_SLOTS,), jnp.int32),
                        pltpu.SemaphoreType.DMA((ROW_BUFFERS,)), pltpu.SemaphoreType.DMA((ROW_BUFFERS,))],
        compiler_params=pltpu.CompilerParams(dimension_semantics=("arbitrary",), vmem_limit_bytes=VMEM_LIMIT),
        name="out_proj",
    )(oa, ob, oc, x2, ga, gb, gc, w, gf, wr, br)


def _moe_ffn_kernel(blk_e_ref, blk_row_ref, blk_valid_ref, blk_first_ref, blk_next_ref, blk_slot_ref,
                    x_ref, wgu_hbm, bgu_ref, wd_hbm, bd_ref, y_ref, wgu_f32, wd_f32, wgu_bf, wd_bf, gsem, dsem,
                    *, layer):
    i = pl.program_id(0)
    n_valid = blk_valid_ref[i]

    def weight_copies(e, s):
        return (pltpu.make_async_copy(wgu_hbm.at[layer, e], wgu_f32.at[s], gsem.at[s]),
                pltpu.make_async_copy(wd_hbm.at[layer, e], wd_f32.at[s], dsem.at[s]))

    @pl.when(i == 0)
    def _():
        for copy in weight_copies(blk_e_ref[0], blk_slot_ref[0]):
            copy.start()

    @pl.when(blk_first_ref[i] == 1)
    def _():
        s = blk_slot_ref[i]
        for copy in weight_copies(blk_e_ref[i], s):
            copy.wait()
        nxt = blk_next_ref[i]

        @pl.when(nxt >= 0)
        def _():
            for copy in weight_copies(nxt, 1 - s):
                copy.start()

        for c in range(4):
            cols = slice(c * 512, (c + 1) * 512)
            wgu_bf[:, cols] = wgu_f32[s, :, cols].astype(jnp.bfloat16)
        for c in range(2):
            cols = slice(c * 512, (c + 1) * 512)
            wd_bf[:, cols] = wd_f32[s, :, cols].astype(jnp.bfloat16)

    @pl.when(n_valid > 0)
    def _():
        live = lax.broadcasted_iota(jnp.int32, (EXPERT_ROWS, 128), 0) < n_valid
        x = jnp.concatenate([jnp.where(live, _load_row_tiles(x_ref, 0, EXPERT_ROWS, c), 0.0).astype(jnp.bfloat16)
                             for c in range(LANE_TILES)], axis=1)
        glu = jnp.dot(x, wgu_bf[:, 0:D_EXPERT], preferred_element_type=jnp.float32) + bgu_ref[0, 0, :, 0:D_EXPERT]
        lin = jnp.dot(x, wgu_bf[:, D_EXPERT:], preferred_element_type=jnp.float32) + bgu_ref[0, 0, :, D_EXPERT:]
        glu = jnp.minimum(glu, SWIGLU_LIMIT)
        lin = jnp.clip(lin, -SWIGLU_LIMIT, SWIGLU_LIMIT)
        act = glu * jax.nn.sigmoid(SWIGLU_ALPHA * glu) * (lin + 1.0)
        y = jnp.dot(act.astype(jnp.bfloat16), wd_bf[...], preferred_element_type=jnp.float32) + bd_ref[0, 0]
        _store_row_tiles(y_ref, y)

    @pl.when(n_valid <= 0)
    def _():
        y_ref[...] = jnp.zeros_like(y_ref)


def _moe_ffn(layer, table, x_rows, w_gu, b_gu, w_down, b_down):
    n_blocks = table[0].shape[0]
    depth = w_gu.shape[0]
    rows = lambda i, be, br, *_: (br[i], 0)
    expert = lambda i, be, *_: (layer, be[i], 0, 0)
    grid_spec = pltpu.PrefetchScalarGridSpec(
        num_scalar_prefetch=len(table),
        grid=(n_blocks,),
        in_specs=[
            pl.BlockSpec((EXPERT_ROWS * LANE_TILES, 128), rows),
            pl.BlockSpec(memory_space=pl.ANY),
            pl.BlockSpec((1, 1, 1, 2 * D_EXPERT), expert),
            pl.BlockSpec(memory_space=pl.ANY),
            pl.BlockSpec((1, 1, 1, D_MODEL), expert),
        ],
        out_specs=pl.BlockSpec((EXPERT_ROWS * LANE_TILES, 128), rows),
        scratch_shapes=[pltpu.VMEM((2, D_MODEL, 2 * D_EXPERT), jnp.float32),
                        pltpu.VMEM((2, D_EXPERT, D_MODEL), jnp.float32),
                        pltpu.VMEM((D_MODEL, 2 * D_EXPERT), jnp.bfloat16),
                        pltpu.VMEM((D_EXPERT, D_MODEL), jnp.bfloat16),
                        pltpu.SemaphoreType.DMA((2,)), pltpu.SemaphoreType.DMA((2,))],
    )
    return pl.pallas_call(
        functools.partial(_moe_ffn_kernel, layer=layer),
        grid_spec=grid_spec,
        out_shape=jax.ShapeDtypeStruct(x_rows.shape, jnp.float32),
        compiler_params=pltpu.CompilerParams(dimension_semantics=("arbitrary",), vmem_limit_bytes=VMEM_LIMIT),
        name="moe_ffn",
    )(*table, x_rows, w_gu, b_gu.reshape(depth, N_EXPERTS, 1, -1), w_down, b_down.reshape(depth, N_EXPERTS, 1, -1))


def _combine_kernel(dest_hbm, y_hbm, x_ref, g_ref, *rest, project):
    if project:
        gain_ref, w_ref, hg_ref, xo_ref, o_ref, ybuf0, ybuf1, idx_smem, isem, gsem = rest
    else:
        xo_ref, ybuf0, ybuf1, idx_smem, isem, gsem = rest
    ybufs = (ybuf0, ybuf1)
    i = pl.program_id(0)
    nt = pl.num_programs(0)

    def idx_copy(tile, s):
        return pltpu.make_async_copy(dest_hbm.at[jnp.minimum(tile, nt - 1)], _index_slot(idx_smem, s), isem.at[s])

    def row_copy(s, r, k):
        src = idx_smem[s * TILE_SLOTS + r * TOP_K + k]
        return pltpu.make_async_copy(_row_tile(y_hbm, src), _row_tile(ybufs[s], k * TOKEN_TILE + r), gsem.at[s])

    def wait_gather(s):
        pltpu.make_async_copy(y_hbm.at[pl.ds(0, TILE_SLOTS * LANE_TILES)], ybufs[s], gsem.at[s]).wait()

    @pl.when(i == 0)
    def _():
        idx_copy(0, 0).start()
        idx_copy(0, 0).wait()

        def body(r, carry):
            for k in range(TOP_K):
                row_copy(0, r, k).start(priority=k % 2)
            return carry

        lax.fori_loop(0, TOKEN_TILE, body, 0, unroll=4)
        idx_copy(1, 1).start()

    def step(slot):
        wait_gather(slot)
        idx_copy(i + 1, 1 - slot).wait()
        for r in range(TOKEN_TILE):
            for k in range(TOP_K):
                row_copy(1 - slot, r, k).start(priority=k % 2)
        idx_copy(i + 2, slot).start()

        g = g_ref[...]
        gates = [jnp.broadcast_to(g[:, k:k + 1], (TOKEN_TILE, 128)) for k in range(TOP_K)]
        cols = []
        for c in range(LANE_TILES):
            acc = x_ref[:, c * 128:(c + 1) * 128]
            for k in range(TOP_K):
                acc = acc + gates[k] * _load_row_tiles(ybufs[slot], k * TOKEN_TILE, TOKEN_TILE, c)
            cols.append(acc)
        x = jnp.concatenate(cols, axis=1)
        xo_ref[...] = x
        if project:
            _project(_rms(x, gain_ref[...]).astype(jnp.bfloat16), w_ref, hg_ref, o_ref)

        @pl.when(i == nt - 1)
        def _():
            wait_gather(1 - slot)
            idx_copy(i + 2, slot).wait()

    for parity in range(2):
        pl.when(i % 2 == parity)(functools.partial(step, parity))


def _combine_call(dest_tiles, y_rows, x2, gates, gain=None, w=None, head_gain=None):
    t = x2.shape[0]
    project = w is not None
    row = lambda width: pl.BlockSpec((TOKEN_TILE, width), lambda i: (i, 0))
    full = lambda shape: pl.BlockSpec(shape, lambda i: (0, 0))
    in_specs = [pl.BlockSpec(memory_space=pl.ANY), pl.BlockSpec(memory_space=pl.ANY), row(D_MODEL), row(ROUTER_PAD)]
    out_specs = [row(D_MODEL)]
    out_shape = [jax.ShapeDtypeStruct((t, D_MODEL), jnp.float32)]
    args = [dest_tiles, y_rows, x2, gates]
    if project:
        in_specs += [full((1, D_MODEL)), full((D_MODEL, PROJ_WIDTH)), full((1, PROJ_WIDTH))]
        out_specs.append(row(PROJ_WIDTH))
        out_shape.append(jax.ShapeDtypeStruct((t, PROJ_WIDTH), jnp.bfloat16))
        args += [gain, w, head_gain]
    out = pl.pallas_call(
        functools.partial(_combine_kernel, project=project),
        grid=(t // TOKEN_TILE,),
        in_specs=in_specs, out_specs=out_specs, out_shape=out_shape,
        scratch_shapes=[pltpu.VMEM((TILE_SLOTS * LANE_TILES, 128), jnp.float32),
                        pltpu.VMEM((TILE_SLOTS * LANE_TILES, 128), jnp.float32),
                        pltpu.SMEM((2 * TILE_SLOTS,), jnp.int32),
                        pltpu.SemaphoreType.DMA((2,)), pltpu.SemaphoreType.DMA((2,))],
        compiler_params=pltpu.CompilerParams(dimension_semantics=("arbitrary",), vmem_limit_bytes=VMEM_LIMIT),
        name="combine_in_proj" if project else "combine",
    )(*args)
    return out if project else out[0]


def _t5_bucket(rel):
    half = T5_BUCKETS // 2
    max_exact = half // 2
    ret = np.where(rel > 0, half, 0)
    n = np.abs(rel)
    nf = np.maximum(n, 1).astype(np.float32)
    large = max_exact + (np.log(nf / max_exact) / math.log(T5_MAX_DIST / max_exact)
                         * (half - max_exact)).astype(np.int32)
    large = np.minimum(large, half - 1)
    return ret + np.where(n < max_exact, n, large)


def _band_ok(n_prev, band):
    q_chunk = np.arange(Q_TILE)[:, None] // CHUNK + n_prev
    k_chunk = np.arange(band)[None, :] // CHUNK
    return (k_chunk <= q_chunk) & (k_chunk >= q_chunk - n_prev)


def _toeplitz(u, band):
    heads, length = u.shape
    flat = jnp.broadcast_to(u[:, None, :], (heads, Q_TILE, length)).reshape(heads, Q_TILE * length)
    skew = flat[:, :Q_TILE * (length - 1)].reshape(heads, Q_TILE, length - 1)
    return skew[:, :, Q_TILE - 1:Q_TILE - 1 + band]


def _a_bias(t5_bias):
    rel = np.arange(Q_TILE + A_BAND) - (Q_TILE - 1) - A_PAD
    bias = _toeplitz(t5_bias[_t5_bucket(rel)].T.astype(jnp.float32), A_BAND)
    bias = jnp.where(_band_ok(A_PREV_CHUNKS, A_BAND)[None], bias, NEG_INF)
    pairs = [jnp.concatenate([bias[4 * hk + parity], bias[4 * hk + 2 + parity]], axis=0)
             for hk in range(A_KV_HEADS) for parity in range(2)]
    return jnp.stack(pairs).transpose(0, 2, 1)


def _c_bias(rel_table):
    dist = C_PAD + (Q_TILE - 1) - np.arange(Q_TILE + C_BAND)
    idx = np.clip(dist, -(CHUNK - 1), C_REL_CLIP) + (CHUNK - 1)
    bias = _toeplitz(rel_table[idx].T.astype(jnp.float32), C_BAND)
    return jnp.where(_band_ok(C_PREV_CHUNKS, C_BAND)[None], bias, NEG_INF).transpose(0, 2, 1)


def _rope_tables(seq):
    half = B_ROPE_DIM // 2
    inv_freq = ROPE_THETA ** (-(jnp.arange(half, dtype=jnp.float32) / half))
    ang = jnp.arange(seq, dtype=jnp.float32)[:, None] * inv_freq[None, :]
    cos, sin = jnp.cos(ang), jnp.sin(ang)
    zeros = jnp.zeros((seq, B_NOPE_DIM), jnp.float32)
    tail = jnp.zeros((seq, B_HEAD_PAD - B_QK_DIM), jnp.float32)
    zh = jnp.zeros((seq, half), jnp.float32)
    cos_t = jnp.concatenate([zeros + 1.0, cos, cos, tail + 1.0], axis=1)
    sa = jnp.concatenate([zeros, -sin, zh, tail], axis=1)
    sb = jnp.concatenate([zeros, zh, sin, tail], axis=1)
    return cos_t, sa, sb


def _regroup_w_in(w):
    splits = np.cumsum((512, 128, 128, 256, 128, 32, 256, 256))
    aq, ak, av, bcq, bckv, bkpe, cq, ck, cv = jnp.split(w, splits.tolist(), axis=1)
    z = lambda n: jnp.zeros((w.shape[0], n), w.dtype)
    return jnp.concatenate([aq, ak, av, bcq, bckv, z(64), bkpe, z(32), cq, ck, cv], axis=1).astype(jnp.bfloat16)


def _head_gain_row(a_q, a_k, c_q, c_k):
    scale = HEAD_DIM ** -0.5
    row = jnp.ones((PROJ_WIDTH,), jnp.float32)
    for sec, heads, gain in ((SEC_AQ, A_Q_HEADS, a_q * scale), (SEC_AKV, A_KV_HEADS, a_k),
                             (SEC_CQ, C_HEADS, c_q * scale), (SEC_CK, C_HEADS, c_k)):
        row = row.at[sec:sec + heads * HEAD_DIM].set(jnp.tile(gain, heads))
    return row.reshape(1, PROJ_WIDTH)


def _pad_heads(w, head_dim, keep):
    rank = w.shape[0]
    w = w.reshape(rank, B_HEADS, head_dim)[:, :, :keep]
    w = jnp.pad(w, ((0, 0), (0, 0), (0, B_HEAD_PAD - keep)))
    return w.reshape(rank, B_HEADS * B_HEAD_PAD).astype(jnp.bfloat16)


def _pad_lanes(g, offset, width):
    return jnp.pad(g, (offset, width - offset - g.shape[0])).reshape(1, width)


def _block_table(counts, t):
    tk = t * TOP_K
    blocks_per_expert = t // EXPERT_ROWS
    nblk = (counts + EXPERT_ROWS - 1) // EXPERT_ROWS
    blk_end = jnp.cumsum(nblk)
    blk_first = blk_end - nblk
    n_blocks = (tk + N_EXPERTS * (EXPERT_ROWS - 1) + EXPERT_ROWS - 1) // EXPERT_ROWS
    blk = jnp.arange(n_blocks, dtype=jnp.int32)
    blk_e = jnp.minimum(jnp.sum(blk[:, None] >= blk_end[None, :], axis=1), N_EXPERTS - 1).astype(jnp.int32)
    own = blk_e[:, None] == jnp.arange(N_EXPERTS, dtype=jnp.int32)[None, :]
    pick = lambda v: jnp.sum(jnp.where(own, v[None, :], 0), axis=1)
    j = blk - pick(blk_first)
    blk_valid = jnp.clip(pick(counts) - j * EXPERT_ROWS, 0, EXPERT_ROWS).astype(jnp.int32)
    spare = N_EXPERTS * blocks_per_expert
    blk_row = jnp.where(blk_valid > 0, blk_e * blocks_per_expert + j, spare).astype(jnp.int32)
    experts = jnp.arange(N_EXPERTS, dtype=jnp.int32)
    used = counts > 0
    later = jnp.logical_and(experts[None, :] > experts[:, None], used[None, :])
    nxt = jnp.min(jnp.where(later, experts[None, :], N_EXPERTS), axis=1)
    nxt = jnp.where(nxt < N_EXPERTS, nxt, -1)
    ordinal = jnp.cumsum(used.astype(jnp.int32)) - 1
    blk_first = jnp.logical_and(j == 0, blk_valid > 0).astype(jnp.int32)
    blk_next = pick(nxt).astype(jnp.int32)
    blk_slot = (pick(ordinal) % 2).astype(jnp.int32)
    return blk_e, blk_row, blk_valid, blk_first, blk_next, blk_slot


def kernel(x, attn_norm, w_in, a_q_norm, a_k_norm, a_sinks, t5_bias, b_q_a_norm, b_w_q_b, b_kv_a_norm, b_w_kv_b,
           b_q_norm, b_k_norm, c_q_norm, c_k_norm, c_rel_bias, a_out_norm, b_out_norm, c_out_norm, w_out, ffn_norm,
           w_router, b_router, w_gate_up, b_gate_up, w_down, b_down):
    b, s, d = x.shape
    t = b * s
    depth = w_in.shape[0]
    a_bias = _a_bias(t5_bias)
    cos_t, sin_a, sin_b = _rope_tables(s)
    x2 = x.reshape(t, d)
    head_gain = lambda l: _head_gain_row(a_q_norm[l], a_k_norm[l], c_q_norm[l], c_k_norm[l])
    proj = _in_proj(x2, attn_norm[0].reshape(1, d), _regroup_w_in(w_in[0]), head_gain(0))
    for l in range(depth):
        proj3 = proj.reshape(b, s, PROJ_WIDTH)
        o_a = _attn_a(proj3, a_bias, a_sinks[l])
        kv_w = b_w_kv_b[l].reshape(B_KV_RANK, B_HEADS, B_NOPE_DIM + B_V_DIM)
        wv = kv_w[:, :, B_NOPE_DIM:].reshape(B_KV_RANK, B_HEADS * B_V_DIM).astype(jnp.bfloat16)
        qb, kb, vb = _latent_prep(
            proj, b_q_a_norm[l].reshape(1, -1), b_kv_a_norm[l].reshape(1, -1),
            _pad_heads(b_w_q_b[l], B_QK_DIM, B_QK_DIM), _pad_heads(b_w_kv_b[l], B_NOPE_DIM + B_V_DIM, B_NOPE_DIM), wv,
            _pad_lanes(b_q_norm[l], 0, B_HEAD_PAD), _pad_lanes(b_k_norm[l], 0, B_HEAD_PAD), cos_t, sin_a, sin_b, s)
        o_b = _attn_b(qb.reshape(b, s, -1), kb.reshape(b, s, -1), vb.reshape(b, s, -1))
        o_c = _attn_c(proj3, _c_bias(c_rel_bias[l]))
        wr = jnp.pad(w_router[l], ((0, 0), (0, ROUTER_PAD - N_EXPERTS)))
        wr_hi = wr.astype(jnp.bfloat16)
        wr = jnp.stack([wr_hi, (wr - wr_hi.astype(jnp.float32)).astype(jnp.bfloat16)])
        br = jnp.pad(b_router[l], (0, ROUTER_PAD - N_EXPERTS)).reshape(1, ROUTER_PAD)
        n_rows = N_EXPERTS * t + SPARE_ROWS
        x2, x_rows, dest, gates, counts = _out_proj(
            o_a.reshape(t, -1), o_b.reshape(t, -1), o_c.reshape(t, -1), x2,
            a_out_norm[l].reshape(1, -1), b_out_norm[l].reshape(1, -1), c_out_norm[l].reshape(1, -1),
            w_out[l].astype(jnp.bfloat16), ffn_norm[l].reshape(1, -1), wr, br, n_rows)
        dest_tiles = dest[:, :TOP_K].reshape(t // TOKEN_TILE, TOKEN_TILE * TOP_K)
        table = _block_table(counts[0, :N_EXPERTS], t)
        y_rows = _moe_ffn(l, table, x_rows, w_gate_up, b_gate_up, w_down, b_down)
        if l + 1 < depth:
            x2, proj = _combine_call(dest_tiles, y_rows, x2, gates, attn_norm[l + 1].reshape(1, d),
                                     _regroup_w_in(w_in[l + 1]), head_gain(l + 1))
        else:
            x2 = _combine_call(dest_tiles, y_rows, x2, gates)
    return x2.reshape(b, s, d)
```

```python
import functools
import math

import jax
import jax.numpy as jnp
import numpy as np
from jax import lax
from jax.experimental import pallas as pl
from jax.experimental.pallas import tpu as pltpu

D_MODEL = 1024
CHUNK = 64
HEAD_DIM = 64
EPS = 1e-6
NEG_INF = -1e30
A_Q_HEADS = 8
A_KV_HEADS = 2
A_GROUP = 4
A_PREV_CHUNKS = 2
T5_BUCKETS = 32
T5_MAX_DIST = 128
B_HEADS = 4
B_Q_RANK = 256
B_KV_RANK = 128
B_NOPE_DIM = 64
B_ROPE_DIM = 32
B_QK_DIM = 96
B_V_DIM = 64
ROPE_THETA = 10000.0
C_HEADS = 4
C_PREV_CHUNKS = 8
C_REL_CLIP = 256
N_EXPERTS = 32
TOP_K = 4
D_EXPERT = 1024
SWIGLU_LIMIT = 7.0
SWIGLU_ALPHA = 1.702

SEC_AQ = 0
SEC_AKV = 512
SEC_BCQ = 768
SEC_BCKV = 1024
SEC_BKPE = 1152
SEC_CQ = 1280
SEC_CK = 1536
SEC_CV = 1792
PROJ_WIDTH = 2048
HEAD_NORM_SLABS = tuple(sec // 128 + j for sec, n in ((SEC_AQ, 4), (SEC_AKV, 1), (SEC_CQ, 2), (SEC_CK, 2))
                        for j in range(n))
B_HEAD_PAD = 128

Q_TILE = 128
A_TILES_PER_STEP = 1
C_TILES_PER_STEP = 4
A_STEP_ROWS = Q_TILE * A_TILES_PER_STEP
C_STEP_ROWS = Q_TILE * C_TILES_PER_STEP
A_BAND = Q_TILE + A_PREV_CHUNKS * CHUNK
C_BAND = Q_TILE + C_PREV_CHUNKS * CHUNK
A_PAD = A_PREV_CHUNKS * CHUNK
C_PAD = C_PREV_CHUNKS * CHUNK
B_TILE = 512
ROW_TILE = 512
EXPERT_ROWS = 256
ROUTER_PAD = 128
TOKEN_TILE = 512
TILE_SLOTS = TOKEN_TILE * TOP_K
ROW_BUFFERS = 3
SPARE_ROWS = TILE_SLOTS
LANE_TILES = D_MODEL // 128
VMEM_LIMIT = 48 * 1024 * 1024


def _rms(x, gain):
    return x * lax.rsqrt(jnp.mean(x * x, axis=-1, keepdims=True) + EPS) * gain


def _nt_dot(a, b):
    return lax.dot_general(a, b, (((1,), (1,)), ((), ())), preferred_element_type=jnp.float32)


def _load_row_tiles(ref, first_row, n_rows, c):
    return ref[pl.ds(first_row * LANE_TILES + c, n_rows, stride=LANE_TILES), :]


def _store_row_tiles(ref, value):
    for c in range(LANE_TILES):
        ref[pl.ds(c, value.shape[0], stride=LANE_TILES), :] = value[:, c * 128:(c + 1) * 128]


def _row_tile(ref, row):
    return ref.at[pl.ds(pl.multiple_of(row * LANE_TILES, LANE_TILES), LANE_TILES)]


def _index_slot(idx_smem, s):
    return idx_smem.at[pl.ds(pl.multiple_of(s * TILE_SLOTS, TILE_SLOTS), TILE_SLOTS)]


def _project(h, w_ref, hg_ref, o_ref):
    li = lax.broadcasted_iota(jnp.int32, (128, 128), 0) // HEAD_DIM
    lj = lax.broadcasted_iota(jnp.int32, (128, 128), 1) // HEAD_DIM
    same_head = jnp.where(li == lj, 1.0, 0.0).astype(jnp.bfloat16)
    for n in range(PROJ_WIDTH // 512):
        acc = jnp.dot(h, w_ref[:, n * 512:(n + 1) * 512], preferred_element_type=jnp.float32)
        slabs = []
        for j in range(4):
            slab = n * 4 + j
            a = acc[:, j * 128:(j + 1) * 128]
            if slab in HEAD_NORM_SLABS:
                sq = a * a
                hi = sq.astype(jnp.bfloat16)
                lo = (sq - hi.astype(jnp.float32)).astype(jnp.bfloat16)
                ss = (jnp.dot(hi, same_head, preferred_element_type=jnp.float32)
                      + jnp.dot(lo, same_head, preferred_element_type=jnp.float32))
                a = a * lax.rsqrt(ss * (1.0 / HEAD_DIM) + EPS) * hg_ref[:, slab * 128:(slab + 1) * 128]
            slabs.append(a.astype(o_ref.dtype))
        o_ref[:, n * 512:(n + 1) * 512] = jnp.concatenate(slabs, axis=1)


def _in_proj_kernel(x_ref, g_ref, w_ref, hg_ref, o_ref):
    _project(_rms(x_ref[...], g_ref[...]).astype(jnp.bfloat16), w_ref, hg_ref, o_ref)


def _in_proj(x2, gain, w, head_gain):
    t = x2.shape[0]
    return pl.pallas_call(
        _in_proj_kernel,
        grid=(t // ROW_TILE,),
        in_specs=[
            pl.BlockSpec((ROW_TILE, D_MODEL), lambda i: (i, 0)),
            pl.BlockSpec((1, D_MODEL), lambda i: (0, 0)),
            pl.BlockSpec((D_MODEL, PROJ_WIDTH), lambda i: (0, 0)),
            pl.BlockSpec((1, PROJ_WIDTH), lambda i: (0, 0)),
        ],
        out_specs=pl.BlockSpec((ROW_TILE, PROJ_WIDTH), lambda i: (i, 0)),
        out_shape=jax.ShapeDtypeStruct((t, PROJ_WIDTH), jnp.bfloat16),
        compiler_params=pltpu.CompilerParams(dimension_semantics=("arbitrary",), vmem_limit_bytes=VMEM_LIMIT),
        name="in_proj",
    )(x2, gain, w, head_gain)


def _half_lanes(x, parity, fill=0.0):
    lane = lax.broadcasted_iota(jnp.int32, x.shape, x.ndim - 1) % 128
    keep = lane < HEAD_DIM if parity == 0 else lane >= HEAD_DIM
    return jnp.where(keep, x, jnp.full_like(x, fill))


def _tn_dot(a, b):
    return lax.dot_general(a, b, (((0,), (0,)), ((), ())), preferred_element_type=jnp.float32)


def _softmax_pv(s, key_ok, v_aug, sink=None, value_half=0):
    if key_ok is not None:
        s = jnp.where(key_ok, s, NEG_INF)
    m = jnp.max(s, axis=0, keepdims=True)
    if sink is not None:
        m = jnp.maximum(m, sink)
    e = jnp.exp(s - m).astype(jnp.bfloat16)
    o = _tn_dot(e, v_aug)
    if sink is not None:
        row = lax.broadcasted_iota(jnp.int32, (8, s.shape[1]), 0)
        e_sink = jnp.where(row == 0, jnp.exp(sink - m), 0.0).astype(jnp.bfloat16)
        first = lax.broadcasted_iota(jnp.int32, (8, 128), 0) == 0
        sel = _half_lanes(jnp.where(first, 1.0, 0.0), 1 - value_half).astype(jnp.bfloat16)
        o = o + _tn_dot(e_sink, sel)
    return o


def _merge_halves(halves):
    lane = lax.broadcasted_iota(jnp.int32, halves[0].shape, 1)
    normed = [o / pltpu.roll(o, HEAD_DIM, 1) for o in halves]
    return jnp.where(lane < HEAD_DIM, normed[0], normed[1])


def _attn_a_kernel(q_ref, kv_ref, bias_ref, sink_ref, o_ref, k_pad, v_pad):
    i = pl.program_id(1)
    seq = kv_ref.shape[1]

    @pl.when(i == 0)
    def _():
        for n in range(2 * A_KV_HEADS):
            k_pad[n, 0:A_PAD, :] = jnp.zeros((A_PAD, 128), k_pad.dtype)
            v_pad[n, 0:A_PAD, :] = jnp.zeros((A_PAD, 128), v_pad.dtype)

        def fill(r, carry):
            rows = pl.ds(pl.multiple_of(r * 256, 256), 256)
            dst = pl.ds(pl.multiple_of(r * 256 + A_PAD, 128), 256)
            kv = kv_ref[0, rows, :].astype(jnp.float32)
            for src, pad, other in ((kv[:, 0:128], k_pad, 0.0), (kv[:, 128:256], v_pad, 1.0)):
                swapped = pltpu.roll(src, HEAD_DIM, 1)
                pad[0, dst, :] = _half_lanes(src, 0, other).astype(pad.dtype)
                pad[1, dst, :] = _half_lanes(swapped, 1, other).astype(pad.dtype)
                pad[2, dst, :] = _half_lanes(swapped, 0, other).astype(pad.dtype)
                pad[3, dst, :] = _half_lanes(src, 1, other).astype(pad.dtype)
            return carry

        lax.fori_loop(0, seq // 256, fill, 0)

    first_slab = lax.broadcasted_iota(jnp.int32, (1, 2 * Q_TILE), 1) < Q_TILE

    def tiles(front):
        for sub in range(A_TILES_PER_STEP):
            rows = slice(sub * Q_TILE, (sub + 1) * Q_TILE)
            start = pl.multiple_of((i * A_TILES_PER_STEP + sub) * Q_TILE, Q_TILE)
            band = pl.ds(start, A_BAND)
            key_ok = None
            if front:
                key_ok = lax.broadcasted_iota(jnp.int32, (A_BAND, 2 * Q_TILE), 0) + (start - A_PAD) >= 0
            for hk in range(A_KV_HEADS):
                q2 = jnp.concatenate([q_ref[0, rows, (2 * hk) * 128:(2 * hk + 1) * 128],
                                      q_ref[0, rows, (2 * hk + 1) * 128:(2 * hk + 2) * 128]], axis=0)
                halves = []
                for parity in range(2):
                    sink = jnp.where(first_slab, sink_ref[4 * hk + parity], sink_ref[4 * hk + 2 + parity])
                    s = _nt_dot(k_pad[2 * hk + parity, band, :], q2) + bias_ref[2 * hk + parity]
                    halves.append(_softmax_pv(s, key_ok, v_pad[2 * hk + parity, band, :], sink, parity))
                out = _merge_halves(halves)
                o_ref[0, rows, (2 * hk) * 128:(2 * hk + 1) * 128] = out[0:Q_TILE].astype(o_ref.dtype)
                o_ref[0, rows, (2 * hk + 1) * 128:(2 * hk + 2) * 128] = out[Q_TILE:2 * Q_TILE].astype(o_ref.dtype)

    front_steps = -(-A_PAD // A_STEP_ROWS)
    pl.when(i < front_steps)(functools.partial(tiles, True))
    pl.when(i >= front_steps)(functools.partial(tiles, False))


def _attn_a(proj3, bias, sinks):
    b, s, _ = proj3.shape
    return pl.pallas_call(
        _attn_a_kernel,
        grid=(b, s // A_STEP_ROWS),
        in_specs=[
            pl.BlockSpec((1, A_STEP_ROWS, 512), lambda bi, i: (bi, i, SEC_AQ // 512)),
            pl.BlockSpec((1, s, 256), lambda bi, i: (bi, 0, SEC_AKV // 256)),
            pl.BlockSpec((2 * A_KV_HEADS, 2 * Q_TILE, A_BAND), lambda bi, i: (0, 0, 0)),
            pl.BlockSpec(memory_space=pltpu.SMEM),
        ],
        out_specs=pl.BlockSpec((1, A_STEP_ROWS, 512), lambda bi, i: (bi, i, 0)),
        out_shape=jax.ShapeDtypeStruct((b, s, 512), jnp.bfloat16),
        scratch_shapes=[pltpu.VMEM((2 * A_KV_HEADS, s + A_PAD, 128), jnp.bfloat16),
                        pltpu.VMEM((2 * A_KV_HEADS, s + A_PAD, 128), jnp.bfloat16)],
        compiler_params=pltpu.CompilerParams(dimension_semantics=("arbitrary", "arbitrary"),
                                             vmem_limit_bytes=VMEM_LIMIT),
        name="attn_a",
    )(proj3, proj3, bias, sinks)


def _latent_prep_kernel(cq_ref, ckv_ref, kpe_ref, qag_ref, kvag_ref, wq_ref, wk_ref, wv_ref,
                        qg_ref, kg_ref, cos_ref, sa_ref, sb_ref, qo_ref, ko_ref, vo_ref):
    lane = lax.broadcasted_iota(jnp.int32, (ROW_TILE, B_HEAD_PAD), 1)
    is_nope = lane < B_NOPE_DIM
    is_rope = jnp.logical_and(lane >= B_NOPE_DIM, lane < B_QK_DIM)
    cos = cos_ref[...]
    sa = sa_ref[...]
    sb = sb_ref[...]

    def split_norm(x, gain):
        sq = x * x
        nope = jnp.sum(jnp.where(is_nope, sq, 0.0), axis=-1, keepdims=True) * (1.0 / B_NOPE_DIM)
        rope = jnp.sum(jnp.where(is_rope, sq, 0.0), axis=-1, keepdims=True) * (1.0 / B_ROPE_DIM)
        r = jnp.where(is_nope, lax.rsqrt(nope + EPS), lax.rsqrt(rope + EPS))
        return x * r * gain

    def rotate(x):
        return x * cos + pltpu.roll(x, 112, 1) * sa + pltpu.roll(x, 16, 1) * sb

    cq = _rms(cq_ref[...].astype(jnp.float32), qag_ref[...]).astype(jnp.bfloat16)
    ckv = _rms(ckv_ref[...].astype(jnp.float32), kvag_ref[...]).astype(jnp.bfloat16)
    kpe = kpe_ref[...].astype(jnp.float32)
    kpe_n = rotate(split_norm(kpe, kg_ref[...]))
    kpe_n = jnp.where(is_rope, kpe_n, 0.0)
    v = jnp.dot(ckv, wv_ref[...], preferred_element_type=jnp.float32)
    for h in range(B_HEADS):
        pair = v[:, (h // 2) * 128:(h // 2 + 1) * 128]
        vo_ref[:, h * 128:(h + 1) * 128] = _half_lanes(pair, h % 2, 1.0).astype(vo_ref.dtype)
    qscale = B_QK_DIM ** -0.5
    for h in range(B_HEADS):
        cols = slice(h * B_HEAD_PAD, (h + 1) * B_HEAD_PAD)
        qh = jnp.dot(cq, wq_ref[:, cols], preferred_element_type=jnp.float32)
        qh = rotate(split_norm(qh, qg_ref[...])) * qscale
        qo_ref[:, cols] = qh.astype(qo_ref.dtype)
        kh = jnp.dot(ckv, wk_ref[:, cols], preferred_element_type=jnp.float32)
        kh = jnp.where(is_nope, split_norm(kh, kg_ref[...]), 0.0) + kpe_n
        ko_ref[:, cols] = kh.astype(ko_ref.dtype)


def _latent_prep(proj, q_a_gain, kv_a_gain, wq, wk, wv, q_gain, k_gain, cos, sa, sb, seq):
    t = proj.shape[0]
    pos_blocks = seq // ROW_TILE
    full = lambda shape: pl.BlockSpec(shape, lambda i: (0, 0))
    tab = pl.BlockSpec((ROW_TILE, B_HEAD_PAD), lambda i: (i % pos_blocks, 0))
    return pl.pallas_call(
        _latent_prep_kernel,
        grid=(t // ROW_TILE,),
        in_specs=[
            pl.BlockSpec((ROW_TILE, 256), lambda i: (i, SEC_BCQ // 256)),
            pl.BlockSpec((ROW_TILE, 128), lambda i: (i, SEC_BCKV // 128)),
            pl.BlockSpec((ROW_TILE, 128), lambda i: (i, SEC_BKPE // 128)),
            full((1, B_Q_RANK)), full((1, B_KV_RANK)),
            full((B_Q_RANK, B_HEADS * B_HEAD_PAD)), full((B_KV_RANK, B_HEADS * B_HEAD_PAD)),
            full((B_KV_RANK, B_HEADS * B_V_DIM)),
            full((1, B_HEAD_PAD)), full((1, B_HEAD_PAD)),
            tab, tab, tab,
        ],
        out_specs=[
            pl.BlockSpec((ROW_TILE, B_HEADS * B_HEAD_PAD), lambda i: (i, 0)),
            pl.BlockSpec((ROW_TILE, B_HEADS * B_HEAD_PAD), lambda i: (i, 0)),
            pl.BlockSpec((ROW_TILE, B_HEADS * B_HEAD_PAD), lambda i: (i, 0)),
        ],
        out_shape=[
            jax.ShapeDtypeStruct((t, B_HEADS * B_HEAD_PAD), jnp.bfloat16),
            jax.ShapeDtypeStruct((t, B_HEADS * B_HEAD_PAD), jnp.bfloat16),
            jax.ShapeDtypeStruct((t, B_HEADS * B_HEAD_PAD), jnp.bfloat16),
        ],
        compiler_params=pltpu.CompilerParams(dimension_semantics=("arbitrary",), vmem_limit_bytes=VMEM_LIMIT),
        name="latent_prep",
    )(proj, proj, proj, q_a_gain, kv_a_gain, wq, wk, wv, q_gain, k_gain, cos, sa, sb)


def _attn_b_kernel(q_ref, k_ref, v_ref, o_ref):
    i = pl.program_id(1)
    query_chunk = lax.broadcasted_iota(jnp.int32, (B_TILE, B_TILE), 0) // CHUNK
    key_chunk = lax.broadcasted_iota(jnp.int32, (B_TILE, B_TILE), 1) // CHUNK
    diag_ok = key_chunk <= query_chunk

    def step(j, carry, masked):
        rows = pl.ds(pl.multiple_of(j * B_TILE, B_TILE), B_TILE)
        out = []
        for h in range(B_HEADS):
            m, acc = carry[h]
            head = slice(h * B_HEAD_PAD, (h + 1) * B_HEAD_PAD)
            s = _nt_dot(q_ref[0, :, head], k_ref[0, rows, head])
            if masked:
                s = jnp.where(diag_ok, s, NEG_INF)
            m_new = jnp.maximum(m, jnp.max(s, axis=-1, keepdims=True))
            e = jnp.exp(s - m_new).astype(jnp.bfloat16)
            acc = jnp.exp(m - m_new) * acc + jnp.dot(e, v_ref[0, rows, head], preferred_element_type=jnp.float32)
            out.append((m_new, acc))
        return tuple(out)

    init = tuple((jnp.full((B_TILE, 1), NEG_INF, jnp.float32), jnp.zeros((B_TILE, B_HEAD_PAD), jnp.float32))
                 for _ in range(B_HEADS))
    carry = lax.fori_loop(0, i, lambda j, c: step(j, c, False), init)
    carry = step(i, carry, True)
    for pair in range(B_HEADS // 2):
        halves = [carry[2 * pair + parity][1] for parity in range(2)]
        o_ref[0, :, pair * 128:(pair + 1) * 128] = _merge_halves(halves).astype(o_ref.dtype)


def _attn_b(qb, kb, vb):
    b, s, _ = qb.shape
    return pl.pallas_call(
        _attn_b_kernel,
        grid=(b, s // B_TILE),
        in_specs=[
            pl.BlockSpec((1, B_TILE, B_HEADS * B_HEAD_PAD), lambda bi, i: (bi, i, 0)),
            pl.BlockSpec((1, s, B_HEADS * B_HEAD_PAD), lambda bi, i: (bi, 0, 0)),
            pl.BlockSpec((1, s, B_HEADS * B_HEAD_PAD), lambda bi, i: (bi, 0, 0)),
        ],
        out_specs=pl.BlockSpec((1, B_TILE, B_HEADS * B_V_DIM), lambda bi, i: (bi, i, 0)),
        out_shape=jax.ShapeDtypeStruct((b, s, B_HEADS * B_V_DIM), jnp.bfloat16),
        compiler_params=pltpu.CompilerParams(dimension_semantics=("arbitrary", "arbitrary"),
                                             vmem_limit_bytes=VMEM_LIMIT),
        name="attn_b",
    )(qb, kb, vb)


def _attn_c_kernel(q_ref, k_ref, v_ref, bias_ref, o_ref, k_pad, v_pad):
    i = pl.program_id(1)
    seq = k_ref.shape[1]
    width = C_HEADS * HEAD_DIM

    @pl.when(i == 0)
    def _():
        for parity in range(2):
            k_pad[parity, 0:C_PAD, :] = jnp.zeros((C_PAD, width), k_pad.dtype)
            v_pad[parity, 0:C_PAD, :] = jnp.zeros((C_PAD, width), v_pad.dtype)

        def fill(r, carry):
            rows = pl.ds(pl.multiple_of(r * 256, 256), 256)
            dst = pl.ds(pl.multiple_of(r * 256 + C_PAD, 256), 256)
            k = k_ref[0, rows, :]
            v = v_ref[0, rows, :]
            for parity in range(2):
                k_pad[parity, dst, :] = _half_lanes(k, parity)
                v_pad[parity, dst, :] = _half_lanes(v, parity, fill=1.0)
            return carry

        lax.fori_loop(0, seq // 256, fill, 0)

    def tiles(front):
        for sub in range(C_TILES_PER_STEP):
            rows = slice(sub * Q_TILE, (sub + 1) * Q_TILE)
            start = pl.multiple_of((i * C_TILES_PER_STEP + sub) * Q_TILE, Q_TILE)
            band = pl.ds(start, C_BAND)
            key_ok = None
            if front:
                key_ok = lax.broadcasted_iota(jnp.int32, (C_BAND, Q_TILE), 0) + (start - C_PAD) >= 0
            for slab in range(width // 128):
                lanes = slice(slab * 128, (slab + 1) * 128)
                q = q_ref[0, rows, lanes]
                halves = []
                for parity in range(2):
                    s = _nt_dot(k_pad[parity, band, lanes], q) + bias_ref[2 * slab + parity]
                    halves.append(_softmax_pv(s, key_ok, v_pad[parity, band, lanes]))
                o_ref[0, rows, lanes] = _merge_halves(halves).astype(o_ref.dtype)

    front_steps = -(-C_PAD // C_STEP_ROWS)
    pl.when(i < front_steps)(functools.partial(tiles, True))
    pl.when(i >= front_steps)(functools.partial(tiles, False))


def _attn_c(proj3, bias):
    b, s, _ = proj3.shape
    width = C_HEADS * HEAD_DIM
    return pl.pallas_call(
        _attn_c_kernel,
        grid=(b, s // C_STEP_ROWS),
        in_specs=[
            pl.BlockSpec((1, C_STEP_ROWS, width), lambda bi, i: (bi, i, SEC_CQ // width)),
            pl.BlockSpec((1, s, width), lambda bi, i: (bi, 0, SEC_CK // width)),
            pl.BlockSpec((1, s, width), lambda bi, i: (bi, 0, SEC_CV // width)),
            pl.BlockSpec((C_HEADS, C_BAND, Q_TILE), lambda bi, i: (0, 0, 0)),
        ],
        out_specs=pl.BlockSpec((1, C_STEP_ROWS, width), lambda bi, i: (bi, i, 0)),
        out_shape=jax.ShapeDtypeStruct((b, s, width), jnp.bfloat16),
        scratch_shapes=[pltpu.VMEM((2, s + C_PAD, width), jnp.bfloat16),
                        pltpu.VMEM((2, s + C_PAD, width), jnp.bfloat16)],
        compiler_params=pltpu.CompilerParams(dimension_semantics=("arbitrary", "arbitrary"),
                                             vmem_limit_bytes=VMEM_LIMIT),
        name="attn_c",
    )(proj3, proj3, proj3, bias)


def _out_proj_kernel(oa_ref, ob_ref, oc_ref, x_ref, ga_ref, gb_ref, gc_ref, w_ref, gf_ref, wr_ref, br_ref,
                     xo_ref, x_hbm, dest_ref, tg_ref, cnt_ref, count, hbuf, dvm, idx_smem, isem, dsem):
    i = pl.program_id(0)
    nt = pl.num_programs(0)
    cap = nt * TOKEN_TILE
    slot = i % ROW_BUFFERS
    prev = (i + ROW_BUFFERS - 1) % ROW_BUFFERS
    spare_row = x_hbm.shape[0] // LANE_TILES - TILE_SLOTS

    def index_copies(s):
        return [pltpu.make_async_copy(
            dvm.at[s, k], idx_smem.at[pl.ds(pl.multiple_of(s * TILE_SLOTS + k * TOKEN_TILE, TOKEN_TILE), TOKEN_TILE)],
            isem.at[s]) for k in range(TOP_K)]

    def row_copy(s, base, r, k):
        dst = idx_smem[base + k * TOKEN_TILE + r]
        return pltpu.make_async_copy(_row_tile(hbuf.at[s], r), _row_tile(x_hbm, dst), dsem.at[s])

    def wait_rows(s):
        for _ in range(TOP_K):
            pltpu.make_async_copy(hbuf.at[s], x_hbm.at[pl.ds(0, TOKEN_TILE * LANE_TILES)], dsem.at[s]).wait()

    @pl.when(i == 0)
    def _():
        count[...] = jnp.zeros_like(count)
        hbuf[ROW_BUFFERS - 1] = jnp.zeros(hbuf.shape[1:], hbuf.dtype)

        def fill(j, carry):
            idx_smem[(ROW_BUFFERS - 1) * TILE_SLOTS + j] = spare_row + j
            return carry

        lax.fori_loop(0, TILE_SLOTS, fill, 0)

    @pl.when(i >= ROW_BUFFERS - 1)
    def _():
        wait_rows(slot)

    @pl.when(i >= 1)
    def _():
        for copy in index_copies(prev):
            copy.wait()

    prev_base = prev * TILE_SLOTS
    for r in range(TOKEN_TILE):
        for k in range(TOP_K):
            row_copy(prev, prev_base, r, k).start(priority=k % 2)

    acc = x_ref[...]
    parts = ((oa_ref, ga_ref, 0, 512), (ob_ref, gb_ref, 512, 256), (oc_ref, gc_ref, 768, 256))
    for ref, g_ref, off, width in parts:
        o = _rms(ref[...].astype(jnp.float32), g_ref[...]).astype(jnp.bfloat16)
        acc = acc + jnp.dot(o, w_ref[off:off + width, :], preferred_element_type=jnp.float32)
    xo_ref[...] = acc
    h = _rms(acc, gf_ref[...])
    _store_row_tiles(hbuf.at[slot], h)
    h_hi = h.astype(jnp.bfloat16)
    h_lo = (h - h_hi.astype(jnp.float32)).astype(jnp.bfloat16)
    logits = (jnp.dot(h_hi, wr_ref[0], preferred_element_type=jnp.float32)
              + jnp.dot(h_lo, wr_ref[0], preferred_element_type=jnp.float32)
              + jnp.dot(h_hi, wr_ref[1], preferred_element_type=jnp.float32)) + br_ref[...]
    lane = lax.broadcasted_iota(jnp.int32, logits.shape, 1)
    work = jnp.where(lane < N_EXPERTS, logits, -jnp.inf)
    gate_out = jnp.zeros(logits.shape, jnp.float32)
    denom = jnp.zeros((logits.shape[0], 1), jnp.float32)
    top = None
    picks = []
    for k in range(TOP_K):
        m = jnp.max(work, axis=-1, keepdims=True)
        idx = jnp.min(jnp.where(work == m, lane, ROUTER_PAD), axis=-1, keepdims=True)
        pick = lane == idx
        work = jnp.where(pick, -jnp.inf, work)
        top = m if top is None else top
        e = jnp.exp(m - top)
        denom = denom + e
        gate_out = jnp.where(lane == k, e, gate_out)
        picks.append((idx, pick))
    tg_ref[...] = gate_out / denom
    chosen = jnp.zeros(logits.shape, jnp.float32)
    for _, pick in picks:
        chosen = jnp.where(pick, 1.0, chosen)
    r_i = lax.broadcasted_iota(jnp.int32, (TOKEN_TILE, TOKEN_TILE), 0)
    c_i = lax.broadcasted_iota(jnp.int32, (TOKEN_TILE, TOKEN_TILE), 1)
    earlier = jnp.where(c_i < r_i, 1.0, 0.0).astype(jnp.bfloat16)
    before = jnp.dot(earlier, chosen.astype(jnp.bfloat16), preferred_element_type=jnp.float32) + count[...]
    dest_out = jnp.zeros(logits.shape, jnp.int32)
    for k, (idx, pick) in enumerate(picks):
        rank = jnp.sum(jnp.where(pick, before, 0.0), axis=-1, keepdims=True).astype(jnp.int32)
        dest_out = jnp.where(lane == k, idx * cap + rank, dest_out)
    dest_ref[...] = dest_out
    count[...] = count[...] + jnp.sum(chosen, axis=0, keepdims=True)
    cnt_ref[...] = count[...].astype(jnp.int32)
    dvm[slot] = dest_out.T[0:8, :]
    for copy in index_copies(slot):
        copy.start()

    @pl.when(i == nt - 1)
    def _():
        for copy in index_copies(slot):
            copy.wait()
        base = slot * TILE_SLOTS

        def body(r, carry):
            for k in range(TOP_K):
                row_copy(slot, base, r, k).start(priority=k % 2)
            return carry

        lax.fori_loop(0, TOKEN_TILE, body, 0, unroll=4)
        for s in range(ROW_BUFFERS):
            wait_rows(s)


def _out_proj(oa, ob, oc, x2, ga, gb, gc, w, gf, wr, br, n_rows):
    t = x2.shape[0]
    assert t % TOKEN_TILE == 0 and t // TOKEN_TILE >= ROW_BUFFERS and n_rows == N_EXPERTS * t + SPARE_ROWS
    row = lambda width: pl.BlockSpec((TOKEN_TILE, width), lambda i: (i, 0))
    full = lambda shape: pl.BlockSpec(shape, lambda i: (0, 0))
    return pl.pallas_call(
        _out_proj_kernel,
        grid=(t // TOKEN_TILE,),
        in_specs=[row(512), row(256), row(256), row(D_MODEL), full((1, 512)), full((1, 256)), full((1, 256)),
                  full((D_MODEL, D_MODEL)), full((1, D_MODEL)),
                  pl.BlockSpec((2, D_MODEL, ROUTER_PAD), lambda i: (0, 0, 0)), full((1, ROUTER_PAD))],
        out_specs=[row(D_MODEL), pl.BlockSpec(memory_space=pl.ANY),
                   row(ROUTER_PAD), row(ROUTER_PAD), full((1, ROUTER_PAD))],
        out_shape=[jax.ShapeDtypeStruct((t, D_MODEL), jnp.float32),
                   jax.ShapeDtypeStruct((n_rows * LANE_TILES, 128), jnp.float32),
                   jax.ShapeDtypeStruct((t, ROUTER_PAD), jnp.int32),
                   jax.ShapeDtypeStruct((t, ROUTER_PAD), jnp.float32),
                   jax.ShapeDtypeStruct((1, ROUTER_PAD), jnp.int32)],
        scratch_shapes=[pltpu.VMEM((1, ROUTER_PAD), jnp.float32),
                        pltpu.VMEM((ROW_BUFFERS, TOKEN_TILE * LANE_TILES, 128), jnp.float32),
                        pltpu.VMEM((ROW_BUFFERS, 8, TOKEN_TILE), jnp.int32),
                        pltpu.SMEM((ROW_BUFFERS * TILE_SLOTS,), jnp.int32),
                        pltpu.SemaphoreType.DMA((ROW_BUFFERS,)), pltpu.SemaphoreType.DMA((ROW_BUFFERS,))],
        compiler_params=pltpu.CompilerParams(dimension_semantics=("arbitrary",), vmem_limit_bytes=VMEM_LIMIT),
        name="out_proj",
    )(oa, ob, oc, x2, ga, gb, gc, w, gf, wr, br)


def _moe_ffn_kernel(blk_e_ref, blk_row_ref, blk_valid_ref, blk_first_ref, blk_next_ref, blk_slot_ref,
                    x_ref, wgu_hbm, bgu_ref, wd_hbm, bd_ref, y_ref, wgu_f32, wd_f32, wgu_bf, wd_bf, gsem, dsem,
                    *, layer):
    i = pl.program_id(0)
    n_valid = blk_valid_ref[i]

    def weight_copies(e, s):
        return (pltpu.make_async_copy(wgu_hbm.at[layer, e], wgu_f32.at[s], gsem.at[s]),
                pltpu.make_async_copy(wd_hbm.at[layer, e], wd_f32.at[s], dsem.at[s]))

    @pl.when(i == 0)
    def _():
        for copy in weight_copies(blk_e_ref[0], blk_slot_ref[0]):
            copy.start()

    @pl.when(blk_first_ref[i] == 1)
    def _():
        s = blk_slot_ref[i]
        for copy in weight_copies(blk_e_ref[i], s):
            copy.wait()
        nxt = blk_next_ref[i]

        @pl.when(nxt >= 0)
        def _():
            for copy in weight_copies(nxt, 1 - s):
                copy.start()

        for c in range(4):
            cols = slice(c * 512, (c + 1) * 512)
            wgu_bf[:, cols] = wgu_f32[s, :, cols].astype(jnp.bfloat16)
        for c in range(2):
            cols = slice(c * 512, (c + 1) * 512)
            wd_bf[:, cols] = wd_f32[s, :, cols].astype(jnp.bfloat16)

    @pl.when(n_valid > 0)
    def _():
        live = lax.broadcasted_iota(jnp.int32, (EXPERT_ROWS, 128), 0) < n_valid
        x = jnp.concatenate([jnp.where(live, _load_row_tiles(x_ref, 0, EXPERT_ROWS, c), 0.0).astype(jnp.bfloat16)
                             for c in range(LANE_TILES)], axis=1)
        glu = jnp.dot(x, wgu_bf[:, 0:D_EXPERT], preferred_element_type=jnp.float32) + bgu_ref[0, 0, :, 0:D_EXPERT]
        lin = jnp.dot(x, wgu_bf[:, D_EXPERT:], preferred_element_type=jnp.float32) + bgu_ref[0, 0, :, D_EXPERT:]
        glu = jnp.minimum(glu, SWIGLU_LIMIT)
        lin = jnp.clip(lin, -SWIGLU_LIMIT, SWIGLU_LIMIT)
        act = glu * jax.nn.sigmoid(SWIGLU_ALPHA * glu) * (lin + 1.0)
        y = jnp.dot(act.astype(jnp.bfloat16), wd_bf[...], preferred_element_type=jnp.float32) + bd_ref[0, 0]
        _store_row_tiles(y_ref, y)

    @pl.when(n_valid <= 0)
    def _():
        y_ref[...] = jnp.zeros_like(y_ref)


def _moe_ffn(layer, table, x_rows, w_gu, b_gu, w_down, b_down):
    n_blocks = table[0].shape[0]
    depth = w_gu.shape[0]
    rows = lambda i, be, br, *_: (br[i], 0)
    expert = lambda i, be, *_: (layer, be[i], 0, 0)
    grid_spec = pltpu.PrefetchScalarGridSpec(
        num_scalar_prefetch=len(table),
        grid=(n_blocks,),
        in_specs=[
            pl.BlockSpec((EXPERT_ROWS * LANE_TILES, 128), rows),
            pl.BlockSpec(memory_space=pl.ANY),
            pl.BlockSpec((1, 1, 1, 2 * D_EXPERT), expert),
            pl.BlockSpec(memory_space=pl.ANY),
            pl.BlockSpec((1, 1, 1, D_MODEL), expert),
        ],
        out_specs=pl.BlockSpec((EXPERT_ROWS * LANE_TILES, 128), rows),
        scratch_shapes=[pltpu.VMEM((2, D_MODEL, 2 * D_EXPERT), jnp.float32),
                        pltpu.VMEM((2, D_EXPERT, D_MODEL), jnp.float32),
                        pltpu.VMEM((D_MODEL, 2 * D_EXPERT), jnp.bfloat16),
                        pltpu.VMEM((D_EXPERT, D_MODEL), jnp.bfloat16),
                        pltpu.SemaphoreType.DMA((2,)), pltpu.SemaphoreType.DMA((2,))],
    )
    return pl.pallas_call(
        functools.partial(_moe_ffn_kernel, layer=layer),
        grid_spec=grid_spec,
        out_shape=jax.ShapeDtypeStruct(x_rows.shape, jnp.float32),
        compiler_params=pltpu.CompilerParams(dimension_semantics=("arbitrary",), vmem_limit_bytes=VMEM_LIMIT),
        name="moe_ffn",
    )(*table, x_rows, w_gu, b_gu.reshape(depth, N_EXPERTS, 1, -1), w_down, b_down.reshape(depth, N_EXPERTS, 1, -1))


def _combine_kernel(dest_hbm, y_hbm, x_ref, g_ref, *rest, project):
    if project:
        gain_ref, w_ref, hg_ref, xo_ref, o_ref, ybuf0, ybuf1, idx_smem, isem, gsem = rest
    else:
        xo_ref, ybuf0, ybuf1, idx_smem, isem, gsem = rest
    ybufs = (ybuf0, ybuf1)
    i = pl.program_id(0)
    nt = pl.num_programs(0)

    def idx_copy(tile, s):
        return pltpu.make_async_copy(dest_hbm.at[jnp.minimum(tile, nt - 1)], _index_slot(idx_smem, s), isem.at[s])

    def row_copy(s, r, k):
        src = idx_smem[s * TILE_SLOTS + r * TOP_K + k]
        return pltpu.make_async_copy(_row_tile(y_hbm, src), _row_tile(ybufs[s], k * TOKEN_TILE + r), gsem.at[s])

    def wait_gather(s):
        pltpu.make_async_copy(y_hbm.at[pl.ds(0, TILE_SLOTS * LANE_TILES)], ybufs[s], gsem.at[s]).wait()

    @pl.when(i == 0)
    def _():
        idx_copy(0, 0).start()
        idx_copy(0, 0).wait()

        def body(r, carry):
            for k in range(TOP_K):
                row_copy(0, r, k).start(priority=k % 2)
            return carry

        lax.fori_loop(0, TOKEN_TILE, body, 0, unroll=4)
        idx_copy(1, 1).start()

    def step(slot):
        wait_gather(slot)
        idx_copy(i + 1, 1 - slot).wait()
        for r in range(TOKEN_TILE):
            for k in range(TOP_K):
                row_copy(1 - slot, r, k).start(priority=k % 2)
        idx_copy(i + 2, slot).start()

        g = g_ref[...]
        gates = [jnp.broadcast_to(g[:, k:k + 1], (TOKEN_TILE, 128)) for k in range(TOP_K)]
        cols = []
        for c in range(LANE_TILES):
            acc = x_ref[:, c * 128:(c + 1) * 128]
            for k in range(TOP_K):
                acc = acc + gates[k] * _load_row_tiles(ybufs[slot], k * TOKEN_TILE, TOKEN_TILE, c)
            cols.append(acc)
        x = jnp.concatenate(cols, axis=1)
        xo_ref[...] = x
        if project:
            _project(_rms(x, gain_ref[...]).astype(jnp.bfloat16), w_ref, hg_ref, o_ref)

        @pl.when(i == nt - 1)
        def _():
            wait_gather(1 - slot)
            idx_copy(i + 2, slot).wait()

    for parity in range(2):
        pl.when(i % 2 == parity)(functools.partial(step, parity))


def _combine_call(dest_tiles, y_rows, x2, gates, gain=None, w=None, head_gain=None):
    t = x2.shape[0]
    project = w is not None
    row = lambda width: pl.BlockSpec((TOKEN_TILE, width), lambda i: (i, 0))
    full = lambda shape: pl.BlockSpec(shape, lambda i: (0, 0))
    in_specs = [pl.BlockSpec(memory_space=pl.ANY), pl.BlockSpec(memory_space=pl.ANY), row(D_MODEL), row(ROUTER_PAD)]
    out_specs = [row(D_MODEL)]
    out_shape = [jax.ShapeDtypeStruct((t, D_MODEL), jnp.float32)]
    args = [dest_tiles, y_rows, x2, gates]
    if project:
        in_specs += [full((1, D_MODEL)), full((D_MODEL, PROJ_WIDTH)), full((1, PROJ_WIDTH))]
        out_specs.append(row(PROJ_WIDTH))
        out_shape.append(jax.ShapeDtypeStruct((t, PROJ_WIDTH), jnp.bfloat16))
        args += [gain, w, head_gain]
    out = pl.pallas_call(
        functools.partial(_combine_kernel, project=project),
        grid=(t // TOKEN_TILE,),
        in_specs=in_specs, out_specs=out_specs, out_shape=out_shape,
        scratch_shapes=[pltpu.VMEM((TILE_SLOTS * LANE_TILES, 128), jnp.float32),
                        pltpu.VMEM((TILE_SLOTS * LANE_TILES, 128), jnp.float32),
                        pltpu.SMEM((2 * TILE_SLOTS,), jnp.int32),
                        pltpu.SemaphoreType.DMA((2,)), pltpu.SemaphoreType.DMA((2,))],
        compiler_params=pltpu.CompilerParams(dimension_semantics=("arbitrary",), vmem_limit_bytes=VMEM_LIMIT),
        name="combine_in_proj" if project else "combine",
    )(*args)
    return out if project else out[0]


def _t5_bucket(rel):
    half = T5_BUCKETS // 2
    max_exact = half // 2
    ret = np.where(rel > 0, half, 0)
    n = np.abs(rel)
    nf = np.maximum(n, 1).astype(np.float32)
    large = max_exact + (np.log(nf / max_exact) / math.log(T5_MAX_DIST / max_exact)
                         * (half - max_exact)).astype(np.int32)
    large = np.minimum(large, half - 1)
    return ret + np.where(n < max_exact, n, large)


def _band_ok(n_prev, band):
    q_chunk = np.arange(Q_TILE)[:, None] // CHUNK + n_prev
    k_chunk = np.arange(band)[None, :] // CHUNK
    return (k_chunk <= q_chunk) & (k_chunk >= q_chunk - n_prev)


def _toeplitz(u, band):
    heads, length = u.shape
    flat = jnp.broadcast_to(u[:, None, :], (heads, Q_TILE, length)).reshape(heads, Q_TILE * length)
    skew = flat[:, :Q_TILE * (length - 1)].reshape(heads, Q_TILE, length - 1)
    return skew[:, :, Q_TILE - 1:Q_TILE - 1 + band]


def _a_bias(t5_bias):
    rel = np.arange(Q_TILE + A_BAND) - (Q_TILE - 1) - A_PAD
    bias = _toeplitz(t5_bias[_t5_bucket(rel)].T.astype(jnp.float32), A_BAND)
    bias = jnp.where(_band_ok(A_PREV_CHUNKS, A_BAND)[None], bias, NEG_INF)
    pairs = [jnp.concatenate([bias[4 * hk + parity], bias[4 * hk + 2 + parity]], axis=0)
             for hk in range(A_KV_HEADS) for parity in range(2)]
    return jnp.stack(pairs).transpose(0, 2, 1)


def _c_bias(rel_table):
    dist = C_PAD + (Q_TILE - 1) - np.arange(Q_TILE + C_BAND)
    idx = np.clip(dist, -(CHUNK - 1), C_REL_CLIP) + (CHUNK - 1)
    bias = _toeplitz(rel_table[idx].T.astype(jnp.float32), C_BAND)
    return jnp.where(_band_ok(C_PREV_CHUNKS, C_BAND)[None], bias, NEG_INF).transpose(0, 2, 1)


def _rope_tables(seq):
    half = B_ROPE_DIM // 2
    inv_freq = ROPE_THETA ** (-(jnp.arange(half, dtype=jnp.float32) / half))
    ang = jnp.arange(seq, dtype=jnp.float32)[:, None] * inv_freq[None, :]
    cos, sin = jnp.cos(ang), jnp.sin(ang)
    zeros = jnp.zeros((seq, B_NOPE_DIM), jnp.float32)
    tail = jnp.zeros((seq, B_HEAD_PAD - B_QK_DIM), jnp.float32)
    zh = jnp.zeros((seq, half), jnp.float32)
    cos_t = jnp.concatenate([zeros + 1.0, cos, cos, tail + 1.0], axis=1)
    sa = jnp.concatenate([zeros, -sin, zh, tail], axis=1)
    sb = jnp.concatenate([zeros, zh, sin, tail], axis=1)
    return cos_t, sa, sb


def _regroup_w_in(w):
    splits = np.cumsum((512, 128, 128, 256, 128, 32, 256, 256))
    aq, ak, av, bcq, bckv, bkpe, cq, ck, cv = jnp.split(w, splits.tolist(), axis=1)
    z = lambda n: jnp.zeros((w.shape[0], n), w.dtype)
    return jnp.concatenate([aq, ak, av, bcq, bckv, z(64), bkpe, z(32), cq, ck, cv], axis=1).astype(jnp.bfloat16)


def _head_gain_row(a_q, a_k, c_q, c_k):
    scale = HEAD_DIM ** -0.5
    row = jnp.ones((PROJ_WIDTH,), jnp.float32)
    for sec, heads, gain in ((SEC_AQ, A_Q_HEADS, a_q * scale), (SEC_AKV, A_KV_HEADS, a_k),
                             (SEC_CQ, C_HEADS, c_q * scale), (SEC_CK, C_HEADS, c_k)):
        row = row.at[sec:sec + heads * HEAD_DIM].set(jnp.tile(gain, heads))
    return row.reshape(1, PROJ_WIDTH)


def _pad_heads(w, head_dim, keep):
    rank = w.shape[0]
    w = w.reshape(rank, B_HEADS, head_dim)[:, :, :keep]
    w = jnp.pad(w, ((0, 0), (0, 0), (0, B_HEAD_PAD - keep)))
    return w.reshape(rank, B_HEADS * B_HEAD_PAD).astype(jnp.bfloat16)


def _pad_lanes(g, offset, width):
    return jnp.pad(g, (offset, width - offset - g.shape[0])).reshape(1, width)


def _block_table(counts, t):
    tk = t * TOP_K
    blocks_per_expert = t // EXPERT_ROWS
    nblk = (counts + EXPERT_ROWS - 1) // EXPERT_ROWS
    blk_end = jnp.cumsum(nblk)
    blk_first = blk_end - nblk
    n_blocks = (tk + N_EXPERTS * (EXPERT_ROWS - 1) + EXPERT_ROWS - 1) // EXPERT_ROWS
    blk = jnp.arange(n_blocks, dtype=jnp.int32)
    blk_e = jnp.minimum(jnp.sum(blk[:, None] >= blk_end[None, :], axis=1), N_EXPERTS - 1).astype(jnp.int32)
    own = blk_e[:, None] == jnp.arange(N_EXPERTS, dtype=jnp.int32)[None, :]
    pick = lambda v: jnp.sum(jnp.where(own, v[None, :], 0), axis=1)
    j = blk - pick(blk_first)
    blk_valid = jnp.clip(pick(counts) - j * EXPERT_ROWS, 0, EXPERT_ROWS).astype(jnp.int32)
    spare = N_EXPERTS * blocks_per_expert
    blk_row = jnp.where(blk_valid > 0, blk_e * blocks_per_expert + j, spare).astype(jnp.int32)
    experts = jnp.arange(N_EXPERTS, dtype=jnp.int32)
    used = counts > 0
    later = jnp.logical_and(experts[None, :] > experts[:, None], used[None, :])
    nxt = jnp.min(jnp.where(later, experts[None, :], N_EXPERTS), axis=1)
    nxt = jnp.where(nxt < N_EXPERTS, nxt, -1)
    ordinal = jnp.cumsum(used.astype(jnp.int32)) - 1
    blk_first = jnp.logical_and(j == 0, blk_valid > 0).astype(jnp.int32)
    blk_next = pick(nxt).astype(jnp.int32)
    blk_slot = (pick(ordinal) % 2).astype(jnp.int32)
    return blk_e, blk_row, blk_valid, blk_first, blk_next, blk_slot


def kernel(x, attn_norm, w_in, a_q_norm, a_k_norm, a_sinks, t5_bias, b_q_a_norm, b_w_q_b, b_kv_a_norm, b_w_kv_b,
           b_q_norm, b_k_norm, c_q_norm, c_k_norm, c_rel_bias, a_out_norm, b_out_norm, c_out_norm, w_out, ffn_norm,
           w_router, b_router, w_gate_up, b_gate_up, w_down, b_down):
    b, s, d = x.shape
    t = b * s
    depth = w_in.shape[0]
    a_bias = _a_bias(t5_bias)
    cos_t, sin_a, sin_b = _rope_tables(s)
    x2 = x.reshape(t, d)
    head_gain = lambda l: _head_gain_row(a_q_norm[l], a_k_norm[l], c_q_norm[l], c_k_norm[l])
    proj = _in_proj(x2, attn_norm[0].reshape(1, d), _regroup_w_in(w_in[0]), head_gain(0))
    for l in range(depth):
        proj3 = proj.reshape(b, s, PROJ_WIDTH)
        o_a = _attn_a(proj3, a_bias, a_sinks[l])
        kv_w = b_w_kv_b[l].reshape(B_KV_RANK, B_HEADS, B_NOPE_DIM + B_V_DIM)
        wv = kv_w[:, :, B_NOPE_DIM:].reshape(B_KV_RANK, B_HEADS * B_V_DIM).astype(jnp.bfloat16)
        qb, kb, vb = _latent_prep(
            proj, b_q_a_norm[l].reshape(1, -1), b_kv_a_norm[l].reshape(1, -1),
            _pad_heads(b_w_q_b[l], B_QK_DIM, B_QK_DIM), _pad_heads(b_w_kv_b[l], B_NOPE_DIM + B_V_DIM, B_NOPE_DIM), wv,
            _pad_lanes(b_q_norm[l], 0, B_HEAD_PAD), _pad_lanes(b_k_norm[l], 0, B_HEAD_PAD), cos_t, sin_a, sin_b, s)
        o_b = _attn_b(qb.reshape(b, s, -1), kb.reshape(b, s, -1), vb.reshape(b, s, -1))
        o_c = _attn_c(proj3, _c_bias(c_rel_bias[l]))
        wr = jnp.pad(w_router[l], ((0, 0), (0, ROUTER_PAD - N_EXPERTS)))
        wr_hi = wr.astype(jnp.bfloat16)
        wr = jnp.stack([wr_hi, (wr - wr_hi.astype(jnp.float32)).astype(jnp.bfloat16)])
        br = jnp.pad(b_router[l], (0, ROUTER_PAD - N_EXPERTS)).reshape(1, ROUTER_PAD)
        n_rows = N_EXPERTS * t + SPARE_ROWS
        x2, x_rows, dest, gates, counts = _out_proj(
            o_a.reshape(t, -1), o_b.reshape(t, -1), o_c.reshape(t, -1), x2,
            a_out_norm[l].reshape(1, -1), b_out_norm[l].reshape(1, -1), c_out_norm[l].reshape(1, -1),
            w_out[l].astype(jnp.bfloat16), ffn_norm[l].reshape(1, -1), wr, br, n_rows)
        dest_tiles = dest[:, :TOP_K].reshape(t // TOKEN_TILE, TOKEN_TILE * TOP_K)
        table = _block_table(counts[0, :N_EXPERTS], t)
        y_rows = _moe_ffn(l, table, x_rows, w_gate_up, b_gate_up, w_down, b_down)
        if l + 1 < depth:
            x2, proj = _combine_call(dest_tiles, y_rows, x2, gates, attn_norm[l + 1].reshape(1, d),
                                     _regroup_w_in(w_in[l + 1]), head_gain(l + 1))
        else:
            x2 = _combine_call(dest_tiles, y_rows, x2, gates)
    return x2.reshape(b, s, d)
```

```python
import functools
import math

import jax
import jax.numpy as jnp
import numpy as np
from jax import lax
from jax.experimental import pallas as pl
from jax.experimental.pallas import tpu as pltpu

D_MODEL = 1024
CHUNK = 64
HEAD_DIM = 64
EPS = 1e-6
NEG_INF = -1e30
A_Q_HEADS = 8
A_KV_HEADS = 2
A_GROUP = 4
A_PREV_CHUNKS = 2
T5_BUCKETS = 32
T5_MAX_DIST = 128
B_HEADS = 4
B_Q_RANK = 256
B_KV_RANK = 128
B_NOPE_DIM = 64
B_ROPE_DIM = 32
B_QK_DIM = 96
B_V_DIM = 64
ROPE_THETA = 10000.0
C_HEADS = 4
C_PREV_CHUNKS = 8
C_REL_CLIP = 256
N_EXPERTS = 32
TOP_K = 4
D_EXPERT = 1024
SWIGLU_LIMIT = 7.0
SWIGLU_ALPHA = 1.702

SEC_AQ = 0
SEC_AKV = 512
SEC_BCQ = 768
SEC_BCKV = 1024
SEC_BKPE = 1152
SEC_CQ = 1280
SEC_CK = 1536
SEC_CV = 1792
PROJ_WIDTH = 2048
HEAD_NORM_SLABS = tuple(sec // 128 + j for sec, n in ((SEC_AQ, 4), (SEC_AKV, 1), (SEC_CQ, 2), (SEC_CK, 2))
                        for j in range(n))
B_HEAD_PAD = 128

Q_TILE = 128
A_TILES_PER_STEP = 1
C_TILES_PER_STEP = 4
A_STEP_ROWS = Q_TILE * A_TILES_PER_STEP
C_STEP_ROWS = Q_TILE * C_TILES_PER_STEP
A_BAND = Q_TILE + A_PREV_CHUNKS * CHUNK
C_BAND = Q_TILE + C_PREV_CHUNKS * CHUNK
A_PAD = A_PREV_CHUNKS * CHUNK
C_PAD = C_PREV_CHUNKS * CHUNK
B_TILE = 512
ROW_TILE = 512
EXPERT_ROWS = 256
ROUTER_PAD = 128
TOKEN_TILE = 512
TILE_SLOTS = TOKEN_TILE * TOP_K
ROW_BUFFERS = 3
SPARE_ROWS = TILE_SLOTS
LANE_TILES = D_MODEL // 128
V7X_VMEM_BYTES = 64 * 1024 * 1024
VMEM_LIMIT = V7X_VMEM_BYTES * 3 // 4


def _rms(x, gain):
    return x * lax.rsqrt(jnp.mean(x * x, axis=-1, keepdims=True) + EPS) * gain


def _nt_dot(a, b):
    return lax.dot_general(a, b, (((1,), (1,)), ((), ())), preferred_element_type=jnp.float32)


def _load_row_tiles(ref, first_row, n_rows, c):
    return ref[pl.ds(first_row * LANE_TILES + c, n_rows, stride=LANE_TILES), :]


def _store_row_tiles(ref, value):
    for c in range(LANE_TILES):
        ref[pl.ds(c, value.shape[0], stride=LANE_TILES), :] = value[:, c * 128:(c + 1) * 128]


def _row_tile(ref, row):
    return ref.at[pl.ds(pl.multiple_of(row * LANE_TILES, LANE_TILES), LANE_TILES)]


def _index_slot(idx_smem, s):
    return idx_smem.at[pl.ds(pl.multiple_of(s * TILE_SLOTS, TILE_SLOTS), TILE_SLOTS)]


def _project(h, w_ref, hg_ref, o_ref):
    li = lax.broadcasted_iota(jnp.int32, (256, 128), 0) % 128 // HEAD_DIM
    lj = lax.broadcasted_iota(jnp.int32, (256, 128), 1) // HEAD_DIM
    same_head = jnp.where(li == lj, 1.0, 0.0).astype(jnp.bfloat16)
    for n in range(PROJ_WIDTH // 512):
        acc = jnp.dot(h, w_ref[:, n * 512:(n + 1) * 512], preferred_element_type=jnp.float32)
        slabs = []
        for j in range(4):
            slab = n * 4 + j
            a = acc[:, j * 128:(j + 1) * 128]
            if slab in HEAD_NORM_SLABS:
                sq = a * a
                hi = sq.astype(jnp.bfloat16)
                lo = (sq - hi.astype(jnp.float32)).astype(jnp.bfloat16)
                ss = jnp.dot(jnp.concatenate([hi, lo], axis=1), same_head, preferred_element_type=jnp.float32)
                a = a * lax.rsqrt(ss * (1.0 / HEAD_DIM) + EPS) * hg_ref[:, slab * 128:(slab + 1) * 128]
            slabs.append(a.astype(o_ref.dtype))
        o_ref[:, n * 512:(n + 1) * 512] = jnp.concatenate(slabs, axis=1)


def _in_proj_kernel(x_ref, g_ref, w_ref, hg_ref, o_ref):
    _project(_rms(x_ref[...], g_ref[...]).astype(jnp.bfloat16), w_ref, hg_ref, o_ref)


def _in_proj(x2, gain, w, head_gain):
    t = x2.shape[0]
    return pl.pallas_call(
        _in_proj_kernel,
        grid=(t // ROW_TILE,),
        in_specs=[
            pl.BlockSpec((ROW_TILE, D_MODEL), lambda i: (i, 0)),
            pl.BlockSpec((1, D_MODEL), lambda i: (0, 0)),
            pl.BlockSpec((D_MODEL, PROJ_WIDTH), lambda i: (0, 0)),
            pl.BlockSpec((1, PROJ_WIDTH), lambda i: (0, 0)),
        ],
        out_specs=pl.BlockSpec((ROW_TILE, PROJ_WIDTH), lambda i: (i, 0)),
        out_shape=jax.ShapeDtypeStruct((t, PROJ_WIDTH), jnp.bfloat16),
        compiler_params=pltpu.CompilerParams(dimension_semantics=("arbitrary",), vmem_limit_bytes=VMEM_LIMIT),
        name="in_proj",
    )(x2, gain, w, head_gain)


def _half_lanes(x, parity, fill=0.0):
    lane = lax.broadcasted_iota(jnp.int32, x.shape, x.ndim - 1) % 128
    keep = lane < HEAD_DIM if parity == 0 else lane >= HEAD_DIM
    return jnp.where(keep, x, jnp.full_like(x, fill))


def _tn_dot(a, b):
    return lax.dot_general(a, b, (((0,), (0,)), ((), ())), preferred_element_type=jnp.float32)


def _softmax_pv(s, key_ok, v_aug, sink=None, value_half=0):
    if key_ok is not None:
        s = jnp.where(key_ok, s, NEG_INF)
    m = jnp.max(s, axis=0, keepdims=True)
    if sink is not None:
        m = jnp.maximum(m, sink)
    e = jnp.exp(s - m).astype(jnp.bfloat16)
    o = _tn_dot(e, v_aug)
    if sink is not None:
        row = lax.broadcasted_iota(jnp.int32, (8, s.shape[1]), 0)
        e_sink = jnp.where(row == 0, jnp.exp(sink - m), 0.0).astype(jnp.bfloat16)
        first = lax.broadcasted_iota(jnp.int32, (8, 128), 0) == 0
        sel = _half_lanes(jnp.where(first, 1.0, 0.0), 1 - value_half).astype(jnp.bfloat16)
        o = o + _tn_dot(e_sink, sel)
    return o


def _merge_halves(halves):
    lane = lax.broadcasted_iota(jnp.int32, halves[0].shape, 1)
    normed = [o / pltpu.roll(o, HEAD_DIM, 1) for o in halves]
    return jnp.where(lane < HEAD_DIM, normed[0], normed[1])


def _attn_a_kernel(q_ref, kv_ref, bias_ref, sink_ref, o_ref, k_pad, v_pad):
    i = pl.program_id(1)
    seq = kv_ref.shape[1]

    @pl.when(i == 0)
    def _():
        for n in range(2 * A_KV_HEADS):
            k_pad[n, 0:A_PAD, :] = jnp.zeros((A_PAD, 128), k_pad.dtype)
            v_pad[n, 0:A_PAD, :] = jnp.zeros((A_PAD, 128), v_pad.dtype)

        def fill(r, carry):
            rows = pl.ds(pl.multiple_of(r * 256, 256), 256)
            dst = pl.ds(pl.multiple_of(r * 256 + A_PAD, 128), 256)
            kv = kv_ref[0, rows, :].astype(jnp.float32)
            for src, pad, other in ((kv[:, 0:128], k_pad, 0.0), (kv[:, 128:256], v_pad, 1.0)):
                swapped = pltpu.roll(src, HEAD_DIM, 1)
                pad[0, dst, :] = _half_lanes(src, 0, other).astype(pad.dtype)
                pad[1, dst, :] = _half_lanes(swapped, 1, other).astype(pad.dtype)
                pad[2, dst, :] = _half_lanes(swapped, 0, other).astype(pad.dtype)
                pad[3, dst, :] = _half_lanes(src, 1, other).astype(pad.dtype)
            return carry

        lax.fori_loop(0, seq // 256, fill, 0)

    first_slab = lax.broadcasted_iota(jnp.int32, (1, 2 * Q_TILE), 1) < Q_TILE

    def tiles(front):
        for sub in range(A_TILES_PER_STEP):
            rows = slice(sub * Q_TILE, (sub + 1) * Q_TILE)
            start = pl.multiple_of((i * A_TILES_PER_STEP + sub) * Q_TILE, Q_TILE)
            band = pl.ds(start, A_BAND)
            key_ok = None
            if front:
                key_ok = lax.broadcasted_iota(jnp.int32, (A_BAND, 2 * Q_TILE), 0) + (start - A_PAD) >= 0
            for hk in range(A_KV_HEADS):
                q2 = jnp.concatenate([q_ref[0, rows, (2 * hk) * 128:(2 * hk + 1) * 128],
                                      q_ref[0, rows, (2 * hk + 1) * 128:(2 * hk + 2) * 128]], axis=0)
                halves = []
                for parity in range(2):
                    sink = jnp.where(first_slab, sink_ref[4 * hk + parity], sink_ref[4 * hk + 2 + parity])
                    s = _nt_dot(k_pad[2 * hk + parity, band, :], q2) + bias_ref[2 * hk + parity]
                    halves.append(_softmax_pv(s, key_ok, v_pad[2 * hk + parity, band, :], sink, parity))
                out = _merge_halves(halves)
                o_ref[0, rows, (2 * hk) * 128:(2 * hk + 1) * 128] = out[0:Q_TILE].astype(o_ref.dtype)
                o_ref[0, rows, (2 * hk + 1) * 128:(2 * hk + 2) * 128] = out[Q_TILE:2 * Q_TILE].astype(o_ref.dtype)

    front_steps = -(-A_PAD // A_STEP_ROWS)
    pl.when(i < front_steps)(functools.partial(tiles, True))
    pl.when(i >= front_steps)(functools.partial(tiles, False))


def _attn_a(proj3, bias, sinks):
    b, s, _ = proj3.shape
    return pl.pallas_call(
        _attn_a_kernel,
        grid=(b, s // A_STEP_ROWS),
        in_specs=[
            pl.BlockSpec((1, A_STEP_ROWS, 512), lambda bi, i: (bi, i, SEC_AQ // 512)),
            pl.BlockSpec((1, s, 256), lambda bi, i: (bi, 0, SEC_AKV // 256)),
            pl.BlockSpec((2 * A_KV_HEADS, 2 * Q_TILE, A_BAND), lambda bi, i: (0, 0, 0)),
            pl.BlockSpec(memory_space=pltpu.SMEM),
        ],
        out_specs=pl.BlockSpec((1, A_STEP_ROWS, 512), lambda bi, i: (bi, i, 0)),
        out_shape=jax.ShapeDtypeStruct((b, s, 512), jnp.bfloat16),
        scratch_shapes=[pltpu.VMEM((2 * A_KV_HEADS, s + A_PAD, 128), jnp.bfloat16),
                        pltpu.VMEM((2 * A_KV_HEADS, s + A_PAD, 128), jnp.bfloat16)],
        compiler_params=pltpu.CompilerParams(dimension_semantics=("arbitrary", "arbitrary"),
                                             vmem_limit_bytes=VMEM_LIMIT),
        name="attn_a",
    )(proj3, proj3, bias, sinks)


def _latent_prep_kernel(cq_ref, ckv_ref, kpe_ref, qag_ref, kvag_ref, wq_ref, wk_ref, wv_ref,
                        qg_ref, kg_ref, cos_ref, sa_ref, sb_ref, qo_ref, ko_ref, vo_ref):
    lane = lax.broadcasted_iota(jnp.int32, (ROW_TILE, B_HEAD_PAD), 1)
    is_nope = lane < B_NOPE_DIM
    is_rope = jnp.logical_and(lane >= B_NOPE_DIM, lane < B_QK_DIM)
    cos = cos_ref[...]
    sa = sa_ref[...]
    sb = sb_ref[...]

    def split_norm(x, gain):
        sq = x * x
        nope = jnp.sum(jnp.where(is_nope, sq, 0.0), axis=-1, keepdims=True) * (1.0 / B_NOPE_DIM)
        rope = jnp.sum(jnp.where(is_rope, sq, 0.0), axis=-1, keepdims=True) * (1.0 / B_ROPE_DIM)
        r = jnp.where(is_nope, lax.rsqrt(nope + EPS), lax.rsqrt(rope + EPS))
        return x * r * gain

    def rotate(x):
        return x * cos + pltpu.roll(x, 112, 1) * sa + pltpu.roll(x, 16, 1) * sb

    cq = _rms(cq_ref[...].astype(jnp.float32), qag_ref[...]).astype(jnp.bfloat16)
    ckv = _rms(ckv_ref[...].astype(jnp.float32), kvag_ref[...]).astype(jnp.bfloat16)
    kpe = kpe_ref[...].astype(jnp.float32)
    kpe_n = rotate(split_norm(kpe, kg_ref[...]))
    kpe_n = jnp.where(is_rope, kpe_n, 0.0)
    v = jnp.dot(ckv, wv_ref[...], preferred_element_type=jnp.float32)
    for h in range(B_HEADS):
        pair = v[:, (h // 2) * 128:(h // 2 + 1) * 128]
        vo_ref[:, h * 128:(h + 1) * 128] = _half_lanes(pair, h % 2, 1.0).astype(vo_ref.dtype)
    qscale = B_QK_DIM ** -0.5
    for h in range(B_HEADS):
        cols = slice(h * B_HEAD_PAD, (h + 1) * B_HEAD_PAD)
        qh = jnp.dot(cq, wq_ref[:, cols], preferred_element_type=jnp.float32)
        qh = rotate(split_norm(qh, qg_ref[...])) * qscale
        qo_ref[:, cols] = qh.astype(qo_ref.dtype)
        kh = jnp.dot(ckv, wk_ref[:, cols], preferred_element_type=jnp.float32)
        kh = jnp.where(is_nope, split_norm(kh, kg_ref[...]), 0.0) + kpe_n
        ko_ref[:, cols] = kh.astype(ko_ref.dtype)


def _latent_prep(proj, q_a_gain, kv_a_gain, wq, wk, wv, q_gain, k_gain, cos, sa, sb, seq):
    t = proj.shape[0]
    pos_blocks = seq // ROW_TILE
    full = lambda shape: pl.BlockSpec(shape, lambda i: (0, 0))
    tab = pl.BlockSpec((ROW_TILE, B_HEAD_PAD), lambda i: (i % pos_blocks, 0))
    return pl.pallas_call(
        _latent_prep_kernel,
        grid=(t // ROW_TILE,),
        in_specs=[
            pl.BlockSpec((ROW_TILE, 256), lambda i: (i, SEC_BCQ // 256)),
            pl.BlockSpec((ROW_TILE, 128), lambda i: (i, SEC_BCKV // 128)),
            pl.BlockSpec((ROW_TILE, 128), lambda i: (i, SEC_BKPE // 128)),
            full((1, B_Q_RANK)), full((1, B_KV_RANK)),
            full((B_Q_RANK, B_HEADS * B_HEAD_PAD)), full((B_KV_RANK, B_HEADS * B_HEAD_PAD)),
            full((B_KV_RANK, B_HEADS * B_V_DIM)),
            full((1, B_HEAD_PAD)), full((1, B_HEAD_PAD)),
            tab, tab, tab,
        ],
        out_specs=[
            pl.BlockSpec((ROW_TILE, B_HEADS * B_HEAD_PAD), lambda i: (i, 0)),
            pl.BlockSpec((ROW_TILE, B_HEADS * B_HEAD_PAD), lambda i: (i, 0)),
            pl.BlockSpec((ROW_TILE, B_HEADS * B_HEAD_PAD), lambda i: (i, 0)),
        ],
        out_shape=[
            jax.ShapeDtypeStruct((t, B_HEADS * B_HEAD_PAD), jnp.bfloat16),
            jax.ShapeDtypeStruct((t, B_HEADS * B_HEAD_PAD), jnp.bfloat16),
            jax.ShapeDtypeStruct((t, B_HEADS * B_HEAD_PAD), jnp.bfloat16),
        ],
        compiler_params=pltpu.CompilerParams(dimension_semantics=("arbitrary",), vmem_limit_bytes=VMEM_LIMIT),
        name="latent_prep",
    )(proj, proj, proj, q_a_gain, kv_a_gain, wq, wk, wv, q_gain, k_gain, cos, sa, sb)


def _attn_b_kernel(q_ref, k_ref, v_ref, o_ref):
    i = pl.program_id(1)
    query_chunk = lax.broadcasted_iota(jnp.int32, (B_TILE, B_TILE), 0) // CHUNK
    key_chunk = lax.broadcasted_iota(jnp.int32, (B_TILE, B_TILE), 1) // CHUNK
    diag_ok = key_chunk <= query_chunk

    def step(j, carry, masked):
        rows = pl.ds(pl.multiple_of(j * B_TILE, B_TILE), B_TILE)
        out = []
        for h in range(B_HEADS):
            m, acc = carry[h]
            head = slice(h * B_HEAD_PAD, (h + 1) * B_HEAD_PAD)
            s = _nt_dot(q_ref[0, :, head], k_ref[0, rows, head])
            if masked:
                s = jnp.where(diag_ok, s, NEG_INF)
            m_new = jnp.maximum(m, jnp.max(s, axis=-1, keepdims=True))
            e = jnp.exp(s - m_new).astype(jnp.bfloat16)
            acc = jnp.exp(m - m_new) * acc + jnp.dot(e, v_ref[0, rows, head], preferred_element_type=jnp.float32)
            out.append((m_new, acc))
        return tuple(out)

    init = tuple((jnp.full((B_TILE, 1), NEG_INF, jnp.float32), jnp.zeros((B_TILE, B_HEAD_PAD), jnp.float32))
                 for _ in range(B_HEADS))
    carry = lax.fori_loop(0, i, lambda j, c: step(j, c, False), init)
    carry = step(i, carry, True)
    for pair in range(B_HEADS // 2):
        halves = [carry[2 * pair + parity][1] for parity in range(2)]
        o_ref[0, :, pair * 128:(pair + 1) * 128] = _merge_halves(halves).astype(o_ref.dtype)


def _attn_b(qb, kb, vb):
    b, s, _ = qb.shape
    return pl.pallas_call(
        _attn_b_kernel,
        grid=(b, s // B_TILE),
        in_specs=[
            pl.BlockSpec((1, B_TILE, B_HEADS * B_HEAD_PAD), lambda bi, i: (bi, i, 0)),
            pl.BlockSpec((1, s, B_HEADS * B_HEAD_PAD), lambda bi, i: (bi, 0, 0)),
            pl.BlockSpec((1, s, B_HEADS * B_HEAD_PAD), lambda bi, i: (bi, 0, 0)),
        ],
        out_specs=pl.BlockSpec((1, B_TILE, B_HEADS * B_V_DIM), lambda bi, i: (bi, i, 0)),
        out_shape=jax.ShapeDtypeStruct((b, s, B_HEADS * B_V_DIM), jnp.bfloat16),
        compiler_params=pltpu.CompilerParams(dimension_semantics=("arbitrary", "arbitrary"),
                                             vmem_limit_bytes=VMEM_LIMIT),
        name="attn_b",
    )(qb, kb, vb)


def _attn_c_kernel(q_ref, k_ref, v_ref, bias_ref, o_ref, k_pad, v_pad):
    i = pl.program_id(1)
    seq = k_ref.shape[1]
    width = C_HEADS * HEAD_DIM

    @pl.when(i == 0)
    def _():
        for parity in range(2):
            k_pad[parity, 0:C_PAD, :] = jnp.zeros((C_PAD, width), k_pad.dtype)
            v_pad[parity, 0:C_PAD, :] = jnp.zeros((C_PAD, width), v_pad.dtype)

        def fill(r, carry):
            rows = pl.ds(pl.multiple_of(r * 256, 256), 256)
            dst = pl.ds(pl.multiple_of(r * 256 + C_PAD, 256), 256)
            k = k_ref[0, rows, :]
            v = v_ref[0, rows, :]
            for parity in range(2):
                k_pad[parity, dst, :] = _half_lanes(k, parity)
                v_pad[parity, dst, :] = _half_lanes(v, parity, fill=1.0)
            return carry

        lax.fori_loop(0, seq // 256, fill, 0)

    def tiles(front):
        for sub in range(C_TILES_PER_STEP):
            rows = slice(sub * Q_TILE, (sub + 1) * Q_TILE)
            start = pl.multiple_of((i * C_TILES_PER_STEP + sub) * Q_TILE, Q_TILE)
            band = pl.ds(start, C_BAND)
            key_ok = None
            if front:
                key_ok = lax.broadcasted_iota(jnp.int32, (C_BAND, Q_TILE), 0) + (start - C_PAD) >= 0
            for slab in range(width // 128):
                lanes = slice(slab * 128, (slab + 1) * 128)
                q = q_ref[0, rows, lanes]
                halves = []
                for parity in range(2):
                    s = _nt_dot(k_pad[parity, band, lanes], q) + bias_ref[2 * slab + parity]
                    halves.append(_softmax_pv(s, key_ok, v_pad[parity, band, lanes]))
                o_ref[0, rows, lanes] = _merge_halves(halves).astype(o_ref.dtype)

    front_steps = -(-C_PAD // C_STEP_ROWS)
    pl.when(i < front_steps)(functools.partial(tiles, True))
    pl.when(i >= front_steps)(functools.partial(tiles, False))


def _attn_c(proj3, bias):
    b, s, _ = proj3.shape
    width = C_HEADS * HEAD_DIM
    return pl.pallas_call(
        _attn_c_kernel,
        grid=(b, s // C_STEP_ROWS),
        in_specs=[
            pl.BlockSpec((1, C_STEP_ROWS, width), lambda bi, i: (bi, i, SEC_CQ // width)),
            pl.BlockSpec((1, s, width), lambda bi, i: (bi, 0, SEC_CK // width)),
            pl.BlockSpec((1, s, width), lambda bi, i: (bi, 0, SEC_CV // width)),
            pl.BlockSpec((C_HEADS, C_BAND, Q_TILE), lambda bi, i: (0, 0, 0)),
        ],
        out_specs=pl.BlockSpec((1, C_STEP_ROWS, width), lambda bi, i: (bi, i, 0)),
        out_shape=jax.ShapeDtypeStruct((b, s, width), jnp.bfloat16),
        scratch_shapes=[pltpu.VMEM((2, s + C_PAD, width), jnp.bfloat16),
                        pltpu.VMEM((2, s + C_PAD, width), jnp.bfloat16)],
        compiler_params=pltpu.CompilerParams(dimension_semantics=("arbitrary", "arbitrary"),
                                             vmem_limit_bytes=VMEM_LIMIT),
        name="attn_c",
    )(proj3, proj3, proj3, bias)


def _out_proj_kernel(oa_ref, ob_ref, oc_ref, x_ref, ga_ref, gb_ref, gc_ref, w_ref, gf_ref, wr_ref, br_ref,
                     xo_ref, x_hbm, dest_ref, tg_ref, cnt_ref, count, hbuf, dvm, idx_smem, isem, dsem):
    i = pl.program_id(0)
    nt = pl.num_programs(0)
    cap = nt * TOKEN_TILE
    slot = i % ROW_BUFFERS
    prev = (i + ROW_BUFFERS - 1) % ROW_BUFFERS
    spare_row = x_hbm.shape[0] // LANE_TILES - TILE_SLOTS

    def index_copies(s):
        return [pltpu.make_async_copy(
            dvm.at[s, k], idx_smem.at[pl.ds(pl.multiple_of(s * TILE_SLOTS + k * TOKEN_TILE, TOKEN_TILE), TOKEN_TILE)],
            isem.at[s]) for k in range(TOP_K)]

    def row_copy(s, base, r, k):
        dst = idx_smem[base + k * TOKEN_TILE + r]
        return pltpu.make_async_copy(_row_tile(hbuf.at[s], r), _row_tile(x_hbm, dst), dsem.at[s])

    def wait_rows(s):
        for _ in range(TOP_K):
            pltpu.make_async_copy(hbuf.at[s], x_hbm.at[pl.ds(0, TOKEN_TILE * LANE_TILES)], dsem.at[s]).wait()

    @pl.when(i == 0)
    def _():
        count[...] = jnp.zeros_like(count)
        hbuf[ROW_BUFFERS - 1] = jnp.zeros(hbuf.shape[1:], hbuf.dtype)

        def fill(j, carry):
            idx_smem[(ROW_BUFFERS - 1) * TILE_SLOTS + j] = spare_row + j
            return carry

        lax.fori_loop(0, TILE_SLOTS, fill, 0)

    @pl.when(i >= ROW_BUFFERS - 1)
    def _():
        wait_rows(slot)

    @pl.when(i >= 1)
    def _():
        for copy in index_copies(prev):
            copy.wait()

    prev_base = prev * TILE_SLOTS
    for r in range(TOKEN_TILE):
        for k in range(TOP_K):
            row_copy(prev, prev_base, r, k).start(priority=k % 2)

    acc = x_ref[...]
    parts = ((oa_ref, ga_ref, 0, 512), (ob_ref, gb_ref, 512, 256), (oc_ref, gc_ref, 768, 256))
    for ref, g_ref, off, width in parts:
        o = _rms(ref[...].astype(jnp.float32), g_ref[...]).astype(jnp.bfloat16)
        acc = acc + jnp.dot(o, w_ref[off:off + width, :], preferred_element_type=jnp.float32)
    xo_ref[...] = acc
    h = _rms(acc, gf_ref[...])
    _store_row_tiles(hbuf.at[slot], h)
    h_hi = h.astype(jnp.bfloat16)
    h_lo = (h - h_hi.astype(jnp.float32)).astype(jnp.bfloat16)
    logits = (jnp.dot(h_hi, wr_ref[0], preferred_element_type=jnp.float32)
              + jnp.dot(h_lo, wr_ref[0], preferred_element_type=jnp.float32)
              + jnp.dot(h_hi, wr_ref[1], preferred_element_type=jnp.float32)) + br_ref[...]
    lane = lax.broadcasted_iota(jnp.int32, logits.shape, 1)
    work = jnp.where(lane < N_EXPERTS, logits, -jnp.inf)
    gate_out = jnp.zeros(logits.shape, jnp.float32)
    denom = jnp.zeros((logits.shape[0], 1), jnp.float32)
    top = None
    picks = []
    for k in range(TOP_K):
        m = jnp.max(work, axis=-1, keepdims=True)
        idx = jnp.min(jnp.where(work == m, lane, ROUTER_PAD), axis=-1, keepdims=True)
        pick = lane == idx
        work = jnp.where(pick, -jnp.inf, work)
        top = m if top is None else top
        e = jnp.exp(m - top)
        denom = denom + e
        gate_out = jnp.where(lane == k, e, gate_out)
        picks.append((idx, pick))
    tg_ref[...] = gate_out / denom
    chosen = jnp.zeros(logits.shape, jnp.float32)
    for _, pick in picks:
        chosen = jnp.where(pick, 1.0, chosen)
    r_i = lax.broadcasted_iota(jnp.int32, (TOKEN_TILE, TOKEN_TILE), 0)
    c_i = lax.broadcasted_iota(jnp.int32, (TOKEN_TILE, TOKEN_TILE), 1)
    earlier = jnp.where(c_i < r_i, 1.0, 0.0).astype(jnp.bfloat16)
    before = jnp.dot(earlier, chosen.astype(jnp.bfloat16), preferred_element_type=jnp.float32) + count[...]
    dest_out = jnp.zeros(logits.shape, jnp.int32)
    for k, (idx, pick) in enumerate(picks):
        rank = jnp.sum(jnp.where(pick, before, 0.0), axis=-1, keepdims=True).astype(jnp.int32)
        dest_out = jnp.where(lane == k, idx * cap + rank, dest_out)
    dest_ref[...] = dest_out
    count[...] = count[...] + jnp.sum(chosen, axis=0, keepdims=True)
    cnt_ref[...] = count[...].astype(jnp.int32)
    dvm[slot] = dest_out.T[0:8, :]
    for copy in index_copies(slot):
        copy.start()

    @pl.when(i == nt - 1)
    def _():
        for copy in index_copies(slot):
            copy.wait()
        base = slot * TILE_SLOTS

        def body(r, carry):
            for k in range(TOP_K):
                row_copy(slot, base, r, k).start(priority=k % 2)
            return carry

        lax.fori_loop(0, TOKEN_TILE, body, 0, unroll=4)
        for s in range(ROW_BUFFERS):
            wait_rows(s)


def _out_proj(oa, ob, oc, x2, ga, gb, gc, w, gf, wr, br, n_rows):
    t = x2.shape[0]
    assert t % TOKEN_TILE == 0 and t // TOKEN_TILE >= ROW_BUFFERS and n_rows == N_EXPERTS * t + SPARE_ROWS
    row = lambda width: pl.BlockSpec((TOKEN_TILE, width), lambda i: (i, 0))
    full = lambda shape: pl.BlockSpec(shape, lambda i: (0, 0))
    return pl.pallas_call(
        _out_proj_kernel,
        grid=(t // TOKEN_TILE,),
        in_specs=[row(512), row(256), row(256), row(D_MODEL), full((1, 512)), full((1, 256)), full((1, 256)),
                  full((D_MODEL, D_MODEL)), full((1, D_MODEL)),
                  pl.BlockSpec((2, D_MODEL, ROUTER_PAD), lambda i: (0, 0, 0)), full((1, ROUTER_PAD))],
        out_specs=[row(D_MODEL), pl.BlockSpec(memory_space=pl.ANY),
                   row(ROUTER_PAD), row(ROUTER_PAD), full((1, ROUTER_PAD))],
        out_shape=[jax.ShapeDtypeStruct((t, D_MODEL), jnp.float32),
                   jax.ShapeDtypeStruct((n_rows * LANE_TILES, 128), jnp.float32),
                   jax.ShapeDtypeStruct((t, ROUTER_PAD), jnp.int32),
                   jax.ShapeDtypeStruct((t, ROUTER_PAD), jnp.float32),
                   jax.ShapeDtypeStruct((1, ROUTER_PAD), jnp.int32)],
        scratch_shapes=[pltpu.VMEM((1, ROUTER_PAD), jnp.float32),
                        pltpu.VMEM((ROW_BUFFERS, TOKEN_TILE * LANE_TILES, 128), jnp.float32),
                        pltpu.VMEM((ROW_BUFFERS, 8, TOKEN_TILE), jnp.int32),
                        pltpu.SMEM((ROW_BUFFERS * TILE_SLOTS,), jnp.int32),
                        pltpu.SemaphoreType.DMA((ROW_BUFFERS,)), pltpu.SemaphoreType.DMA((ROW_BUFFERS,))],
        compiler_params=pltpu.CompilerParams(dimension_semantics=("arbitrary",), vmem_limit_bytes=VMEM_LIMIT),
        name="out_proj",
    )(oa, ob, oc, x2, ga, gb, gc, w, gf, wr, br)


def _moe_ffn_kernel(blk_e_ref, blk_row_ref, blk_valid_ref, blk_first_ref, blk_next_ref, blk_slot_ref,
                    x_ref, wgu_hbm, bgu_ref, wd_hbm, bd_ref, y_ref, wgu_f32, wd_f32, wgu_bf, wd_bf, gsem, dsem,
                    *, layer):
    i = pl.program_id(0)
    n_valid = blk_valid_ref[i]

    def weight_copies(e, s):
        return (pltpu.make_async_copy(wgu_hbm.at[layer, e], wgu_f32.at[s], gsem.at[s]),
                pltpu.make_async_copy(wd_hbm.at[layer, e], wd_f32.at[s], dsem.at[s]))

    @pl.when(i == 0)
    def _():
        for copy in weight_copies(blk_e_ref[0], blk_slot_ref[0]):
            copy.start()

    @pl.when(blk_first_ref[i] == 1)
    def _():
        s = blk_slot_ref[i]
        for copy in weight_copies(blk_e_ref[i], s):
            copy.wait()
        nxt = blk_next_ref[i]

        @pl.when(nxt >= 0)
        def _():
            for copy in weight_copies(nxt, 1 - s):
                copy.start()

        for c in range(4):
            cols = slice(c * 512, (c + 1) * 512)
            wgu_bf[:, cols] = wgu_f32[s, :, cols].astype(jnp.bfloat16)
        for c in range(2):
            cols = slice(c * 512, (c + 1) * 512)
            wd_bf[:, cols] = wd_f32[s, :, cols].astype(jnp.bfloat16)

    @pl.when(n_valid > 0)
    def _():
        live = lax.broadcasted_iota(jnp.int32, (EXPERT_ROWS, 128), 0) < n_valid
        x = jnp.concatenate([jnp.where(live, _load_row_tiles(x_ref, 0, EXPERT_ROWS, c), 0.0).astype(jnp.bfloat16)
                             for c in range(LANE_TILES)], axis=1)
        glu = jnp.dot(x, wgu_bf[:, 0:D_EXPERT], preferred_element_type=jnp.float32) + bgu_ref[0, 0, :, 0:D_EXPERT]
        lin = jnp.dot(x, wgu_bf[:, D_EXPERT:], preferred_element_type=jnp.float32) + bgu_ref[0, 0, :, D_EXPERT:]
        glu = jnp.minimum(glu, SWIGLU_LIMIT)
        lin = jnp.clip(lin, -SWIGLU_LIMIT, SWIGLU_LIMIT)
        act = glu * jax.nn.sigmoid(SWIGLU_ALPHA * glu) * (lin + 1.0)
        y = jnp.dot(act.astype(jnp.bfloat16), wd_bf[...], preferred_element_type=jnp.float32) + bd_ref[0, 0]
        _store_row_tiles(y_ref, y)

    @pl.when(n_valid <= 0)
    def _():
        y_ref[...] = jnp.zeros_like(y_ref)


def _moe_ffn(layer, table, x_rows, w_gu, b_gu, w_down, b_down):
    n_blocks = table[0].shape[0]
    depth = w_gu.shape[0]
    rows = lambda i, be, br, *_: (br[i], 0)
    expert = lambda i, be, *_: (layer, be[i], 0, 0)
    grid_spec = pltpu.PrefetchScalarGridSpec(
        num_scalar_prefetch=len(table),
        grid=(n_blocks,),
        in_specs=[
            pl.BlockSpec((EXPERT_ROWS * LANE_TILES, 128), rows),
            pl.BlockSpec(memory_space=pl.ANY),
            pl.BlockSpec((1, 1, 1, 2 * D_EXPERT), expert),
            pl.BlockSpec(memory_space=pl.ANY),
            pl.BlockSpec((1, 1, 1, D_MODEL), expert),
        ],
        out_specs=pl.BlockSpec((EXPERT_ROWS * LANE_TILES, 128), rows),
        scratch_shapes=[pltpu.VMEM((2, D_MODEL, 2 * D_EXPERT), jnp.float32),
                        pltpu.VMEM((2, D_EXPERT, D_MODEL), jnp.float32),
                        pltpu.VMEM((D_MODEL, 2 * D_EXPERT), jnp.bfloat16),
                        pltpu.VMEM((D_EXPERT, D_MODEL), jnp.bfloat16),
                        pltpu.SemaphoreType.DMA((2,)), pltpu.SemaphoreType.DMA((2,))],
    )
    return pl.pallas_call(
        functools.partial(_moe_ffn_kernel, layer=layer),
        grid_spec=grid_spec,
        out_shape=jax.ShapeDtypeStruct(x_rows.shape, jnp.float32),
        compiler_params=pltpu.CompilerParams(dimension_semantics=("arbitrary",), vmem_limit_bytes=VMEM_LIMIT),
        name="moe_ffn",
    )(*table, x_rows, w_gu, b_gu.reshape(depth, N_EXPERTS, 1, -1), w_down, b_down.reshape(depth, N_EXPERTS, 1, -1))


def _combine_kernel(dest_hbm, y_hbm, x_ref, g_ref, *rest, project):
    if project:
        gain_ref, w_ref, hg_ref, xo_ref, o_ref, ybuf0, ybuf1, idx_smem, isem, gsem = rest
    else:
        xo_ref, ybuf0, ybuf1, idx_smem, isem, gsem = rest
    ybufs = (ybuf0, ybuf1)
    i = pl.program_id(0)
    nt = pl.num_programs(0)

    def idx_copy(tile, s):
        return pltpu.make_async_copy(dest_hbm.at[jnp.minimum(tile, nt - 1)], _index_slot(idx_smem, s), isem.at[s])

    def row_copy(s, r, k):
        src = idx_smem[s * TILE_SLOTS + r * TOP_K + k]
        return pltpu.make_async_copy(_row_tile(y_hbm, src), _row_tile(ybufs[s], k * TOKEN_TILE + r), gsem.at[s])

    def wait_gather(s):
        pltpu.make_async_copy(y_hbm.at[pl.ds(0, TILE_SLOTS * LANE_TILES)], ybufs[s], gsem.at[s]).wait()

    @pl.when(i == 0)
    def _():
        idx_copy(0, 0).start()
        idx_copy(0, 0).wait()

        def body(r, carry):
            for k in range(TOP_K):
                row_copy(0, r, k).start(priority=k % 2)
            return carry

        lax.fori_loop(0, TOKEN_TILE, body, 0, unroll=4)
        idx_copy(1, 1).start()

    def step(slot):
        wait_gather(slot)
        idx_copy(i + 1, 1 - slot).wait()
        for r in range(TOKEN_TILE):
            for k in range(TOP_K):
                row_copy(1 - slot, r, k).start(priority=k % 2)
        idx_copy(i + 2, slot).start()

        g = g_ref[...]
        gates = [jnp.broadcast_to(g[:, k:k + 1], (TOKEN_TILE, 128)) for k in range(TOP_K)]
        cols = []
        for c in range(LANE_TILES):
            acc = x_ref[:, c * 128:(c + 1) * 128]
            for k in range(TOP_K):
                acc = acc + gates[k] * _load_row_tiles(ybufs[slot], k * TOKEN_TILE, TOKEN_TILE, c)
            cols.append(acc)
        x = jnp.concatenate(cols, axis=1)
        xo_ref[...] = x
        if project:
            _project(_rms(x, gain_ref[...]).astype(jnp.bfloat16), w_ref, hg_ref, o_ref)

        @pl.when(i == nt - 1)
        def _():
            wait_gather(1 - slot)
            idx_copy(i + 2, slot).wait()

    for parity in range(2):
        pl.when(i % 2 == parity)(functools.partial(step, parity))


def _combine_call(dest_tiles, y_rows, x2, gates, gain=None, w=None, head_gain=None):
    t = x2.shape[0]
    project = w is not None
    row = lambda width: pl.BlockSpec((TOKEN_TILE, width), lambda i: (i, 0))
    full = lambda shape: pl.BlockSpec(shape, lambda i: (0, 0))
    in_specs = [pl.BlockSpec(memory_space=pl.ANY), pl.BlockSpec(memory_space=pl.ANY), row(D_MODEL), row(ROUTER_PAD)]
    out_specs = [row(D_MODEL)]
    out_shape = [jax.ShapeDtypeStruct((t, D_MODEL), jnp.float32)]
    args = [dest_tiles, y_rows, x2, gates]
    if project:
        in_specs += [full((1, D_MODEL)), full((D_MODEL, PROJ_WIDTH)), full((1, PROJ_WIDTH))]
        out_specs.append(row(PROJ_WIDTH))
        out_shape.append(jax.ShapeDtypeStruct((t, PROJ_WIDTH), jnp.bfloat16))
        args += [gain, w, head_gain]
    out = pl.pallas_call(
        functools.partial(_combine_kernel, project=project),
        grid=(t // TOKEN_TILE,),
        in_specs=in_specs, out_specs=out_specs, out_shape=out_shape,
        scratch_shapes=[pltpu.VMEM((TILE_SLOTS * LANE_TILES, 128), jnp.float32),
                        pltpu.VMEM((TILE_SLOTS * LANE_TILES, 128), jnp.float32),
                        pltpu.SMEM((2 * TILE_SLOTS,), jnp.int32),
                        pltpu.SemaphoreType.DMA((2,)), pltpu.SemaphoreType.DMA((2,))],
        compiler_params=pltpu.CompilerParams(dimension_semantics=("arbitrary",), vmem_limit_bytes=VMEM_LIMIT),
        name="combine_in_proj" if project else "combine",
    )(*args)
    return out if project else out[0]


def _t5_bucket(rel):
    half = T5_BUCKETS // 2
    max_exact = half // 2
    ret = np.where(rel > 0, half, 0)
    n = np.abs(rel)
    nf = np.maximum(n, 1).astype(np.float32)
    large = max_exact + (np.log(nf / max_exact) / math.log(T5_MAX_DIST / max_exact)
                         * (half - max_exact)).astype(np.int32)
    large = np.minimum(large, half - 1)
    return ret + np.where(n < max_exact, n, large)


def _band_ok(n_prev, band):
    q_chunk = np.arange(Q_TILE)[:, None] // CHUNK + n_prev
    k_chunk = np.arange(band)[None, :] // CHUNK
    return (k_chunk <= q_chunk) & (k_chunk >= q_chunk - n_prev)


def _toeplitz(u, band):
    heads, length = u.shape
    flat = jnp.broadcast_to(u[:, None, :], (heads, Q_TILE, length)).reshape(heads, Q_TILE * length)
    skew = flat[:, :Q_TILE * (length - 1)].reshape(heads, Q_TILE, length - 1)
    return skew[:, :, Q_TILE - 1:Q_TILE - 1 + band]


def _a_bias(t5_bias):
    rel = np.arange(Q_TILE + A_BAND) - (Q_TILE - 1) - A_PAD
    bias = _toeplitz(t5_bias[_t5_bucket(rel)].T.astype(jnp.float32), A_BAND)
    bias = jnp.where(_band_ok(A_PREV_CHUNKS, A_BAND)[None], bias, NEG_INF)
    pairs = [jnp.concatenate([bias[4 * hk + parity], bias[4 * hk + 2 + parity]], axis=0)
             for hk in range(A_KV_HEADS) for parity in range(2)]
    return jnp.stack(pairs).transpose(0, 2, 1)


def _c_bias(rel_table):
    dist = C_PAD + (Q_TILE - 1) - np.arange(Q_TILE + C_BAND)
    idx = np.clip(dist, -(CHUNK - 1), C_REL_CLIP) + (CHUNK - 1)
    bias = _toeplitz(rel_table[idx].T.astype(jnp.float32), C_BAND)
    return jnp.where(_band_ok(C_PREV_CHUNKS, C_BAND)[None], bias, NEG_INF).transpose(0, 2, 1)


def _rope_tables(seq):
    half = B_ROPE_DIM // 2
    inv_freq = ROPE_THETA ** (-(jnp.arange(half, dtype=jnp.float32) / half))
    ang = jnp.arange(seq, dtype=jnp.float32)[:, None] * inv_freq[None, :]
    cos, sin = jnp.cos(ang), jnp.sin(ang)
    zeros = jnp.zeros((seq, B_NOPE_DIM), jnp.float32)
    tail = jnp.zeros((seq, B_HEAD_PAD - B_QK_DIM), jnp.float32)
    zh = jnp.zeros((seq, half), jnp.float32)
    cos_t = jnp.concatenate([zeros + 1.0, cos, cos, tail + 1.0], axis=1)
    sa = jnp.concatenate([zeros, -sin, zh, tail], axis=1)
    sb = jnp.concatenate([zeros, zh, sin, tail], axis=1)
    return cos_t, sa, sb


def _regroup_w_in(w):
    splits = np.cumsum((512, 128, 128, 256, 128, 32, 256, 256))
    aq, ak, av, bcq, bckv, bkpe, cq, ck, cv = jnp.split(w, splits.tolist(), axis=1)
    z = lambda n: jnp.zeros((w.shape[0], n), w.dtype)
    return jnp.concatenate([aq, ak, av, bcq, bckv, z(64), bkpe, z(32), cq, ck, cv], axis=1).astype(jnp.bfloat16)


def _head_gain_row(a_q, a_k, c_q, c_k):
    scale = HEAD_DIM ** -0.5
    row = jnp.ones((PROJ_WIDTH,), jnp.float32)
    for sec, heads, gain in ((SEC_AQ, A_Q_HEADS, a_q * scale), (SEC_AKV, A_KV_HEADS, a_k),
                             (SEC_CQ, C_HEADS, c_q * scale), (SEC_CK, C_HEADS, c_k)):
        row = row.at[sec:sec + heads * HEAD_DIM].set(jnp.tile(gain, heads))
    return row.reshape(1, PROJ_WIDTH)


def _pad_heads(w, head_dim, keep):
    rank = w.shape[0]
    w = w.reshape(rank, B_HEADS, head_dim)[:, :, :keep]
    w = jnp.pad(w, ((0, 0), (0, 0), (0, B_HEAD_PAD - keep)))
    return w.reshape(rank, B_HEADS * B_HEAD_PAD).astype(jnp.bfloat16)


def _pad_lanes(g, offset, width):
    return jnp.pad(g, (offset, width - offset - g.shape[0])).reshape(1, width)


def _block_table(counts, t):
    tk = t * TOP_K
    blocks_per_expert = t // EXPERT_ROWS
    nblk = (counts + EXPERT_ROWS - 1) // EXPERT_ROWS
    blk_end = jnp.cumsum(nblk)
    blk_first = blk_end - nblk
    n_blocks = (tk + N_EXPERTS * (EXPERT_ROWS - 1) + EXPERT_ROWS - 1) // EXPERT_ROWS
    blk = jnp.arange(n_blocks, dtype=jnp.int32)
    blk_e = jnp.minimum(jnp.sum(blk[:, None] >= blk_end[None, :], axis=1), N_EXPERTS - 1).astype(jnp.int32)
    own = blk_e[:, None] == jnp.arange(N_EXPERTS, dtype=jnp.int32)[None, :]
    pick = lambda v: jnp.sum(jnp.where(own, v[None, :], 0), axis=1)
    j = blk - pick(blk_first)
    blk_valid = jnp.clip(pick(counts) - j * EXPERT_ROWS, 0, EXPERT_ROWS).astype(jnp.int32)
    spare = N_EXPERTS * blocks_per_expert
    blk_row = jnp.where(blk_valid > 0, blk_e * blocks_per_expert + j, spare).astype(jnp.int32)
    experts = jnp.arange(N_EXPERTS, dtype=jnp.int32)
    used = counts > 0
    later = jnp.logical_and(experts[None, :] > experts[:, None], used[None, :])
    nxt = jnp.min(jnp.where(later, experts[None, :], N_EXPERTS), axis=1)
    nxt = jnp.where(nxt < N_EXPERTS, nxt, -1)
    ordinal = jnp.cumsum(used.astype(jnp.int32)) - 1
    blk_first = jnp.logical_and(j == 0, blk_valid > 0).astype(jnp.int32)
    blk_next = pick(nxt).astype(jnp.int32)
    blk_slot = (pick(ordinal) % 2).astype(jnp.int32)
    return blk_e, blk_row, blk_valid, blk_first, blk_next, blk_slot


def kernel(x, attn_norm, w_in, a_q_norm, a_k_norm, a_sinks, t5_bias, b_q_a_norm, b_w_q_b, b_kv_a_norm, b_w_kv_b,
           b_q_norm, b_k_norm, c_q_norm, c_k_norm, c_rel_bias, a_out_norm, b_out_norm, c_out_norm, w_out, ffn_norm,
           w_router, b_router, w_gate_up, b_gate_up, w_down, b_down):
    b, s, d = x.shape
    t = b * s
    depth = w_in.shape[0]
    a_bias = _a_bias(t5_bias)
    cos_t, sin_a, sin_b = _rope_tables(s)
    x2 = x.reshape(t, d)
    head_gain = lambda l: _head_gain_row(a_q_norm[l], a_k_norm[l], c_q_norm[l], c_k_norm[l])
    proj = _in_proj(x2, attn_norm[0].reshape(1, d), _regroup_w_in(w_in[0]), head_gain(0))
    for l in range(depth):
        proj3 = proj.reshape(b, s, PROJ_WIDTH)
        o_a = _attn_a(proj3, a_bias, a_sinks[l])
        kv_w = b_w_kv_b[l].reshape(B_KV_RANK, B_HEADS, B_NOPE_DIM + B_V_DIM)
        wv = kv_w[:, :, B_NOPE_DIM:].reshape(B_KV_RANK, B_HEADS * B_V_DIM).astype(jnp.bfloat16)
        qb, kb, vb = _latent_prep(
            proj, b_q_a_norm[l].reshape(1, -1), b_kv_a_norm[l].reshape(1, -1),
            _pad_heads(b_w_q_b[l], B_QK_DIM, B_QK_DIM), _pad_heads(b_w_kv_b[l], B_NOPE_DIM + B_V_DIM, B_NOPE_DIM), wv,
            _pad_lanes(b_q_norm[l], 0, B_HEAD_PAD), _pad_lanes(b_k_norm[l], 0, B_HEAD_PAD), cos_t, sin_a, sin_b, s)
        o_b = _attn_b(qb.reshape(b, s, -1), kb.reshape(b, s, -1), vb.reshape(b, s, -1))
        o_c = _attn_c(proj3, _c_bias(c_rel_bias[l]))
        wr = jnp.pad(w_router[l], ((0, 0), (0, ROUTER_PAD - N_EXPERTS)))
        wr_hi = wr.astype(jnp.bfloat16)
        wr = jnp.stack([wr_hi, (wr - wr_hi.astype(jnp.float32)).astype(jnp.bfloat16)])
        br = jnp.pad(b_router[l], (0, ROUTER_PAD - N_EXPERTS)).reshape(1, ROUTER_PAD)
        n_rows = N_EXPERTS * t + SPARE_ROWS
        x2, x_rows, dest, gates, counts = _out_proj(
            o_a.reshape(t, -1), o_b.reshape(t, -1), o_c.reshape(t, -1), x2,
            a_out_norm[l].reshape(1, -1), b_out_norm[l].reshape(1, -1), c_out_norm[l].reshape(1, -1),
            w_out[l].astype(jnp.bfloat16), ffn_norm[l].reshape(1, -1), wr, br, n_rows)
        dest_tiles = dest[:, :TOP_K].reshape(t // TOKEN_TILE, TOKEN_TILE * TOP_K)
        table = _block_table(counts[0, :N_EXPERTS], t)
        y_rows = _moe_ffn(l, table, x_rows, w_gate_up, b_gate_up, w_down, b_down)
        if l + 1 < depth:
            x2, proj = _combine_call(dest_tiles, y_rows, x2, gates, attn_norm[l + 1].reshape(1, d),
                                     _regroup_w_in(w_in[l + 1]), head_gain(l + 1))
        else:
            x2 = _combine_call(dest_tiles, y_rows, x2, gates)
    return x2.reshape(b, s, d)
```

```python
import functools
import math

import jax
import jax.numpy as jnp
import numpy as np
from jax import lax
from jax.experimental import pallas as pl
from jax.experimental.pallas import tpu as pltpu

D_MODEL = 1024
CHUNK = 64
HEAD_DIM = 64
EPS = 1e-6
NEG_INF = -1e30
A_Q_HEADS = 8
A_KV_HEADS = 2
A_GROUP = 4
A_PREV_CHUNKS = 2
T5_BUCKETS = 32
T5_MAX_DIST = 128
B_HEADS = 4
B_Q_RANK = 256
B_KV_RANK = 128
B_NOPE_DIM = 64
B_ROPE_DIM = 32
B_QK_DIM = 96
B_V_DIM = 64
ROPE_THETA = 10000.0
C_HEADS = 4
C_PREV_CHUNKS = 8
C_REL_CLIP = 256
N_EXPERTS = 32
TOP_K = 4
D_EXPERT = 1024
SWIGLU_LIMIT = 7.0
SWIGLU_ALPHA = 1.702

SEC_AQ = 0
SEC_AKV = 512
SEC_BCQ = 768
SEC_BCKV = 1024
SEC_BKPE = 1152
SEC_CQ = 1280
SEC_CK = 1536
SEC_CV = 1792
PROJ_WIDTH = 2048
HEAD_NORM_SLABS = tuple(sec // 128 + j for sec, n in ((SEC_AQ, 4), (SEC_AKV, 1), (SEC_CQ, 2), (SEC_CK, 2))
                        for j in range(n))
B_HEAD_PAD = 128

Q_TILE = 128
A_TILES_PER_STEP = 1
C_TILES_PER_STEP = 4
A_STEP_ROWS = Q_TILE * A_TILES_PER_STEP
C_STEP_ROWS = Q_TILE * C_TILES_PER_STEP
A_BAND = Q_TILE + A_PREV_CHUNKS * CHUNK
C_BAND = Q_TILE + C_PREV_CHUNKS * CHUNK
A_PAD = A_PREV_CHUNKS * CHUNK
C_PAD = C_PREV_CHUNKS * CHUNK
B_TILE = 512
ROW_TILE = 512
EXPERT_ROWS = 512
ROUTER_PAD = 128
TOKEN_TILE = 512
TILE_SLOTS = TOKEN_TILE * TOP_K
ROW_BUFFERS = 3
SPARE_ROWS = TILE_SLOTS
LANE_TILES = D_MODEL // 128
V7X_VMEM_BYTES = 64 * 1024 * 1024
VMEM_LIMIT = V7X_VMEM_BYTES * 3 // 4


def _rms(x, gain):
    return x * lax.rsqrt(jnp.mean(x * x, axis=-1, keepdims=True) + EPS) * gain


def _nt_dot(a, b):
    return lax.dot_general(a, b, (((1,), (1,)), ((), ())), preferred_element_type=jnp.float32)


def _load_row_tiles(ref, first_row, n_rows, c):
    return ref[pl.ds(first_row * LANE_TILES + c, n_rows, stride=LANE_TILES), :]


def _store_row_tiles(ref, value):
    for c in range(LANE_TILES):
        ref[pl.ds(c, value.shape[0], stride=LANE_TILES), :] = value[:, c * 128:(c + 1) * 128]


def _row_tile(ref, row):
    return ref.at[pl.ds(pl.multiple_of(row * LANE_TILES, LANE_TILES), LANE_TILES)]


def _index_slot(idx_smem, s):
    return idx_smem.at[pl.ds(pl.multiple_of(s * TILE_SLOTS, TILE_SLOTS), TILE_SLOTS)]


def _project(h, w_ref, hg_ref, o_ref):
    li = lax.broadcasted_iota(jnp.int32, (256, 128), 0) % 128 // HEAD_DIM
    lj = lax.broadcasted_iota(jnp.int32, (256, 128), 1) // HEAD_DIM
    same_head = jnp.where(li == lj, 1.0, 0.0).astype(jnp.bfloat16)
    for n in range(PROJ_WIDTH // 512):
        acc = jnp.dot(h, w_ref[:, n * 512:(n + 1) * 512], preferred_element_type=jnp.float32)
        slabs = []
        for j in range(4):
            slab = n * 4 + j
            a = acc[:, j * 128:(j + 1) * 128]
            if slab in HEAD_NORM_SLABS:
                sq = a * a
                hi = sq.astype(jnp.bfloat16)
                lo = (sq - hi.astype(jnp.float32)).astype(jnp.bfloat16)
                ss = jnp.dot(jnp.concatenate([hi, lo], axis=1), same_head, preferred_element_type=jnp.float32)
                a = a * lax.rsqrt(ss * (1.0 / HEAD_DIM) + EPS) * hg_ref[:, slab * 128:(slab + 1) * 128]
            slabs.append(a.astype(o_ref.dtype))
        o_ref[:, n * 512:(n + 1) * 512] = jnp.concatenate(slabs, axis=1)


def _in_proj_kernel(x_ref, g_ref, w_ref, hg_ref, o_ref):
    _project(_rms(x_ref[...], g_ref[...]).astype(jnp.bfloat16), w_ref, hg_ref, o_ref)


def _in_proj(x2, gain, w, head_gain):
    t = x2.shape[0]
    return pl.pallas_call(
        _in_proj_kernel,
        grid=(t // ROW_TILE,),
        in_specs=[
            pl.BlockSpec((ROW_TILE, D_MODEL), lambda i: (i, 0)),
            pl.BlockSpec((1, D_MODEL), lambda i: (0, 0)),
            pl.BlockSpec((D_MODEL, PROJ_WIDTH), lambda i: (0, 0)),
            pl.BlockSpec((1, PROJ_WIDTH), lambda i: (0, 0)),
        ],
        out_specs=pl.BlockSpec((ROW_TILE, PROJ_WIDTH), lambda i: (i, 0)),
        out_shape=jax.ShapeDtypeStruct((t, PROJ_WIDTH), jnp.bfloat16),
        compiler_params=pltpu.CompilerParams(dimension_semantics=("arbitrary",), vmem_limit_bytes=VMEM_LIMIT),
        name="in_proj",
    )(x2, gain, w, head_gain)


def _half_lanes(x, parity, fill=0.0):
    lane = lax.broadcasted_iota(jnp.int32, x.shape, x.ndim - 1) % 128
    keep = lane < HEAD_DIM if parity == 0 else lane >= HEAD_DIM
    return jnp.where(keep, x, jnp.full_like(x, fill))


def _tn_dot(a, b):
    return lax.dot_general(a, b, (((0,), (0,)), ((), ())), preferred_element_type=jnp.float32)


def _softmax_pv(s, key_ok, v_aug, sink=None, value_half=0):
    if key_ok is not None:
        s = jnp.where(key_ok, s, NEG_INF)
    m = jnp.max(s, axis=0, keepdims=True)
    if sink is not None:
        m = jnp.maximum(m, sink)
    e = jnp.exp(s - m).astype(jnp.bfloat16)
    o = _tn_dot(e, v_aug)
    if sink is not None:
        row = lax.broadcasted_iota(jnp.int32, (8, s.shape[1]), 0)
        e_sink = jnp.where(row == 0, jnp.exp(sink - m), 0.0).astype(jnp.bfloat16)
        first = lax.broadcasted_iota(jnp.int32, (8, 128), 0) == 0
        sel = _half_lanes(jnp.where(first, 1.0, 0.0), 1 - value_half).astype(jnp.bfloat16)
        o = o + _tn_dot(e_sink, sel)
    return o


def _merge_halves(halves):
    lane = lax.broadcasted_iota(jnp.int32, halves[0].shape, 1)
    normed = [o / pltpu.roll(o, HEAD_DIM, 1) for o in halves]
    return jnp.where(lane < HEAD_DIM, normed[0], normed[1])


def _attn_a_kernel(q_ref, kv_ref, bias_ref, sink_ref, o_ref, k_pad, v_pad):
    i = pl.program_id(1)
    seq = kv_ref.shape[1]

    @pl.when(i == 0)
    def _():
        for n in range(2 * A_KV_HEADS):
            k_pad[n, 0:A_PAD, :] = jnp.zeros((A_PAD, 128), k_pad.dtype)
            v_pad[n, 0:A_PAD, :] = jnp.zeros((A_PAD, 128), v_pad.dtype)

        def fill(r, carry):
            rows = pl.ds(pl.multiple_of(r * 256, 256), 256)
            dst = pl.ds(pl.multiple_of(r * 256 + A_PAD, 128), 256)
            kv = kv_ref[0, rows, :].astype(jnp.float32)
            for src, pad, other in ((kv[:, 0:128], k_pad, 0.0), (kv[:, 128:256], v_pad, 1.0)):
                swapped = pltpu.roll(src, HEAD_DIM, 1)
                pad[0, dst, :] = _half_lanes(src, 0, other).astype(pad.dtype)
                pad[1, dst, :] = _half_lanes(swapped, 1, other).astype(pad.dtype)
                pad[2, dst, :] = _half_lanes(swapped, 0, other).astype(pad.dtype)
                pad[3, dst, :] = _half_lanes(src, 1, other).astype(pad.dtype)
            return carry

        lax.fori_loop(0, seq // 256, fill, 0)

    first_slab = lax.broadcasted_iota(jnp.int32, (1, 2 * Q_TILE), 1) < Q_TILE

    def tiles(front):
        for sub in range(A_TILES_PER_STEP):
            rows = slice(sub * Q_TILE, (sub + 1) * Q_TILE)
            start = pl.multiple_of((i * A_TILES_PER_STEP + sub) * Q_TILE, Q_TILE)
            band = pl.ds(start, A_BAND)
            key_ok = None
            if front:
                key_ok = lax.broadcasted_iota(jnp.int32, (A_BAND, 2 * Q_TILE), 0) + (start - A_PAD) >= 0
            for hk in range(A_KV_HEADS):
                q2 = jnp.concatenate([q_ref[0, rows, (2 * hk) * 128:(2 * hk + 1) * 128],
                                      q_ref[0, rows, (2 * hk + 1) * 128:(2 * hk + 2) * 128]], axis=0)
                halves = []
                for parity in range(2):
                    sink = jnp.where(first_slab, sink_ref[4 * hk + parity], sink_ref[4 * hk + 2 + parity])
                    s = _nt_dot(k_pad[2 * hk + parity, band, :], q2) + bias_ref[2 * hk + parity]
                    halves.append(_softmax_pv(s, key_ok, v_pad[2 * hk + parity, band, :], sink, parity))
                out = _merge_halves(halves)
                o_ref[0, rows, (2 * hk) * 128:(2 * hk + 1) * 128] = out[0:Q_TILE].astype(o_ref.dtype)
                o_ref[0, rows, (2 * hk + 1) * 128:(2 * hk + 2) * 128] = out[Q_TILE:2 * Q_TILE].astype(o_ref.dtype)

    front_steps = -(-A_PAD // A_STEP_ROWS)
    pl.when(i < front_steps)(functools.partial(tiles, True))
    pl.when(i >= front_steps)(functools.partial(tiles, False))


def _attn_a(proj3, bias, sinks):
    b, s, _ = proj3.shape
    return pl.pallas_call(
        _attn_a_kernel,
        grid=(b, s // A_STEP_ROWS),
        in_specs=[
            pl.BlockSpec((1, A_STEP_ROWS, 512), lambda bi, i: (bi, i, SEC_AQ // 512)),
            pl.BlockSpec((1, s, 256), lambda bi, i: (bi, 0, SEC_AKV // 256)),
            pl.BlockSpec((2 * A_KV_HEADS, 2 * Q_TILE, A_BAND), lambda bi, i: (0, 0, 0)),
            pl.BlockSpec(memory_space=pltpu.SMEM),
        ],
        out_specs=pl.BlockSpec((1, A_STEP_ROWS, 512), lambda bi, i: (bi, i, 0)),
        out_shape=jax.ShapeDtypeStruct((b, s, 512), jnp.bfloat16),
        scratch_shapes=[pltpu.VMEM((2 * A_KV_HEADS, s + A_PAD, 128), jnp.bfloat16),
                        pltpu.VMEM((2 * A_KV_HEADS, s + A_PAD, 128), jnp.bfloat16)],
        compiler_params=pltpu.CompilerParams(dimension_semantics=("arbitrary", "arbitrary"),
                                             vmem_limit_bytes=VMEM_LIMIT),
        name="attn_a",
    )(proj3, proj3, bias, sinks)


def _latent_prep_kernel(cq_ref, ckv_ref, kpe_ref, qag_ref, kvag_ref, wq_ref, wk_ref, wv_ref,
                        qg_ref, kg_ref, cos_ref, sa_ref, sb_ref, qo_ref, ko_ref, vo_ref):
    lane = lax.broadcasted_iota(jnp.int32, (ROW_TILE, B_HEAD_PAD), 1)
    is_nope = lane < B_NOPE_DIM
    is_rope = jnp.logical_and(lane >= B_NOPE_DIM, lane < B_QK_DIM)
    cos = cos_ref[...]
    sa = sa_ref[...]
    sb = sb_ref[...]

    def split_norm(x, gain):
        sq = x * x
        nope = jnp.sum(jnp.where(is_nope, sq, 0.0), axis=-1, keepdims=True) * (1.0 / B_NOPE_DIM)
        rope = jnp.sum(jnp.where(is_rope, sq, 0.0), axis=-1, keepdims=True) * (1.0 / B_ROPE_DIM)
        r = jnp.where(is_nope, lax.rsqrt(nope + EPS), lax.rsqrt(rope + EPS))
        return x * r * gain

    def rotate(x):
        return x * cos + pltpu.roll(x, 112, 1) * sa + pltpu.roll(x, 16, 1) * sb

    cq = _rms(cq_ref[...].astype(jnp.float32), qag_ref[...]).astype(jnp.bfloat16)
    ckv = _rms(ckv_ref[...].astype(jnp.float32), kvag_ref[...]).astype(jnp.bfloat16)
    kpe = kpe_ref[...].astype(jnp.float32)
    kpe_n = rotate(split_norm(kpe, kg_ref[...]))
    kpe_n = jnp.where(is_rope, kpe_n, 0.0)
    v = jnp.dot(ckv, wv_ref[...], preferred_element_type=jnp.float32)
    for h in range(B_HEADS):
        pair = v[:, (h // 2) * 128:(h // 2 + 1) * 128]
        vo_ref[:, h * 128:(h + 1) * 128] = _half_lanes(pair, h % 2, 1.0).astype(vo_ref.dtype)
    qscale = B_QK_DIM ** -0.5
    for h in range(B_HEADS):
        cols = slice(h * B_HEAD_PAD, (h + 1) * B_HEAD_PAD)
        qh = jnp.dot(cq, wq_ref[:, cols], preferred_element_type=jnp.float32)
        qh = rotate(split_norm(qh, qg_ref[...])) * qscale
        qo_ref[:, cols] = qh.astype(qo_ref.dtype)
        kh = jnp.dot(ckv, wk_ref[:, cols], preferred_element_type=jnp.float32)
        kh = jnp.where(is_nope, split_norm(kh, kg_ref[...]), 0.0) + kpe_n
        ko_ref[:, cols] = kh.astype(ko_ref.dtype)


def _latent_prep(proj, q_a_gain, kv_a_gain, wq, wk, wv, q_gain, k_gain, cos, sa, sb, seq):
    t = proj.shape[0]
    pos_blocks = seq // ROW_TILE
    full = lambda shape: pl.BlockSpec(shape, lambda i: (0, 0))
    tab = pl.BlockSpec((ROW_TILE, B_HEAD_PAD), lambda i: (i % pos_blocks, 0))
    return pl.pallas_call(
        _latent_prep_kernel,
        grid=(t // ROW_TILE,),
        in_specs=[
            pl.BlockSpec((ROW_TILE, 256), lambda i: (i, SEC_BCQ // 256)),
            pl.BlockSpec((ROW_TILE, 128), lambda i: (i, SEC_BCKV // 128)),
            pl.BlockSpec((ROW_TILE, 128), lambda i: (i, SEC_BKPE // 128)),
            full((1, B_Q_RANK)), full((1, B_KV_RANK)),
            full((B_Q_RANK, B_HEADS * B_HEAD_PAD)), full((B_KV_RANK, B_HEADS * B_HEAD_PAD)),
            full((B_KV_RANK, B_HEADS * B_V_DIM)),
            full((1, B_HEAD_PAD)), full((1, B_HEAD_PAD)),
            tab, tab, tab,
        ],
        out_specs=[
            pl.BlockSpec((ROW_TILE, B_HEADS * B_HEAD_PAD), lambda i: (i, 0)),
            pl.BlockSpec((ROW_TILE, B_HEADS * B_HEAD_PAD), lambda i: (i, 0)),
            pl.BlockSpec((ROW_TILE, B_HEADS * B_HEAD_PAD), lambda i: (i, 0)),
        ],
        out_shape=[
            jax.ShapeDtypeStruct((t, B_HEADS * B_HEAD_PAD), jnp.bfloat16),
            jax.ShapeDtypeStruct((t, B_HEADS * B_HEAD_PAD), jnp.bfloat16),
            jax.ShapeDtypeStruct((t, B_HEADS * B_HEAD_PAD), jnp.bfloat16),
        ],
        compiler_params=pltpu.CompilerParams(dimension_semantics=("arbitrary",), vmem_limit_bytes=VMEM_LIMIT),
        name="latent_prep",
    )(proj, proj, proj, q_a_gain, kv_a_gain, wq, wk, wv, q_gain, k_gain, cos, sa, sb)


def _attn_b_kernel(q_ref, k_ref, v_ref, o_ref):
    i = pl.program_id(1)
    query_chunk = lax.broadcasted_iota(jnp.int32, (B_TILE, B_TILE), 0) // CHUNK
    key_chunk = lax.broadcasted_iota(jnp.int32, (B_TILE, B_TILE), 1) // CHUNK
    diag_ok = key_chunk <= query_chunk

    def step(j, carry, masked):
        rows = pl.ds(pl.multiple_of(j * B_TILE, B_TILE), B_TILE)
        out = []
        for h in range(B_HEADS):
            m, acc = carry[h]
            head = slice(h * B_HEAD_PAD, (h + 1) * B_HEAD_PAD)
            s = _nt_dot(q_ref[0, :, head], k_ref[0, rows, head])
            if masked:
                s = jnp.where(diag_ok, s, NEG_INF)
            m_new = jnp.maximum(m, jnp.max(s, axis=-1, keepdims=True))
            e = jnp.exp(s - m_new).astype(jnp.bfloat16)
            acc = jnp.exp(m - m_new) * acc + jnp.dot(e, v_ref[0, rows, head], preferred_element_type=jnp.float32)
            out.append((m_new, acc))
        return tuple(out)

    init = tuple((jnp.full((B_TILE, 1), NEG_INF, jnp.float32), jnp.zeros((B_TILE, B_HEAD_PAD), jnp.float32))
                 for _ in range(B_HEADS))
    carry = lax.fori_loop(0, i, lambda j, c: step(j, c, False), init)
    carry = step(i, carry, True)
    for pair in range(B_HEADS // 2):
        halves = [carry[2 * pair + parity][1] for parity in range(2)]
        o_ref[0, :, pair * 128:(pair + 1) * 128] = _merge_halves(halves).astype(o_ref.dtype)


def _attn_b(qb, kb, vb):
    b, s, _ = qb.shape
    return pl.pallas_call(
        _attn_b_kernel,
        grid=(b, s // B_TILE),
        in_specs=[
            pl.BlockSpec((1, B_TILE, B_HEADS * B_HEAD_PAD), lambda bi, i: (bi, i, 0)),
            pl.BlockSpec((1, s, B_HEADS * B_HEAD_PAD), lambda bi, i: (bi, 0, 0)),
            pl.BlockSpec((1, s, B_HEADS * B_HEAD_PAD), lambda bi, i: (bi, 0, 0)),
        ],
        out_specs=pl.BlockSpec((1, B_TILE, B_HEADS * B_V_DIM), lambda bi, i: (bi, i, 0)),
        out_shape=jax.ShapeDtypeStruct((b, s, B_HEADS * B_V_DIM), jnp.bfloat16),
        compiler_params=pltpu.CompilerParams(dimension_semantics=("arbitrary", "arbitrary"),
                                             vmem_limit_bytes=VMEM_LIMIT),
        name="attn_b",
    )(qb, kb, vb)


def _attn_c_kernel(q_ref, k_ref, v_ref, bias_ref, o_ref, k_pad, v_pad):
    i = pl.program_id(1)
    seq = k_ref.shape[1]
    width = C_HEADS * HEAD_DIM

    @pl.when(i == 0)
    def _():
        for parity in range(2):
            k_pad[parity, 0:C_PAD, :] = jnp.zeros((C_PAD, width), k_pad.dtype)
            v_pad[parity, 0:C_PAD, :] = jnp.zeros((C_PAD, width), v_pad.dtype)

        def fill(r, carry):
            rows = pl.ds(pl.multiple_of(r * 256, 256), 256)
            dst = pl.ds(pl.multiple_of(r * 256 + C_PAD, 256), 256)
            k = k_ref[0, rows, :]
            v = v_ref[0, rows, :]
            for parity in range(2):
                k_pad[parity, dst, :] = _half_lanes(k, parity)
                v_pad[parity, dst, :] = _half_lanes(v, parity, fill=1.0)
            return carry

        lax.fori_loop(0, seq // 256, fill, 0)

    def tiles(front):
        for sub in range(C_TILES_PER_STEP):
            rows = slice(sub * Q_TILE, (sub + 1) * Q_TILE)
            start = pl.multiple_of((i * C_TILES_PER_STEP + sub) * Q_TILE, Q_TILE)
            band = pl.ds(start, C_BAND)
            key_ok = None
            if front:
                key_ok = lax.broadcasted_iota(jnp.int32, (C_BAND, Q_TILE), 0) + (start - C_PAD) >= 0
            for slab in range(width // 128):
                lanes = slice(slab * 128, (slab + 1) * 128)
                q = q_ref[0, rows, lanes]
                halves = []
                for parity in range(2):
                    s = _nt_dot(k_pad[parity, band, lanes], q) + bias_ref[2 * slab + parity]
                    halves.append(_softmax_pv(s, key_ok, v_pad[parity, band, lanes]))
                o_ref[0, rows, lanes] = _merge_halves(halves).astype(o_ref.dtype)

    front_steps = -(-C_PAD // C_STEP_ROWS)
    pl.when(i < front_steps)(functools.partial(tiles, True))
    pl.when(i >= front_steps)(functools.partial(tiles, False))


def _attn_c(proj3, bias):
    b, s, _ = proj3.shape
    width = C_HEADS * HEAD_DIM
    return pl.pallas_call(
        _attn_c_kernel,
        grid=(b, s // C_STEP_ROWS),
        in_specs=[
            pl.BlockSpec((1, C_STEP_ROWS, width), lambda bi, i: (bi, i, SEC_CQ // width)),
            pl.BlockSpec((1, s, width), lambda bi, i: (bi, 0, SEC_CK // width)),
            pl.BlockSpec((1, s, width), lambda bi, i: (bi, 0, SEC_CV // width)),
            pl.BlockSpec((C_HEADS, C_BAND, Q_TILE), lambda bi, i: (0, 0, 0)),
        ],
        out_specs=pl.BlockSpec((1, C_STEP_ROWS, width), lambda bi, i: (bi, i, 0)),
        out_shape=jax.ShapeDtypeStruct((b, s, width), jnp.bfloat16),
        scratch_shapes=[pltpu.VMEM((2, s + C_PAD, width), jnp.bfloat16),
                        pltpu.VMEM((2, s + C_PAD, width), jnp.bfloat16)],
        compiler_params=pltpu.CompilerParams(dimension_semantics=("arbitrary", "arbitrary"),
                                             vmem_limit_bytes=VMEM_LIMIT),
        name="attn_c",
    )(proj3, proj3, proj3, bias)


def _out_proj_kernel(oa_ref, ob_ref, oc_ref, x_ref, ga_ref, gb_ref, gc_ref, w_ref, gf_ref, wr_ref, br_ref,
                     xo_ref, x_hbm, dest_ref, tg_ref, cnt_ref, count, hbuf, dvm, idx_smem, isem, dsem):
    i = pl.program_id(0)
    nt = pl.num_programs(0)
    cap = nt * TOKEN_TILE
    slot = i % ROW_BUFFERS
    prev = (i + ROW_BUFFERS - 1) % ROW_BUFFERS
    spare_row = x_hbm.shape[0] // LANE_TILES - TILE_SLOTS

    def index_copies(s):
        return [pltpu.make_async_copy(
            dvm.at[s, k], idx_smem.at[pl.ds(pl.multiple_of(s * TILE_SLOTS + k * TOKEN_TILE, TOKEN_TILE), TOKEN_TILE)],
            isem.at[s]) for k in range(TOP_K)]

    def row_copy(s, base, r, k):
        dst = idx_smem[base + k * TOKEN_TILE + r]
        return pltpu.make_async_copy(_row_tile(hbuf.at[s], r), _row_tile(x_hbm, dst), dsem.at[s])

    def wait_rows(s):
        for _ in range(TOP_K):
            pltpu.make_async_copy(hbuf.at[s], x_hbm.at[pl.ds(0, TOKEN_TILE * LANE_TILES)], dsem.at[s]).wait()

    @pl.when(i == 0)
    def _():
        count[...] = jnp.zeros_like(count)
        hbuf[ROW_BUFFERS - 1] = jnp.zeros(hbuf.shape[1:], hbuf.dtype)

        def fill(j, carry):
            idx_smem[(ROW_BUFFERS - 1) * TILE_SLOTS + j] = spare_row + j
            return carry

        lax.fori_loop(0, TILE_SLOTS, fill, 0)

    @pl.when(i >= ROW_BUFFERS - 1)
    def _():
        wait_rows(slot)

    @pl.when(i >= 1)
    def _():
        for copy in index_copies(prev):
            copy.wait()

    prev_base = prev * TILE_SLOTS
    for r in range(TOKEN_TILE):
        for k in range(TOP_K):
            row_copy(prev, prev_base, r, k).start(priority=k % 2)

    acc = x_ref[...]
    parts = ((oa_ref, ga_ref, 0, 512), (ob_ref, gb_ref, 512, 256), (oc_ref, gc_ref, 768, 256))
    for ref, g_ref, off, width in parts:
        o = _rms(ref[...].astype(jnp.float32), g_ref[...]).astype(jnp.bfloat16)
        acc = acc + jnp.dot(o, w_ref[off:off + width, :], preferred_element_type=jnp.float32)
    xo_ref[...] = acc
    h = _rms(acc, gf_ref[...])
    _store_row_tiles(hbuf.at[slot], h)
    h_hi = h.astype(jnp.bfloat16)
    h_lo = (h - h_hi.astype(jnp.float32)).astype(jnp.bfloat16)
    logits = (jnp.dot(h_hi, wr_ref[0], preferred_element_type=jnp.float32)
              + jnp.dot(h_lo, wr_ref[0], preferred_element_type=jnp.float32)
              + jnp.dot(h_hi, wr_ref[1], preferred_element_type=jnp.float32)) + br_ref[...]
    lane = lax.broadcasted_iota(jnp.int32, logits.shape, 1)
    work = jnp.where(lane < N_EXPERTS, logits, -jnp.inf)
    gate_out = jnp.zeros(logits.shape, jnp.float32)
    denom = jnp.zeros((logits.shape[0], 1), jnp.float32)
    top = None
    picks = []
    for k in range(TOP_K):
        m = jnp.max(work, axis=-1, keepdims=True)
        idx = jnp.min(jnp.where(work == m, lane, ROUTER_PAD), axis=-1, keepdims=True)
        pick = lane == idx
        work = jnp.where(pick, -jnp.inf, work)
        top = m if top is None else top
        e = jnp.exp(m - top)
        denom = denom + e
        gate_out = jnp.where(lane == k, e, gate_out)
        picks.append((idx, pick))
    tg_ref[...] = gate_out / denom
    chosen = jnp.zeros(logits.shape, jnp.float32)
    for _, pick in picks:
        chosen = jnp.where(pick, 1.0, chosen)
    r_i = lax.broadcasted_iota(jnp.int32, (TOKEN_TILE, TOKEN_TILE), 0)
    c_i = lax.broadcasted_iota(jnp.int32, (TOKEN_TILE, TOKEN_TILE), 1)
    earlier = jnp.where(c_i < r_i, 1.0, 0.0).astype(jnp.bfloat16)
    before = jnp.dot(earlier, chosen.astype(jnp.bfloat16), preferred_element_type=jnp.float32) + count[...]
    dest_out = jnp.zeros(logits.shape, jnp.int32)
    for k, (idx, pick) in enumerate(picks):
        rank = jnp.sum(jnp.where(pick, before, 0.0), axis=-1, keepdims=True).astype(jnp.int32)
        dest_out = jnp.where(lane == k, idx * cap + rank, dest_out)
    dest_ref[...] = dest_out
    count[...] = count[...] + jnp.sum(chosen, axis=0, keepdims=True)
    cnt_ref[...] = count[...].astype(jnp.int32)
    dvm[slot] = dest_out.T[0:8, :]
    for copy in index_copies(slot):
        copy.start()

    @pl.when(i == nt - 1)
    def _():
        for copy in index_copies(slot):
            copy.wait()
        base = slot * TILE_SLOTS

        def body(r, carry):
            for k in range(TOP_K):
                row_copy(slot, base, r, k).start(priority=k % 2)
            return carry

        lax.fori_loop(0, TOKEN_TILE, body, 0, unroll=4)
        for s in range(ROW_BUFFERS):
            wait_rows(s)


def _out_proj(oa, ob, oc, x2, ga, gb, gc, w, gf, wr, br, n_rows):
    t = x2.shape[0]
    assert t % TOKEN_TILE == 0 and t // TOKEN_TILE >= ROW_BUFFERS and n_rows == N_EXPERTS * t + SPARE_ROWS
    row = lambda width: pl.BlockSpec((TOKEN_TILE, width), lambda i: (i, 0))
    full = lambda shape: pl.BlockSpec(shape, lambda i: (0, 0))
    return pl.pallas_call(
        _out_proj_kernel,
        grid=(t // TOKEN_TILE,),
        in_specs=[row(512), row(256), row(256), row(D_MODEL), full((1, 512)), full((1, 256)), full((1, 256)),
                  full((D_MODEL, D_MODEL)), full((1, D_MODEL)),
                  pl.BlockSpec((2, D_MODEL, ROUTER_PAD), lambda i: (0, 0, 0)), full((1, ROUTER_PAD))],
        out_specs=[row(D_MODEL), pl.BlockSpec(memory_space=pl.ANY),
                   row(ROUTER_PAD), row(ROUTER_PAD), full((1, ROUTER_PAD))],
        out_shape=[jax.ShapeDtypeStruct((t, D_MODEL), jnp.float32),
                   jax.ShapeDtypeStruct((n_rows * LANE_TILES, 128), jnp.float32),
                   jax.ShapeDtypeStruct((t, ROUTER_PAD), jnp.int32),
                   jax.ShapeDtypeStruct((t, ROUTER_PAD), jnp.float32),
                   jax.ShapeDtypeStruct((1, ROUTER_PAD), jnp.int32)],
        scratch_shapes=[pltpu.VMEM((1, ROUTER_PAD), jnp.float32),
                        pltpu.VMEM((ROW_BUFFERS, TOKEN_TILE * LANE_TILES, 128), jnp.float32),
                        pltpu.VMEM((ROW_BUFFERS, 8, TOKEN_TILE), jnp.int32),
                        pltpu.SMEM((ROW_BUFFERS * TILE_SLOTS,), jnp.int32),
                        pltpu.SemaphoreType.DMA((ROW_BUFFERS,)), pltpu.SemaphoreType.DMA((ROW_BUFFERS,))],
        compiler_params=pltpu.CompilerParams(dimension_semantics=("arbitrary",), vmem_limit_bytes=VMEM_LIMIT),
        name="out_proj",
    )(oa, ob, oc, x2, ga, gb, gc, w, gf, wr, br)


def _moe_ffn_kernel(blk_e_ref, blk_row_ref, blk_valid_ref, blk_first_ref, blk_next_ref, blk_slot_ref,
                    x_ref, wgu_hbm, bgu_ref, wd_hbm, bd_ref, y_ref, wgu_f32, wd_f32, wgu_bf, wd_bf, gsem, dsem,
                    *, layer):
    i = pl.program_id(0)
    n_valid = blk_valid_ref[i]

    def weight_copies(e, s):
        return (pltpu.make_async_copy(wgu_hbm.at[layer, e], wgu_f32.at[s], gsem.at[s]),
                pltpu.make_async_copy(wd_hbm.at[layer, e], wd_f32.at[s], dsem.at[s]))

    @pl.when(i == 0)
    def _():
        for copy in weight_copies(blk_e_ref[0], blk_slot_ref[0]):
            copy.start()

    @pl.when(blk_first_ref[i] == 1)
    def _():
        s = blk_slot_ref[i]
        for copy in weight_copies(blk_e_ref[i], s):
            copy.wait()
        nxt = blk_next_ref[i]

        @pl.when(nxt >= 0)
        def _():
            for copy in weight_copies(nxt, 1 - s):
                copy.start()

        for c in range(4):
            cols = slice(c * 512, (c + 1) * 512)
            wgu_bf[:, cols] = wgu_f32[s, :, cols].astype(jnp.bfloat16)
        for c in range(2):
            cols = slice(c * 512, (c + 1) * 512)
            wd_bf[:, cols] = wd_f32[s, :, cols].astype(jnp.bfloat16)

    @pl.when(n_valid > 0)
    def _():
        live = lax.broadcasted_iota(jnp.int32, (EXPERT_ROWS, 128), 0) < n_valid
        x = jnp.concatenate([jnp.where(live, _load_row_tiles(x_ref, 0, EXPERT_ROWS, c), 0.0).astype(jnp.bfloat16)
                             for c in range(LANE_TILES)], axis=1)
        glu = jnp.dot(x, wgu_bf[:, 0:D_EXPERT], preferred_element_type=jnp.float32) + bgu_ref[0, 0, :, 0:D_EXPERT]
        lin = jnp.dot(x, wgu_bf[:, D_EXPERT:], preferred_element_type=jnp.float32) + bgu_ref[0, 0, :, D_EXPERT:]
        glu = jnp.minimum(glu, SWIGLU_LIMIT)
        lin = jnp.clip(lin, -SWIGLU_LIMIT, SWIGLU_LIMIT)
        act = glu * jax.nn.sigmoid(SWIGLU_ALPHA * glu) * (lin + 1.0)
        y = jnp.dot(act.astype(jnp.bfloat16), wd_bf[...], preferred_element_type=jnp.float32) + bd_ref[0, 0]
        _store_row_tiles(y_ref, y)

    @pl.when(n_valid <= 0)
    def _():
        y_ref[...] = jnp.zeros_like(y_ref)


def _moe_ffn(layer, table, x_rows, w_gu, b_gu, w_down, b_down):
    n_blocks = table[0].shape[0]
    depth = w_gu.shape[0]
    rows = lambda i, be, br, *_: (br[i], 0)
    expert = lambda i, be, *_: (layer, be[i], 0, 0)
    grid_spec = pltpu.PrefetchScalarGridSpec(
        num_scalar_prefetch=len(table),
        grid=(n_blocks,),
        in_specs=[
            pl.BlockSpec((EXPERT_ROWS * LANE_TILES, 128), rows),
            pl.BlockSpec(memory_space=pl.ANY),
            pl.BlockSpec((1, 1, 1, 2 * D_EXPERT), expert),
            pl.BlockSpec(memory_space=pl.ANY),
            pl.BlockSpec((1, 1, 1, D_MODEL), expert),
        ],
        out_specs=pl.BlockSpec((EXPERT_ROWS * LANE_TILES, 128), rows),
        scratch_shapes=[pltpu.VMEM((2, D_MODEL, 2 * D_EXPERT), jnp.float32),
                        pltpu.VMEM((2, D_EXPERT, D_MODEL), jnp.float32),
                        pltpu.VMEM((D_MODEL, 2 * D_EXPERT), jnp.bfloat16),
                        pltpu.VMEM((D_EXPERT, D_MODEL), jnp.bfloat16),
                        pltpu.SemaphoreType.DMA((2,)), pltpu.SemaphoreType.DMA((2,))],
    )
    return pl.pallas_call(
        functools.partial(_moe_ffn_kernel, layer=layer),
        grid_spec=grid_spec,
        out_shape=jax.ShapeDtypeStruct(x_rows.shape, jnp.float32),
        compiler_params=pltpu.CompilerParams(dimension_semantics=("arbitrary",), vmem_limit_bytes=VMEM_LIMIT),
        name="moe_ffn",
    )(*table, x_rows, w_gu, b_gu.reshape(depth, N_EXPERTS, 1, -1), w_down, b_down.reshape(depth, N_EXPERTS, 1, -1))


def _combine_kernel(dest_hbm, y_hbm, x_ref, g_ref, *rest, project):
    if project:
        gain_ref, w_ref, hg_ref, xo_ref, o_ref, ybuf0, ybuf1, idx_smem, isem, gsem = rest
    else:
        xo_ref, ybuf0, ybuf1, idx_smem, isem, gsem = rest
    ybufs = (ybuf0, ybuf1)
    i = pl.program_id(0)
    nt = pl.num_programs(0)

    def idx_copy(tile, s):
        return pltpu.make_async_copy(dest_hbm.at[jnp.minimum(tile, nt - 1)], _index_slot(idx_smem, s), isem.at[s])

    def row_copy(s, r, k):
        src = idx_smem[s * TILE_SLOTS + r * TOP_K + k]
        return pltpu.make_async_copy(_row_tile(y_hbm, src), _row_tile(ybufs[s], k * TOKEN_TILE + r), gsem.at[s])

    def wait_gather(s):
        pltpu.make_async_copy(y_hbm.at[pl.ds(0, TILE_SLOTS * LANE_TILES)], ybufs[s], gsem.at[s]).wait()

    @pl.when(i == 0)
    def _():
        idx_copy(0, 0).start()
        idx_copy(0, 0).wait()

        def body(r, carry):
            for k in range(TOP_K):
                row_copy(0, r, k).start(priority=k % 2)
            return carry

        lax.fori_loop(0, TOKEN_TILE, body, 0, unroll=4)
        idx_copy(1, 1).start()

    def step(slot):
        wait_gather(slot)
        idx_copy(i + 1, 1 - slot).wait()
        for r in range(TOKEN_TILE):
            for k in range(TOP_K):
                row_copy(1 - slot, r, k).start(priority=k % 2)
        idx_copy(i + 2, slot).start()

        g = g_ref[...]
        gates = [jnp.broadcast_to(g[:, k:k + 1], (TOKEN_TILE, 128)) for k in range(TOP_K)]
        cols = []
        for c in range(LANE_TILES):
            acc = x_ref[:, c * 128:(c + 1) * 128]
            for k in range(TOP_K):
                acc = acc + gates[k] * _load_row_tiles(ybufs[slot], k * TOKEN_TILE, TOKEN_TILE, c)
            cols.append(acc)
        x = jnp.concatenate(cols, axis=1)
        xo_ref[...] = x
        if project:
            _project(_rms(x, gain_ref[...]).astype(jnp.bfloat16), w_ref, hg_ref, o_ref)

        @pl.when(i == nt - 1)
        def _():
            wait_gather(1 - slot)
            idx_copy(i + 2, slot).wait()

    for parity in range(2):
        pl.when(i % 2 == parity)(functools.partial(step, parity))


def _combine_call(dest_tiles, y_rows, x2, gates, gain=None, w=None, head_gain=None):
    t = x2.shape[0]
    project = w is not None
    row = lambda width: pl.BlockSpec((TOKEN_TILE, width), lambda i: (i, 0))
    full = lambda shape: pl.BlockSpec(shape, lambda i: (0, 0))
    in_specs = [pl.BlockSpec(memory_space=pl.ANY), pl.BlockSpec(memory_space=pl.ANY), row(D_MODEL), row(ROUTER_PAD)]
    out_specs = [row(D_MODEL)]
    out_shape = [jax.ShapeDtypeStruct((t, D_MODEL), jnp.float32)]
    args = [dest_tiles, y_rows, x2, gates]
    if project:
        in_specs += [full((1, D_MODEL)), full((D_MODEL, PROJ_WIDTH)), full((1, PROJ_WIDTH))]
        out_specs.append(row(PROJ_WIDTH))
        out_shape.append(jax.ShapeDtypeStruct((t, PROJ_WIDTH), jnp.bfloat16))
        args += [gain, w, head_gain]
    out = pl.pallas_call(
        functools.partial(_combine_kernel, project=project),
        grid=(t // TOKEN_TILE,),
        in_specs=in_specs, out_specs=out_specs, out_shape=out_shape,
        scratch_shapes=[pltpu.VMEM((TILE_SLOTS * LANE_TILES, 128), jnp.float32),
                        pltpu.VMEM((TILE_SLOTS * LANE_TILES, 128), jnp.float32),
                        pltpu.SMEM((2 * TILE_SLOTS,), jnp.int32),
                        pltpu.SemaphoreType.DMA((2,)), pltpu.SemaphoreType.DMA((2,))],
        compiler_params=pltpu.CompilerParams(dimension_semantics=("arbitrary",), vmem_limit_bytes=VMEM_LIMIT),
        name="combine_in_proj" if project else "combine",
    )(*args)
    return out if project else out[0]


def _t5_bucket(rel):
    half = T5_BUCKETS // 2
    max_exact = half // 2
    ret = np.where(rel > 0, half, 0)
    n = np.abs(rel)
    nf = np.maximum(n, 1).astype(np.float32)
    large = max_exact + (np.log(nf / max_exact) / math.log(T5_MAX_DIST / max_exact)
                         * (half - max_exact)).astype(np.int32)
    large = np.minimum(large, half - 1)
    return ret + np.where(n < max_exact, n, large)


def _band_ok(n_prev, band):
    q_chunk = np.arange(Q_TILE)[:, None] // CHUNK + n_prev
    k_chunk = np.arange(band)[None, :] // CHUNK
    return (k_chunk <= q_chunk) & (k_chunk >= q_chunk - n_prev)


def _toeplitz(u, band):
    heads, length = u.shape
    flat = jnp.broadcast_to(u[:, None, :], (heads, Q_TILE, length)).reshape(heads, Q_TILE * length)
    skew = flat[:, :Q_TILE * (length - 1)].reshape(heads, Q_TILE, length - 1)
    return skew[:, :, Q_TILE - 1:Q_TILE - 1 + band]


def _a_bias(t5_bias):
    rel = np.arange(Q_TILE + A_BAND) - (Q_TILE - 1) - A_PAD
    bias = _toeplitz(t5_bias[_t5_bucket(rel)].T.astype(jnp.float32), A_BAND)
    bias = jnp.where(_band_ok(A_PREV_CHUNKS, A_BAND)[None], bias, NEG_INF)
    pairs = [jnp.concatenate([bias[4 * hk + parity], bias[4 * hk + 2 + parity]], axis=0)
             for hk in range(A_KV_HEADS) for parity in range(2)]
    return jnp.stack(pairs).transpose(0, 2, 1)


def _c_bias(rel_table):
    dist = C_PAD + (Q_TILE - 1) - np.arange(Q_TILE + C_BAND)
    idx = np.clip(dist, -(CHUNK - 1), C_REL_CLIP) + (CHUNK - 1)
    bias = _toeplitz(rel_table[idx].T.astype(jnp.float32), C_BAND)
    return jnp.where(_band_ok(C_PREV_CHUNKS, C_BAND)[None], bias, NEG_INF).transpose(0, 2, 1)


def _rope_tables(seq):
    half = B_ROPE_DIM // 2
    inv_freq = ROPE_THETA ** (-(jnp.arange(half, dtype=jnp.float32) / half))
    ang = jnp.arange(seq, dtype=jnp.float32)[:, None] * inv_freq[None, :]
    cos, sin = jnp.cos(ang), jnp.sin(ang)
    zeros = jnp.zeros((seq, B_NOPE_DIM), jnp.float32)
    tail = jnp.zeros((seq, B_HEAD_PAD - B_QK_DIM), jnp.float32)
    zh = jnp.zeros((seq, half), jnp.float32)
    cos_t = jnp.concatenate([zeros + 1.0, cos, cos, tail + 1.0], axis=1)
    sa = jnp.concatenate([zeros, -sin, zh, tail], axis=1)
    sb = jnp.concatenate([zeros, zh, sin, tail], axis=1)
    return cos_t, sa, sb


def _regroup_w_in(w):
    splits = np.cumsum((512, 128, 128, 256, 128, 32, 256, 256))
    aq, ak, av, bcq, bckv, bkpe, cq, ck, cv = jnp.split(w, splits.tolist(), axis=1)
    z = lambda n: jnp.zeros((w.shape[0], n), w.dtype)
    return jnp.concatenate([aq, ak, av, bcq, bckv, z(64), bkpe, z(32), cq, ck, cv], axis=1).astype(jnp.bfloat16)


def _head_gain_row(a_q, a_k, c_q, c_k):
    scale = HEAD_DIM ** -0.5
    row = jnp.ones((PROJ_WIDTH,), jnp.float32)
    for sec, heads, gain in ((SEC_AQ, A_Q_HEADS, a_q * scale), (SEC_AKV, A_KV_HEADS, a_k),
                             (SEC_CQ, C_HEADS, c_q * scale), (SEC_CK, C_HEADS, c_k)):
        row = row.at[sec:sec + heads * HEAD_DIM].set(jnp.tile(gain, heads))
    return row.reshape(1, PROJ_WIDTH)


def _pad_heads(w, head_dim, keep):
    rank = w.shape[0]
    w = w.reshape(rank, B_HEADS, head_dim)[:, :, :keep]
    w = jnp.pad(w, ((0, 0), (0, 0), (0, B_HEAD_PAD - keep)))
    return w.reshape(rank, B_HEADS * B_HEAD_PAD).astype(jnp.bfloat16)


def _pad_lanes(g, offset, width):
    return jnp.pad(g, (offset, width - offset - g.shape[0])).reshape(1, width)


def _block_table(counts, t):
    tk = t * TOP_K
    blocks_per_expert = t // EXPERT_ROWS
    nblk = (counts + EXPERT_ROWS - 1) // EXPERT_ROWS
    blk_end = jnp.cumsum(nblk)
    blk_first = blk_end - nblk
    n_blocks = (tk + N_EXPERTS * (EXPERT_ROWS - 1) + EXPERT_ROWS - 1) // EXPERT_ROWS
    blk = jnp.arange(n_blocks, dtype=jnp.int32)
    blk_e = jnp.minimum(jnp.sum(blk[:, None] >= blk_end[None, :], axis=1), N_EXPERTS - 1).astype(jnp.int32)
    own = blk_e[:, None] == jnp.arange(N_EXPERTS, dtype=jnp.int32)[None, :]
    pick = lambda v: jnp.sum(jnp.where(own, v[None, :], 0), axis=1)
    j = blk - pick(blk_first)
    blk_valid = jnp.clip(pick(counts) - j * EXPERT_ROWS, 0, EXPERT_ROWS).astype(jnp.int32)
    spare = N_EXPERTS * blocks_per_expert
    blk_row = jnp.where(blk_valid > 0, blk_e * blocks_per_expert + j, spare).astype(jnp.int32)
    experts = jnp.arange(N_EXPERTS, dtype=jnp.int32)
    used = counts > 0
    later = jnp.logical_and(experts[None, :] > experts[:, None], used[None, :])
    nxt = jnp.min(jnp.where(later, experts[None, :], N_EXPERTS), axis=1)
    nxt = jnp.where(nxt < N_EXPERTS, nxt, -1)
    ordinal = jnp.cumsum(used.astype(jnp.int32)) - 1
    blk_first = jnp.logical_and(j == 0, blk_valid > 0).astype(jnp.int32)
    blk_next = pick(nxt).astype(jnp.int32)
    blk_slot = (pick(ordinal) % 2).astype(jnp.int32)
    return blk_e, blk_row, blk_valid, blk_first, blk_next, blk_slot


def kernel(x, attn_norm, w_in, a_q_norm, a_k_norm, a_sinks, t5_bias, b_q_a_norm, b_w_q_b, b_kv_a_norm, b_w_kv_b,
           b_q_norm, b_k_norm, c_q_norm, c_k_norm, c_rel_bias, a_out_norm, b_out_norm, c_out_norm, w_out, ffn_norm,
           w_router, b_router, w_gate_up, b_gate_up, w_down, b_down):
    b, s, d = x.shape
    t = b * s
    depth = w_in.shape[0]
    a_bias = _a_bias(t5_bias)
    cos_t, sin_a, sin_b = _rope_tables(s)
    x2 = x.reshape(t, d)
    head_gain = lambda l: _head_gain_row(a_q_norm[l], a_k_norm[l], c_q_norm[l], c_k_norm[l])
    proj = _in_proj(x2, attn_norm[0].reshape(1, d), _regroup_w_in(w_in[0]), head_gain(0))
    for l in range(depth):
        proj3 = proj.reshape(b, s, PROJ_WIDTH)
        o_a = _attn_a(proj3, a_bias, a_sinks[l])
        kv_w = b_w_kv_b[l].reshape(B_KV_RANK, B_HEADS, B_NOPE_DIM + B_V_DIM)
        wv = kv_w[:, :, B_NOPE_DIM:].reshape(B_KV_RANK, B_HEADS * B_V_DIM).astype(jnp.bfloat16)
        qb, kb, vb = _latent_prep(
            proj, b_q_a_norm[l].reshape(1, -1), b_kv_a_norm[l].reshape(1, -1),
            _pad_heads(b_w_q_b[l], B_QK_DIM, B_QK_DIM), _pad_heads(b_w_kv_b[l], B_NOPE_DIM + B_V_DIM, B_NOPE_DIM), wv,
            _pad_lanes(b_q_norm[l], 0, B_HEAD_PAD), _pad_lanes(b_k_norm[l], 0, B_HEAD_PAD), cos_t, sin_a, sin_b, s)
        o_b = _attn_b(qb.reshape(b, s, -1), kb.reshape(b, s, -1), vb.reshape(b, s, -1))
        o_c = _attn_c(proj3, _c_bias(c_rel_bias[l]))
        wr = jnp.pad(w_router[l], ((0, 0), (0, ROUTER_PAD - N_EXPERTS)))
        wr_hi = wr.astype(jnp.bfloat16)
        wr = jnp.stack([wr_hi, (wr - wr_hi.astype(jnp.float32)).astype(jnp.bfloat16)])
        br = jnp.pad(b_router[l], (0, ROUTER_PAD - N_EXPERTS)).reshape(1, ROUTER_PAD)
        n_rows = N_EXPERTS * t + SPARE_ROWS
        x2, x_rows, dest, gates, counts = _out_proj(
            o_a.reshape(t, -1), o_b.reshape(t, -1), o_c.reshape(t, -1), x2,
            a_out_norm[l].reshape(1, -1), b_out_norm[l].reshape(1, -1), c_out_norm[l].reshape(1, -1),
            w_out[l].astype(jnp.bfloat16), ffn_norm[l].reshape(1, -1), wr, br, n_rows)
        dest_tiles = dest[:, :TOP_K].reshape(t // TOKEN_TILE, TOKEN_TILE * TOP_K)
        table = _block_table(counts[0, :N_EXPERTS], t)
        y_rows = _moe_ffn(l, table, x_rows, w_gate_up, b_gate_up, w_down, b_down)
        if l + 1 < depth:
            x2, proj = _combine_call(dest_tiles, y_rows, x2, gates, attn_norm[l + 1].reshape(1, d),
                                     _regroup_w_in(w_in[l + 1]), head_gain(l + 1))
        else:
            x2 = _combine_call(dest_tiles, y_rows, x2, gates)
    return x2.reshape(b, s, d)
```
